```python
import math
import jax
import jax.numpy as jnp
from jax import lax
import numpy as np

D_MODEL = 1024
BATCH = 2
SEQ = 8192
DEPTH = 4
DEC_BATCH = 8
DEC_SEQ = 16
PAST_LEN = 1024

CHUNK = 64
Q_BLOCK = 128
N_MIXERS = 3
N_A = len(range(0, DEPTH, N_MIXERS))
N_B = len(range(1, DEPTH, N_MIXERS))
N_C = len(range(2, DEPTH, N_MIXERS))
ROPE_THETA = 10000.0
EPS = 1e-6
D_FF = 2816

A_HEADS = 8
A_KV_HEADS = 2
A_HEAD_DIM = 128
IDX_HEADS = 8
IDX_DIM = 64
TOPK_MAX = 256
A_SCALE = A_HEAD_DIM ** -0.5
IDX_W_SCALE = (IDX_HEADS * IDX_DIM) ** -0.5
A_SPLIT = (A_HEADS * A_HEAD_DIM, A_KV_HEADS * A_HEAD_DIM, A_KV_HEADS * A_HEAD_DIM,
           IDX_HEADS * IDX_DIM, IDX_DIM, IDX_HEADS)
A_IN = sum(A_SPLIT)

B_HEADS = 8
Q_LORA = 384
KV_LORA = 256
NOPE_DIM = 128
ROPE_DIM = 64
V_DIM = 128
B_SCALE = (NOPE_DIM + ROPE_DIM) ** -0.5
B_SPLIT = (Q_LORA, KV_LORA, ROPE_DIM)
B_IN = sum(B_SPLIT)

C_HEADS = 4
C_DIM = 128
C_SCALE = C_DIM ** -0.5
C_SPLIT = (C_HEADS * 2 * C_DIM, C_HEADS * 2 * C_DIM, C_HEADS * 2 * C_DIM)
C_IN = sum(C_SPLIT)

kernel_name = 'hybrid_streaming_dsa_mla_diff_step'


def _rms(x, g):
    x32 = x.astype(jnp.float32)
    y = x32 * lax.rsqrt(jnp.mean(x32 * x32, axis=-1, keepdims=True) + EPS)
    return (y * g.astype(jnp.float32)).astype(x.dtype)


def _rope(x, pos):
    half = x.shape[-1] // 2
    inv = jnp.power(ROPE_THETA, -jnp.arange(half, dtype=jnp.float32) / half)
    ang = pos.astype(jnp.float32)[:, None] * inv[None, :]
    bshape = (ang.shape[0],) + (1,) * (x.ndim - 3) + (half,)
    cos = jnp.cos(ang).reshape(bshape)
    sin = jnp.sin(ang).reshape(bshape)
    x32 = x.astype(jnp.float32)
    x1, x2 = x32[..., :half], x32[..., half:]
    return jnp.concatenate([x1 * cos - x2 * sin, x2 * cos + x1 * sin], axis=-1).astype(x.dtype)


def _split(z, sizes):
    cuts, acc = [], 0
    for s in sizes[:-1]:
        acc += s
        cuts.append(acc)
    return jnp.split(z, cuts, axis=-1)


def _admissible(q_pos, k_pos):
    return (k_pos // CHUNK)[None, None, :] <= (q_pos // CHUNK)[:, :, None]


def _over_query_blocks(fn, *q_args):
    T = q_args[0].shape[1]
    if T <= Q_BLOCK or T % Q_BLOCK != 0:
        return fn(*q_args)
    nb = T // Q_BLOCK
    blocks = tuple(jnp.moveaxis(a.reshape((a.shape[0], nb, Q_BLOCK) + a.shape[2:]), 1, 0) for a in q_args)
    out = lax.map(lambda args: fn(*args), blocks)
    out = jnp.moveaxis(out, 0, 1)
    return out.reshape((out.shape[0], T) + out.shape[3:])


def _swiglu(h, wg, wu, wd):
    return (jax.nn.silu(h @ wg) * (h @ wu)) @ wd


def _dense_block(q, k, v, q_pos, k_pos, scale):
    s = jnp.einsum('bthd,bshd->bhts', q, k).astype(jnp.float32) * scale
    ok = _admissible(q_pos, k_pos)[:, None]
    p = jax.nn.softmax(jnp.where(ok, s, -jnp.inf), axis=-1)
    return jnp.einsum('bhts,bshe->bthe', p.astype(v.dtype), v)


def _diff_block(q, k, v, q_pos, k_pos, lam):
    s = jnp.einsum('btnpd,bsnpd->bnpts', q, k).astype(jnp.float32) * C_SCALE
    ok = _admissible(q_pos, k_pos)[:, None, None]
    p = jax.nn.softmax(jnp.where(ok, s, -jnp.inf), axis=-1)
    w = p[:, :, 0] - lam * p[:, :, 1]
    return jnp.einsum('bnts,bsne->btne', w.astype(v.dtype), v)


def _dsa_block(q, qi, wi, q_pos, k, v, ki, topk):
    B, t = q.shape[0], q.shape[1]
    sc = jnp.einsum('bthe,bse->bths', qi, ki).astype(jnp.float32)
    score = jnp.einsum('bth,bths->bts', wi.astype(jnp.float32), jax.nn.relu(sc))
    k_pos = jnp.arange(ki.shape[1])
    score = jnp.where(_admissible(q_pos, k_pos), score, -jnp.inf)
    _, sel = lax.top_k(score, topk)
    gather = jax.vmap(lambda rows, ids: rows[ids])
    k_sel = gather(k, sel)
    v_sel = gather(v, sel)
    valid = (sel // CHUNK) <= (q_pos[:, :, None] // CHUNK)
    qg = q.reshape(B, t, A_KV_HEADS, A_HEADS // A_KV_HEADS, A_HEAD_DIM)
    s = jnp.einsum('btkgd,btnkd->btkgn', qg, k_sel).astype(jnp.float32) * A_SCALE
    s = jnp.where(valid[:, :, None, None, :], s, -jnp.inf)
    p = jax.nn.softmax(s, axis=-1)
    o = jnp.einsum('btkgn,btnkd->btkgd', p.astype(v.dtype), v_sel)
    return o.reshape(B, t, A_HEADS * A_HEAD_DIM)


def _mixer_dsa(h, pos, past, w_in, q_norm, k_norm, idx_k_norm, w_out):
    B, T, _ = h.shape
    q, k, v, qi, ki, wi = _split(h @ w_in, A_SPLIT)
    q = _rope(_rms(q.reshape(B, T, A_HEADS, A_HEAD_DIM), q_norm), pos)
    k = _rope(_rms(k.reshape(B, T, A_KV_HEADS, A_HEAD_DIM), k_norm), pos)
    v = v.reshape(B, T, A_KV_HEADS, A_HEAD_DIM)
    qi = _rope(qi.reshape(B, T, IDX_HEADS, IDX_DIM), pos)
    ki = _rope(_rms(ki, idx_k_norm), pos)
    wi = wi * IDX_W_SCALE
    if past is None:
        k_all, v_all, ki_all = k, v, ki
    else:
        k_all = jnp.concatenate([past[0], k], axis=1)
        v_all = jnp.concatenate([past[1], v], axis=1)
        ki_all = jnp.concatenate([past[2], ki], axis=1)
    topk = min(TOPK_MAX, k_all.shape[1] // 4)
    q_pos = jnp.broadcast_to(pos, (B, T))
    fn = lambda qb, qib, wib, pb: _dsa_block(qb, qib, wib, pb, k_all, v_all, ki_all, topk)
    o = _over_query_blocks(fn, q, qi, wi, q_pos)
    return o @ w_out, (k, v, ki)


def _mixer_mla(h, pos, past, w_in, q_a_norm, kv_a_norm, w_uq, w_ukv,
               q_nope_norm, q_rope_norm, k_nope_norm, k_rope_norm, w_out):
    B, T, _ = h.shape
    cq, ckv, kr = _split(h @ w_in, B_SPLIT)
    cq = _rms(cq, q_a_norm)
    ckv = _rms(ckv, kv_a_norm)
    kr = _rope(_rms(kr, k_rope_norm), pos)
    if past is None:
        ckv_all, kr_all = ckv, kr
    else:
        ckv_all = jnp.concatenate([past[0], ckv], axis=1)
        kr_all = jnp.concatenate([past[1], kr], axis=1)
    S = ckv_all.shape[1]
    q_nope, q_rope = jnp.split((cq @ w_uq).reshape(B, T, B_HEADS, NOPE_DIM + ROPE_DIM), [NOPE_DIM], axis=-1)
    q = jnp.concatenate([_rms(q_nope, q_nope_norm), _rope(_rms(q_rope, q_rope_norm), pos)], axis=-1)
    k_nope, v = jnp.split((ckv_all @ w_ukv).reshape(B, S, B_HEADS, NOPE_DIM + V_DIM), [NOPE_DIM], axis=-1)
    k = jnp.concatenate([_rms(k_nope, k_nope_norm),
                         jnp.broadcast_to(kr_all[:, :, None, :], (B, S, B_HEADS, ROPE_DIM))], axis=-1)
    k_pos = jnp.arange(S)
    q_pos = jnp.broadcast_to(pos, (B, T))
    fn = lambda qb, pb: _dense_block(qb, k, v, pb, k_pos, B_SCALE)
    o = _over_query_blocks(fn, q, q_pos)
    return o.reshape(B, T, B_HEADS * V_DIM) @ w_out, (ckv, kr)


def _mixer_diff(h, pos, past, lam_init, w_in, q_norm, k_norm, lq1, lk1, lq2, lk2, sub_norm, w_out):
    B, T, _ = h.shape
    q, k, v = _split(h @ w_in, C_SPLIT)
    q = _rope(_rms(q.reshape(B, T, C_HEADS, 2, C_DIM), q_norm), pos)
    k = _rope(_rms(k.reshape(B, T, C_HEADS, 2, C_DIM), k_norm), pos)
    v = v.reshape(B, T, C_HEADS, 2 * C_DIM)
    if past is None:
        k_all, v_all = k, v
    else:
        k_all = jnp.concatenate([past[0], k], axis=1)
        v_all = jnp.concatenate([past[1], v], axis=1)
    f32 = jnp.float32
    lam = (jnp.exp(jnp.sum(lq1.astype(f32) * lk1.astype(f32)))
           - jnp.exp(jnp.sum(lq2.astype(f32) * lk2.astype(f32))) + lam_init)
    k_pos = jnp.arange(k_all.shape[1])
    q_pos = jnp.broadcast_to(pos, (B, T))
    fn = lambda qb, pb: _diff_block(qb, k_all, v_all, pb, k_pos, lam)
    o = _over_query_blocks(fn, q, q_pos)
    o = _rms(o, sub_norm) * (1.0 - lam_init)
    return o.reshape(B, T, C_HEADS * 2 * C_DIM) @ w_out, (k, v)


def _trunk(x, offset, past, W):
    B, T, _ = x.shape
    pos = offset + jnp.arange(T, dtype=jnp.int32)
    rows = {'a_k': [], 'a_v': [], 'a_ik': [], 'b_ckv': [], 'b_kr': [], 'c_k': [], 'c_v': []}
    for i in range(DEPTH):
        x = x + 0.5 * _swiglu(_rms(x, W['ffn1_norm'][i]), W['ffn1_wg'][i], W['ffn1_wu'][i], W['ffn1_wd'][i])
        h = _rms(x, W['mix_norm'][i])
        kind, j = i % N_MIXERS, i // N_MIXERS
        if kind == 0:
            pa = None if past is None else (past['a_k'][j], past['a_v'][j], past['a_ik'][j])
            m, (nk, nv, nik) = _mixer_dsa(h, pos, pa, W['a_w_in'][j], W['a_q_norm'][j], W['a_k_norm'][j],
                                          W['a_idx_k_norm'][j], W['a_w_out'][j])
            rows['a_k'].append(nk)
            rows['a_v'].append(nv)
            rows['a_ik'].append(nik)
        elif kind == 1:
            pb = None if past is None else (past['b_ckv'][j], past['b_kr'][j])
            m, (nckv, nkr) = _mixer_mla(h, pos, pb, W['b_w_in'][j], W['b_q_a_norm'][j], W['b_kv_a_norm'][j],
                                        W['b_w_uq'][j], W['b_w_ukv'][j], W['b_q_nope_norm'][j],
                                        W['b_q_rope_norm'][j], W['b_k_nope_norm'][j], W['b_k_rope_norm'][j],
                                        W['b_w_out'][j])
            rows['b_ckv'].append(nckv)
            rows['b_kr'].append(nkr)
        else:
            lam_init = 0.8 - 0.6 * math.exp(-0.3 * i)
            pc = None if past is None else (past['c_k'][j], past['c_v'][j])
            m, (nk, nv) = _mixer_diff(h, pos, pc, lam_init, W['c_w_in'][j], W['c_q_norm'][j], W['c_k_norm'][j],
                                      W['c_lambda_q1'][j], W['c_lambda_k1'][j], W['c_lambda_q2'][j],
                                      W['c_lambda_k2'][j], W['c_sub_norm'][j], W['c_w_out'][j])
            rows['c_k'].append(nk)
            rows['c_v'].append(nv)
        x = x + m
        x = x + 0.5 * _swiglu(_rms(x, W['ffn2_norm'][i]), W['ffn2_wg'][i], W['ffn2_wu'][i], W['ffn2_wd'][i])
    new_rows = (jnp.stack(rows['a_k']), jnp.stack(rows['a_v']), jnp.stack(rows['a_ik']),
                jnp.stack(rows['b_ckv']), jnp.stack(rows['b_kr']),
                jnp.stack(rows['c_k']), jnp.stack(rows['c_v']))
    return x, new_rows


def setup_inputs(seed: int = 0) -> dict:
    key = jax.random.key(seed)
    ks = jax.random.split(key, 64)
    counter = [0]

    def nrm(shape, scale=1.0):
        k = ks[counter[0]]
        counter[0] += 1
        return scale * jax.random.normal(k, shape, jnp.float32)

    def gain(shape):
        return 1.0 + nrm(shape, 0.02)

    D, F = D_MODEL, D_FF
    return {
        'x_prompt': nrm((BATCH, SEQ, D)),
        'x_sample': nrm((DEC_BATCH, DEC_SEQ, D)),
        'cache_a_k': nrm((N_A, DEC_BATCH, PAST_LEN, A_KV_HEADS, A_HEAD_DIM)),
        'cache_a_v': nrm((N_A, DEC_BATCH, PAST_LEN, A_KV_HEADS, A_HEAD_DIM)),
        'cache_a_idx_k': nrm((N_A, DEC_BATCH, PAST_LEN, IDX_DIM)),
        'cache_b_ckv': nrm((N_B, DEC_BATCH, PAST_LEN, KV_LORA)),
        'cache_b_krope': nrm((N_B, DEC_BATCH, PAST_LEN, ROPE_DIM)),
        'cache_c_k': nrm((N_C, DEC_BATCH, PAST_LEN, C_HEADS, 2, C_DIM)),
        'cache_c_v': nrm((N_C, DEC_BATCH, PAST_LEN, C_HEADS, 2 * C_DIM)),
        'ffn1_norm': gain((DEPTH, D)),
        'ffn1_wg': nrm((DEPTH, D, F), D ** -0.5),
        'ffn1_wu': nrm((DEPTH, D, F), D ** -0.5),
        'ffn1_wd': nrm((DEPTH, F, D), F ** -0.5),
        'mix_norm': gain((DEPTH, D)),
        'ffn2_norm': gain((DEPTH, D)),
        'ffn2_wg': nrm((DEPTH, D, F), D ** -0.5),
        'ffn2_wu': nrm((DEPTH, D, F), D ** -0.5),
        'ffn2_wd': nrm((DEPTH, F, D), F ** -0.5),
        'a_w_in': nrm((N_A, D, A_IN), D ** -0.5),
        'a_q_norm': gain((N_A, A_HEAD_DIM)),
        'a_k_norm': gain((N_A, A_HEAD_DIM)),
        'a_idx_k_norm': gain((N_A, IDX_DIM)),
        'a_w_out': nrm((N_A, A_HEADS * A_HEAD_DIM, D), (A_HEADS * A_HEAD_DIM) ** -0.5),
        'b_w_in': nrm((N_B, D, B_IN), D ** -0.5),
        'b_q_a_norm': gain((N_B, Q_LORA)),
        'b_kv_a_norm': gain((N_B, KV_LORA)),
        'b_w_uq': nrm((N_B, Q_LORA, B_HEADS * (NOPE_DIM + ROPE_DIM)), Q_LORA ** -0.5),
        'b_w_ukv': nrm((N_B, KV_LORA, B_HEADS * (NOPE_DIM + V_DIM)), KV_LORA ** -0.5),
        'b_q_nope_norm': gain((N_B, NOPE_DIM)),
        'b_q_rope_norm': gain((N_B, ROPE_DIM)),
        'b_k_nope_norm': gain((N_B, NOPE_DIM)),
        'b_k_rope_norm': gain((N_B, ROPE_DIM)),
        'b_w_out': nrm((N_B, B_HEADS * V_DIM, D), (B_HEADS * V_DIM) ** -0.5),
        'c_w_in': nrm((N_C, D, C_IN), D ** -0.5),
        'c_q_norm': gain((N_C, C_DIM)),
        'c_k_norm': gain((N_C, C_DIM)),
        'c_lambda_q1': nrm((N_C, C_DIM), 0.1),
        'c_lambda_k1': nrm((N_C, C_DIM), 0.1),
        'c_lambda_q2': nrm((N_C, C_DIM), 0.1),
        'c_lambda_k2': nrm((N_C, C_DIM), 0.1),
        'c_sub_norm': gain((N_C, 2 * C_DIM)),
        'c_w_out': nrm((N_C, C_HEADS * 2 * C_DIM, D), (C_HEADS * 2 * C_DIM) ** -0.5),
    }


def reference(x_prompt, x_sample, cache_a_k, cache_a_v, cache_a_idx_k, cache_b_ckv, cache_b_krope,
              cache_c_k, cache_c_v, ffn1_norm, ffn1_wg, ffn1_wu, ffn1_wd, mix_norm, ffn2_norm, ffn2_wg,
              ffn2_wu, ffn2_wd, a_w_in, a_q_norm, a_k_norm, a_idx_k_norm, a_w_out, b_w_in, b_q_a_norm,
              b_kv_a_norm, b_w_uq, b_w_ukv, b_q_nope_norm, b_q_rope_norm, b_k_nope_norm, b_k_rope_norm,
              b_w_out, c_w_in, c_q_norm, c_k_norm, c_lambda_q1, c_lambda_k1, c_lambda_q2, c_lambda_k2,
              c_sub_norm, c_w_out):
    W = {
        'ffn1_norm': ffn1_norm, 'ffn1_wg': ffn1_wg, 'ffn1_wu': ffn1_wu, 'ffn1_wd': ffn1_wd,
        'mix_norm': mix_norm,
        'ffn2_norm': ffn2_norm, 'ffn2_wg': ffn2_wg, 'ffn2_wu': ffn2_wu, 'ffn2_wd': ffn2_wd,
        'a_w_in': a_w_in, 'a_q_norm': a_q_norm, 'a_k_norm': a_k_norm, 'a_idx_k_norm': a_idx_k_norm,
        'a_w_out': a_w_out,
        'b_w_in': b_w_in, 'b_q_a_norm': b_q_a_norm, 'b_kv_a_norm': b_kv_a_norm, 'b_w_uq': b_w_uq,
        'b_w_ukv': b_w_ukv, 'b_q_nope_norm': b_q_nope_norm, 'b_q_rope_norm': b_q_rope_norm,
        'b_k_nope_norm': b_k_nope_norm, 'b_k_rope_norm': b_k_rope_norm, 'b_w_out': b_w_out,
        'c_w_in': c_w_in, 'c_q_norm': c_q_norm, 'c_k_norm': c_k_norm, 'c_lambda_q1': c_lambda_q1,
        'c_lambda_k1': c_lambda_k1, 'c_lambda_q2': c_lambda_q2, 'c_lambda_k2': c_lambda_k2,
        'c_sub_norm': c_sub_norm, 'c_w_out': c_w_out,
    }
    past = {'a_k': cache_a_k, 'a_v': cache_a_v, 'a_ik': cache_a_idx_k, 'b_ckv': cache_b_ckv,
            'b_kr': cache_b_krope, 'c_k': cache_c_k, 'c_v': cache_c_v}
    y_prompt, (new_a_k_p, new_a_v_p, new_a_idx_k_p, new_b_ckv_p, new_b_krope_p,
               new_c_k_p, new_c_v_p) = _trunk(x_prompt, 0, None, W)
    y_sample, (new_a_k_s, new_a_v_s, new_a_idx_k_s, new_b_ckv_s, new_b_krope_s,
               new_c_k_s, new_c_v_s) = _trunk(x_sample, PAST_LEN, past, W)
    return (y_prompt, y_sample,
            new_a_k_p, new_a_v_p, new_a_idx_k_p, new_b_ckv_p, new_b_krope_p, new_c_k_p, new_c_v_p,
            new_a_k_s, new_a_v_s, new_a_idx_k_s, new_b_ckv_s, new_b_krope_s, new_c_k_s, new_c_v_s)
```

```python
import functools
import math

import jax
import jax.numpy as jnp
from jax import lax
from jax.experimental import pallas as pl
from jax.experimental.pallas import tpu as pltpu

F32 = jnp.float32
BF16 = jnp.bfloat16
I32 = jnp.int32

D_MODEL = 1024
DEPTH = 4
CHUNK_SHIFT = 6
N_MIXERS = 3
ROPE_THETA = 10000.0
EPS = 1e-6
D_FF = 2816

A_HEADS = 8
A_KV_HEADS = 2
A_GROUP = A_HEADS // A_KV_HEADS
A_HEAD_DIM = 128
IDX_HEADS = 8
IDX_DIM = 64
TOPK_MAX = 256
A_SCALE = A_HEAD_DIM ** -0.5
IDX_W_SCALE = (IDX_HEADS * IDX_DIM) ** -0.5
A_IN = 2120
A_IN_PAD = 2176

B_HEADS = 8
Q_LORA = 384
KV_LORA = 256
NOPE_DIM = 128
ROPE_DIM = 64
V_DIM = 128
B_SCALE = (NOPE_DIM + ROPE_DIM) ** -0.5
B_IN = 704
B_IN_PAD = 768

C_HEADS = 4
C_DIM = 128
C_SCALE = C_DIM ** -0.5

LANES = 128
NEG = -1e30
INT_MIN = -(2 ** 31)
VMEM_LIMIT = 56 * 1024 * 1024


def _cparams(n_axes):
    return pltpu.CompilerParams(dimension_semantics=("arbitrary",) * n_axes, vmem_limit_bytes=VMEM_LIMIT)


def _dot(a, b):
    return jnp.dot(a, b, preferred_element_type=F32)


def _dot_nt(a, b):
    return lax.dot_general(a, b, (((1,), (1,)), ((), ())), preferred_element_type=F32)


def _rms(x, g, n):
    ms = jnp.sum(x * x, axis=-1, keepdims=True) * (1.0 / n)
    return x * lax.rsqrt(ms + EPS) * g


def _rope128(x, c, s):
    return x * c + pltpu.roll(x, 64, 1) * s


def _rope64(x, c, sa, sb):
    return x * c + pltpu.roll(x, 96, 1) * sa + pltpu.roll(x, 32, 1) * sb


def _ffn_body(x_ref, g_ref, wg_ref, wu_ref, wd_ref, o_ref, *, fc):
    x = x_ref[...]
    h = _rms(x, g_ref[...], D_MODEL).astype(BF16)
    y = None
    for c in range(D_FF // fc):
        a = _dot(h, wg_ref[:, c * fc:(c + 1) * fc])
        u = _dot(h, wu_ref[:, c * fc:(c + 1) * fc])
        act = (a * jax.nn.sigmoid(a) * u).astype(BF16)
        part = _dot(act, wd_ref[c * fc:(c + 1) * fc, :])
        y = part if y is None else y + part
    o_ref[...] = x + 0.5 * y


def _const_spec(shape):
    nd = len(shape)
    return pl.BlockSpec(shape, lambda *_: (0,) * nd, pipeline_mode=pl.Buffered(1))


def _row_spec(tm, n):
    return pl.BlockSpec((tm, n), lambda i: (i, 0))


def _ffn(x, g, wg, wu, wd, tm):
    n = x.shape[0]
    return pl.pallas_call(
        functools.partial(_ffn_body, fc=D_FF // 2),
        grid=(n // tm,),
        in_specs=[_row_spec(tm, D_MODEL), _const_spec((1, D_MODEL)), _const_spec((D_MODEL, D_FF)),
                  _const_spec((D_MODEL, D_FF)), _const_spec((D_FF, D_MODEL))],
        out_specs=_row_spec(tm, D_MODEL),
        out_shape=jax.ShapeDtypeStruct((n, D_MODEL), F32),
        compiler_params=_cparams(1),
        name="ffn",
    )(x, g, wg, wu, wd)


def _out_proj_body(o_ref, w_ref, x_ref, y_ref):
    y_ref[...] = x_ref[...] + _dot(o_ref[...], w_ref[...])


def _out_proj(o, w, x, tm):
    n, k = o.shape
    return pl.pallas_call(
        _out_proj_body,
        grid=(n // tm,),
        in_specs=[_row_spec(tm, k), _const_spec((k, D_MODEL)), _row_spec(tm, D_MODEL)],
        out_specs=_row_spec(tm, D_MODEL),
        out_shape=jax.ShapeDtypeStruct((n, D_MODEL), F32),
        compiler_params=_cparams(1),
        name="out_proj",
    )(o, w, x)


def _a_proj_body(x_ref, g_ref, w_ref, qn_ref, kn_ref, ikn_ref, c128_ref, s128_ref, c64_ref, sa64_ref, sb64_ref,
                 q_ref, k32_ref, v32_ref, kbf_ref, vbf_ref, qi_ref, ki32_ref, kidx_ref, wi_ref):
    h = _rms(x_ref[...], g_ref[...], D_MODEL).astype(BF16)
    y = _dot(h, w_ref[...])
    c128, s128 = c128_ref[...], s128_ref[...]
    c64, sa64, sb64 = c64_ref[...], sa64_ref[...], sb64_ref[...]
    for hd in range(A_HEADS):
        sl = slice(hd * 128, (hd + 1) * 128)
        qh = _rope128(_rms(y[:, sl], qn_ref[...], A_HEAD_DIM), c128, s128)
        q_ref[:, sl] = (qh * A_SCALE).astype(BF16)
    for hd in range(A_KV_HEADS):
        sl = slice(hd * 128, (hd + 1) * 128)
        kh = _rope128(_rms(y[:, 1024 + hd * 128:1024 + (hd + 1) * 128], kn_ref[...], A_HEAD_DIM), c128, s128)
        k32_ref[:, sl] = kh
        kbf_ref[:, sl] = kh.astype(BF16)
    v = y[:, 1280:1536]
    v32_ref[...] = v
    vbf_ref[...] = v.astype(BF16)
    for p in range(IDX_HEADS // 2):
        sl = slice(p * 128, (p + 1) * 128)
        qi_ref[:, sl] = _rope64(y[:, 1536 + p * 128:1536 + (p + 1) * 128], c64, sa64, sb64).astype(BF16)
    tail = y[:, 2048:2176]
    lane = lax.broadcasted_iota(I32, tail.shape, 1)
    low = lane < IDX_DIM
    kin = jnp.where(low, tail, 0.0)
    ki = _rope64(_rms(kin, ikn_ref[...], IDX_DIM), c64, sa64, sb64)
    ki = jnp.where(low, ki, 0.0)
    ki32_ref[...] = ki[:, :IDX_DIM]
    kidx_ref[:, 0:128] = ki.astype(BF16)
    kidx_ref[:, 128:256] = pltpu.roll(ki, 64, 1).astype(BF16)
    wi_ref[...] = pltpu.roll(tail, 64, 1) * IDX_W_SCALE


def _tab_spec(tm, t):
    nt = t // tm
    return pl.BlockSpec((tm, LANES), lambda i: (i % nt, 0))


def _a_proj(x, g, w, qn, kn, ikn, tabs, tm):
    n = x.shape[0]
    t = tabs["c128"].shape[0]
    outs = [(1024, BF16), (256, F32), (256, F32), (256, BF16), (256, BF16), (512, BF16), (IDX_DIM, F32),
            (256, BF16), (LANES, F32)]
    return pl.pallas_call(
        _a_proj_body,
        grid=(n // tm,),
        in_specs=[_row_spec(tm, D_MODEL), _const_spec((1, D_MODEL)), _const_spec((D_MODEL, A_IN_PAD)),
                  _const_spec((1, 128)), _const_spec((1, 128)), _const_spec((1, 128))]
        + [_tab_spec(tm, t)] * 5,
        out_specs=[_row_spec(tm, c) for c, _ in outs],
        out_shape=[jax.ShapeDtypeStruct((n, c), dt) for c, dt in outs],
        compiler_params=_cparams(1),
        name="a_proj",
    )(x, g, w, qn, kn, ikn, tabs["c128"], tabs["s128"], tabs["c64"], tabs["sa64"], tabs["sb64"])


def _b_proj_body(x_ref, g_ref, w_ref, qan_ref, kvan_ref, krn_ref, wuq_ref, qnn_ref, qrn_ref,
                 c64_ref, sa64_ref, sb64_ref, q_ref, ckv_ref, kr32_ref, krbf_ref):
    h = _rms(x_ref[...], g_ref[...], D_MODEL).astype(BF16)
    y = _dot(h, w_ref[...])
    c64, sa64, sb64 = c64_ref[...], sa64_ref[...], sb64_ref[...]
    ckv_ref[...] = _rms(y[:, 384:640], kvan_ref[...], KV_LORA)
    kr = _rope64(_rms(y[:, 640:768], krn_ref[...], ROPE_DIM), c64, sa64, sb64)
    kr32_ref[...] = kr[:, :ROPE_DIM]
    krbf_ref[...] = kr.astype(BF16)
    cq = _rms(y[:, 0:384], qan_ref[...], Q_LORA).astype(BF16)
    qq = _dot(cq, wuq_ref[...])
    for hd in range(B_HEADS):
        qn = _rms(qq[:, hd * 128:(hd + 1) * 128], qnn_ref[...], NOPE_DIM)
        qr = _rms(qq[:, 1024 + hd * 128:1024 + (hd + 1) * 128], qrn_ref[...], ROPE_DIM)
        qr = _rope64(qr, c64, sa64, sb64)
        q_ref[:, hd * 256:hd * 256 + 128] = (qn * B_SCALE).astype(BF16)
        q_ref[:, hd * 256 + 128:(hd + 1) * 256] = (qr * B_SCALE).astype(BF16)


def _b_proj(x, g, w, qan, kvan, krn, wuq, qnn, qrn, tabs, tm):
    n = x.shape[0]
    t = tabs["c64"].shape[0]
    outs = [(B_HEADS * 256, BF16), (KV_LORA, F32), (ROPE_DIM, F32), (LANES, BF16)]
    return pl.pallas_call(
        _b_proj_body,
        grid=(n // tm,),
        in_specs=[_row_spec(tm, D_MODEL), _const_spec((1, D_MODEL)), _const_spec((D_MODEL, B_IN_PAD)),
                  _const_spec((1, Q_LORA)), _const_spec((1, KV_LORA)), _const_spec((1, 128)),
                  _const_spec((Q_LORA, 2048)), _const_spec((1, 128)), _const_spec((1, 128))]
        + [_tab_spec(tm, t)] * 3,
        out_specs=[_row_spec(tm, c) for c, _ in outs],
        out_shape=[jax.ShapeDtypeStruct((n, c), dt) for c, dt in outs],
        compiler_params=_cparams(1),
        name="b_proj",
    )(x, g, w, qan, kvan, krn, wuq, qnn, qrn, tabs["c64"], tabs["sa64"], tabs["sb64"])


def _kv_up_body(ckv_ref, w_ref, knn_ref, kn_ref, v_ref):
    y = _dot(ckv_ref[...].astype(BF16), w_ref[...])
    for hd in range(B_HEADS):
        kn_ref[:, hd * 128:(hd + 1) * 128] = _rms(y[:, hd * 256:hd * 256 + 128], knn_ref[...], NOPE_DIM).astype(BF16)
        v_ref[:, hd * 128:(hd + 1) * 128] = y[:, hd * 256 + 128:(hd + 1) * 256].astype(BF16)


def _kv_up(ckv, w, knn, tm):
    n = ckv.shape[0]
    return pl.pallas_call(
        _kv_up_body,
        grid=(n // tm,),
        in_specs=[_row_spec(tm, KV_LORA), _const_spec((KV_LORA, 2048)), _const_spec((1, 128))],
        out_specs=[_row_spec(tm, 1024), _row_spec(tm, 1024)],
        out_shape=[jax.ShapeDtypeStruct((n, 1024), BF16)] * 2,
        compiler_params=_cparams(1),
        name="kv_up",
    )(ckv, w, knn)


def _c_proj_body(x_ref, g_ref, w_ref, qn_ref, kn_ref, c128_ref, s128_ref,
                 q_ref, k32_ref, v32_ref, kbf_ref, vbf_ref):
    h = _rms(x_ref[...], g_ref[...], D_MODEL).astype(BF16)
    y = _dot(h, w_ref[...])
    c128, s128 = c128_ref[...], s128_ref[...]
    for hd in range(2 * C_HEADS):
        sl = slice(hd * 128, (hd + 1) * 128)
        qh = _rope128(_rms(y[:, sl], qn_ref[...], C_DIM), c128, s128)
        q_ref[:, sl] = (qh * C_SCALE).astype(BF16)
        kh = _rope128(_rms(y[:, 1024 + hd * 128:1024 + (hd + 1) * 128], kn_ref[...], C_DIM), c128, s128)
        k32_ref[:, sl] = kh
        kbf_ref[:, sl] = kh.astype(BF16)
    v = y[:, 2048:3072]
    v32_ref[...] = v
    vbf_ref[...] = v.astype(BF16)


def _c_proj(x, g, w, qn, kn, tabs, tm):
    n = x.shape[0]
    t = tabs["c128"].shape[0]
    outs = [(1024, BF16), (1024, F32), (1024, F32), (1024, BF16), (1024, BF16)]
    return pl.pallas_call(
        _c_proj_body,
        grid=(n // tm,),
        in_specs=[_row_spec(tm, D_MODEL), _const_spec((1, D_MODEL)), _const_spec((D_MODEL, 3072)),
                  _const_spec((1, 128)), _const_spec((1, 128))] + [_tab_spec(tm, t)] * 2,
        out_specs=[_row_spec(tm, c) for c, _ in outs],
        out_shape=[jax.ShapeDtypeStruct((n, c), dt) for c, dt in outs],
        compiler_params=_cparams(1),
        name="c_proj",
    )(x, g, w, qn, kn, tabs["c128"], tabs["s128"])


def _n_key_blocks(i, tq, kb, q_off, s_valid):
    last_chunk = lax.shift_right_logical(q_off + (i + 1) * tq - 1, CHUNK_SHIFT)
    kend = jnp.minimum((last_chunk + 1) << CHUNK_SHIFT, s_valid)
    return lax.shift_right_logical(kend + kb - 1, int(math.log2(kb)))


def _q_chunk(i, tq, q_off):
    row = lax.broadcasted_iota(I32, (tq, 1), 0)
    return lax.shift_right_logical(q_off + i * tq + row, CHUNK_SHIFT)


def _k_chunk(ks, kb, s_valid):
    kpos = ks + lax.broadcasted_iota(I32, (1, kb), 1)
    return jnp.where(kpos < s_valid, lax.shift_right_logical(kpos, CHUNK_SHIFT), 2 ** 30)


def _flash_step(s, v, m_ref, l_ref, acc_ref):
    m_prev = m_ref[...]
    m_new = jnp.maximum(m_prev, jnp.max(s, axis=1, keepdims=True))
    alpha = jnp.exp(m_prev - m_new)
    p = jnp.exp(s - m_new)
    l_ref[...] = alpha * l_ref[...] + jnp.sum(p, axis=1, keepdims=True)
    acc_ref[...] = alpha * acc_ref[...] + _dot(p.astype(BF16), v)
    m_ref[...] = m_new


def _flash_init(m_ref, l_ref, acc_ref):
    m_ref[...] = jnp.full(m_ref.shape, NEG, F32)
    l_ref[...] = jnp.zeros(l_ref.shape, F32)
    acc_ref[...] = jnp.zeros(acc_ref.shape, F32)


def _dsa_body(q_ref, qi_ref, wi_ref, kidx_ref, k_ref, v_ref, o_ref, keys_ref, m_ref, l_ref, acc_ref,
              *, tq, kb, q_off, s_valid, topk):
    i = pl.program_id(1)
    n_blk = _n_key_blocks(i, tq, kb, q_off, s_valid)
    qc = _q_chunk(i, tq, q_off)
    wi = wi_ref[...]
    wcol = [wi[:, hd:hd + 1] for hd in range(IDX_HEADS)]

    def score_block(j, carry):
        ks = pl.multiple_of(j * kb, kb)
        k_lo = kidx_ref[pl.ds(ks, kb), 0:128]
        k_hi = kidx_ref[pl.ds(ks, kb), 128:256]
        sc = jnp.zeros((tq, kb), F32)
        for p in range(IDX_HEADS // 2):
            qp = qi_ref[:, p * 128:(p + 1) * 128]
            sc = sc + wcol[2 * p] * jnp.maximum(_dot_nt(qp, k_lo), 0.0)
            sc = sc + wcol[2 * p + 1] * jnp.maximum(_dot_nt(qp, k_hi), 0.0)
        bits = lax.bitcast_convert_type(sc, I32)
        key = bits ^ ((bits >> 31) & 0x7FFFFFFF)
        keys_ref[j] = jnp.where(_k_chunk(ks, kb, s_valid) <= qc, key, INT_MIN)
        return carry

    lax.fori_loop(0, n_blk, score_block, 0)

    def count(pred):
        def blk(j, c):
            x = jnp.where(pred(keys_ref[j], j * kb), 1.0, 0.0)
            part = x[:, 0:LANES]
            for g in range(1, kb // LANES):
                part = part + x[:, g * LANES:(g + 1) * LANES]
            return c + part
        c = lax.fori_loop(0, n_blk, blk, jnp.zeros((tq, LANES), F32))
        return jnp.sum(c, axis=1, keepdims=True)

    kf = float(topk)

    def bit_step(b, t):
        cand = t + jnp.left_shift(jnp.int32(1), 31 - b)
        cnt = count(lambda kblk, _: kblk >= cand)
        return jnp.where(cnt >= kf, cand, t)

    thr = lax.fori_loop(0, 32, bit_step, jnp.full((tq, 1), INT_MIN, I32))
    need = kf - count(lambda kblk, _: kblk > thr)
    n_eq = count(lambda kblk, _: kblk == thr)
    partial = jnp.logical_and(n_eq > need, thr != INT_MIN)

    @pl.when(jnp.max(jnp.where(partial, 1.0, 0.0)) > 0.0)
    def _():
        lane = lax.broadcasted_iota(I32, (1, kb), 1)

        def idx_step(b, x):
            cand = x + jnp.left_shift(jnp.int32(1), b)
            cnt = count(lambda kblk, k0: jnp.logical_and(kblk == thr, k0 + lane < cand))
            return jnp.where(cnt < need, cand, x)

        nbits = max(1, int(s_valid - 1).bit_length())
        cut = lax.fori_loop(0, nbits, lambda b, x: idx_step(nbits - 1 - b, x), jnp.zeros((tq, 1), I32))

        def demote(j, carry):
            kblk = keys_ref[j]
            drop = jnp.logical_and(jnp.logical_and(kblk == thr, j * kb + lane > cut), partial)
            keys_ref[j] = jnp.where(drop, kblk - 1, kblk)
            return carry

        lax.fori_loop(0, n_blk, demote, 0)

    thr_sel = jnp.maximum(thr, INT_MIN + 1)

    _flash_init(m_ref, l_ref, acc_ref)

    def attend(j, carry):
        ks = pl.multiple_of(j * kb, kb)
        bias = jnp.where(keys_ref[j] >= thr_sel, 0.0, NEG)
        for g in range(A_KV_HEADS):
            qg = jnp.concatenate([q_ref[:, (g * A_GROUP + a) * 128:(g * A_GROUP + a + 1) * 128]
                                  for a in range(A_GROUP)], axis=0)
            s = _dot_nt(qg, k_ref[pl.ds(ks, kb), g * 128:(g + 1) * 128])
            s = (s.reshape(A_GROUP, tq, kb) + bias[None]).reshape(A_GROUP * tq, kb)
            _flash_step(s, v_ref[pl.ds(ks, kb), g * 128:(g + 1) * 128], m_ref.at[g], l_ref.at[g], acc_ref.at[g])
        return carry

    lax.fori_loop(0, n_blk, attend, 0)
    for g in range(A_KV_HEADS):
        o = acc_ref[g] / l_ref[g]
        for a in range(A_GROUP):
            hd = g * A_GROUP + a
            o_ref[:, hd * 128:(hd + 1) * 128] = o[a * tq:(a + 1) * tq].astype(BF16)


def _dsa_attn(q, qi, wi, kidx, k, v, *, tq, kb, q_off, s_valid, topk):
    b, t, _ = q.shape
    s_pad = k.shape[1]
    qspec = lambda c: pl.BlockSpec((None, tq, c), lambda bi, i: (bi, i, 0))
    kspec = lambda c: pl.BlockSpec((None, s_pad, c), lambda bi, i: (bi, 0, 0))
    rows = A_GROUP * tq
    return pl.pallas_call(
        functools.partial(_dsa_body, tq=tq, kb=kb, q_off=q_off, s_valid=s_valid, topk=topk),
        grid=(b, t // tq),
        in_specs=[qspec(1024), qspec(512), qspec(LANES), kspec(256), kspec(256), kspec(256)],
        out_specs=qspec(1024),
        out_shape=jax.ShapeDtypeStruct((b, t, 1024), BF16),
        scratch_shapes=[pltpu.VMEM((s_pad // kb, tq, kb), I32), pltpu.VMEM((A_KV_HEADS, rows, 1), F32),
                        pltpu.VMEM((A_KV_HEADS, rows, 1), F32), pltpu.VMEM((A_KV_HEADS, rows, 128), F32)],
        compiler_params=_cparams(2),
        name="dsa_attn",
    )(q, qi, wi, kidx, k, v)


def _mla_body(q_ref, kn_ref, kr_ref, v_ref, o_ref, m_ref, l_ref, acc_ref, *, tq, kb, q_off, s_valid):
    i = pl.program_id(2)
    n_blk = _n_key_blocks(i, tq, kb, q_off, s_valid)
    qc = _q_chunk(i, tq, q_off)
    _flash_init(m_ref, l_ref, acc_ref)

    def attend(j, carry):
        ks = pl.multiple_of(j * kb, kb)
        kcat = jnp.concatenate([kn_ref[pl.ds(ks, kb), :], kr_ref[pl.ds(ks, kb), :]], axis=1)
        s = _dot_nt(q_ref[...], kcat)
        s = jnp.where(_k_chunk(ks, kb, s_valid) <= qc, s, NEG)
        _flash_step(s, v_ref[pl.ds(ks, kb), :], m_ref, l_ref, acc_ref)
        return carry

    lax.fori_loop(0, n_blk, attend, 0)
    o_ref[...] = (acc_ref[...] / l_ref[...]).astype(BF16)


def _mla_attn(q, kn, kr, v, *, tq, kb, q_off, s_valid):
    b, t, _ = q.shape
    s_pad = kn.shape[1]
    return pl.pallas_call(
        functools.partial(_mla_body, tq=tq, kb=kb, q_off=q_off, s_valid=s_valid),
        grid=(b, B_HEADS, t // tq),
        in_specs=[pl.BlockSpec((None, tq, 256), lambda bi, h, i: (bi, i, h)),
                  pl.BlockSpec((None, s_pad, 128), lambda bi, h, i: (bi, 0, h)),
                  pl.BlockSpec((None, s_pad, 128), lambda bi, h, i: (bi, 0, 0)),
                  pl.BlockSpec((None, s_pad, 128), lambda bi, h, i: (bi, 0, h))],
        out_specs=pl.BlockSpec((None, tq, 128), lambda bi, h, i: (bi, i, h)),
        out_shape=jax.ShapeDtypeStruct((b, t, B_HEADS * V_DIM), BF16),
        scratch_shapes=[pltpu.VMEM((tq, 1), F32), pltpu.VMEM((tq, 1), F32), pltpu.VMEM((tq, 128), F32)],
        compiler_params=_cparams(3),
        name="mla_attn",
    )(q, kn, kr, v)


def _diff_body(q_ref, k_ref, v_ref, lq1_ref, lk1_ref, lq2_ref, lk2_ref, sn_ref, o_ref, m_ref, l_ref, acc_ref,
               *, tq, kb, q_off, s_valid, lam_init):
    i = pl.program_id(2)
    n_blk = _n_key_blocks(i, tq, kb, q_off, s_valid)
    qc = _q_chunk(i, tq, q_off)
    _flash_init(m_ref, l_ref, acc_ref)

    def attend(j, carry):
        ks = pl.multiple_of(j * kb, kb)
        ok = _k_chunk(ks, kb, s_valid) <= qc
        v = v_ref[pl.ds(ks, kb), :]
        for p in range(2):
            s = _dot_nt(q_ref[:, p * 128:(p + 1) * 128], k_ref[pl.ds(ks, kb), p * 128:(p + 1) * 128])
            s = jnp.where(ok, s, NEG)
            _flash_step(s, v, m_ref.at[p], l_ref.at[p], acc_ref.at[p])
        return carry

    lax.fori_loop(0, n_blk, attend, 0)
    lam = (jnp.exp(jnp.sum(lq1_ref[...] * lk1_ref[...], axis=1, keepdims=True))
           - jnp.exp(jnp.sum(lq2_ref[...] * lk2_ref[...], axis=1, keepdims=True)) + lam_init)
    o = acc_ref[0] / l_ref[0] - lam * (acc_ref[1] / l_ref[1])
    o_ref[...] = (_rms(o, sn_ref[...], 2 * C_DIM) * (1.0 - lam_init)).astype(BF16)


def _diff_attn(q, k, v, lq1, lk1, lq2, lk2, sn, *, tq, kb, q_off, s_valid, lam_init):
    b, t, _ = q.shape
    s_pad = k.shape[1]
    vec = lambda c: pl.BlockSpec((1, c), lambda bi, h, i: (0, 0))
    return pl.pallas_call(
        functools.partial(_diff_body, tq=tq, kb=kb, q_off=q_off, s_valid=s_valid, lam_init=lam_init),
        grid=(b, C_HEADS, t // tq),
        in_specs=[pl.BlockSpec((None, tq, 256), lambda bi, h, i: (bi, i, h)),
                  pl.BlockSpec((None, s_pad, 256), lambda bi, h, i: (bi, 0, h)),
                  pl.BlockSpec((None, s_pad, 256), lambda bi, h, i: (bi, 0, h)),
                  vec(128), vec(128), vec(128), vec(128), vec(256)],
        out_specs=pl.BlockSpec((None, tq, 256), lambda bi, h, i: (bi, i, h)),
        out_shape=jax.ShapeDtypeStruct((b, t, C_HEADS * 2 * C_DIM), BF16),
        scratch_shapes=[pltpu.VMEM((2, tq, 1), F32), pltpu.VMEM((2, tq, 1), F32), pltpu.VMEM((2, tq, 256), F32)],
        compiler_params=_cparams(3),
        name="diff_attn",
    )(q, k, v, lq1, lk1, lq2, lk2, sn)


def _rope_tables(pos, reps):
    p = pos.astype(F32)[:, None]
    inv64 = jnp.power(ROPE_THETA, -jnp.arange(64, dtype=F32) / 64)
    inv32 = jnp.power(ROPE_THETA, -jnp.arange(32, dtype=F32) / 32)
    c, s = jnp.cos(p * inv64), jnp.sin(p * inv64)
    c3, s3 = jnp.cos(p * inv32), jnp.sin(p * inv32)
    z = jnp.zeros_like(s3)
    tabs = {
        "c128": jnp.concatenate([c, c], axis=1),
        "s128": jnp.concatenate([-s, s], axis=1),
        "c64": jnp.concatenate([c3, c3, c3, c3], axis=1),
        "sa64": jnp.concatenate([-s3, z, -s3, z], axis=1),
        "sb64": jnp.concatenate([z, s3, z, s3], axis=1),
    }
    return {k: jnp.tile(v, (reps, 1)) for k, v in tabs.items()}


def _pad_cols(w, n):
    return jnp.pad(w, ((0, 0), (0, n - w.shape[1])))


def _pad_lanes(g, n=128):
    g = g.reshape(1, -1)
    return jnp.pad(g, ((0, 0), (0, n - g.shape[1])))


def _prep_weights(W):
    P = {}
    for nm in ("ffn1", "ffn2"):
        P[nm] = [(W[nm + "_norm"][i].reshape(1, -1), W[nm + "_wg"][i].astype(BF16), W[nm + "_wu"][i].astype(BF16),
                  W[nm + "_wd"][i].astype(BF16)) for i in range(DEPTH)]
    P["mix_norm"] = [W["mix_norm"][i].reshape(1, -1) for i in range(DEPTH)]
    P["a"] = [dict(w=_pad_cols(W["a_w_in"][j], A_IN_PAD).astype(BF16), qn=W["a_q_norm"][j].reshape(1, -1),
                   kn=W["a_k_norm"][j].reshape(1, -1), ikn=_pad_lanes(W["a_idx_k_norm"][j]),
                   wo=W["a_w_out"][j].astype(BF16)) for j in range(W["a_w_in"].shape[0])]
    P["b"] = []
    for j in range(W["b_w_in"].shape[0]):
        wuq = W["b_w_uq"][j].reshape(Q_LORA, B_HEADS, NOPE_DIM + ROPE_DIM)
        wuq_rope = jnp.pad(wuq[:, :, NOPE_DIM:], ((0, 0), (0, 0), (0, 128 - ROPE_DIM)))
        wuq = jnp.concatenate([wuq[:, :, :NOPE_DIM].reshape(Q_LORA, -1), wuq_rope.reshape(Q_LORA, -1)], axis=1)
        P["b"].append(dict(
            w=_pad_cols(W["b_w_in"][j], B_IN_PAD).astype(BF16), qan=W["b_q_a_norm"][j].reshape(1, -1),
            kvan=W["b_kv_a_norm"][j].reshape(1, -1), krn=_pad_lanes(W["b_k_rope_norm"][j]), wuq=wuq.astype(BF16),
            qnn=W["b_q_nope_norm"][j].reshape(1, -1), qrn=_pad_lanes(W["b_q_rope_norm"][j]),
            wukv=W["b_w_ukv"][j].astype(BF16), knn=W["b_k_nope_norm"][j].reshape(1, -1),
            wo=W["b_w_out"][j].astype(BF16)))
    P["c"] = [dict(w=W["c_w_in"][j].astype(BF16), qn=W["c_q_norm"][j].reshape(1, -1),
                   kn=W["c_k_norm"][j].reshape(1, -1), lq1=W["c_lambda_q1"][j].reshape(1, -1),
                   lk1=W["c_lambda_k1"][j].reshape(1, -1), lq2=W["c_lambda_q2"][j].reshape(1, -1),
                   lk2=W["c_lambda_k2"][j].reshape(1, -1), sn=W["c_sub_norm"][j].reshape(1, -1),
                   wo=W["c_w_out"][j].astype(BF16)) for j in range(W["c_w_in"].shape[0])]
    return P


def _with_past(past, new, s_pad):
    parts = [new] if past is None else [past.astype(new.dtype), new]
    x = jnp.concatenate(parts, axis=1)
    return jnp.pad(x, ((0, 0), (0, s_pad - x.shape[1]), (0, 0)))


def _trunk(x, offset, past, P, cfg):
    b, t, _ = x.shape
    n = b * t
    tm, tq_a, tq, kb = cfg["tm"], cfg["tq_a"], cfg["tq"], cfg["kb"]
    p_len = 0 if past is None else past["a_k"].shape[2]
    s_valid = p_len + t
    s_pad = -(-s_valid // kb) * kb
    tabs = _rope_tables(offset + jnp.arange(t, dtype=I32), tm // t if tm > t else 1)
    att = dict(kb=kb, q_off=offset, s_valid=s_valid)
    rows = {k: [] for k in ("a_k", "a_v", "a_ik", "b_ckv", "b_kr", "c_k", "c_v")}
    x = x.reshape(n, D_MODEL)
    r3 = lambda a: a.reshape(b, t, a.shape[-1])
    pj = lambda nm, j: None if past is None else past[nm][j].reshape(b, p_len, -1)
    for i in range(DEPTH):
        x = _ffn(x, *P["ffn1"][i], tm)
        kind, j = i % N_MIXERS, i // N_MIXERS
        g = P["mix_norm"][i]
        if kind == 0:
            pa = P["a"][j]
            q, k32, v32, kbf, vbf, qi, ki32, kidx, wi = _a_proj(x, g, pa["w"], pa["qn"], pa["kn"], pa["ikn"], tabs, tm)
            rows["a_k"].append(k32.reshape(b, t, A_KV_HEADS, A_HEAD_DIM))
            rows["a_v"].append(v32.reshape(b, t, A_KV_HEADS, A_HEAD_DIM))
            rows["a_ik"].append(ki32.reshape(b, t, IDX_DIM))
            pik = pj("a_ik", j)
            if pik is not None:
                z = jnp.zeros_like(pik)
                pik = jnp.concatenate([pik, z, z, pik], axis=-1)
            o = _dsa_attn(r3(q), r3(qi), r3(wi), _with_past(pik, r3(kidx), s_pad),
                          _with_past(pj("a_k", j), r3(kbf), s_pad), _with_past(pj("a_v", j), r3(vbf), s_pad),
                          tq=tq_a, topk=min(TOPK_MAX, s_valid // 4), **att)
        elif kind == 1:
            pb = P["b"][j]
            q, ckv, kr32, krbf = _b_proj(x, g, pb["w"], pb["qan"], pb["kvan"], pb["krn"], pb["wuq"], pb["qnn"],
                                         pb["qrn"], tabs, tm)
            rows["b_ckv"].append(ckv.reshape(b, t, KV_LORA))
            rows["b_kr"].append(kr32.reshape(b, t, ROPE_DIM))
            ckv_all = _with_past(pj("b_ckv", j), r3(ckv), s_pad)
            pkr = pj("b_kr", j)
            if pkr is not None:
                pkr = jnp.concatenate([pkr, jnp.zeros_like(pkr)], axis=-1)
            kn, v = _kv_up(ckv_all.reshape(b * s_pad, KV_LORA), pb["wukv"], pb["knn"], min(tm, 512))
            o = _mla_attn(r3(q), kn.reshape(b, s_pad, -1), _with_past(pkr, r3(krbf), s_pad),
                          v.reshape(b, s_pad, -1), tq=tq, **att)
        else:
            pc = P["c"][j]
            q, k32, v32, kbf, vbf = _c_proj(x, g, pc["w"], pc["qn"], pc["kn"], tabs, tm)
            rows["c_k"].append(k32.reshape(b, t, C_HEADS, 2, C_DIM))
            rows["c_v"].append(v32.reshape(b, t, C_HEADS, 2 * C_DIM))
            o = _diff_attn(r3(q), _with_past(pj("c_k", j), r3(kbf), s_pad), _with_past(pj("c_v", j), r3(vbf), s_pad),
                           pc["lq1"], pc["lk1"], pc["lq2"], pc["lk2"], pc["sn"], tq=tq,
                           lam_init=0.8 - 0.6 * math.exp(-0.3 * i), **att)
        wo = (P["a"], P["b"], P["c"])[kind][j]["wo"]
        x = _out_proj(o.reshape(n, -1), wo, x, tm)
        x = _ffn(x, *P["ffn2"][i], tm)
    order = ("a_k", "a_v", "a_ik", "b_ckv", "b_kr", "c_k", "c_v")
    return x.reshape(b, t, D_MODEL), tuple(jnp.stack(rows[k]) for k in order)


PROMPT_CFG = dict(tm=512, tq_a=128, tq=256, kb=512)
SAMPLE_CFG = dict(tm=128, tq_a=16, tq=16, kb=256)


@jax.jit
def _forward(x_prompt, x_sample, past, W):
    P = _prep_weights(W)
    y_p, rows_p = _trunk(x_prompt, 0, None, P, PROMPT_CFG)
    y_s, rows_s = _trunk(x_sample, past["a_k"].shape[2], past, P, SAMPLE_CFG)
    return (y_p, y_s) + rows_p + rows_s


def kernel(x_prompt, x_sample, cache_a_k, cache_a_v, cache_a_idx_k, cache_b_ckv, cache_b_krope, cache_c_k, cache_c_v, ffn1_norm, ffn1_wg, ffn1_wu, ffn1_wd, mix_norm, ffn2_norm, ffn2_wg, ffn2_wu, ffn2_wd, a_w_in, a_q_norm, a_k_norm, a_idx_k_norm, a_w_out, b_w_in, b_q_a_norm, b_kv_a_norm, b_w_uq, b_w_ukv, b_q_nope_norm, b_q_rope_norm, b_k_nope_norm, b_k_rope_norm, b_w_out, c_w_in, c_q_norm, c_k_norm, c_lambda_q1, c_lambda_k1, c_lambda_q2, c_lambda_k2, c_sub_norm, c_w_out):
    W = dict(ffn1_norm=ffn1_norm, ffn1_wg=ffn1_wg, ffn1_wu=ffn1_wu, ffn1_wd=ffn1_wd, mix_norm=mix_norm,
             ffn2_norm=ffn2_norm, ffn2_wg=ffn2_wg, ffn2_wu=ffn2_wu, ffn2_wd=ffn2_wd,
             a_w_in=a_w_in, a_q_norm=a_q_norm, a_k_norm=a_k_norm, a_idx_k_norm=a_idx_k_norm, a_w_out=a_w_out,
             b_w_in=b_w_in, b_q_a_norm=b_q_a_norm, b_kv_a_norm=b_kv_a_norm, b_w_uq=b_w_uq, b_w_ukv=b_w_ukv,
             b_q_nope_norm=b_q_nope_norm, b_q_rope_norm=b_q_rope_norm, b_k_nope_norm=b_k_nope_norm,
             b_k_rope_norm=b_k_rope_norm, b_w_out=b_w_out,
             c_w_in=c_w_in, c_q_norm=c_q_norm, c_k_norm=c_k_norm, c_lambda_q1=c_lambda_q1,
             c_lambda_k1=c_lambda_k1, c_lambda_q2=c_lambda_q2, c_lambda_k2=c_lambda_k2, c_sub_norm=c_sub_norm,
             c_w_out=c_w_out)
    past = dict(a_k=cache_a_k, a_v=cache_a_v, a_ik=cache_a_idx_k, b_ckv=cache_b_ckv, b_kr=cache_b_krope,
                c_k=cache_c_k, c_v=cache_c_v)
    return _forward(x_prompt, x_sample, past, W)
```

```python
import functools
import math

import jax
import jax.numpy as jnp
from jax import lax
from jax.experimental import pallas as pl
from jax.experimental.pallas import tpu as pltpu

F32 = jnp.float32
BF16 = jnp.bfloat16
I32 = jnp.int32

D_MODEL = 1024
DEPTH = 4
CHUNK_SHIFT = 6
N_MIXERS = 3
ROPE_THETA = 10000.0
EPS = 1e-6
D_FF = 2816

A_HEADS = 8
A_KV_HEADS = 2
A_GROUP = A_HEADS // A_KV_HEADS
A_HEAD_DIM = 128
IDX_HEADS = 8
IDX_DIM = 64
TOPK_MAX = 256
LOG2E = math.log2(math.e)
A_SCALE = A_HEAD_DIM ** -0.5 * LOG2E
IDX_W_SCALE = (IDX_HEADS * IDX_DIM) ** -0.5
A_IN = 2120
A_IN_PAD = 2176

B_HEADS = 8
Q_LORA = 384
KV_LORA = 256
NOPE_DIM = 128
ROPE_DIM = 64
V_DIM = 128
B_SCALE = (NOPE_DIM + ROPE_DIM) ** -0.5 * LOG2E
B_IN = 704
B_IN_PAD = 768

C_HEADS = 4
C_DIM = 128
C_SCALE = C_DIM ** -0.5 * LOG2E

LANES = 128
NEG = -1e30
INT_MIN = -(2 ** 31)
VMEM_LIMIT = 56 * 1024 * 1024


def _cparams(n_axes):
    return pltpu.CompilerParams(dimension_semantics=("arbitrary",) * n_axes, vmem_limit_bytes=VMEM_LIMIT)


def _dot(a, b):
    return jnp.dot(a, b, preferred_element_type=F32)


def _dot_nt(a, b):
    return lax.dot_general(a, b, (((1,), (1,)), ((), ())), preferred_element_type=F32)


def _rms(x, g, n):
    ms = jnp.sum(x * x, axis=-1, keepdims=True) * (1.0 / n)
    return x * lax.rsqrt(ms + EPS) * g


def _rope128(x, c, s):
    return x * c + pltpu.roll(x, 64, 1) * s


def _rope64(x, c, sa, sb):
    return x * c + pltpu.roll(x, 96, 1) * sa + pltpu.roll(x, 32, 1) * sb


def _ffn_body(x_ref, g_ref, wg_ref, wu_ref, wd_ref, o_ref, *, fc):
    x = x_ref[...]
    h = _rms(x, g_ref[...], D_MODEL).astype(BF16)
    y = None
    for c in range(D_FF // fc):
        a = _dot(h, wg_ref[:, c * fc:(c + 1) * fc])
        u = _dot(h, wu_ref[:, c * fc:(c + 1) * fc])
        act = (a * jax.nn.sigmoid(a) * u).astype(BF16)
        part = _dot(act, wd_ref[c * fc:(c + 1) * fc, :])
        y = part if y is None else y + part
    o_ref[...] = x + 0.5 * y


def _const_spec(shape):
    nd = len(shape)
    return pl.BlockSpec(shape, lambda *_: (0,) * nd, pipeline_mode=pl.Buffered(1))


def _row_spec(tm, n):
    return pl.BlockSpec((tm, n), lambda i: (i, 0))


def _ffn(x, g, wg, wu, wd, tm):
    n = x.shape[0]
    return pl.pallas_call(
        functools.partial(_ffn_body, fc=D_FF // 2),
        grid=(n // tm,),
        in_specs=[_row_spec(tm, D_MODEL), _const_spec((1, D_MODEL)), _const_spec((D_MODEL, D_FF)),
                  _const_spec((D_MODEL, D_FF)), _const_spec((D_FF, D_MODEL))],
        out_specs=_row_spec(tm, D_MODEL),
        out_shape=jax.ShapeDtypeStruct((n, D_MODEL), F32),
        compiler_params=_cparams(1),
        name="ffn",
    )(x, g, wg, wu, wd)


def _out_proj_body(o_ref, w_ref, x_ref, y_ref):
    y_ref[...] = x_ref[...] + _dot(o_ref[...], w_ref[...])


def _out_proj(o, w, x, tm):
    n, k = o.shape
    return pl.pallas_call(
        _out_proj_body,
        grid=(n // tm,),
        in_specs=[_row_spec(tm, k), _const_spec((k, D_MODEL)), _row_spec(tm, D_MODEL)],
        out_specs=_row_spec(tm, D_MODEL),
        out_shape=jax.ShapeDtypeStruct((n, D_MODEL), F32),
        compiler_params=_cparams(1),
        name="out_proj",
    )(o, w, x)


def _a_proj_body(x_ref, g_ref, w_ref, qn_ref, kn_ref, ikn_ref, c128_ref, s128_ref, c64_ref, sa64_ref, sb64_ref,
                 q_ref, k32_ref, v32_ref, kbf_ref, vbf_ref, qi_ref, ki32_ref, kidx_ref, wi_ref):
    h = _rms(x_ref[...], g_ref[...], D_MODEL).astype(BF16)
    y = _dot(h, w_ref[...])
    c128, s128 = c128_ref[...], s128_ref[...]
    c64, sa64, sb64 = c64_ref[...], sa64_ref[...], sb64_ref[...]
    for hd in range(A_HEADS):
        sl = slice(hd * 128, (hd + 1) * 128)
        qh = _rope128(_rms(y[:, sl], qn_ref[...], A_HEAD_DIM), c128, s128)
        q_ref[:, sl] = (qh * A_SCALE).astype(BF16)
    for hd in range(A_KV_HEADS):
        sl = slice(hd * 128, (hd + 1) * 128)
        kh = _rope128(_rms(y[:, 1024 + hd * 128:1024 + (hd + 1) * 128], kn_ref[...], A_HEAD_DIM), c128, s128)
        k32_ref[:, sl] = kh
        kbf_ref[:, sl] = kh.astype(BF16)
    v = y[:, 1280:1536]
    v32_ref[...] = v
    vbf_ref[...] = v.astype(BF16)
    for p in range(IDX_HEADS // 2):
        sl = slice(p * 128, (p + 1) * 128)
        qi_ref[:, sl] = _rope64(y[:, 1536 + p * 128:1536 + (p + 1) * 128], c64, sa64, sb64).astype(BF16)
    tail = y[:, 2048:2176]
    lane = lax.broadcasted_iota(I32, tail.shape, 1)
    low = lane < IDX_DIM
    kin = jnp.where(low, tail, 0.0)
    ki = _rope64(_rms(kin, ikn_ref[...], IDX_DIM), c64, sa64, sb64)
    ki = jnp.where(low, ki, 0.0)
    ki32_ref[...] = ki[:, :IDX_DIM]
    kidx_ref[:, 0:128] = ki.astype(BF16)
    kidx_ref[:, 128:256] = pltpu.roll(ki, 64, 1).astype(BF16)
    wi_ref[...] = pltpu.roll(tail, 64, 1) * IDX_W_SCALE


def _tab_spec(tm, t):
    nt = t // tm
    return pl.BlockSpec((tm, LANES), lambda i: (i % nt, 0))


def _a_proj(x, g, w, qn, kn, ikn, tabs, tm):
    n = x.shape[0]
    t = tabs["c128"].shape[0]
    outs = [(1024, BF16), (256, F32), (256, F32), (256, BF16), (256, BF16), (512, BF16), (IDX_DIM, F32),
            (256, BF16), (LANES, F32)]
    return pl.pallas_call(
        _a_proj_body,
        grid=(n // tm,),
        in_specs=[_row_spec(tm, D_MODEL), _const_spec((1, D_MODEL)), _const_spec((D_MODEL, A_IN_PAD)),
                  _const_spec((1, 128)), _const_spec((1, 128)), _const_spec((1, 128))]
        + [_tab_spec(tm, t)] * 5,
        out_specs=[_row_spec(tm, c) for c, _ in outs],
        out_shape=[jax.ShapeDtypeStruct((n, c), dt) for c, dt in outs],
        compiler_params=_cparams(1),
        name="a_proj",
    )(x, g, w, qn, kn, ikn, tabs["c128"], tabs["s128"], tabs["c64"], tabs["sa64"], tabs["sb64"])


def _b_proj_body(x_ref, g_ref, w_ref, qan_ref, kvan_ref, krn_ref, wuq_ref, qnn_ref, qrn_ref,
                 c64_ref, sa64_ref, sb64_ref, q_ref, ckv_ref, kr32_ref, krbf_ref):
    h = _rms(x_ref[...], g_ref[...], D_MODEL).astype(BF16)
    y = _dot(h, w_ref[...])
    c64, sa64, sb64 = c64_ref[...], sa64_ref[...], sb64_ref[...]
    ckv_ref[...] = _rms(y[:, 384:640], kvan_ref[...], KV_LORA)
    kr = _rope64(_rms(y[:, 640:768], krn_ref[...], ROPE_DIM), c64, sa64, sb64)
    kr32_ref[...] = kr[:, :ROPE_DIM]
    krbf_ref[...] = kr.astype(BF16)
    cq = _rms(y[:, 0:384], qan_ref[...], Q_LORA).astype(BF16)
    qq = _dot(cq, wuq_ref[...])
    for hd in range(B_HEADS):
        qn = _rms(qq[:, hd * 128:(hd + 1) * 128], qnn_ref[...], NOPE_DIM)
        qr = _rms(qq[:, 1024 + hd * 128:1024 + (hd + 1) * 128], qrn_ref[...], ROPE_DIM)
        qr = _rope64(qr, c64, sa64, sb64)
        q_ref[:, hd * 256:hd * 256 + 128] = (qn * B_SCALE).astype(BF16)
        q_ref[:, hd * 256 + 128:(hd + 1) * 256] = (qr * B_SCALE).astype(BF16)


def _b_proj(x, g, w, qan, kvan, krn, wuq, qnn, qrn, tabs, tm):
    n = x.shape[0]
    t = tabs["c64"].shape[0]
    outs = [(B_HEADS * 256, BF16), (KV_LORA, F32), (ROPE_DIM, F32), (LANES, BF16)]
    return pl.pallas_call(
        _b_proj_body,
        grid=(n // tm,),
        in_specs=[_row_spec(tm, D_MODEL), _const_spec((1, D_MODEL)), _const_spec((D_MODEL, B_IN_PAD)),
                  _const_spec((1, Q_LORA)), _const_spec((1, KV_LORA)), _const_spec((1, 128)),
                  _const_spec((Q_LORA, 2048)), _const_spec((1, 128)), _const_spec((1, 128))]
        + [_tab_spec(tm, t)] * 3,
        out_specs=[_row_spec(tm, c) for c, _ in outs],
        out_shape=[jax.ShapeDtypeStruct((n, c), dt) for c, dt in outs],
        compiler_params=_cparams(1),
        name="b_proj",
    )(x, g, w, qan, kvan, krn, wuq, qnn, qrn, tabs["c64"], tabs["sa64"], tabs["sb64"])


def _kv_up_body(ckv_ref, w_ref, knn_ref, kn_ref, v_ref):
    y = _dot(ckv_ref[...].astype(BF16), w_ref[...])
    for hd in range(B_HEADS):
        kn_ref[:, hd * 128:(hd + 1) * 128] = _rms(y[:, hd * 256:hd * 256 + 128], knn_ref[...], NOPE_DIM).astype(BF16)
        v_ref[:, hd * 128:(hd + 1) * 128] = y[:, hd * 256 + 128:(hd + 1) * 256].astype(BF16)


def _kv_up(ckv, w, knn, tm):
    n = ckv.shape[0]
    return pl.pallas_call(
        _kv_up_body,
        grid=(n // tm,),
        in_specs=[_row_spec(tm, KV_LORA), _const_spec((KV_LORA, 2048)), _const_spec((1, 128))],
        out_specs=[_row_spec(tm, 1024), _row_spec(tm, 1024)],
        out_shape=[jax.ShapeDtypeStruct((n, 1024), BF16)] * 2,
        compiler_params=_cparams(1),
        name="kv_up",
    )(ckv, w, knn)


def _c_proj_body(x_ref, g_ref, w_ref, qn_ref, kn_ref, c128_ref, s128_ref,
                 q_ref, k32_ref, v32_ref, kbf_ref, vbf_ref):
    h = _rms(x_ref[...], g_ref[...], D_MODEL).astype(BF16)
    y = _dot(h, w_ref[...])
    c128, s128 = c128_ref[...], s128_ref[...]
    for hd in range(2 * C_HEADS):
        sl = slice(hd * 128, (hd + 1) * 128)
        qh = _rope128(_rms(y[:, sl], qn_ref[...], C_DIM), c128, s128)
        q_ref[:, sl] = (qh * C_SCALE).astype(BF16)
        kh = _rope128(_rms(y[:, 1024 + hd * 128:1024 + (hd + 1) * 128], kn_ref[...], C_DIM), c128, s128)
        k32_ref[:, sl] = kh
        kbf_ref[:, sl] = kh.astype(BF16)
    v = y[:, 2048:3072]
    v32_ref[...] = v
    vbf_ref[...] = v.astype(BF16)


def _c_proj(x, g, w, qn, kn, tabs, tm):
    n = x.shape[0]
    t = tabs["c128"].shape[0]
    outs = [(1024, BF16), (1024, F32), (1024, F32), (1024, BF16), (1024, BF16)]
    return pl.pallas_call(
        _c_proj_body,
        grid=(n // tm,),
        in_specs=[_row_spec(tm, D_MODEL), _const_spec((1, D_MODEL)), _const_spec((D_MODEL, 3072)),
                  _const_spec((1, 128)), _const_spec((1, 128))] + [_tab_spec(tm, t)] * 2,
        out_specs=[_row_spec(tm, c) for c, _ in outs],
        out_shape=[jax.ShapeDtypeStruct((n, c), dt) for c, dt in outs],
        compiler_params=_cparams(1),
        name="c_proj",
    )(x, g, w, qn, kn, tabs["c128"], tabs["s128"])


def _n_key_blocks(i, tq, kb, q_off, s_valid):
    last_chunk = lax.shift_right_logical(q_off + (i + 1) * tq - 1, CHUNK_SHIFT)
    kend = jnp.minimum((last_chunk + 1) << CHUNK_SHIFT, s_valid)
    return lax.shift_right_logical(kend + kb - 1, int(math.log2(kb)))


def _q_chunk(i, tq, q_off):
    row = lax.broadcasted_iota(I32, (tq, 1), 0)
    return lax.shift_right_logical(q_off + i * tq + row, CHUNK_SHIFT)


def _k_chunk(ks, kb, s_valid):
    kpos = ks + lax.broadcasted_iota(I32, (1, kb), 1)
    return jnp.where(kpos < s_valid, lax.shift_right_logical(kpos, CHUNK_SHIFT), 2 ** 30)


def _flash_step(s, v, m_ref, l_ref, acc_ref):
    m_prev = m_ref[...]
    m_new = jnp.maximum(m_prev, jnp.max(s, axis=1, keepdims=True))
    alpha = jnp.exp2(m_prev - m_new)
    p = jnp.exp2(s - m_new)
    l_ref[...] = alpha * l_ref[...] + jnp.sum(p, axis=1, keepdims=True)
    acc_ref[...] = alpha * acc_ref[...] + _dot(p.astype(BF16), v)
    m_ref[...] = m_new


def _key_limit(i, tq, q_off, s_valid):
    qpos = q_off + i * tq + lax.broadcasted_iota(I32, (1, tq), 1)
    return jnp.minimum((lax.shift_right_logical(qpos, CHUNK_SHIFT) + 1) << CHUNK_SHIFT, s_valid)


def _n_full_blocks(i, tq, kb, q_off, s_valid):
    first_chunk = lax.shift_right_logical(q_off + i * tq, CHUNK_SHIFT)
    kend = jnp.minimum((first_chunk + 1) << CHUNK_SHIFT, s_valid)
    return lax.shift_right_logical(kend, int(math.log2(kb)))


def _flash_init(m_ref, l_ref, acc_ref):
    m_ref[...] = jnp.full(m_ref.shape, NEG, F32)
    l_ref[...] = jnp.zeros(l_ref.shape, F32)
    acc_ref[...] = jnp.zeros(acc_ref.shape, F32)


def _dsa_body(q_ref, qi_ref, wi_ref, kidx_ref, k_ref, v_ref, o_ref, keys_ref, m_ref, l_ref, acc_ref,
              *, tq, kb, q_off, s_valid, topk):
    i = pl.program_id(1)
    n_blk = _n_key_blocks(i, tq, kb, q_off, s_valid)
    qc = _q_chunk(i, tq, q_off)
    wi = wi_ref[...]
    wcol = [wi[:, hd:hd + 1] for hd in range(IDX_HEADS)]

    def score_block(j, carry):
        ks = pl.multiple_of(j * kb, kb)
        k_lo = kidx_ref[pl.ds(ks, kb), 0:128]
        k_hi = kidx_ref[pl.ds(ks, kb), 128:256]
        sc = jnp.zeros((tq, kb), F32)
        for p in range(IDX_HEADS // 2):
            qp = qi_ref[:, p * 128:(p + 1) * 128]
            sc = sc + wcol[2 * p] * jnp.maximum(_dot_nt(qp, k_lo), 0.0)
            sc = sc + wcol[2 * p + 1] * jnp.maximum(_dot_nt(qp, k_hi), 0.0)
        bits = lax.bitcast_convert_type(sc, I32)
        key = bits ^ ((bits >> 31) & 0x7FFFFFFF)
        keys_ref[j] = jnp.where(_k_chunk(ks, kb, s_valid) <= qc, key, INT_MIN)
        return carry

    lax.fori_loop(0, n_blk, score_block, 0)

    def count(pred):
        def blk(j, c):
            x = jnp.where(pred(keys_ref[j], j * kb), 1.0, 0.0)
            part = x[:, 0:LANES]
            for g in range(1, kb // LANES):
                part = part + x[:, g * LANES:(g + 1) * LANES]
            return c + part
        c = lax.fori_loop(0, n_blk, blk, jnp.zeros((tq, LANES), F32))
        return jnp.sum(c, axis=1, keepdims=True)

    kf = float(topk)

    def bit_step(b, t):
        cand = t + jnp.left_shift(jnp.int32(1), 31 - b)
        cnt = count(lambda kblk, _: kblk >= cand)
        return jnp.where(cnt >= kf, cand, t)

    thr = lax.fori_loop(0, 32, bit_step, jnp.full((tq, 1), INT_MIN, I32))
    need = kf - count(lambda kblk, _: kblk > thr)
    n_eq = count(lambda kblk, _: kblk == thr)
    partial = jnp.logical_and(n_eq > need, thr != INT_MIN)

    @pl.when(jnp.max(jnp.where(partial, 1.0, 0.0)) > 0.0)
    def _():
        lane = lax.broadcasted_iota(I32, (1, kb), 1)

        def idx_step(b, x):
            cand = x + jnp.left_shift(jnp.int32(1), b)
            cnt = count(lambda kblk, k0: jnp.logical_and(kblk == thr, k0 + lane < cand))
            return jnp.where(cnt < need, cand, x)

        nbits = max(1, int(s_valid - 1).bit_length())
        cut = lax.fori_loop(0, nbits, lambda b, x: idx_step(nbits - 1 - b, x), jnp.zeros((tq, 1), I32))

        def demote(j, carry):
            kblk = keys_ref[j]
            drop = jnp.logical_and(jnp.logical_and(kblk == thr, j * kb + lane > cut), partial)
            keys_ref[j] = jnp.where(drop, kblk - 1, kblk)
            return carry

        lax.fori_loop(0, n_blk, demote, 0)

    thr_sel = jnp.maximum(thr, INT_MIN + 1)

    _flash_init(m_ref, l_ref, acc_ref)

    def attend(j, carry):
        ks = pl.multiple_of(j * kb, kb)
        bias = jnp.where(keys_ref[j] >= thr_sel, 0.0, NEG)
        for g in range(A_KV_HEADS):
            qg = jnp.concatenate([q_ref[:, (g * A_GROUP + a) * 128:(g * A_GROUP + a + 1) * 128]
                                  for a in range(A_GROUP)], axis=0)
            s = _dot_nt(qg, k_ref[pl.ds(ks, kb), g * 128:(g + 1) * 128])
            s = (s.reshape(A_GROUP, tq, kb) + bias[None]).reshape(A_GROUP * tq, kb)
            _flash_step(s, v_ref[pl.ds(ks, kb), g * 128:(g + 1) * 128], m_ref.at[g], l_ref.at[g], acc_ref.at[g])
        return carry

    lax.fori_loop(0, n_blk, attend, 0)
    for g in range(A_KV_HEADS):
        o = acc_ref[g] / l_ref[g]
        for a in range(A_GROUP):
            hd = g * A_GROUP + a
            o_ref[:, hd * 128:(hd + 1) * 128] = o[a * tq:(a + 1) * tq].astype(BF16)


def _dsa_attn(q, qi, wi, kidx, k, v, *, tq, kb, q_off, s_valid, topk):
    b, t, _ = q.shape
    s_pad = k.shape[1]
    qspec = lambda c: pl.BlockSpec((None, tq, c), lambda bi, i: (bi, i, 0))
    kspec = lambda c: pl.BlockSpec((None, s_pad, c), lambda bi, i: (bi, 0, 0))
    rows = A_GROUP * tq
    return pl.pallas_call(
        functools.partial(_dsa_body, tq=tq, kb=kb, q_off=q_off, s_valid=s_valid, topk=topk),
        grid=(b, t // tq),
        in_specs=[qspec(1024), qspec(512), qspec(LANES), kspec(256), kspec(256), kspec(256)],
        out_specs=qspec(1024),
        out_shape=jax.ShapeDtypeStruct((b, t, 1024), BF16),
        scratch_shapes=[pltpu.VMEM((s_pad // kb, tq, kb), I32), pltpu.VMEM((A_KV_HEADS, rows, 1), F32),
                        pltpu.VMEM((A_KV_HEADS, rows, 1), F32), pltpu.VMEM((A_KV_HEADS, rows, 128), F32)],
        compiler_params=_cparams(2),
        name="dsa_attn",
    )(q, qi, wi, kidx, k, v)


def _mla_body(q_ref, kn_ref, kr_ref, v_ref, o_ref, m_ref, l_ref, acc_ref, *, tq, kb, q_off, s_valid):
    i = pl.program_id(2)
    n_blk = _n_key_blocks(i, tq, kb, q_off, s_valid)
    qc = _q_chunk(i, tq, q_off)
    _flash_init(m_ref, l_ref, acc_ref)

    def attend(j, carry):
        ks = pl.multiple_of(j * kb, kb)
        kcat = jnp.concatenate([kn_ref[pl.ds(ks, kb), :], kr_ref[pl.ds(ks, kb), :]], axis=1)
        s = _dot_nt(q_ref[...], kcat)
        s = jnp.where(_k_chunk(ks, kb, s_valid) <= qc, s, NEG)
        _flash_step(s, v_ref[pl.ds(ks, kb), :], m_ref, l_ref, acc_ref)
        return carry

    lax.fori_loop(0, n_blk, attend, 0)
    o_ref[...] = (acc_ref[...] / l_ref[...]).astype(BF16)


def _mla_attn(q, kn, kr, v, *, tq, kb, q_off, s_valid):
    b, t, _ = q.shape
    s_pad = kn.shape[1]
    return pl.pallas_call(
        functools.partial(_mla_body, tq=tq, kb=kb, q_off=q_off, s_valid=s_valid),
        grid=(b, B_HEADS, t // tq),
        in_specs=[pl.BlockSpec((None, tq, 256), lambda bi, h, i: (bi, i, h)),
                  pl.BlockSpec((None, s_pad, 128), lambda bi, h, i: (bi, 0, h)),
                  pl.BlockSpec((None, s_pad, 128), lambda bi, h, i: (bi, 0, 0)),
                  pl.BlockSpec((None, s_pad, 128), lambda bi, h, i: (bi, 0, h))],
        out_specs=pl.BlockSpec((None, tq, 128), lambda bi, h, i: (bi, i, h)),
        out_shape=jax.ShapeDtypeStruct((b, t, B_HEADS * V_DIM), BF16),
        scratch_shapes=[pltpu.VMEM((tq, 1), F32), pltpu.VMEM((tq, 1), F32), pltpu.VMEM((tq, 128), F32)],
        compiler_params=_cparams(3),
        name="mla_attn",
    )(q, kn, kr, v)


def _diff_body(q_ref, k_ref, v_ref, lq1_ref, lk1_ref, lq2_ref, lk2_ref, sn_ref, o_ref, m_ref, l_ref, acc_ref,
               *, tq, kb, q_off, s_valid, lam_init):
    i = pl.program_id(2)
    n_blk = _n_key_blocks(i, tq, kb, q_off, s_valid)
    qc = _q_chunk(i, tq, q_off)
    _flash_init(m_ref, l_ref, acc_ref)

    def attend(j, carry):
        ks = pl.multiple_of(j * kb, kb)
        ok = _k_chunk(ks, kb, s_valid) <= qc
        v = v_ref[pl.ds(ks, kb), :]
        for p in range(2):
            s = _dot_nt(q_ref[:, p * 128:(p + 1) * 128], k_ref[pl.ds(ks, kb), p * 128:(p + 1) * 128])
            s = jnp.where(ok, s, NEG)
            _flash_step(s, v, m_ref.at[p], l_ref.at[p], acc_ref.at[p])
        return carry

    lax.fori_loop(0, n_blk, attend, 0)
    lam = (jnp.exp(jnp.sum(lq1_ref[...] * lk1_ref[...], axis=1, keepdims=True))
           - jnp.exp(jnp.sum(lq2_ref[...] * lk2_ref[...], axis=1, keepdims=True)) + lam_init)
    o = acc_ref[0] / l_ref[0] - lam * (acc_ref[1] / l_ref[1])
    o_ref[...] = (_rms(o, sn_ref[...], 2 * C_DIM) * (1.0 - lam_init)).astype(BF16)


def _diff_attn(q, k, v, lq1, lk1, lq2, lk2, sn, *, tq, kb, q_off, s_valid, lam_init):
    b, t, _ = q.shape
    s_pad = k.shape[1]
    vec = lambda c: pl.BlockSpec((1, c), lambda bi, h, i: (0, 0))
    return pl.pallas_call(
        functools.partial(_diff_body, tq=tq, kb=kb, q_off=q_off, s_valid=s_valid, lam_init=lam_init),
        grid=(b, C_HEADS, t // tq),
        in_specs=[pl.BlockSpec((None, tq, 256), lambda bi, h, i: (bi, i, h)),
                  pl.BlockSpec((None, s_pad, 256), lambda bi, h, i: (bi, 0, h)),
                  pl.BlockSpec((None, s_pad, 256), lambda bi, h, i: (bi, 0, h)),
                  vec(128), vec(128), vec(128), vec(128), vec(256)],
        out_specs=pl.BlockSpec((None, tq, 256), lambda bi, h, i: (bi, i, h)),
        out_shape=jax.ShapeDtypeStruct((b, t, C_HEADS * 2 * C_DIM), BF16),
        scratch_shapes=[pltpu.VMEM((2, tq, 1), F32), pltpu.VMEM((2, tq, 1), F32), pltpu.VMEM((2, tq, 256), F32)],
        compiler_params=_cparams(3),
        name="diff_attn",
    )(q, k, v, lq1, lk1, lq2, lk2, sn)


SEARCH_BLOCKS = 4


def _stage_bufs(n_chain, kb, r):
    return [pltpu.VMEM((n_chain, 2, kb, r), F32), pltpu.VMEM((n_chain, 2, kb, r), BF16),
            pltpu.VMEM((n_chain, 2, 1, r), F32)]


def _staged_flash_t(bufs, state, lo, hi, n_kb, qk, prep, vt_of, first=True, last=True):
    s_ref, p_ref, a_ref = bufs
    m_ref, l_ref, acc_ref = state
    n_chain = s_ref.shape[0]
    clamp = lambda j: jnp.clip(j, 0, n_kb - 1)

    def softmax(c, j, slot):
        st = prep(c, j, s_ref[c, slot])
        m_prev = m_ref[c]
        m_new = jnp.maximum(m_prev, jnp.max(st, axis=0, keepdims=True))
        alpha = jnp.exp2(m_prev - m_new)
        p = jnp.exp2(st - m_new)
        l_ref[c] = alpha * l_ref[c] + jnp.sum(p, axis=0, keepdims=True)
        m_ref[c] = m_new
        p_ref[c, slot] = p.astype(BF16)
        a_ref[c, slot] = alpha

    def values(c, j, slot):
        acc_ref[c] = a_ref[c, slot] * acc_ref[c] + _dot(vt_of(c, clamp(j)), p_ref[c, slot])

    if first:
        for c in range(n_chain):
            s_ref[c, 0] = qk(c, clamp(lo))
            p_ref[c, 1] = jnp.zeros(p_ref.shape[2:], BF16)
            a_ref[c, 1] = jnp.ones(a_ref.shape[2:], F32)

    def turn(t, carry):
        j0 = lo + 2 * t
        for c in range(n_chain):
            s_ref[c, 1] = qk(c, clamp(j0 + 1))
        for c in range(n_chain):
            softmax(c, j0, 0)
        for c in range(n_chain):
            values(c, j0 - 1, 1)
        for c in range(n_chain):
            s_ref[c, 0] = qk(c, clamp(j0 + 2))
        for c in range(n_chain):
            softmax(c, j0 + 1, 1)
        for c in range(n_chain):
            values(c, j0, 0)
        return carry

    n_turn = lax.shift_right_logical(hi - lo + 1, 1)
    lax.fori_loop(0, n_turn, turn, 0)
    if last:
        for c in range(n_chain):
            values(c, lo + 2 * n_turn - 1, 1)


def _dsa_t_body(qt_ref, qit_ref, wit_ref, kidx_ref, k_ref, vt_ref, o_ref, keys_ref, s_ref, p_ref, a_ref,
                m_ref, l_ref, acc_ref, *, tq, kb, q_off, s_valid, topk):
    i = pl.program_id(1)
    n_blk = _n_key_blocks(i, tq, kb, q_off, s_valid)
    n_kb = keys_ref.shape[0]
    n_sb = lax.shift_right_logical(n_blk + SEARCH_BLOCKS - 1, int(math.log2(SEARCH_BLOCKS)))
    limit = _key_limit(i, tq, q_off, s_valid)
    kidx0 = lax.broadcasted_iota(I32, (kb, tq), 0)
    wit = wit_ref[...]
    wrow = [wit[hd:hd + 1, :] for hd in range(IDX_HEADS)]

    def score_block(j, carry):
        ks = pl.multiple_of(j * kb, kb)
        q_all = jnp.concatenate([qit_ref[p * 128:(p + 1) * 128, :] for p in range(IDX_HEADS // 2)], axis=1)
        s_lo = _dot(kidx_ref[pl.ds(ks, kb), 0:128], q_all)
        s_hi = _dot(kidx_ref[pl.ds(ks, kb), 128:256], q_all)
        sc = jnp.zeros((kb, tq), F32)
        for p in range(IDX_HEADS // 2):
            sc = sc + wrow[2 * p] * jnp.maximum(s_lo[:, p * tq:(p + 1) * tq], 0.0)
            sc = sc + wrow[2 * p + 1] * jnp.maximum(s_hi[:, p * tq:(p + 1) * tq], 0.0)
        bits = lax.bitcast_convert_type(sc, I32)
        key = bits ^ ((bits >> 31) & 0x7FFFFFFF)
        keys_ref[j] = jnp.where(kidx0 < limit - ks, key, INT_MIN)
        return carry

    def pad_block(j, carry):
        keys_ref[j] = jnp.full((kb, tq), INT_MIN, I32)
        return carry

    lax.fori_loop(0, n_blk, score_block, 0)
    lax.fori_loop(n_blk, n_sb * SEARCH_BLOCKS, pad_block, 0)

    def count(pred):
        def group(jj, cs):
            out = []
            for u in range(SEARCH_BLOCKS):
                j = jj * SEARCH_BLOCKS + u
                x = jnp.where(pred(keys_ref[j], j * kb), 1.0, 0.0)
                out.append(cs[u] + jnp.sum(x.reshape(kb // 8, 8, tq), axis=0))
            return tuple(out)
        cs = lax.fori_loop(0, n_sb, group, tuple(jnp.zeros((8, tq), F32) for _ in range(SEARCH_BLOCKS)))
        return jnp.sum(functools.reduce(lambda a, b: a + b, cs), axis=0, keepdims=True)

    kf = float(topk)

    def bit_step(b, t):
        cand = t + jnp.left_shift(jnp.int32(1), 31 - b)
        cnt = count(lambda kblk, _: kblk >= cand)
        return jnp.where(cnt >= kf, cand, t)

    thr = lax.fori_loop(0, 32, bit_step, jnp.full((1, tq), INT_MIN, I32))
    need = kf - count(lambda kblk, _: kblk > thr)
    n_eq = count(lambda kblk, _: kblk == thr)
    partial = jnp.logical_and(n_eq > need, thr != INT_MIN)

    @pl.when(jnp.max(jnp.where(partial, 1.0, 0.0)) > 0.0)
    def _():
        def idx_step(b, x):
            cand = x + jnp.left_shift(jnp.int32(1), b)
            cnt = count(lambda kblk, k0: jnp.logical_and(kblk == thr, kidx0 < cand - k0))
            return jnp.where(cnt < need, cand, x)

        nbits = max(1, int(s_valid - 1).bit_length())
        cut = lax.fori_loop(0, nbits, lambda b, x: idx_step(nbits - 1 - b, x), jnp.zeros((1, tq), I32))

        def demote(j, carry):
            kblk = keys_ref[j]
            drop = jnp.logical_and(jnp.logical_and(kblk == thr, kidx0 > cut - j * kb), partial)
            keys_ref[j] = jnp.where(drop, kblk - 1, kblk)
            return carry

        lax.fori_loop(0, n_blk, demote, 0)

    thr_sel = jnp.maximum(thr, INT_MIN + 1)

    _flash_init(m_ref, l_ref, acc_ref)
    n_pair = A_HEADS // 2

    def to_bias(j, carry):
        keys_ref[j] = lax.bitcast_convert_type(jnp.where(keys_ref[j] >= thr_sel, 0.0, NEG), I32)
        return carry

    lax.fori_loop(0, n_sb * SEARCH_BLOCKS, to_bias, 0)
    for g in range(A_KV_HEADS):
        def qk(c, j, g=g):
            p = g * (A_GROUP // 2) + c
            qp = qt_ref[2 * p * 128:(2 * p + 2) * 128, :]
            qp = jnp.concatenate([qp[0:128, :], qp[128:256, :]], axis=1)
            return _dot(k_ref[pl.ds(pl.multiple_of(j * kb, kb), kb), g * 128:(g + 1) * 128], qp)

        def prep(c, j, st):
            bias = lax.bitcast_convert_type(keys_ref[j], F32)
            return st + jnp.concatenate([bias, bias], axis=1)

        def vt_of(c, j, g=g):
            return vt_ref[j, g * 128:(g + 1) * 128, :]

        chains = pl.ds(g * (A_GROUP // 2), A_GROUP // 2)
        _staged_flash_t((s_ref.at[chains], p_ref.at[chains], a_ref.at[chains]),
                        (m_ref.at[chains], l_ref.at[chains], acc_ref.at[chains]), 0, n_blk, n_kb, qk, prep, vt_of)
    for p in range(n_pair):
        ot = acc_ref[p] / l_ref[p]
        for a in range(2):
            hd = 2 * p + a
            o_ref[:, hd * 128:(hd + 1) * 128] = ot[:, a * tq:(a + 1) * tq].T.astype(BF16)


def _dsa_attn_t(qt, qit, wit, kidx, k, vt, *, tq, kb, q_off, s_valid, topk):
    b, _, t = qt.shape
    s_pad = k.shape[1]
    n_kb = s_pad // kb
    qtspec = lambda c: pl.BlockSpec((None, c, tq), lambda bi, i: (bi, 0, i))
    kspec = lambda c: pl.BlockSpec((None, s_pad, c), lambda bi, i: (bi, 0, 0))
    n_pair = A_HEADS // 2
    return pl.pallas_call(
        functools.partial(_dsa_t_body, tq=tq, kb=kb, q_off=q_off, s_valid=s_valid, topk=topk),
        grid=(b, t // tq),
        in_specs=[qtspec(1024), qtspec(512), qtspec(IDX_HEADS),
                  kspec(256), kspec(256), pl.BlockSpec((None, n_kb, 256, kb), lambda bi, i: (bi, 0, 0, 0))],
        out_specs=pl.BlockSpec((None, tq, 1024), lambda bi, i: (bi, i, 0)),
        out_shape=jax.ShapeDtypeStruct((b, t, 1024), BF16),
        scratch_shapes=[pltpu.VMEM((n_kb, kb, tq), I32)] + _stage_bufs(n_pair, kb, 2 * tq)
        + [pltpu.VMEM((n_pair, 1, 2 * tq), F32), pltpu.VMEM((n_pair, 1, 2 * tq), F32),
           pltpu.VMEM((n_pair, 128, 2 * tq), F32)],
        compiler_params=_cparams(2),
        name="dsa_attn_t",
    )(qt, qit, wit, kidx, k, vt)


def _causal_flash_t(bufs, state, i, n_kb, tq, kb, q_off, s_valid, qk, vt_of):
    n_blk = _n_key_blocks(i, tq, kb, q_off, s_valid)
    n_plain = _n_full_blocks(i, tq, kb, q_off, s_valid) & -2
    limit = _key_limit(i, tq, q_off, s_valid)

    def masked(c, j, st):
        return jnp.where(lax.broadcasted_iota(I32, (kb, tq), 0) < limit - j * kb, st, NEG)

    _staged_flash_t(bufs, state, 0, n_plain, n_kb, qk, lambda c, j, st: st, vt_of, last=False)
    _staged_flash_t(bufs, state, n_plain, n_blk, n_kb, qk, masked, vt_of, first=False)


def _mla_t_body(qt_ref, kn_ref, kr_ref, vt_ref, o_ref, s_ref, p_ref, a_ref, m_ref, l_ref, acc_ref,
                *, tq, kb, q_off, s_valid):
    i = pl.program_id(2)
    _flash_init(m_ref, l_ref, acc_ref)

    def qk(a, j):
        ks = pl.multiple_of(j * kb, kb)
        kcat = jnp.concatenate([kn_ref[pl.ds(ks, kb), a * 128:(a + 1) * 128], kr_ref[pl.ds(ks, kb), :]], axis=1)
        return _dot(kcat, qt_ref[a * 256:(a + 1) * 256, :])

    def vt_of(a, j):
        return vt_ref[j, a * 128:(a + 1) * 128, :]

    _causal_flash_t((s_ref, p_ref, a_ref), (m_ref, l_ref, acc_ref), i, vt_ref.shape[0], tq, kb, q_off, s_valid,
                    qk, vt_of)
    for a in range(2):
        o_ref[:, a * 128:(a + 1) * 128] = (acc_ref[a] / l_ref[a]).T.astype(BF16)


def _mla_attn_t(qt, kn, kr, vt, *, tq, kb, q_off, s_valid):
    b, _, t = qt.shape
    s_pad = kn.shape[1]
    n_kb = s_pad // kb
    return pl.pallas_call(
        functools.partial(_mla_t_body, tq=tq, kb=kb, q_off=q_off, s_valid=s_valid),
        grid=(b, B_HEADS // 2, t // tq),
        in_specs=[pl.BlockSpec((None, 512, tq), lambda bi, h, i: (bi, h, i)),
                  pl.BlockSpec((None, s_pad, 256), lambda bi, h, i: (bi, 0, h)),
                  pl.BlockSpec((None, s_pad, 128), lambda bi, h, i: (bi, 0, 0)),
                  pl.BlockSpec((None, n_kb, 256, kb), lambda bi, h, i: (bi, 0, h, 0))],
        out_specs=pl.BlockSpec((None, tq, 256), lambda bi, h, i: (bi, i, h)),
        out_shape=jax.ShapeDtypeStruct((b, t, B_HEADS * V_DIM), BF16),
        scratch_shapes=_stage_bufs(2, kb, tq) + [pltpu.VMEM((2, 1, tq), F32), pltpu.VMEM((2, 1, tq), F32),
                                                 pltpu.VMEM((2, 128, tq), F32)],
        compiler_params=_cparams(3),
        name="mla_attn_t",
    )(qt, kn, kr, vt)


def _diff_t_body(qt_ref, k_ref, vt_ref, lq1_ref, lk1_ref, lq2_ref, lk2_ref, sn_ref, o_ref, s_ref, p_ref, a_ref,
                 m_ref, l_ref, acc_ref, *, tq, kb, q_off, s_valid, lam_init):
    i = pl.program_id(2)
    _flash_init(m_ref, l_ref, acc_ref)

    def qk(p, j):
        ks = pl.multiple_of(j * kb, kb)
        return _dot(k_ref[pl.ds(ks, kb), p * 128:(p + 1) * 128], qt_ref[p * 128:(p + 1) * 128, :])

    _causal_flash_t((s_ref, p_ref, a_ref), (m_ref, l_ref, acc_ref), i, vt_ref.shape[0], tq, kb, q_off, s_valid,
                    qk, lambda p, j: vt_ref[j])
    lam = (jnp.exp(jnp.sum(lq1_ref[...] * lk1_ref[...], axis=1, keepdims=True))
           - jnp.exp(jnp.sum(lq2_ref[...] * lk2_ref[...], axis=1, keepdims=True)) + lam_init)
    ot = acc_ref[0] / l_ref[0] - lam * (acc_ref[1] / l_ref[1])
    o = jnp.concatenate([ot[0:128, :].T, ot[128:256, :].T], axis=1)
    o_ref[...] = (_rms(o, sn_ref[...], 2 * C_DIM) * (1.0 - lam_init)).astype(BF16)


def _diff_attn_t(qt, k, vt, lq1, lk1, lq2, lk2, sn, *, tq, kb, q_off, s_valid, lam_init):
    b, _, t = qt.shape
    s_pad = k.shape[1]
    n_kb = s_pad // kb
    vec = lambda c: pl.BlockSpec((1, c), lambda bi, h, i: (0, 0))
    return pl.pallas_call(
        functools.partial(_diff_t_body, tq=tq, kb=kb, q_off=q_off, s_valid=s_valid, lam_init=lam_init),
        grid=(b, C_HEADS, t // tq),
        in_specs=[pl.BlockSpec((None, 256, tq), lambda bi, h, i: (bi, h, i)),
                  pl.BlockSpec((None, s_pad, 256), lambda bi, h, i: (bi, 0, h)),
                  pl.BlockSpec((None, n_kb, 256, kb), lambda bi, h, i: (bi, 0, h, 0)),
                  vec(128), vec(128), vec(128), vec(128), vec(256)],
        out_specs=pl.BlockSpec((None, tq, 256), lambda bi, h, i: (bi, i, h)),
        out_shape=jax.ShapeDtypeStruct((b, t, C_HEADS * 2 * C_DIM), BF16),
        scratch_shapes=_stage_bufs(2, kb, tq) + [pltpu.VMEM((2, 1, tq), F32), pltpu.VMEM((2, 1, tq), F32),
                                                 pltpu.VMEM((2, 256, tq), F32)],
        compiler_params=_cparams(3),
        name="diff_attn_t",
    )(qt, k, vt, lq1, lk1, lq2, lk2, sn)


def _blocked_t(v, kb):
    b, s, c = v.shape
    return jnp.swapaxes(v.reshape(b, s // kb, kb, c), 2, 3)


def _rope_tables(pos, reps):
    p = pos.astype(F32)[:, None]
    inv64 = jnp.power(ROPE_THETA, -jnp.arange(64, dtype=F32) / 64)
    inv32 = jnp.power(ROPE_THETA, -jnp.arange(32, dtype=F32) / 32)
    c, s = jnp.cos(p * inv64), jnp.sin(p * inv64)
    c3, s3 = jnp.cos(p * inv32), jnp.sin(p * inv32)
    z = jnp.zeros_like(s3)
    tabs = {
        "c128": jnp.concatenate([c, c], axis=1),
        "s128": jnp.concatenate([-s, s], axis=1),
        "c64": jnp.concatenate([c3, c3, c3, c3], axis=1),
        "sa64": jnp.concatenate([-s3, z, -s3, z], axis=1),
        "sb64": jnp.concatenate([z, s3, z, s3], axis=1),
    }
    return {k: jnp.tile(v, (reps, 1)) for k, v in tabs.items()}


def _pad_cols(w, n):
    return jnp.pad(w, ((0, 0), (0, n - w.shape[1])))


def _pad_lanes(g, n=128):
    g = g.reshape(1, -1)
    return jnp.pad(g, ((0, 0), (0, n - g.shape[1])))


def _prep_weights(W):
    P = {}
    for nm in ("ffn1", "ffn2"):
        P[nm] = [(W[nm + "_norm"][i].reshape(1, -1), W[nm + "_wg"][i].astype(BF16), W[nm + "_wu"][i].astype(BF16),
                  W[nm + "_wd"][i].astype(BF16)) for i in range(DEPTH)]
    P["mix_norm"] = [W["mix_norm"][i].reshape(1, -1) for i in range(DEPTH)]
    P["a"] = [dict(w=_pad_cols(W["a_w_in"][j], A_IN_PAD).astype(BF16), qn=W["a_q_norm"][j].reshape(1, -1),
                   kn=W["a_k_norm"][j].reshape(1, -1), ikn=_pad_lanes(W["a_idx_k_norm"][j]),
                   wo=W["a_w_out"][j].astype(BF16)) for j in range(W["a_w_in"].shape[0])]
    P["b"] = []
    for j in range(W["b_w_in"].shape[0]):
        wuq = W["b_w_uq"][j].reshape(Q_LORA, B_HEADS, NOPE_DIM + ROPE_DIM)
        wuq_rope = jnp.pad(wuq[:, :, NOPE_DIM:], ((0, 0), (0, 0), (0, 128 - ROPE_DIM)))
        wuq = jnp.concatenate([wuq[:, :, :NOPE_DIM].reshape(Q_LORA, -1), wuq_rope.reshape(Q_LORA, -1)], axis=1)
        P["b"].append(dict(
            w=_pad_cols(W["b_w_in"][j], B_IN_PAD).astype(BF16), qan=W["b_q_a_norm"][j].reshape(1, -1),
            kvan=W["b_kv_a_norm"][j].reshape(1, -1), krn=_pad_lanes(W["b_k_rope_norm"][j]), wuq=wuq.astype(BF16),
            qnn=W["b_q_nope_norm"][j].reshape(1, -1), qrn=_pad_lanes(W["b_q_rope_norm"][j]),
            wukv=W["b_w_ukv"][j].astype(BF16), knn=W["b_k_nope_norm"][j].reshape(1, -1),
            wo=W["b_w_out"][j].astype(BF16)))
    P["c"] = [dict(w=W["c_w_in"][j].astype(BF16), qn=W["c_q_norm"][j].reshape(1, -1),
                   kn=W["c_k_norm"][j].reshape(1, -1), lq1=W["c_lambda_q1"][j].reshape(1, -1),
                   lk1=W["c_lambda_k1"][j].reshape(1, -1), lq2=W["c_lambda_q2"][j].reshape(1, -1),
                   lk2=W["c_lambda_k2"][j].reshape(1, -1), sn=W["c_sub_norm"][j].reshape(1, -1),
                   wo=W["c_w_out"][j].astype(BF16)) for j in range(W["c_w_in"].shape[0])]
    return P


def _with_past(past, new, s_pad):
    parts = [new] if past is None else [past.astype(new.dtype), new]
    x = jnp.concatenate(parts, axis=1)
    return jnp.pad(x, ((0, 0), (0, s_pad - x.shape[1]), (0, 0)))


def _trunk(x, offset, past, P, cfg):
    b, t, _ = x.shape
    n = b * t
    tm, tq_a, tq, kb, key_major = cfg["tm"], cfg["tq_a"], cfg["tq"], cfg["kb"], cfg["key_major"]
    p_len = 0 if past is None else past["a_k"].shape[2]
    s_valid = p_len + t
    s_pad = -(-s_valid // kb) * kb
    tabs = _rope_tables(offset + jnp.arange(t, dtype=I32), tm // t if tm > t else 1)
    att = dict(kb=kb, q_off=offset, s_valid=s_valid)
    rows = {k: [] for k in ("a_k", "a_v", "a_ik", "b_ckv", "b_kr", "c_k", "c_v")}
    x = x.reshape(n, D_MODEL)
    r3 = lambda a: a.reshape(b, t, a.shape[-1])
    t3 = lambda a: jnp.swapaxes(r3(a), 1, 2)
    pj = lambda nm, j: None if past is None else past[nm][j].reshape(b, p_len, -1)
    for i in range(DEPTH):
        x = _ffn(x, *P["ffn1"][i], tm)
        kind, j = i % N_MIXERS, i // N_MIXERS
        g = P["mix_norm"][i]
        if kind == 0:
            pa = P["a"][j]
            q, k32, v32, kbf, vbf, qi, ki32, kidx, wi = _a_proj(x, g, pa["w"], pa["qn"], pa["kn"], pa["ikn"], tabs, tm)
            rows["a_k"].append(k32.reshape(b, t, A_KV_HEADS, A_HEAD_DIM))
            rows["a_v"].append(v32.reshape(b, t, A_KV_HEADS, A_HEAD_DIM))
            rows["a_ik"].append(ki32.reshape(b, t, IDX_DIM))
            pik = pj("a_ik", j)
            if pik is not None:
                z = jnp.zeros_like(pik)
                pik = jnp.concatenate([pik, z, z, pik], axis=-1)
            kidx_all = _with_past(pik, r3(kidx), s_pad)
            k_all = _with_past(pj("a_k", j), r3(kbf), s_pad)
            v_all = _with_past(pj("a_v", j), r3(vbf), s_pad)
            topk = min(TOPK_MAX, s_valid // 4)
            if key_major:
                o = _dsa_attn_t(t3(q), t3(qi), t3(wi[:, :IDX_HEADS]), kidx_all, k_all, _blocked_t(v_all, kb),
                                tq=tq_a, topk=topk, **att)
            else:
                o = _dsa_attn(r3(q), r3(qi), r3(wi), kidx_all, k_all, v_all, tq=tq_a, topk=topk, **att)
        elif kind == 1:
            pb = P["b"][j]
            q, ckv, kr32, krbf = _b_proj(x, g, pb["w"], pb["qan"], pb["kvan"], pb["krn"], pb["wuq"], pb["qnn"],
                                         pb["qrn"], tabs, tm)
            rows["b_ckv"].append(ckv.reshape(b, t, KV_LORA))
            rows["b_kr"].append(kr32.reshape(b, t, ROPE_DIM))
            ckv_all = _with_past(pj("b_ckv", j), r3(ckv), s_pad)
            pkr = pj("b_kr", j)
            if pkr is not None:
                pkr = jnp.concatenate([pkr, jnp.zeros_like(pkr)], axis=-1)
            kn, v = _kv_up(ckv_all.reshape(b * s_pad, KV_LORA), pb["wukv"], pb["knn"], min(tm, 512))
            kn, v, kr_all = kn.reshape(b, s_pad, -1), v.reshape(b, s_pad, -1), _with_past(pkr, r3(krbf), s_pad)
            if key_major:
                o = _mla_attn_t(t3(q), kn, kr_all, _blocked_t(v, kb), tq=tq, **att)
            else:
                o = _mla_attn(r3(q), kn, kr_all, v, tq=tq, **att)
        else:
            pc = P["c"][j]
            q, k32, v32, kbf, vbf = _c_proj(x, g, pc["w"], pc["qn"], pc["kn"], tabs, tm)
            rows["c_k"].append(k32.reshape(b, t, C_HEADS, 2, C_DIM))
            rows["c_v"].append(v32.reshape(b, t, C_HEADS, 2 * C_DIM))
            k_all = _with_past(pj("c_k", j), r3(kbf), s_pad)
            v_all = _with_past(pj("c_v", j), r3(vbf), s_pad)
            lam = (pc["lq1"], pc["lk1"], pc["lq2"], pc["lk2"], pc["sn"])
            lam_init = 0.8 - 0.6 * math.exp(-0.3 * i)
            if key_major:
                o = _diff_attn_t(t3(q), k_all, _blocked_t(v_all, kb), *lam, tq=tq, lam_init=lam_init, **att)
            else:
                o = _diff_attn(r3(q), k_all, v_all, *lam, tq=tq, lam_init=lam_init, **att)
        wo = (P["a"], P["b"], P["c"])[kind][j]["wo"]
        x = _out_proj(o.reshape(n, -1), wo, x, tm)
        x = _ffn(x, *P["ffn2"][i], tm)
    order = ("a_k", "a_v", "a_ik", "b_ckv", "b_kr", "c_k", "c_v")
    return x.reshape(b, t, D_MODEL), tuple(jnp.stack(rows[k]) for k in order)


PROMPT_CFG = dict(tm=512, tq_a=128, tq=256, kb=256, key_major=True)
SAMPLE_CFG = dict(tm=128, tq_a=16, tq=16, kb=256, key_major=False)


@jax.jit
def _forward(x_prompt, x_sample, past, W):
    P = _prep_weights(W)
    y_p, rows_p = _trunk(x_prompt, 0, None, P, PROMPT_CFG)
    y_s, rows_s = _trunk(x_sample, past["a_k"].shape[2], past, P, SAMPLE_CFG)
    return (y_p, y_s) + rows_p + rows_s


def kernel(x_prompt, x_sample, cache_a_k, cache_a_v, cache_a_idx_k, cache_b_ckv, cache_b_krope, cache_c_k, cache_c_v, ffn1_norm, ffn1_wg, ffn1_wu, ffn1_wd, mix_norm, ffn2_norm, ffn2_wg, ffn2_wu, ffn2_wd, a_w_in, a_q_norm, a_k_norm, a_idx_k_norm, a_w_out, b_w_in, b_q_a_norm, b_kv_a_norm, b_w_uq, b_w_ukv, b_q_nope_norm, b_q_rope_norm, b_k_nope_norm, b_k_rope_norm, b_w_out, c_w_in, c_q_norm, c_k_norm, c_lambda_q1, c_lambda_k1, c_lambda_q2, c_lambda_k2, c_sub_norm, c_w_out):
    W = dict(ffn1_norm=ffn1_norm, ffn1_wg=ffn1_wg, ffn1_wu=ffn1_wu, ffn1_wd=ffn1_wd, mix_norm=mix_norm,
             ffn2_norm=ffn2_norm, ffn2_wg=ffn2_wg, ffn2_wu=ffn2_wu, ffn2_wd=ffn2_wd,
             a_w_in=a_w_in, a_q_norm=a_q_norm, a_k_norm=a_k_norm, a_idx_k_norm=a_idx_k_norm, a_w_out=a_w_out,
             b_w_in=b_w_in, b_q_a_norm=b_q_a_norm, b_kv_a_norm=b_kv_a_norm, b_w_uq=b_w_uq, b_w_ukv=b_w_ukv,
             b_q_nope_norm=b_q_nope_norm, b_q_rope_norm=b_q_rope_norm, b_k_nope_norm=b_k_nope_norm,
             b_k_rope_norm=b_k_rope_norm, b_w_out=b_w_out,
             c_w_in=c_w_in, c_q_norm=c_q_norm, c_k_norm=c_k_norm, c_lambda_q1=c_lambda_q1,
             c_lambda_k1=c_lambda_k1, c_lambda_q2=c_lambda_q2, c_lambda_k2=c_lambda_k2, c_sub_norm=c_sub_norm,
             c_w_out=c_w_out)
    past = dict(a_k=cache_a_k, a_v=cache_a_v, a_ik=cache_a_idx_k, b_ckv=cache_b_ckv, b_kr=cache_b_krope,
                c_k=cache_c_k, c_v=cache_c_v)
    return _forward(x_prompt, x_sample, past, W)
```

```python
import functools
import math

import jax
import jax.numpy as jnp
from jax import lax
from jax.experimental import pallas as pl
from jax.experimental.pallas import tpu as pltpu

F32 = jnp.float32
BF16 = jnp.bfloat16
I32 = jnp.int32

D_MODEL = 1024
DEPTH = 4
CHUNK_SHIFT = 6
N_MIXERS = 3
ROPE_THETA = 10000.0
EPS = 1e-6
D_FF = 2816

A_HEADS = 8
A_KV_HEADS = 2
A_GROUP = A_HEADS // A_KV_HEADS
A_HEAD_DIM = 128
IDX_HEADS = 8
IDX_DIM = 64
TOPK_MAX = 256
LOG2E = math.log2(math.e)
A_SCALE = A_HEAD_DIM ** -0.5 * LOG2E
IDX_W_SCALE = (IDX_HEADS * IDX_DIM) ** -0.5
A_IN = 2120
A_IN_PAD = 2176

B_HEADS = 8
Q_LORA = 384
KV_LORA = 256
NOPE_DIM = 128
ROPE_DIM = 64
V_DIM = 128
B_SCALE = (NOPE_DIM + ROPE_DIM) ** -0.5 * LOG2E
B_IN = 704
B_IN_PAD = 768

C_HEADS = 4
C_DIM = 128
C_SCALE = C_DIM ** -0.5 * LOG2E

LANES = 128
NEG = -1e30
INT_MIN = -(2 ** 31)
VMEM_LIMIT = 56 * 1024 * 1024


def _cparams(n_axes):
    return pltpu.CompilerParams(dimension_semantics=("arbitrary",) * n_axes, vmem_limit_bytes=VMEM_LIMIT)


def _dot(a, b):
    return jnp.dot(a, b, preferred_element_type=F32)


def _dot_nt(a, b):
    return lax.dot_general(a, b, (((1,), (1,)), ((), ())), preferred_element_type=F32)


def _rms(x, g, n):
    ms = jnp.sum(x * x, axis=-1, keepdims=True) * (1.0 / n)
    return x * lax.rsqrt(ms + EPS) * g


def _rope128(x, c, s):
    return x * c + pltpu.roll(x, 64, 1) * s


def _rope64(x, c, sa, sb):
    return x * c + pltpu.roll(x, 96, 1) * sa + pltpu.roll(x, 32, 1) * sb


def _ffn_body(x_ref, g_ref, wg_ref, wu_ref, wd_ref, o_ref, *, fc):
    x = x_ref[...]
    h = _rms(x, g_ref[...], D_MODEL).astype(BF16)
    y = None
    for c in range(D_FF // fc):
        a = _dot(h, wg_ref[:, c * fc:(c + 1) * fc])
        u = _dot(h, wu_ref[:, c * fc:(c + 1) * fc])
        act = (a * jax.nn.sigmoid(a) * u).astype(BF16)
        part = _dot(act, wd_ref[c * fc:(c + 1) * fc, :])
        y = part if y is None else y + part
    o_ref[...] = x + 0.5 * y


def _const_spec(shape):
    nd = len(shape)
    return pl.BlockSpec(shape, lambda *_: (0,) * nd, pipeline_mode=pl.Buffered(1))


def _row_spec(tm, n):
    return pl.BlockSpec((tm, n), lambda i: (i, 0))


def _ffn(x, g, wg, wu, wd, tm):
    n = x.shape[0]
    return pl.pallas_call(
        functools.partial(_ffn_body, fc=D_FF // 2),
        grid=(n // tm,),
        in_specs=[_row_spec(tm, D_MODEL), _const_spec((1, D_MODEL)), _const_spec((D_MODEL, D_FF)),
                  _const_spec((D_MODEL, D_FF)), _const_spec((D_FF, D_MODEL))],
        out_specs=_row_spec(tm, D_MODEL),
        out_shape=jax.ShapeDtypeStruct((n, D_MODEL), F32),
        compiler_params=_cparams(1),
        name="ffn",
    )(x, g, wg, wu, wd)


def _out_proj_body(o_ref, w_ref, x_ref, y_ref):
    y_ref[...] = x_ref[...] + _dot(o_ref[...], w_ref[...])


def _out_proj(o, w, x, tm):
    n, k = o.shape
    return pl.pallas_call(
        _out_proj_body,
        grid=(n // tm,),
        in_specs=[_row_spec(tm, k), _const_spec((k, D_MODEL)), _row_spec(tm, D_MODEL)],
        out_specs=_row_spec(tm, D_MODEL),
        out_shape=jax.ShapeDtypeStruct((n, D_MODEL), F32),
        compiler_params=_cparams(1),
        name="out_proj",
    )(o, w, x)


def _a_proj_body(x_ref, g_ref, w_ref, qn_ref, kn_ref, ikn_ref, c128_ref, s128_ref, c64_ref, sa64_ref, sb64_ref,
                 q_ref, k32_ref, v32_ref, kbf_ref, vbf_ref, qi_ref, ki32_ref, kidx_ref, wi_ref):
    h = _rms(x_ref[...], g_ref[...], D_MODEL).astype(BF16)
    y = _dot(h, w_ref[...])
    c128, s128 = c128_ref[...], s128_ref[...]
    c64, sa64, sb64 = c64_ref[...], sa64_ref[...], sb64_ref[...]
    for hd in range(A_HEADS):
        sl = slice(hd * 128, (hd + 1) * 128)
        qh = _rope128(_rms(y[:, sl], qn_ref[...], A_HEAD_DIM), c128, s128)
        q_ref[:, sl] = (qh * A_SCALE).astype(BF16)
    for hd in range(A_KV_HEADS):
        sl = slice(hd * 128, (hd + 1) * 128)
        kh = _rope128(_rms(y[:, 1024 + hd * 128:1024 + (hd + 1) * 128], kn_ref[...], A_HEAD_DIM), c128, s128)
        k32_ref[:, sl] = kh
        kbf_ref[:, sl] = kh.astype(BF16)
    v = y[:, 1280:1536]
    v32_ref[...] = v
    vbf_ref[...] = v.astype(BF16)
    for p in range(IDX_HEADS // 2):
        sl = slice(p * 128, (p + 1) * 128)
        qi_ref[:, sl] = _rope64(y[:, 1536 + p * 128:1536 + (p + 1) * 128], c64, sa64, sb64).astype(BF16)
    tail = y[:, 2048:2176]
    lane = lax.broadcasted_iota(I32, tail.shape, 1)
    low = lane < IDX_DIM
    kin = jnp.where(low, tail, 0.0)
    ki = _rope64(_rms(kin, ikn_ref[...], IDX_DIM), c64, sa64, sb64)
    ki = jnp.where(low, ki, 0.0)
    ki32_ref[...] = ki[:, :IDX_DIM]
    kidx_ref[:, 0:128] = ki.astype(BF16)
    kidx_ref[:, 128:256] = pltpu.roll(ki, 64, 1).astype(BF16)
    wi_ref[...] = pltpu.roll(tail, 64, 1) * IDX_W_SCALE


def _tab_spec(tm, t):
    nt = t // tm
    return pl.BlockSpec((tm, LANES), lambda i: (i % nt, 0))


def _a_proj(x, g, w, qn, kn, ikn, tabs, tm):
    n = x.shape[0]
    t = tabs["c128"].shape[0]
    outs = [(1024, BF16), (256, F32), (256, F32), (256, BF16), (256, BF16), (512, BF16), (IDX_DIM, F32),
            (256, BF16), (LANES, F32)]
    return pl.pallas_call(
        _a_proj_body,
        grid=(n // tm,),
        in_specs=[_row_spec(tm, D_MODEL), _const_spec((1, D_MODEL)), _const_spec((D_MODEL, A_IN_PAD)),
                  _const_spec((1, 128)), _const_spec((1, 128)), _const_spec((1, 128))]
        + [_tab_spec(tm, t)] * 5,
        out_specs=[_row_spec(tm, c) for c, _ in outs],
        out_shape=[jax.ShapeDtypeStruct((n, c), dt) for c, dt in outs],
        compiler_params=_cparams(1),
        name="a_proj",
    )(x, g, w, qn, kn, ikn, tabs["c128"], tabs["s128"], tabs["c64"], tabs["sa64"], tabs["sb64"])


def _b_proj_body(x_ref, g_ref, w_ref, qan_ref, kvan_ref, krn_ref, wuq_ref, qnn_ref, qrn_ref,
                 c64_ref, sa64_ref, sb64_ref, q_ref, ckv_ref, kr32_ref, krbf_ref):
    h = _rms(x_ref[...], g_ref[...], D_MODEL).astype(BF16)
    y = _dot(h, w_ref[...])
    c64, sa64, sb64 = c64_ref[...], sa64_ref[...], sb64_ref[...]
    ckv_ref[...] = _rms(y[:, 384:640], kvan_ref[...], KV_LORA)
    kr = _rope64(_rms(y[:, 640:768], krn_ref[...], ROPE_DIM), c64, sa64, sb64)
    kr32_ref[...] = kr[:, :ROPE_DIM]
    krbf_ref[...] = kr.astype(BF16)
    cq = _rms(y[:, 0:384], qan_ref[...], Q_LORA).astype(BF16)
    qq = _dot(cq, wuq_ref[...])
    for hd in range(B_HEADS):
        qn = _rms(qq[:, hd * 128:(hd + 1) * 128], qnn_ref[...], NOPE_DIM)
        qr = _rms(qq[:, 1024 + hd * 128:1024 + (hd + 1) * 128], qrn_ref[...], ROPE_DIM)
        qr = _rope64(qr, c64, sa64, sb64)
        q_ref[:, hd * 256:hd * 256 + 128] = (qn * B_SCALE).astype(BF16)
        q_ref[:, hd * 256 + 128:(hd + 1) * 256] = (qr * B_SCALE).astype(BF16)


def _b_proj(x, g, w, qan, kvan, krn, wuq, qnn, qrn, tabs, tm):
    n = x.shape[0]
    t = tabs["c64"].shape[0]
    outs = [(B_HEADS * 256, BF16), (KV_LORA, F32), (ROPE_DIM, F32), (LANES, BF16)]
    return pl.pallas_call(
        _b_proj_body,
        grid=(n // tm,),
        in_specs=[_row_spec(tm, D_MODEL), _const_spec((1, D_MODEL)), _const_spec((D_MODEL, B_IN_PAD)),
                  _const_spec((1, Q_LORA)), _const_spec((1, KV_LORA)), _const_spec((1, 128)),
                  _const_spec((Q_LORA, 2048)), _const_spec((1, 128)), _const_spec((1, 128))]
        + [_tab_spec(tm, t)] * 3,
        out_specs=[_row_spec(tm, c) for c, _ in outs],
        out_shape=[jax.ShapeDtypeStruct((n, c), dt) for c, dt in outs],
        compiler_params=_cparams(1),
        name="b_proj",
    )(x, g, w, qan, kvan, krn, wuq, qnn, qrn, tabs["c64"], tabs["sa64"], tabs["sb64"])


def _kv_up_body(ckv_ref, w_ref, knn_ref, kn_ref, v_ref):
    y = _dot(ckv_ref[...].astype(BF16), w_ref[...])
    for hd in range(B_HEADS):
        kn_ref[:, hd * 128:(hd + 1) * 128] = _rms(y[:, hd * 256:hd * 256 + 128], knn_ref[...], NOPE_DIM).astype(BF16)
        v_ref[:, hd * 128:(hd + 1) * 128] = y[:, hd * 256 + 128:(hd + 1) * 256].astype(BF16)


def _kv_up(ckv, w, knn, tm):
    n = ckv.shape[0]
    return pl.pallas_call(
        _kv_up_body,
        grid=(n // tm,),
        in_specs=[_row_spec(tm, KV_LORA), _const_spec((KV_LORA, 2048)), _const_spec((1, 128))],
        out_specs=[_row_spec(tm, 1024), _row_spec(tm, 1024)],
        out_shape=[jax.ShapeDtypeStruct((n, 1024), BF16)] * 2,
        compiler_params=_cparams(1),
        name="kv_up",
    )(ckv, w, knn)


def _c_proj_body(x_ref, g_ref, w_ref, qn_ref, kn_ref, c128_ref, s128_ref,
                 q_ref, k32_ref, v32_ref, kbf_ref, vbf_ref):
    h = _rms(x_ref[...], g_ref[...], D_MODEL).astype(BF16)
    y = _dot(h, w_ref[...])
    c128, s128 = c128_ref[...], s128_ref[...]
    for hd in range(2 * C_HEADS):
        sl = slice(hd * 128, (hd + 1) * 128)
        qh = _rope128(_rms(y[:, sl], qn_ref[...], C_DIM), c128, s128)
        q_ref[:, sl] = (qh * C_SCALE).astype(BF16)
        kh = _rope128(_rms(y[:, 1024 + hd * 128:1024 + (hd + 1) * 128], kn_ref[...], C_DIM), c128, s128)
        k32_ref[:, sl] = kh
        kbf_ref[:, sl] = kh.astype(BF16)
    v = y[:, 2048:3072]
    v32_ref[...] = v
    vbf_ref[...] = v.astype(BF16)


def _c_proj(x, g, w, qn, kn, tabs, tm):
    n = x.shape[0]
    t = tabs["c128"].shape[0]
    outs = [(1024, BF16), (1024, F32), (1024, F32), (1024, BF16), (1024, BF16)]
    return pl.pallas_call(
        _c_proj_body,
        grid=(n // tm,),
        in_specs=[_row_spec(tm, D_MODEL), _const_spec((1, D_MODEL)), _const_spec((D_MODEL, 3072)),
                  _const_spec((1, 128)), _const_spec((1, 128))] + [_tab_spec(tm, t)] * 2,
        out_specs=[_row_spec(tm, c) for c, _ in outs],
        out_shape=[jax.ShapeDtypeStruct((n, c), dt) for c, dt in outs],
        compiler_params=_cparams(1),
        name="c_proj",
    )(x, g, w, qn, kn, tabs["c128"], tabs["s128"])


def _n_key_blocks(i, tq, kb, q_off, s_valid):
    last_chunk = lax.shift_right_logical(q_off + (i + 1) * tq - 1, CHUNK_SHIFT)
    kend = jnp.minimum((last_chunk + 1) << CHUNK_SHIFT, s_valid)
    return lax.div(kend + kb - 1, jnp.int32(kb))


def _q_chunk(i, tq, q_off):
    row = lax.broadcasted_iota(I32, (tq, 1), 0)
    return lax.shift_right_logical(q_off + i * tq + row, CHUNK_SHIFT)


def _k_chunk(ks, kb, s_valid):
    kpos = ks + lax.broadcasted_iota(I32, (1, kb), 1)
    return jnp.where(kpos < s_valid, lax.shift_right_logical(kpos, CHUNK_SHIFT), 2 ** 30)


def _flash_step(s, v, m_ref, l_ref, acc_ref):
    m_prev = m_ref[...]
    m_new = jnp.maximum(m_prev, jnp.max(s, axis=1, keepdims=True))
    alpha = jnp.exp2(m_prev - m_new)
    p = jnp.exp2(s - m_new)
    l_ref[...] = alpha * l_ref[...] + jnp.sum(p, axis=1, keepdims=True)
    acc_ref[...] = alpha * acc_ref[...] + _dot(p.astype(BF16), v)
    m_ref[...] = m_new


def _key_limit(i, tq, q_off, s_valid):
    qpos = q_off + i * tq + lax.broadcasted_iota(I32, (1, tq), 1)
    return jnp.minimum((lax.shift_right_logical(qpos, CHUNK_SHIFT) + 1) << CHUNK_SHIFT, s_valid)


def _n_full_blocks(i, tq, kb, q_off, s_valid):
    first_chunk = lax.shift_right_logical(q_off + i * tq, CHUNK_SHIFT)
    kend = jnp.minimum((first_chunk + 1) << CHUNK_SHIFT, s_valid)
    return lax.shift_right_logical(kend, int(math.log2(kb)))


def _flash_init(m_ref, l_ref, acc_ref):
    m_ref[...] = jnp.full(m_ref.shape, NEG, F32)
    l_ref[...] = jnp.zeros(l_ref.shape, F32)
    acc_ref[...] = jnp.zeros(acc_ref.shape, F32)


def _dsa_body(q_ref, qi_ref, wi_ref, kidx_ref, k_ref, v_ref, o_ref, keys_ref, m_ref, l_ref, acc_ref,
              *, tq, kb, q_off, s_valid, topk):
    i = pl.program_id(1)
    n_blk = _n_key_blocks(i, tq, kb, q_off, s_valid)
    qc = _q_chunk(i, tq, q_off)
    wi = wi_ref[...]
    wcol = [wi[:, hd:hd + 1] for hd in range(IDX_HEADS)]

    def score_block(j, carry):
        ks = pl.multiple_of(j * kb, kb)
        k_lo = kidx_ref[pl.ds(ks, kb), 0:128]
        k_hi = kidx_ref[pl.ds(ks, kb), 128:256]
        sc = jnp.zeros((tq, kb), F32)
        for p in range(IDX_HEADS // 2):
            qp = qi_ref[:, p * 128:(p + 1) * 128]
            sc = sc + wcol[2 * p] * jnp.maximum(_dot_nt(qp, k_lo), 0.0)
            sc = sc + wcol[2 * p + 1] * jnp.maximum(_dot_nt(qp, k_hi), 0.0)
        bits = lax.bitcast_convert_type(sc, I32)
        key = bits ^ ((bits >> 31) & 0x7FFFFFFF)
        keys_ref[j] = jnp.where(_k_chunk(ks, kb, s_valid) <= qc, key, INT_MIN)
        return carry

    lax.fori_loop(0, n_blk, score_block, 0)

    def count(pred):
        def blk(j, c):
            x = jnp.where(pred(keys_ref[j], j * kb), 1.0, 0.0)
            part = x[:, 0:LANES]
            for g in range(1, kb // LANES):
                part = part + x[:, g * LANES:(g + 1) * LANES]
            return c + part
        c = lax.fori_loop(0, n_blk, blk, jnp.zeros((tq, LANES), F32))
        return jnp.sum(c, axis=1, keepdims=True)

    kf = float(topk)

    def bit_step(b, t):
        cand = t + jnp.left_shift(jnp.int32(1), 31 - b)
        cnt = count(lambda kblk, _: kblk >= cand)
        return jnp.where(cnt >= kf, cand, t)

    thr = lax.fori_loop(0, 32, bit_step, jnp.full((tq, 1), INT_MIN, I32))
    need = kf - count(lambda kblk, _: kblk > thr)
    n_eq = count(lambda kblk, _: kblk == thr)
    partial = jnp.logical_and(n_eq > need, thr != INT_MIN)

    @pl.when(jnp.max(jnp.where(partial, 1.0, 0.0)) > 0.0)
    def _():
        lane = lax.broadcasted_iota(I32, (1, kb), 1)

        def idx_step(b, x):
            cand = x + jnp.left_shift(jnp.int32(1), b)
            cnt = count(lambda kblk, k0: jnp.logical_and(kblk == thr, k0 + lane < cand))
            return jnp.where(cnt < need, cand, x)

        nbits = max(1, int(s_valid - 1).bit_length())
        cut = lax.fori_loop(0, nbits, lambda b, x: idx_step(nbits - 1 - b, x), jnp.zeros((tq, 1), I32))

        def demote(j, carry):
            kblk = keys_ref[j]
            drop = jnp.logical_and(jnp.logical_and(kblk == thr, j * kb + lane > cut), partial)
            keys_ref[j] = jnp.where(drop, kblk - 1, kblk)
            return carry

        lax.fori_loop(0, n_blk, demote, 0)

    thr_sel = jnp.maximum(thr, INT_MIN + 1)

    _flash_init(m_ref, l_ref, acc_ref)

    def attend(j, carry):
        ks = pl.multiple_of(j * kb, kb)
        bias = jnp.where(keys_ref[j] >= thr_sel, 0.0, NEG)
        for g in range(A_KV_HEADS):
            qg = jnp.concatenate([q_ref[:, (g * A_GROUP + a) * 128:(g * A_GROUP + a + 1) * 128]
                                  for a in range(A_GROUP)], axis=0)
            s = _dot_nt(qg, k_ref[pl.ds(ks, kb), g * 128:(g + 1) * 128])
            s = (s.reshape(A_GROUP, tq, kb) + bias[None]).reshape(A_GROUP * tq, kb)
            _flash_step(s, v_ref[pl.ds(ks, kb), g * 128:(g + 1) * 128], m_ref.at[g], l_ref.at[g], acc_ref.at[g])
        return carry

    lax.fori_loop(0, n_blk, attend, 0)
    for g in range(A_KV_HEADS):
        o = acc_ref[g] / l_ref[g]
        for a in range(A_GROUP):
            hd = g * A_GROUP + a
            o_ref[:, hd * 128:(hd + 1) * 128] = o[a * tq:(a + 1) * tq].astype(BF16)


def _dsa_attn(q, qi, wi, kidx, k, v, *, tq, kb, q_off, s_valid, topk):
    b, t, _ = q.shape
    s_pad = k.shape[1]
    qspec = lambda c: pl.BlockSpec((None, tq, c), lambda bi, i: (bi, i, 0))
    kspec = lambda c: pl.BlockSpec((None, s_pad, c), lambda bi, i: (bi, 0, 0))
    rows = A_GROUP * tq
    return pl.pallas_call(
        functools.partial(_dsa_body, tq=tq, kb=kb, q_off=q_off, s_valid=s_valid, topk=topk),
        grid=(b, t // tq),
        in_specs=[qspec(1024), qspec(512), qspec(LANES), kspec(256), kspec(256), kspec(256)],
        out_specs=qspec(1024),
        out_shape=jax.ShapeDtypeStruct((b, t, 1024), BF16),
        scratch_shapes=[pltpu.VMEM((s_pad // kb, tq, kb), I32), pltpu.VMEM((A_KV_HEADS, rows, 1), F32),
                        pltpu.VMEM((A_KV_HEADS, rows, 1), F32), pltpu.VMEM((A_KV_HEADS, rows, 128), F32)],
        compiler_params=_cparams(2),
        name="dsa_attn",
    )(q, qi, wi, kidx, k, v)


def _mla_body(q_ref, kn_ref, kr_ref, v_ref, o_ref, m_ref, l_ref, acc_ref, *, tq, kb, q_off, s_valid):
    i = pl.program_id(2)
    n_blk = _n_key_blocks(i, tq, kb, q_off, s_valid)
    qc = _q_chunk(i, tq, q_off)
    _flash_init(m_ref, l_ref, acc_ref)

    def attend(j, carry):
        ks = pl.multiple_of(j * kb, kb)
        kcat = jnp.concatenate([kn_ref[pl.ds(ks, kb), :], kr_ref[pl.ds(ks, kb), :]], axis=1)
        s = _dot_nt(q_ref[...], kcat)
        s = jnp.where(_k_chunk(ks, kb, s_valid) <= qc, s, NEG)
        _flash_step(s, v_ref[pl.ds(ks, kb), :], m_ref, l_ref, acc_ref)
        return carry

    lax.fori_loop(0, n_blk, attend, 0)
    o_ref[...] = (acc_ref[...] / l_ref[...]).astype(BF16)


def _mla_attn(q, kn, kr, v, *, tq, kb, q_off, s_valid):
    b, t, _ = q.shape
    s_pad = kn.shape[1]
    return pl.pallas_call(
        functools.partial(_mla_body, tq=tq, kb=kb, q_off=q_off, s_valid=s_valid),
        grid=(b, B_HEADS, t // tq),
        in_specs=[pl.BlockSpec((None, tq, 256), lambda bi, h, i: (bi, i, h)),
                  pl.BlockSpec((None, s_pad, 128), lambda bi, h, i: (bi, 0, h)),
                  pl.BlockSpec((None, s_pad, 128), lambda bi, h, i: (bi, 0, 0)),
                  pl.BlockSpec((None, s_pad, 128), lambda bi, h, i: (bi, 0, h))],
        out_specs=pl.BlockSpec((None, tq, 128), lambda bi, h, i: (bi, i, h)),
        out_shape=jax.ShapeDtypeStruct((b, t, B_HEADS * V_DIM), BF16),
        scratch_shapes=[pltpu.VMEM((tq, 1), F32), pltpu.VMEM((tq, 1), F32), pltpu.VMEM((tq, 128), F32)],
        compiler_params=_cparams(3),
        name="mla_attn",
    )(q, kn, kr, v)


def _diff_body(q_ref, k_ref, v_ref, lq1_ref, lk1_ref, lq2_ref, lk2_ref, sn_ref, o_ref, m_ref, l_ref, acc_ref,
               *, tq, kb, q_off, s_valid, lam_init):
    i = pl.program_id(2)
    n_blk = _n_key_blocks(i, tq, kb, q_off, s_valid)
    qc = _q_chunk(i, tq, q_off)
    _flash_init(m_ref, l_ref, acc_ref)

    def attend(j, carry):
        ks = pl.multiple_of(j * kb, kb)
        ok = _k_chunk(ks, kb, s_valid) <= qc
        v = v_ref[pl.ds(ks, kb), :]
        for p in range(2):
            s = _dot_nt(q_ref[:, p * 128:(p + 1) * 128], k_ref[pl.ds(ks, kb), p * 128:(p + 1) * 128])
            s = jnp.where(ok, s, NEG)
            _flash_step(s, v, m_ref.at[p], l_ref.at[p], acc_ref.at[p])
        return carry

    lax.fori_loop(0, n_blk, attend, 0)
    lam = (jnp.exp(jnp.sum(lq1_ref[...] * lk1_ref[...], axis=1, keepdims=True))
           - jnp.exp(jnp.sum(lq2_ref[...] * lk2_ref[...], axis=1, keepdims=True)) + lam_init)
    o = acc_ref[0] / l_ref[0] - lam * (acc_ref[1] / l_ref[1])
    o_ref[...] = (_rms(o, sn_ref[...], 2 * C_DIM) * (1.0 - lam_init)).astype(BF16)


def _diff_attn(q, k, v, lq1, lk1, lq2, lk2, sn, *, tq, kb, q_off, s_valid, lam_init):
    b, t, _ = q.shape
    s_pad = k.shape[1]
    vec = lambda c: pl.BlockSpec((1, c), lambda bi, h, i: (0, 0))
    return pl.pallas_call(
        functools.partial(_diff_body, tq=tq, kb=kb, q_off=q_off, s_valid=s_valid, lam_init=lam_init),
        grid=(b, C_HEADS, t // tq),
        in_specs=[pl.BlockSpec((None, tq, 256), lambda bi, h, i: (bi, i, h)),
                  pl.BlockSpec((None, s_pad, 256), lambda bi, h, i: (bi, 0, h)),
                  pl.BlockSpec((None, s_pad, 256), lambda bi, h, i: (bi, 0, h)),
                  vec(128), vec(128), vec(128), vec(128), vec(256)],
        out_specs=pl.BlockSpec((None, tq, 256), lambda bi, h, i: (bi, i, h)),
        out_shape=jax.ShapeDtypeStruct((b, t, C_HEADS * 2 * C_DIM), BF16),
        scratch_shapes=[pltpu.VMEM((2, tq, 1), F32), pltpu.VMEM((2, tq, 1), F32), pltpu.VMEM((2, tq, 256), F32)],
        compiler_params=_cparams(3),
        name="diff_attn",
    )(q, k, v, lq1, lk1, lq2, lk2, sn)


SEARCH_BLOCKS = 4
SEARCH_BITS_UNCHECKED = 16
SEARCH_BITS_PER_CHECK = 4


def _stage_bufs(n_chain, kb, r):
    return [pltpu.VMEM((n_chain, 2, kb, r), F32), pltpu.VMEM((n_chain, 2, kb, r), BF16),
            pltpu.VMEM((n_chain, 2, 1, r), F32)]


def _staged_flash_t(bufs, state, lo, hi, n_kb, qk, prep, vt_of, first=True, last=True):
    s_ref, p_ref, a_ref = bufs
    m_ref, l_ref, acc_ref = state
    n_chain = s_ref.shape[0]
    clamp = lambda j: jnp.clip(j, 0, n_kb - 1)

    def softmax(c, j, slot):
        st = prep(c, j, s_ref[c, slot])
        m_prev = m_ref[c]
        m_new = jnp.maximum(m_prev, jnp.max(st, axis=0, keepdims=True))
        alpha = jnp.exp2(m_prev - m_new)
        p = jnp.exp2(st - m_new)
        l_ref[c] = alpha * l_ref[c] + jnp.sum(p, axis=0, keepdims=True)
        m_ref[c] = m_new
        p_ref[c, slot] = p.astype(BF16)
        a_ref[c, slot] = alpha

    def values(c, j, slot):
        acc_ref[c] = a_ref[c, slot] * acc_ref[c] + _dot(vt_of(c, clamp(j)), p_ref[c, slot])

    if first:
        for c in range(n_chain):
            s_ref[c, 0] = qk(c, clamp(lo))
            p_ref[c, 1] = jnp.zeros(p_ref.shape[2:], BF16)
            a_ref[c, 1] = jnp.ones(a_ref.shape[2:], F32)

    def turn(t, carry):
        j0 = lo + 2 * t
        for c in range(n_chain):
            s_ref[c, 1] = qk(c, clamp(j0 + 1))
        for c in range(n_chain):
            softmax(c, j0, 0)
        for c in range(n_chain):
            values(c, j0 - 1, 1)
        for c in range(n_chain):
            s_ref[c, 0] = qk(c, clamp(j0 + 2))
        for c in range(n_chain):
            softmax(c, j0 + 1, 1)
        for c in range(n_chain):
            values(c, j0, 0)
        return carry

    n_turn = lax.shift_right_logical(hi - lo + 1, 1)
    lax.fori_loop(0, n_turn, turn, 0)
    if last:
        for c in range(n_chain):
            values(c, lo + 2 * n_turn - 1, 1)


def _dsa_t_body(qt_ref, qit_ref, wit_ref, kidx_ref, k_ref, vt_ref, o_ref, keys_ref, s_ref, p_ref, a_ref,
                m_ref, l_ref, acc_ref, *, tq, kb, q_off, s_valid, topk):
    i = pl.program_id(1)
    n_blk = _n_key_blocks(i, tq, kb, q_off, s_valid)
    n_kb = keys_ref.shape[0]
    n_sb = lax.shift_right_logical(n_blk + SEARCH_BLOCKS - 1, int(math.log2(SEARCH_BLOCKS)))
    limit = _key_limit(i, tq, q_off, s_valid)
    kidx0 = lax.broadcasted_iota(I32, (kb, tq), 0)
    wit = wit_ref[...]
    wrow = [wit[hd:hd + 1, :] for hd in range(IDX_HEADS)]

    def score_block(j, carry):
        ks = pl.multiple_of(j * kb, kb)
        q_all = jnp.concatenate([qit_ref[p * 128:(p + 1) * 128, :] for p in range(IDX_HEADS // 2)], axis=1)
        s_lo = _dot(kidx_ref[pl.ds(ks, kb), 0:128], q_all)
        s_hi = _dot(kidx_ref[pl.ds(ks, kb), 128:256], q_all)
        sc = jnp.zeros((kb, tq), F32)
        for p in range(IDX_HEADS // 2):
            sc = sc + wrow[2 * p] * jnp.maximum(s_lo[:, p * tq:(p + 1) * tq], 0.0)
            sc = sc + wrow[2 * p + 1] * jnp.maximum(s_hi[:, p * tq:(p + 1) * tq], 0.0)
        bits = lax.bitcast_convert_type(sc, I32)
        key = bits ^ ((bits >> 31) & 0x7FFFFFFF)
        keys_ref[j] = jnp.where(kidx0 < limit - ks, key, INT_MIN)
        return carry

    def pad_block(j, carry):
        keys_ref[j] = jnp.full((kb, tq), INT_MIN, I32)
        return carry

    lax.fori_loop(0, n_blk, score_block, 0)
    lax.fori_loop(n_blk, n_sb * SEARCH_BLOCKS, pad_block, 0)

    def count(pred):
        def group(jj, cs):
            out = []
            for u in range(SEARCH_BLOCKS):
                j = jj * SEARCH_BLOCKS + u
                x = jnp.where(pred(keys_ref[j], j * kb), 1.0, 0.0)
                out.append(cs[u] + jnp.sum(x.reshape(kb // 8, 8, tq), axis=0))
            return tuple(out)
        cs = lax.fori_loop(0, n_sb, group, tuple(jnp.zeros((8, tq), F32) for _ in range(SEARCH_BLOCKS)))
        return jnp.sum(functools.reduce(lambda a, b: a + b, cs), axis=0, keepdims=True)

    kf = float(topk)

    def bit_step(b, state):
        t, n_ge = state
        cand = t + jnp.left_shift(jnp.int32(1), 31 - b)
        cnt = count(lambda kblk, _: kblk >= cand)
        take = cnt >= kf
        return jnp.where(take, cand, t), jnp.where(take, cnt, n_ge)

    def unsettled(n_ge):
        open_ = jnp.logical_and(n_ge != kf, limit.astype(F32) > kf)
        return jnp.max(jnp.where(open_, 1.0, 0.0)) > 0.0

    def more_bits(state):
        b, _, n_ge = state
        return jnp.logical_and(b < 32, unsettled(n_ge))

    def four_bits(state):
        b, t, n_ge = state
        for u in range(SEARCH_BITS_PER_CHECK):
            t, n_ge = bit_step(b + u, (t, n_ge))
        return b + SEARCH_BITS_PER_CHECK, t, n_ge

    start = (jnp.full((1, tq), INT_MIN, I32), jnp.full((1, tq), float(n_kb * kb), F32))
    head = lax.fori_loop(0, SEARCH_BITS_UNCHECKED, bit_step, start)
    _, thr, _ = lax.while_loop(more_bits, four_bits, (jnp.int32(SEARCH_BITS_UNCHECKED),) + head)
    need = kf - count(lambda kblk, _: kblk > thr)
    n_eq = count(lambda kblk, _: kblk == thr)
    partial = jnp.logical_and(n_eq > need, thr != INT_MIN)

    @pl.when(jnp.max(jnp.where(partial, 1.0, 0.0)) > 0.0)
    def _():
        def idx_step(b, x):
            cand = x + jnp.left_shift(jnp.int32(1), b)
            cnt = count(lambda kblk, k0: jnp.logical_and(kblk == thr, kidx0 < cand - k0))
            return jnp.where(cnt < need, cand, x)

        nbits = max(1, int(s_valid - 1).bit_length())
        cut = lax.fori_loop(0, nbits, lambda b, x: idx_step(nbits - 1 - b, x), jnp.zeros((1, tq), I32))

        def demote(j, carry):
            kblk = keys_ref[j]
            drop = jnp.logical_and(jnp.logical_and(kblk == thr, kidx0 > cut - j * kb), partial)
            keys_ref[j] = jnp.where(drop, kblk - 1, kblk)
            return carry

        lax.fori_loop(0, n_blk, demote, 0)

    thr_sel = jnp.maximum(thr, INT_MIN + 1)

    _flash_init(m_ref, l_ref, acc_ref)
    n_pair = A_HEADS // 2

    def to_bias(j, carry):
        keys_ref[j] = lax.bitcast_convert_type(jnp.where(keys_ref[j] >= thr_sel, 0.0, NEG), I32)
        return carry

    lax.fori_loop(0, n_sb * SEARCH_BLOCKS, to_bias, 0)
    for g in range(A_KV_HEADS):
        def qk(c, j, g=g):
            p = g * (A_GROUP // 2) + c
            qp = qt_ref[2 * p * 128:(2 * p + 2) * 128, :]
            qp = jnp.concatenate([qp[0:128, :], qp[128:256, :]], axis=1)
            return _dot(k_ref[pl.ds(pl.multiple_of(j * kb, kb), kb), g * 128:(g + 1) * 128], qp)

        def prep(c, j, st):
            bias = lax.bitcast_convert_type(keys_ref[j], F32)
            return st + jnp.concatenate([bias, bias], axis=1)

        def vt_of(c, j, g=g):
            return vt_ref[j, g * 128:(g + 1) * 128, :]

        chains = pl.ds(g * (A_GROUP // 2), A_GROUP // 2)
        _staged_flash_t((s_ref.at[chains], p_ref.at[chains], a_ref.at[chains]),
                        (m_ref.at[chains], l_ref.at[chains], acc_ref.at[chains]), 0, n_blk, n_kb, qk, prep, vt_of)
    for p in range(n_pair):
        ot = acc_ref[p] / l_ref[p]
        for a in range(2):
            hd = 2 * p + a
            o_ref[:, hd * 128:(hd + 1) * 128] = ot[:, a * tq:(a + 1) * tq].T.astype(BF16)


def _dsa_attn_t(qt, qit, wit, kidx, k, vt, *, tq, kb, q_off, s_valid, topk):
    b, _, t = qt.shape
    s_pad = k.shape[1]
    n_kb = s_pad // kb
    qtspec = lambda c: pl.BlockSpec((None, c, tq), lambda bi, i: (bi, 0, i))
    kspec = lambda c: pl.BlockSpec((None, s_pad, c), lambda bi, i: (bi, 0, 0))
    n_pair = A_HEADS // 2
    return pl.pallas_call(
        functools.partial(_dsa_t_body, tq=tq, kb=kb, q_off=q_off, s_valid=s_valid, topk=topk),
        grid=(b, t // tq),
        in_specs=[qtspec(1024), qtspec(512), qtspec(IDX_HEADS),
                  kspec(256), kspec(256), pl.BlockSpec((None, n_kb, 256, kb), lambda bi, i: (bi, 0, 0, 0))],
        out_specs=pl.BlockSpec((None, tq, 1024), lambda bi, i: (bi, i, 0)),
        out_shape=jax.ShapeDtypeStruct((b, t, 1024), BF16),
        scratch_shapes=[pltpu.VMEM((n_kb, kb, tq), I32)] + _stage_bufs(n_pair, kb, 2 * tq)
        + [pltpu.VMEM((n_pair, 1, 2 * tq), F32), pltpu.VMEM((n_pair, 1, 2 * tq), F32),
           pltpu.VMEM((n_pair, 128, 2 * tq), F32)],
        compiler_params=_cparams(2),
        name="dsa_attn_t",
    )(qt, qit, wit, kidx, k, vt)


def _causal_flash_t(bufs, state, i, n_kb, tq, kb, q_off, s_valid, qk, vt_of):
    n_blk = _n_key_blocks(i, tq, kb, q_off, s_valid)
    n_plain = _n_full_blocks(i, tq, kb, q_off, s_valid) & -2
    limit = _key_limit(i, tq, q_off, s_valid)

    def masked(c, j, st):
        return jnp.where(lax.broadcasted_iota(I32, (kb, tq), 0) < limit - j * kb, st, NEG)

    _staged_flash_t(bufs, state, 0, n_plain, n_kb, qk, lambda c, j, st: st, vt_of, last=False)
    _staged_flash_t(bufs, state, n_plain, n_blk, n_kb, qk, masked, vt_of, first=False)


def _mla_t_body(qt_ref, kn_ref, kr_ref, vt_ref, o_ref, s_ref, p_ref, a_ref, m_ref, l_ref, acc_ref,
                *, tq, kb, q_off, s_valid):
    i = pl.program_id(2)
    _flash_init(m_ref, l_ref, acc_ref)

    def qk(a, j):
        ks = pl.multiple_of(j * kb, kb)
        kcat = jnp.concatenate([kn_ref[pl.ds(ks, kb), a * 128:(a + 1) * 128], kr_ref[pl.ds(ks, kb), :]], axis=1)
        return _dot(kcat, qt_ref[a * 256:(a + 1) * 256, :])

    def vt_of(a, j):
        return vt_ref[j, a * 128:(a + 1) * 128, :]

    _causal_flash_t((s_ref, p_ref, a_ref), (m_ref, l_ref, acc_ref), i, vt_ref.shape[0], tq, kb, q_off, s_valid,
                    qk, vt_of)
    for a in range(2):
        o_ref[:, a * 128:(a + 1) * 128] = (acc_ref[a] / l_ref[a]).T.astype(BF16)


def _mla_attn_t(qt, kn, kr, vt, *, tq, kb, q_off, s_valid):
    b, _, t = qt.shape
    s_pad = kn.shape[1]
    n_kb = s_pad // kb
    return pl.pallas_call(
        functools.partial(_mla_t_body, tq=tq, kb=kb, q_off=q_off, s_valid=s_valid),
        grid=(b, B_HEADS // 2, t // tq),
        in_specs=[pl.BlockSpec((None, 512, tq), lambda bi, h, i: (bi, h, i)),
                  pl.BlockSpec((None, s_pad, 256), lambda bi, h, i: (bi, 0, h)),
                  pl.BlockSpec((None, s_pad, 128), lambda bi, h, i: (bi, 0, 0)),
                  pl.BlockSpec((None, n_kb, 256, kb), lambda bi, h, i: (bi, 0, h, 0))],
        out_specs=pl.BlockSpec((None, tq, 256), lambda bi, h, i: (bi, i, h)),
        out_shape=jax.ShapeDtypeStruct((b, t, B_HEADS * V_DIM), BF16),
        scratch_shapes=_stage_bufs(2, kb, tq) + [pltpu.VMEM((2, 1, tq), F32), pltpu.VMEM((2, 1, tq), F32),
                                                 pltpu.VMEM((2, 128, tq), F32)],
        compiler_params=_cparams(3),
        name="mla_attn_t",
    )(qt, kn, kr, vt)


def _diff_t_body(qt_ref, k_ref, vt_ref, lq1_ref, lk1_ref, lq2_ref, lk2_ref, sn_ref, o_ref, s_ref, p_ref, a_ref,
                 m_ref, l_ref, acc_ref, *, tq, kb, q_off, s_valid, lam_init):
    i = pl.program_id(2)
    _flash_init(m_ref, l_ref, acc_ref)

    def qk(p, j):
        ks = pl.multiple_of(j * kb, kb)
        return _dot(k_ref[pl.ds(ks, kb), p * 128:(p + 1) * 128], qt_ref[p * 128:(p + 1) * 128, :])

    _causal_flash_t((s_ref, p_ref, a_ref), (m_ref, l_ref, acc_ref), i, vt_ref.shape[0], tq, kb, q_off, s_valid,
                    qk, lambda p, j: vt_ref[j])
    lam = (jnp.exp(jnp.sum(lq1_ref[...] * lk1_ref[...], axis=1, keepdims=True))
           - jnp.exp(jnp.sum(lq2_ref[...] * lk2_ref[...], axis=1, keepdims=True)) + lam_init)
    ot = acc_ref[0] / l_ref[0] - lam * (acc_ref[1] / l_ref[1])
    o = jnp.concatenate([ot[0:128, :].T, ot[128:256, :].T], axis=1)
    o_ref[...] = (_rms(o, sn_ref[...], 2 * C_DIM) * (1.0 - lam_init)).astype(BF16)


def _diff_attn_t(qt, k, vt, lq1, lk1, lq2, lk2, sn, *, tq, kb, q_off, s_valid, lam_init):
    b, _, t = qt.shape
    s_pad = k.shape[1]
    n_kb = s_pad // kb
    vec = lambda c: pl.BlockSpec((1, c), lambda bi, h, i: (0, 0))
    return pl.pallas_call(
        functools.partial(_diff_t_body, tq=tq, kb=kb, q_off=q_off, s_valid=s_valid, lam_init=lam_init),
        grid=(b, C_HEADS, t // tq),
        in_specs=[pl.BlockSpec((None, 256, tq), lambda bi, h, i: (bi, h, i)),
                  pl.BlockSpec((None, s_pad, 256), lambda bi, h, i: (bi, 0, h)),
                  pl.BlockSpec((None, n_kb, 256, kb), lambda bi, h, i: (bi, 0, h, 0)),
                  vec(128), vec(128), vec(128), vec(128), vec(256)],
        out_specs=pl.BlockSpec((None, tq, 256), lambda bi, h, i: (bi, i, h)),
        out_shape=jax.ShapeDtypeStruct((b, t, C_HEADS * 2 * C_DIM), BF16),
        scratch_shapes=_stage_bufs(2, kb, tq) + [pltpu.VMEM((2, 1, tq), F32), pltpu.VMEM((2, 1, tq), F32),
                                                 pltpu.VMEM((2, 256, tq), F32)],
        compiler_params=_cparams(3),
        name="diff_attn_t",
    )(qt, k, vt, lq1, lk1, lq2, lk2, sn)


def _blocked_t(v, kb):
    b, s, c = v.shape
    return jnp.swapaxes(v.reshape(b, s // kb, kb, c), 2, 3)


def _rope_tables(pos, reps):
    p = pos.astype(F32)[:, None]
    inv64 = jnp.power(ROPE_THETA, -jnp.arange(64, dtype=F32) / 64)
    inv32 = jnp.power(ROPE_THETA, -jnp.arange(32, dtype=F32) / 32)
    c, s = jnp.cos(p * inv64), jnp.sin(p * inv64)
    c3, s3 = jnp.cos(p * inv32), jnp.sin(p * inv32)
    z = jnp.zeros_like(s3)
    tabs = {
        "c128": jnp.concatenate([c, c], axis=1),
        "s128": jnp.concatenate([-s, s], axis=1),
        "c64": jnp.concatenate([c3, c3, c3, c3], axis=1),
        "sa64": jnp.concatenate([-s3, z, -s3, z], axis=1),
        "sb64": jnp.concatenate([z, s3, z, s3], axis=1),
    }
    return {k: jnp.tile(v, (reps, 1)) for k, v in tabs.items()}


def _pad_cols(w, n):
    return jnp.pad(w, ((0, 0), (0, n - w.shape[1])))


def _pad_lanes(g, n=128):
    g = g.reshape(1, -1)
    return jnp.pad(g, ((0, 0), (0, n - g.shape[1])))


def _prep_weights(W):
    P = {}
    for nm in ("ffn1", "ffn2"):
        P[nm] = [(W[nm + "_norm"][i].reshape(1, -1), W[nm + "_wg"][i].astype(BF16), W[nm + "_wu"][i].astype(BF16),
                  W[nm + "_wd"][i].astype(BF16)) for i in range(DEPTH)]
    P["mix_norm"] = [W["mix_norm"][i].reshape(1, -1) for i in range(DEPTH)]
    P["a"] = [dict(w=_pad_cols(W["a_w_in"][j], A_IN_PAD).astype(BF16), qn=W["a_q_norm"][j].reshape(1, -1),
                   kn=W["a_k_norm"][j].reshape(1, -1), ikn=_pad_lanes(W["a_idx_k_norm"][j]),
                   wo=W["a_w_out"][j].astype(BF16)) for j in range(W["a_w_in"].shape[0])]
    P["b"] = []
    for j in range(W["b_w_in"].shape[0]):
        wuq = W["b_w_uq"][j].reshape(Q_LORA, B_HEADS, NOPE_DIM + ROPE_DIM)
        wuq_rope = jnp.pad(wuq[:, :, NOPE_DIM:], ((0, 0), (0, 0), (0, 128 - ROPE_DIM)))
        wuq = jnp.concatenate([wuq[:, :, :NOPE_DIM].reshape(Q_LORA, -1), wuq_rope.reshape(Q_LORA, -1)], axis=1)
        P["b"].append(dict(
            w=_pad_cols(W["b_w_in"][j], B_IN_PAD).astype(BF16), qan=W["b_q_a_norm"][j].reshape(1, -1),
            kvan=W["b_kv_a_norm"][j].reshape(1, -1), krn=_pad_lanes(W["b_k_rope_norm"][j]), wuq=wuq.astype(BF16),
            qnn=W["b_q_nope_norm"][j].reshape(1, -1), qrn=_pad_lanes(W["b_q_rope_norm"][j]),
            wukv=W["b_w_ukv"][j].astype(BF16), knn=W["b_k_nope_norm"][j].reshape(1, -1),
            wo=W["b_w_out"][j].astype(BF16)))
    P["c"] = [dict(w=W["c_w_in"][j].astype(BF16), qn=W["c_q_norm"][j].reshape(1, -1),
                   kn=W["c_k_norm"][j].reshape(1, -1), lq1=W["c_lambda_q1"][j].reshape(1, -1),
                   lk1=W["c_lambda_k1"][j].reshape(1, -1), lq2=W["c_lambda_q2"][j].reshape(1, -1),
                   lk2=W["c_lambda_k2"][j].reshape(1, -1), sn=W["c_sub_norm"][j].reshape(1, -1),
                   wo=W["c_w_out"][j].astype(BF16)) for j in range(W["c_w_in"].shape[0])]
    return P


def _with_past(past, new, s_pad):
    x = new if past is None else jnp.concatenate([past.astype(new.dtype), new], axis=1)
    return x if x.shape[1] == s_pad else jnp.pad(x, ((0, 0), (0, s_pad - x.shape[1]), (0, 0)))


def _trunk(x, offset, past, P, cfg):
    b, t, _ = x.shape
    n = b * t
    tm, tq_a, tq, kb, key_major = cfg["tm"], cfg["tq_a"], cfg["tq"], cfg["kb"], cfg["key_major"]
    p_len = 0 if past is None else past["a_k"].shape[2]
    s_valid = p_len + t
    s_pad = -(-s_valid // kb) * kb
    tabs = _rope_tables(offset + jnp.arange(t, dtype=I32), tm // t if tm > t else 1)
    att = dict(kb=kb, q_off=offset, s_valid=s_valid)
    rows = {k: [] for k in ("a_k", "a_v", "a_ik", "b_ckv", "b_kr", "c_k", "c_v")}
    x = x.reshape(n, D_MODEL)
    r3 = lambda a: a.reshape(b, t, a.shape[-1])
    t3 = lambda a: jnp.swapaxes(r3(a), 1, 2)
    pj = lambda nm, j: None if past is None else past[nm][j].reshape(b, p_len, -1)
    for i in range(DEPTH):
        x = _ffn(x, *P["ffn1"][i], tm)
        kind, j = i % N_MIXERS, i // N_MIXERS
        g = P["mix_norm"][i]
        if kind == 0:
            pa = P["a"][j]
            q, k32, v32, kbf, vbf, qi, ki32, kidx, wi = _a_proj(x, g, pa["w"], pa["qn"], pa["kn"], pa["ikn"], tabs, tm)
            rows["a_k"].append(k32.reshape(b, t, A_KV_HEADS, A_HEAD_DIM))
            rows["a_v"].append(v32.reshape(b, t, A_KV_HEADS, A_HEAD_DIM))
            rows["a_ik"].append(ki32.reshape(b, t, IDX_DIM))
            pik = pj("a_ik", j)
            if pik is not None:
                z = jnp.zeros_like(pik)
                pik = jnp.concatenate([pik, z, z, pik], axis=-1)
            kidx_all = _with_past(pik, r3(kidx), s_pad)
            k_all = _with_past(pj("a_k", j), r3(kbf), s_pad)
            v_all = _with_past(pj("a_v", j), r3(vbf), s_pad)
            topk = min(TOPK_MAX, s_valid // 4)
            if key_major:
                o = _dsa_attn_t(t3(q), t3(qi), t3(wi[:, :IDX_HEADS]), kidx_all, k_all, _blocked_t(v_all, kb),
                                tq=tq_a, topk=topk, **att)
            else:
                o = _dsa_attn(r3(q), r3(qi), r3(wi), kidx_all, k_all, v_all, tq=tq_a, topk=topk, **att)
        elif kind == 1:
            pb = P["b"][j]
            q, ckv, kr32, krbf = _b_proj(x, g, pb["w"], pb["qan"], pb["kvan"], pb["krn"], pb["wuq"], pb["qnn"],
                                         pb["qrn"], tabs, tm)
            rows["b_ckv"].append(ckv.reshape(b, t, KV_LORA))
            rows["b_kr"].append(kr32.reshape(b, t, ROPE_DIM))
            ckv_all = _with_past(pj("b_ckv", j), r3(ckv), s_pad)
            pkr = pj("b_kr", j)
            if pkr is not None:
                pkr = jnp.concatenate([pkr, jnp.zeros_like(pkr)], axis=-1)
            kn, v = _kv_up(ckv_all.reshape(b * s_pad, KV_LORA), pb["wukv"], pb["knn"], min(tm, 512))
            kn, v, kr_all = kn.reshape(b, s_pad, -1), v.reshape(b, s_pad, -1), _with_past(pkr, r3(krbf), s_pad)
            if key_major:
                o = _mla_attn_t(t3(q), kn, kr_all, _blocked_t(v, kb), tq=tq, **att)
            else:
                o = _mla_attn(r3(q), kn, kr_all, v, tq=tq, **att)
        else:
            pc = P["c"][j]
            q, k32, v32, kbf, vbf = _c_proj(x, g, pc["w"], pc["qn"], pc["kn"], tabs, tm)
            rows["c_k"].append(k32.reshape(b, t, C_HEADS, 2, C_DIM))
            rows["c_v"].append(v32.reshape(b, t, C_HEADS, 2 * C_DIM))
            k_all = _with_past(pj("c_k", j), r3(kbf), s_pad)
            v_all = _with_past(pj("c_v", j), r3(vbf), s_pad)
            lam = (pc["lq1"], pc["lk1"], pc["lq2"], pc["lk2"], pc["sn"])
            lam_init = 0.8 - 0.6 * math.exp(-0.3 * i)
            if key_major:
                o = _diff_attn_t(t3(q), k_all, _blocked_t(v_all, kb), *lam, tq=tq, lam_init=lam_init, **att)
            else:
                o = _diff_attn(r3(q), k_all, v_all, *lam, tq=tq, lam_init=lam_init, **att)
        wo = (P["a"], P["b"], P["c"])[kind][j]["wo"]
        x = _out_proj(o.reshape(n, -1), wo, x, tm)
        x = _ffn(x, *P["ffn2"][i], tm)
    order = ("a_k", "a_v", "a_ik", "b_ckv", "b_kr", "c_k", "c_v")
    stack = lambda rs: rs[0][None] if len(rs) == 1 else jnp.stack(rs)
    return x.reshape(b, t, D_MODEL), tuple(stack(rows[k]) for k in order)


PROMPT_CFG = dict(tm=512, tq_a=128, tq=256, kb=256, key_major=True)
SAMPLE_CFG = dict(tm=128, tq_a=16, tq=16, kb=1280, key_major=False)


@jax.jit
def _forward(x_prompt, x_sample, past, W):
    P = _prep_weights(W)
    y_p, rows_p = _trunk(x_prompt, 0, None, P, PROMPT_CFG)
    y_s, rows_s = _trunk(x_sample, past["a_k"].shape[2], past, P, SAMPLE_CFG)
    return (y_p, y_s) + rows_p + rows_s


def kernel(x_prompt, x_sample, cache_a_k, cache_a_v, cache_a_idx_k, cache_b_ckv, cache_b_krope, cache_c_k, cache_c_v, ffn1_norm, ffn1_wg, ffn1_wu, ffn1_wd, mix_norm, ffn2_norm, ffn2_wg, ffn2_wu, ffn2_wd, a_w_in, a_q_norm, a_k_norm, a_idx_k_norm, a_w_out, b_w_in, b_q_a_norm, b_kv_a_norm, b_w_uq, b_w_ukv, b_q_nope_norm, b_q_rope_norm, b_k_nope_norm, b_k_rope_norm, b_w_out, c_w_in, c_q_norm, c_k_norm, c_lambda_q1, c_lambda_k1, c_lambda_q2, c_lambda_k2, c_sub_norm, c_w_out):
    W = dict(ffn1_norm=ffn1_norm, ffn1_wg=ffn1_wg, ffn1_wu=ffn1_wu, ffn1_wd=ffn1_wd, mix_norm=mix_norm,
             ffn2_norm=ffn2_norm, ffn2_wg=ffn2_wg, ffn2_wu=ffn2_wu, ffn2_wd=ffn2_wd,
             a_w_in=a_w_in, a_q_norm=a_q_norm, a_k_norm=a_k_norm, a_idx_k_norm=a_idx_k_norm, a_w_out=a_w_out,
             b_w_in=b_w_in, b_q_a_norm=b_q_a_norm, b_kv_a_norm=b_kv_a_norm, b_w_uq=b_w_uq, b_w_ukv=b_w_ukv,
             b_q_nope_norm=b_q_nope_norm, b_q_rope_norm=b_q_rope_norm, b_k_nope_norm=b_k_nope_norm,
             b_k_rope_norm=b_k_rope_norm, b_w_out=b_w_out,
             c_w_in=c_w_in, c_q_norm=c_q_norm, c_k_norm=c_k_norm, c_lambda_q1=c_lambda_q1,
             c_lambda_k1=c_lambda_k1, c_lambda_q2=c_lambda_q2, c_lambda_k2=c_lambda_k2, c_sub_norm=c_sub_norm,
             c_w_out=c_w_out)
    past = dict(a_k=cache_a_k, a_v=cache_a_v, a_ik=cache_a_idx_k, b_ckv=cache_b_ckv, b_kr=cache_b_krope,
                c_k=cache_c_k, c_v=cache_c_v)
    return _forward(x_prompt, x_sample, past, W)
```

```python
import functools
import math

import jax
import jax.numpy as jnp
from jax import lax
from jax.experimental import pallas as pl
from jax.experimental.pallas import tpu as pltpu

F32 = jnp.float32
BF16 = jnp.bfloat16
I32 = jnp.int32

D_MODEL = 1024
DEPTH = 4
CHUNK_SHIFT = 6
N_MIXERS = 3
ROPE_THETA = 10000.0
EPS = 1e-6
D_FF = 2816

A_HEADS = 8
A_KV_HEADS = 2
A_GROUP = A_HEADS // A_KV_HEADS
A_HEAD_DIM = 128
IDX_HEADS = 8
IDX_DIM = 64
TOPK_MAX = 256
LOG2E = math.log2(math.e)
A_SCALE = A_HEAD_DIM ** -0.5 * LOG2E
IDX_W_SCALE = (IDX_HEADS * IDX_DIM) ** -0.5
A_IN = 2120
A_IN_PAD = 2176

B_HEADS = 8
Q_LORA = 384
KV_LORA = 256
NOPE_DIM = 128
ROPE_DIM = 64
V_DIM = 128
B_SCALE = (NOPE_DIM + ROPE_DIM) ** -0.5 * LOG2E
B_IN = 704
B_IN_PAD = 768

C_HEADS = 4
C_DIM = 128
C_SCALE = C_DIM ** -0.5 * LOG2E

LANES = 128
NEG = -1e30
INT_MIN = -(2 ** 31)
VMEM_LIMIT = 56 * 1024 * 1024


def _cparams(n_axes):
    return pltpu.CompilerParams(dimension_semantics=("arbitrary",) * n_axes, vmem_limit_bytes=VMEM_LIMIT)


def _dot(a, b):
    return jnp.dot(a, b, preferred_element_type=F32)


def _dot_nt(a, b):
    return lax.dot_general(a, b, (((1,), (1,)), ((), ())), preferred_element_type=F32)


def _rms(x, g, n):
    ms = jnp.sum(x * x, axis=-1, keepdims=True) * (1.0 / n)
    return x * lax.rsqrt(ms + EPS) * g


def _rope128(x, c, s):
    return x * c + pltpu.roll(x, 64, 1) * s


def _rope64(x, c, sa, sb):
    return x * c + pltpu.roll(x, 96, 1) * sa + pltpu.roll(x, 32, 1) * sb


def _ffn_body(*refs, fc, mixed):
    if mixed:
        attn_ref, wo_ref, x_ref, g_ref, wg_ref, wu_ref, wd_ref, o_ref = refs
        x = x_ref[...] + _dot(attn_ref[...], wo_ref[...])
    else:
        x_ref, g_ref, wg_ref, wu_ref, wd_ref, o_ref = refs
        x = x_ref[...]
    h = _rms(x, g_ref[...], D_MODEL).astype(BF16)
    y = None
    for c in range(D_FF // fc):
        a = _dot(h, wg_ref[:, c * fc:(c + 1) * fc])
        u = _dot(h, wu_ref[:, c * fc:(c + 1) * fc])
        act = (a * jax.nn.sigmoid(a) * u).astype(BF16)
        part = _dot(act, wd_ref[c * fc:(c + 1) * fc, :])
        y = part if y is None else y + part
    o_ref[...] = x + 0.5 * y


def _const_spec(shape):
    nd = len(shape)
    return pl.BlockSpec(shape, lambda *_: (0,) * nd, pipeline_mode=pl.Buffered(1))


def _row_spec(tm, n):
    return pl.BlockSpec((tm, n), lambda i: (i, 0))


def _ffn(x, g, wg, wu, wd, tm, attn=None, wo=None):
    n = x.shape[0]
    mixed = attn is not None
    pre_specs = [_row_spec(tm, attn.shape[1]), _const_spec(wo.shape)] if mixed else []
    return pl.pallas_call(
        functools.partial(_ffn_body, fc=D_FF // 2, mixed=mixed),
        grid=(n // tm,),
        in_specs=pre_specs + [_row_spec(tm, D_MODEL), _const_spec((1, D_MODEL)), _const_spec((D_MODEL, D_FF)),
                              _const_spec((D_MODEL, D_FF)), _const_spec((D_FF, D_MODEL))],
        out_specs=_row_spec(tm, D_MODEL),
        out_shape=jax.ShapeDtypeStruct((n, D_MODEL), F32),
        compiler_params=_cparams(1),
        name="ffn_mixed" if mixed else "ffn",
    )(*((attn, wo) if mixed else ()), x, g, wg, wu, wd)


def _a_proj_body(x_ref, g_ref, w_ref, qn_ref, kn_ref, ikn_ref, c128_ref, s128_ref, c64_ref, sa64_ref, sb64_ref,
                 q_ref, k32_ref, v32_ref, kbf_ref, vbf_ref, qi_ref, ki32_ref, kidx_ref, wi_ref):
    h = _rms(x_ref[...], g_ref[...], D_MODEL).astype(BF16)
    slab = lambda c0, n=256: _dot(h, w_ref[:, c0:c0 + n])
    c128, s128 = c128_ref[...], s128_ref[...]
    c64, sa64, sb64 = c64_ref[...], sa64_ref[...], sb64_ref[...]
    for pr in range(A_HEADS // 2):
        y = slab(pr * 256)
        for a in range(2):
            qh = _rope128(_rms(y[:, a * 128:(a + 1) * 128], qn_ref[...], A_HEAD_DIM), c128, s128)
            q_ref[:, (2 * pr + a) * 128:(2 * pr + a + 1) * 128] = (qh * A_SCALE).astype(BF16)
    y = slab(1024)
    for hd in range(A_KV_HEADS):
        sl = slice(hd * 128, (hd + 1) * 128)
        kh = _rope128(_rms(y[:, sl], kn_ref[...], A_HEAD_DIM), c128, s128)
        k32_ref[:, sl] = kh
        kbf_ref[:, sl] = kh.astype(BF16)
    v = slab(1280)
    v32_ref[...] = v
    vbf_ref[...] = v.astype(BF16)
    for pr in range(IDX_HEADS // 4):
        y = slab(1536 + pr * 256)
        for a in range(2):
            qi_ref[:, (2 * pr + a) * 128:(2 * pr + a + 1) * 128] = _rope64(
                y[:, a * 128:(a + 1) * 128], c64, sa64, sb64).astype(BF16)
    tail = slab(2048, 128)
    lane = lax.broadcasted_iota(I32, tail.shape, 1)
    low = lane < IDX_DIM
    kin = jnp.where(low, tail, 0.0)
    ki = _rope64(_rms(kin, ikn_ref[...], IDX_DIM), c64, sa64, sb64)
    ki = jnp.where(low, ki, 0.0)
    ki32_ref[...] = ki[:, :IDX_DIM]
    kidx_ref[:, 0:128] = ki.astype(BF16)
    kidx_ref[:, 128:256] = pltpu.roll(ki, 64, 1).astype(BF16)
    wi_ref[...] = pltpu.roll(tail, 64, 1) * IDX_W_SCALE


def _tab_spec(tm, t):
    nt = t // tm
    return pl.BlockSpec((tm, LANES), lambda i: (i % nt, 0))


def _a_proj(x, g, w, qn, kn, ikn, tabs, tm):
    n = x.shape[0]
    t = tabs["c128"].shape[0]
    outs = [(1024, BF16), (256, F32), (256, F32), (256, BF16), (256, BF16), (512, BF16), (IDX_DIM, F32),
            (256, BF16), (LANES, F32)]
    return pl.pallas_call(
        _a_proj_body,
        grid=(n // tm,),
        in_specs=[_row_spec(tm, D_MODEL), _const_spec((1, D_MODEL)), _const_spec((D_MODEL, A_IN_PAD)),
                  _const_spec((1, 128)), _const_spec((1, 128)), _const_spec((1, 128))]
        + [_tab_spec(tm, t)] * 5,
        out_specs=[_row_spec(tm, c) for c, _ in outs],
        out_shape=[jax.ShapeDtypeStruct((n, c), dt) for c, dt in outs],
        compiler_params=_cparams(1),
        name="a_proj",
    )(x, g, w, qn, kn, ikn, tabs["c128"], tabs["s128"], tabs["c64"], tabs["sa64"], tabs["sb64"])


def _b_proj_body(x_ref, g_ref, w_ref, qan_ref, kvan_ref, krn_ref, wuq_ref, qnn_ref, qrn_ref,
                 c64_ref, sa64_ref, sb64_ref, q_ref, ckv_ref, kr32_ref, krbf_ref):
    h = _rms(x_ref[...], g_ref[...], D_MODEL).astype(BF16)
    y = _dot(h, w_ref[...])
    c64, sa64, sb64 = c64_ref[...], sa64_ref[...], sb64_ref[...]
    ckv_ref[...] = _rms(y[:, 384:640], kvan_ref[...], KV_LORA)
    kr = _rope64(_rms(y[:, 640:768], krn_ref[...], ROPE_DIM), c64, sa64, sb64)
    kr32_ref[...] = kr[:, :ROPE_DIM]
    krbf_ref[...] = kr.astype(BF16)
    cq = _rms(y[:, 0:384], qan_ref[...], Q_LORA).astype(BF16)
    for hd in range(B_HEADS):
        qq = _dot(cq, wuq_ref[:, hd * 256:(hd + 1) * 256])
        qn = _rms(qq[:, 0:128], qnn_ref[...], NOPE_DIM)
        qr = _rope64(_rms(qq[:, 128:256], qrn_ref[...], ROPE_DIM), c64, sa64, sb64)
        q_ref[:, hd * 256:hd * 256 + 128] = (qn * B_SCALE).astype(BF16)
        q_ref[:, hd * 256 + 128:(hd + 1) * 256] = (qr * B_SCALE).astype(BF16)


def _b_proj(x, g, w, qan, kvan, krn, wuq, qnn, qrn, tabs, tm):
    n = x.shape[0]
    t = tabs["c64"].shape[0]
    outs = [(B_HEADS * 256, BF16), (KV_LORA, F32), (ROPE_DIM, F32), (LANES, BF16)]
    return pl.pallas_call(
        _b_proj_body,
        grid=(n // tm,),
        in_specs=[_row_spec(tm, D_MODEL), _const_spec((1, D_MODEL)), _const_spec((D_MODEL, B_IN_PAD)),
                  _const_spec((1, Q_LORA)), _const_spec((1, KV_LORA)), _const_spec((1, 128)),
                  _const_spec((Q_LORA, 2048)), _const_spec((1, 128)), _const_spec((1, 128))]
        + [_tab_spec(tm, t)] * 3,
        out_specs=[_row_spec(tm, c) for c, _ in outs],
        out_shape=[jax.ShapeDtypeStruct((n, c), dt) for c, dt in outs],
        compiler_params=_cparams(1),
        name="b_proj",
    )(x, g, w, qan, kvan, krn, wuq, qnn, qrn, tabs["c64"], tabs["sa64"], tabs["sb64"])


def _kv_up_body(ckv_ref, w_ref, knn_ref, kn_ref, v_ref):
    y = _dot(ckv_ref[...].astype(BF16), w_ref[...])
    for hd in range(B_HEADS):
        kn_ref[:, hd * 128:(hd + 1) * 128] = _rms(y[:, hd * 256:hd * 256 + 128], knn_ref[...], NOPE_DIM).astype(BF16)
        v_ref[:, hd * 128:(hd + 1) * 128] = y[:, hd * 256 + 128:(hd + 1) * 256].astype(BF16)


def _kv_up(ckv, w, knn, tm):
    n = ckv.shape[0]
    return pl.pallas_call(
        _kv_up_body,
        grid=(n // tm,),
        in_specs=[_row_spec(tm, KV_LORA), _const_spec((KV_LORA, 2048)), _const_spec((1, 128))],
        out_specs=[_row_spec(tm, 1024), _row_spec(tm, 1024)],
        out_shape=[jax.ShapeDtypeStruct((n, 1024), BF16)] * 2,
        compiler_params=_cparams(1),
        name="kv_up",
    )(ckv, w, knn)


def _c_proj_body(x_ref, g_ref, w_ref, qn_ref, kn_ref, c128_ref, s128_ref,
                 q_ref, k32_ref, v32_ref, kbf_ref, vbf_ref):
    h = _rms(x_ref[...], g_ref[...], D_MODEL).astype(BF16)
    c128, s128 = c128_ref[...], s128_ref[...]
    for hd in range(C_HEADS):
        cols = slice(hd * 256, (hd + 1) * 256)
        yq = _dot(h, w_ref[:, hd * 256:(hd + 1) * 256])
        yk = _dot(h, w_ref[:, 1024 + hd * 256:1024 + (hd + 1) * 256])
        for a in range(2):
            sl = slice(hd * 256 + a * 128, hd * 256 + (a + 1) * 128)
            qh = _rope128(_rms(yq[:, a * 128:(a + 1) * 128], qn_ref[...], C_DIM), c128, s128)
            q_ref[:, sl] = (qh * C_SCALE).astype(BF16)
            kh = _rope128(_rms(yk[:, a * 128:(a + 1) * 128], kn_ref[...], C_DIM), c128, s128)
            k32_ref[:, sl] = kh
            kbf_ref[:, sl] = kh.astype(BF16)
        v = _dot(h, w_ref[:, 2048 + hd * 256:2048 + (hd + 1) * 256])
        v32_ref[:, cols] = v
        vbf_ref[:, cols] = v.astype(BF16)


def _c_proj(x, g, w, qn, kn, tabs, tm):
    n = x.shape[0]
    t = tabs["c128"].shape[0]
    outs = [(1024, BF16), (1024, F32), (1024, F32), (1024, BF16), (1024, BF16)]
    return pl.pallas_call(
        _c_proj_body,
        grid=(n // tm,),
        in_specs=[_row_spec(tm, D_MODEL), _const_spec((1, D_MODEL)), _const_spec((D_MODEL, 3072)),
                  _const_spec((1, 128)), _const_spec((1, 128))] + [_tab_spec(tm, t)] * 2,
        out_specs=[_row_spec(tm, c) for c, _ in outs],
        out_shape=[jax.ShapeDtypeStruct((n, c), dt) for c, dt in outs],
        compiler_params=_cparams(1),
        name="c_proj",
    )(x, g, w, qn, kn, tabs["c128"], tabs["s128"])


def _n_key_blocks(i, tq, kb, q_off, s_valid):
    last_chunk = lax.shift_right_logical(q_off + (i + 1) * tq - 1, CHUNK_SHIFT)
    kend = jnp.minimum((last_chunk + 1) << CHUNK_SHIFT, s_valid)
    return lax.div(kend + kb - 1, jnp.int32(kb))


def _q_chunk(i, tq, q_off):
    row = lax.broadcasted_iota(I32, (tq, 1), 0)
    return lax.shift_right_logical(q_off + i * tq + row, CHUNK_SHIFT)


def _k_chunk(ks, kb, s_valid):
    kpos = ks + lax.broadcasted_iota(I32, (1, kb), 1)
    return jnp.where(kpos < s_valid, lax.shift_right_logical(kpos, CHUNK_SHIFT), 2 ** 30)


def _flash_step(s, v, m_ref, l_ref, acc_ref):
    m_prev = m_ref[...]
    m_new = jnp.maximum(m_prev, jnp.max(s, axis=1, keepdims=True))
    alpha = jnp.exp2(m_prev - m_new)
    p = jnp.exp2(s - m_new)
    l_ref[...] = alpha * l_ref[...] + jnp.sum(p, axis=1, keepdims=True)
    acc_ref[...] = alpha * acc_ref[...] + _dot(p.astype(BF16), v)
    m_ref[...] = m_new


def _key_limit(i, tq, q_off, s_valid):
    qpos = q_off + i * tq + lax.broadcasted_iota(I32, (1, tq), 1)
    return jnp.minimum((lax.shift_right_logical(qpos, CHUNK_SHIFT) + 1) << CHUNK_SHIFT, s_valid)


def _n_full_blocks(i, tq, kb, q_off, s_valid):
    first_chunk = lax.shift_right_logical(q_off + i * tq, CHUNK_SHIFT)
    kend = jnp.minimum((first_chunk + 1) << CHUNK_SHIFT, s_valid)
    return lax.shift_right_logical(kend, int(math.log2(kb)))


def _flash_init(m_ref, l_ref, acc_ref):
    m_ref[...] = jnp.full(m_ref.shape, NEG, F32)
    l_ref[...] = jnp.zeros(l_ref.shape, F32)
    acc_ref[...] = jnp.zeros(acc_ref.shape, F32)


def _dsa_body(q_ref, qi_ref, wi_ref, kidx_ref, k_ref, v_ref, o_ref, keys_ref, m_ref, l_ref, acc_ref,
              *, tq, kb, q_off, s_valid, topk):
    i = pl.program_id(1)
    n_blk = _n_key_blocks(i, tq, kb, q_off, s_valid)
    qc = _q_chunk(i, tq, q_off)
    wi = wi_ref[...]
    wcol = [wi[:, hd:hd + 1] for hd in range(IDX_HEADS)]

    def score_block(j, carry):
        ks = pl.multiple_of(j * kb, kb)
        k_lo = kidx_ref[pl.ds(ks, kb), 0:128]
        k_hi = kidx_ref[pl.ds(ks, kb), 128:256]
        sc = jnp.zeros((tq, kb), F32)
        for p in range(IDX_HEADS // 2):
            qp = qi_ref[:, p * 128:(p + 1) * 128]
            sc = sc + wcol[2 * p] * jnp.maximum(_dot_nt(qp, k_lo), 0.0)
            sc = sc + wcol[2 * p + 1] * jnp.maximum(_dot_nt(qp, k_hi), 0.0)
        bits = lax.bitcast_convert_type(sc, I32)
        key = bits ^ ((bits >> 31) & 0x7FFFFFFF)
        keys_ref[j] = jnp.where(_k_chunk(ks, kb, s_valid) <= qc, key, INT_MIN)
        return carry

    lax.fori_loop(0, n_blk, score_block, 0)

    def count(pred):
        def blk(j, c):
            x = jnp.where(pred(keys_ref[j], j * kb), 1.0, 0.0)
            part = x[:, 0:LANES]
            for g in range(1, kb // LANES):
                part = part + x[:, g * LANES:(g + 1) * LANES]
            return c + part
        c = lax.fori_loop(0, n_blk, blk, jnp.zeros((tq, LANES), F32))
        return jnp.sum(c, axis=1, keepdims=True)

    kf = float(topk)

    def bit_step(b, t):
        cand = t + jnp.left_shift(jnp.int32(1), 31 - b)
        cnt = count(lambda kblk, _: kblk >= cand)
        return jnp.where(cnt >= kf, cand, t)

    thr = lax.fori_loop(0, 32, bit_step, jnp.full((tq, 1), INT_MIN, I32))
    need = kf - count(lambda kblk, _: kblk > thr)
    n_eq = count(lambda kblk, _: kblk == thr)
    partial = jnp.logical_and(n_eq > need, thr != INT_MIN)

    @pl.when(jnp.max(jnp.where(partial, 1.0, 0.0)) > 0.0)
    def _():
        lane = lax.broadcasted_iota(I32, (1, kb), 1)

        def idx_step(b, x):
            cand = x + jnp.left_shift(jnp.int32(1), b)
            cnt = count(lambda kblk, k0: jnp.logical_and(kblk == thr, k0 + lane < cand))
            return jnp.where(cnt < need, cand, x)

        nbits = max(1, int(s_valid - 1).bit_length())
        cut = lax.fori_loop(0, nbits, lambda b, x: idx_step(nbits - 1 - b, x), jnp.zeros((tq, 1), I32))

        def demote(j, carry):
            kblk = keys_ref[j]
            drop = jnp.logical_and(jnp.logical_and(kblk == thr, j * kb + lane > cut), partial)
            keys_ref[j] = jnp.where(drop, kblk - 1, kblk)
            return carry

        lax.fori_loop(0, n_blk, demote, 0)

    thr_sel = jnp.maximum(thr, INT_MIN + 1)

    _flash_init(m_ref, l_ref, acc_ref)

    def attend(j, carry):
        ks = pl.multiple_of(j * kb, kb)
        bias = jnp.where(keys_ref[j] >= thr_sel, 0.0, NEG)
        for g in range(A_KV_HEADS):
            qg = jnp.concatenate([q_ref[:, (g * A_GROUP + a) * 128:(g * A_GROUP + a + 1) * 128]
                                  for a in range(A_GROUP)], axis=0)
            s = _dot_nt(qg, k_ref[pl.ds(ks, kb), g * 128:(g + 1) * 128])
            s = (s.reshape(A_GROUP, tq, kb) + bias[None]).reshape(A_GROUP * tq, kb)
            _flash_step(s, v_ref[pl.ds(ks, kb), g * 128:(g + 1) * 128], m_ref.at[g], l_ref.at[g], acc_ref.at[g])
        return carry

    lax.fori_loop(0, n_blk, attend, 0)
    for g in range(A_KV_HEADS):
        o = acc_ref[g] / l_ref[g]
        for a in range(A_GROUP):
            hd = g * A_GROUP + a
            o_ref[:, hd * 128:(hd + 1) * 128] = o[a * tq:(a + 1) * tq].astype(BF16)


def _dsa_attn(q, qi, wi, kidx, k, v, *, tq, kb, q_off, s_valid, topk):
    b, t, _ = q.shape
    s_pad = k.shape[1]
    qspec = lambda c: pl.BlockSpec((None, tq, c), lambda bi, i: (bi, i, 0))
    kspec = lambda c: pl.BlockSpec((None, s_pad, c), lambda bi, i: (bi, 0, 0))
    rows = A_GROUP * tq
    return pl.pallas_call(
        functools.partial(_dsa_body, tq=tq, kb=kb, q_off=q_off, s_valid=s_valid, topk=topk),
        grid=(b, t // tq),
        in_specs=[qspec(1024), qspec(512), qspec(LANES), kspec(256), kspec(256), kspec(256)],
        out_specs=qspec(1024),
        out_shape=jax.ShapeDtypeStruct((b, t, 1024), BF16),
        scratch_shapes=[pltpu.VMEM((s_pad // kb, tq, kb), I32), pltpu.VMEM((A_KV_HEADS, rows, 1), F32),
                        pltpu.VMEM((A_KV_HEADS, rows, 1), F32), pltpu.VMEM((A_KV_HEADS, rows, 128), F32)],
        compiler_params=_cparams(2),
        name="dsa_attn",
    )(q, qi, wi, kidx, k, v)


def _mla_body(q_ref, kn_ref, kr_ref, v_ref, o_ref, m_ref, l_ref, acc_ref, *, tq, kb, q_off, s_valid):
    i = pl.program_id(2)
    n_blk = _n_key_blocks(i, tq, kb, q_off, s_valid)
    qc = _q_chunk(i, tq, q_off)
    _flash_init(m_ref, l_ref, acc_ref)

    def attend(j, carry):
        ks = pl.multiple_of(j * kb, kb)
        kcat = jnp.concatenate([kn_ref[pl.ds(ks, kb), :], kr_ref[pl.ds(ks, kb), :]], axis=1)
        s = _dot_nt(q_ref[...], kcat)
        s = jnp.where(_k_chunk(ks, kb, s_valid) <= qc, s, NEG)
        _flash_step(s, v_ref[pl.ds(ks, kb), :], m_ref, l_ref, acc_ref)
        return carry

    lax.fori_loop(0, n_blk, attend, 0)
    o_ref[...] = (acc_ref[...] / l_ref[...]).astype(BF16)


def _mla_attn(q, kn, kr, v, *, tq, kb, q_off, s_valid):
    b, t, _ = q.shape
    s_pad = kn.shape[1]
    return pl.pallas_call(
        functools.partial(_mla_body, tq=tq, kb=kb, q_off=q_off, s_valid=s_valid),
        grid=(b, B_HEADS, t // tq),
        in_specs=[pl.BlockSpec((None, tq, 256), lambda bi, h, i: (bi, i, h)),
                  pl.BlockSpec((None, s_pad, 128), lambda bi, h, i: (bi, 0, h)),
                  pl.BlockSpec((None, s_pad, 128), lambda bi, h, i: (bi, 0, 0)),
                  pl.BlockSpec((None, s_pad, 128), lambda bi, h, i: (bi, 0, h))],
        out_specs=pl.BlockSpec((None, tq, 128), lambda bi, h, i: (bi, i, h)),
        out_shape=jax.ShapeDtypeStruct((b, t, B_HEADS * V_DIM), BF16),
        scratch_shapes=[pltpu.VMEM((tq, 1), F32), pltpu.VMEM((tq, 1), F32), pltpu.VMEM((tq, 128), F32)],
        compiler_params=_cparams(3),
        name="mla_attn",
    )(q, kn, kr, v)


def _diff_body(q_ref, k_ref, v_ref, lq1_ref, lk1_ref, lq2_ref, lk2_ref, sn_ref, o_ref, m_ref, l_ref, acc_ref,
               *, tq, kb, q_off, s_valid, lam_init):
    i = pl.program_id(2)
    n_blk = _n_key_blocks(i, tq, kb, q_off, s_valid)
    qc = _q_chunk(i, tq, q_off)
    _flash_init(m_ref, l_ref, acc_ref)

    def attend(j, carry):
        ks = pl.multiple_of(j * kb, kb)
        ok = _k_chunk(ks, kb, s_valid) <= qc
        v = v_ref[pl.ds(ks, kb), :]
        for p in range(2):
            s = _dot_nt(q_ref[:, p * 128:(p + 1) * 128], k_ref[pl.ds(ks, kb), p * 128:(p + 1) * 128])
            s = jnp.where(ok, s, NEG)
            _flash_step(s, v, m_ref.at[p], l_ref.at[p], acc_ref.at[p])
        return carry

    lax.fori_loop(0, n_blk, attend, 0)
    lam = (jnp.exp(jnp.sum(lq1_ref[...] * lk1_ref[...], axis=1, keepdims=True))
           - jnp.exp(jnp.sum(lq2_ref[...] * lk2_ref[...], axis=1, keepdims=True)) + lam_init)
    o = acc_ref[0] / l_ref[0] - lam * (acc_ref[1] / l_ref[1])
    o_ref[...] = (_rms(o, sn_ref[...], 2 * C_DIM) * (1.0 - lam_init)).astype(BF16)


def _diff_attn(q, k, v, lq1, lk1, lq2, lk2, sn, *, tq, kb, q_off, s_valid, lam_init):
    b, t, _ = q.shape
    s_pad = k.shape[1]
    vec = lambda c: pl.BlockSpec((1, c), lambda bi, h, i: (0, 0))
    return pl.pallas_call(
        functools.partial(_diff_body, tq=tq, kb=kb, q_off=q_off, s_valid=s_valid, lam_init=lam_init),
        grid=(b, C_HEADS, t // tq),
        in_specs=[pl.BlockSpec((None, tq, 256), lambda bi, h, i: (bi, i, h)),
                  pl.BlockSpec((None, s_pad, 256), lambda bi, h, i: (bi, 0, h)),
                  pl.BlockSpec((None, s_pad, 256), lambda bi, h, i: (bi, 0, h)),
                  vec(128), vec(128), vec(128), vec(128), vec(256)],
        out_specs=pl.BlockSpec((None, tq, 256), lambda bi, h, i: (bi, i, h)),
        out_shape=jax.ShapeDtypeStruct((b, t, C_HEADS * 2 * C_DIM), BF16),
        scratch_shapes=[pltpu.VMEM((2, tq, 1), F32), pltpu.VMEM((2, tq, 1), F32), pltpu.VMEM((2, tq, 256), F32)],
        compiler_params=_cparams(3),
        name="diff_attn",
    )(q, k, v, lq1, lk1, lq2, lk2, sn)


SCORE_BLOCKS = 2
SEARCH_BLOCKS = 4
SEARCH_BITS_UNCHECKED = 16
SEARCH_BITS_PER_CHECK = 4


def _stage_bufs(n_chain, kb, r):
    return [pltpu.VMEM((n_chain, 2, kb, r), F32), pltpu.VMEM((n_chain, 2, kb, r), BF16),
            pltpu.VMEM((n_chain, 2, 1, r), F32)]


def _staged_flash_t(bufs, state, lo, hi, n_kb, qk, prep, vt_of, first=True, last=True):
    s_ref, p_ref, a_ref = bufs
    m_ref, l_ref, acc_ref = state
    n_chain = s_ref.shape[0]
    clamp = lambda j: jnp.clip(j, 0, n_kb - 1)

    def softmax(c, j, slot):
        st = prep(c, j, s_ref[c, slot])
        m_prev = m_ref[c]
        m_new = jnp.maximum(m_prev, jnp.max(st, axis=0, keepdims=True))
        alpha = jnp.exp2(m_prev - m_new)
        p = jnp.exp2(st - m_new)
        l_ref[c] = alpha * l_ref[c] + jnp.sum(p, axis=0, keepdims=True)
        m_ref[c] = m_new
        p_ref[c, slot] = p.astype(BF16)
        a_ref[c, slot] = alpha

    def values(c, j, slot):
        acc_ref[c] = a_ref[c, slot] * acc_ref[c] + _dot(vt_of(c, clamp(j)), p_ref[c, slot])

    if first:
        for c in range(n_chain):
            s_ref[c, 0] = qk(c, clamp(lo))
            p_ref[c, 1] = jnp.zeros(p_ref.shape[2:], BF16)
            a_ref[c, 1] = jnp.ones(a_ref.shape[2:], F32)

    def turn(t, carry):
        j0 = lo + 2 * t
        for c in range(n_chain):
            s_ref[c, 1] = qk(c, clamp(j0 + 1))
        for c in range(n_chain):
            softmax(c, j0, 0)
        for c in range(n_chain):
            values(c, j0 - 1, 1)
        for c in range(n_chain):
            s_ref[c, 0] = qk(c, clamp(j0 + 2))
        for c in range(n_chain):
            softmax(c, j0 + 1, 1)
        for c in range(n_chain):
            values(c, j0, 0)
        return carry

    n_turn = lax.shift_right_logical(hi - lo + 1, 1)
    lax.fori_loop(0, n_turn, turn, 0)
    if last:
        for c in range(n_chain):
            values(c, lo + 2 * n_turn - 1, 1)


def _dsa_t_body(qt_ref, qit_ref, wit_ref, kidx_ref, k_ref, vt_ref, o_ref, keys_ref, s_ref, p_ref, a_ref,
                m_ref, l_ref, acc_ref, *, tq, kb, q_off, s_valid, topk):
    i = pl.program_id(1)
    n_blk = _n_key_blocks(i, tq, kb, q_off, s_valid)
    n_kb = keys_ref.shape[0]
    n_sb = lax.shift_right_logical(n_blk + SEARCH_BLOCKS - 1, int(math.log2(SEARCH_BLOCKS)))
    limit = _key_limit(i, tq, q_off, s_valid)
    kidx0 = lax.broadcasted_iota(I32, (kb, tq), 0)
    wit = wit_ref[...]
    wrow = [wit[hd:hd + 1, :] for hd in range(IDX_HEADS)]

    def score_blocks(jj, carry):
        ks = pl.multiple_of(jj * (SCORE_BLOCKS * kb), SCORE_BLOCKS * kb)
        rows = pl.ds(ks, SCORE_BLOCKS * kb)
        q_all = jnp.concatenate([qit_ref[p * 128:(p + 1) * 128, :] for p in range(IDX_HEADS // 2)], axis=1)
        s_lo = _dot(kidx_ref[rows, 0:128], q_all)
        s_hi = _dot(kidx_ref[rows, 128:256], q_all)
        for u in range(SCORE_BLOCKS):
            blk = slice(u * kb, (u + 1) * kb)
            sc = jnp.zeros((kb, tq), F32)
            for p in range(IDX_HEADS // 2):
                sc = sc + wrow[2 * p] * jnp.maximum(s_lo[blk, p * tq:(p + 1) * tq], 0.0)
                sc = sc + wrow[2 * p + 1] * jnp.maximum(s_hi[blk, p * tq:(p + 1) * tq], 0.0)
            bits = lax.bitcast_convert_type(sc, I32)
            key = bits ^ ((bits >> 31) & 0x7FFFFFFF)
            keys_ref[jj * SCORE_BLOCKS + u] = jnp.where(kidx0 < limit - (ks + u * kb), key, INT_MIN)
        return carry

    def pad_block(j, carry):
        keys_ref[j] = jnp.full((kb, tq), INT_MIN, I32)
        return carry

    n_scored = lax.shift_right_logical(n_blk + SCORE_BLOCKS - 1, int(math.log2(SCORE_BLOCKS)))
    lax.fori_loop(0, n_scored, score_blocks, 0)
    lax.fori_loop(n_scored * SCORE_BLOCKS, n_sb * SEARCH_BLOCKS, pad_block, 0)

    def count(pred):
        def group(jj, cs):
            out = []
            for u in range(SEARCH_BLOCKS):
                j = jj * SEARCH_BLOCKS + u
                x = jnp.where(pred(keys_ref[j], j * kb), 1.0, 0.0)
                out.append(cs[u] + jnp.sum(x.reshape(kb // 8, 8, tq), axis=0))
            return tuple(out)
        cs = lax.fori_loop(0, n_sb, group, tuple(jnp.zeros((8, tq), F32) for _ in range(SEARCH_BLOCKS)))
        return jnp.sum(functools.reduce(lambda a, b: a + b, cs), axis=0, keepdims=True)

    kf = float(topk)

    def bit_step(b, state):
        t, n_ge = state
        cand = t + jnp.left_shift(jnp.int32(1), 31 - b)
        cnt = count(lambda kblk, _: kblk >= cand)
        take = cnt >= kf
        return jnp.where(take, cand, t), jnp.where(take, cnt, n_ge)

    def unsettled(n_ge):
        open_ = jnp.logical_and(n_ge != kf, limit.astype(F32) > kf)
        return jnp.max(jnp.where(open_, 1.0, 0.0)) > 0.0

    def more_bits(state):
        b, _, n_ge = state
        return jnp.logical_and(b < 32, unsettled(n_ge))

    def four_bits(state):
        b, t, n_ge = state
        for u in range(SEARCH_BITS_PER_CHECK):
            t, n_ge = bit_step(b + u, (t, n_ge))
        return b + SEARCH_BITS_PER_CHECK, t, n_ge

    start = (jnp.full((1, tq), INT_MIN, I32), jnp.full((1, tq), float(n_kb * kb), F32))
    head = lax.fori_loop(0, SEARCH_BITS_UNCHECKED, bit_step, start)
    _, thr, _ = lax.while_loop(more_bits, four_bits, (jnp.int32(SEARCH_BITS_UNCHECKED),) + head)
    need = kf - count(lambda kblk, _: kblk > thr)
    n_eq = count(lambda kblk, _: kblk == thr)
    partial = jnp.logical_and(n_eq > need, thr != INT_MIN)

    @pl.when(jnp.max(jnp.where(partial, 1.0, 0.0)) > 0.0)
    def _():
        def idx_step(b, x):
            cand = x + jnp.left_shift(jnp.int32(1), b)
            cnt = count(lambda kblk, k0: jnp.logical_and(kblk == thr, kidx0 < cand - k0))
            return jnp.where(cnt < need, cand, x)

        nbits = max(1, int(s_valid - 1).bit_length())
        cut = lax.fori_loop(0, nbits, lambda b, x: idx_step(nbits - 1 - b, x), jnp.zeros((1, tq), I32))

        def demote(j, carry):
            kblk = keys_ref[j]
            drop = jnp.logical_and(jnp.logical_and(kblk == thr, kidx0 > cut - j * kb), partial)
            keys_ref[j] = jnp.where(drop, kblk - 1, kblk)
            return carry

        lax.fori_loop(0, n_blk, demote, 0)

    thr_sel = jnp.maximum(thr, INT_MIN + 1)

    _flash_init(m_ref, l_ref, acc_ref)
    n_pair = A_HEADS // 2

    def to_bias(j, carry):
        keys_ref[j] = lax.bitcast_convert_type(jnp.where(keys_ref[j] >= thr_sel, 0.0, NEG), I32)
        return carry

    lax.fori_loop(0, n_sb * SEARCH_BLOCKS, to_bias, 0)
    for g in range(A_KV_HEADS):
        def qk(c, j, g=g):
            p = g * (A_GROUP // 2) + c
            qp = qt_ref[2 * p * 128:(2 * p + 2) * 128, :]
            qp = jnp.concatenate([qp[0:128, :], qp[128:256, :]], axis=1)
            return _dot(k_ref[pl.ds(pl.multiple_of(j * kb, kb), kb), g * 128:(g + 1) * 128], qp)

        def prep(c, j, st):
            bias = lax.bitcast_convert_type(keys_ref[j], F32)
            return st + jnp.concatenate([bias, bias], axis=1)

        def vt_of(c, j, g=g):
            return vt_ref[j, g * 128:(g + 1) * 128, :]

        chains = pl.ds(g * (A_GROUP // 2), A_GROUP // 2)
        _staged_flash_t((s_ref.at[chains], p_ref.at[chains], a_ref.at[chains]),
                        (m_ref.at[chains], l_ref.at[chains], acc_ref.at[chains]), 0, n_blk, n_kb, qk, prep, vt_of)
    for p in range(n_pair):
        ot = acc_ref[p] / l_ref[p]
        for a in range(2):
            hd = 2 * p + a
            o_ref[:, hd * 128:(hd + 1) * 128] = ot[:, a * tq:(a + 1) * tq].T.astype(BF16)


def _dsa_attn_t(qt, qit, wit, kidx, k, vt, *, tq, kb, q_off, s_valid, topk):
    b, _, t = qt.shape
    s_pad = k.shape[1]
    n_kb = s_pad // kb
    qtspec = lambda c: pl.BlockSpec((None, c, tq), lambda bi, i: (bi, 0, i))
    kspec = lambda c: pl.BlockSpec((None, s_pad, c), lambda bi, i: (bi, 0, 0))
    n_pair = A_HEADS // 2
    return pl.pallas_call(
        functools.partial(_dsa_t_body, tq=tq, kb=kb, q_off=q_off, s_valid=s_valid, topk=topk),
        grid=(b, t // tq),
        in_specs=[qtspec(1024), qtspec(512), qtspec(IDX_HEADS),
                  kspec(256), kspec(256), pl.BlockSpec((None, n_kb, 256, kb), lambda bi, i: (bi, 0, 0, 0))],
        out_specs=pl.BlockSpec((None, tq, 1024), lambda bi, i: (bi, i, 0)),
        out_shape=jax.ShapeDtypeStruct((b, t, 1024), BF16),
        scratch_shapes=[pltpu.VMEM((n_kb, kb, tq), I32)] + _stage_bufs(n_pair, kb, 2 * tq)
        + [pltpu.VMEM((n_pair, 1, 2 * tq), F32), pltpu.VMEM((n_pair, 1, 2 * tq), F32),
           pltpu.VMEM((n_pair, 128, 2 * tq), F32)],
        compiler_params=_cparams(2),
        name="dsa_attn_t",
    )(qt, qit, wit, kidx, k, vt)


def _causal_flash_t(bufs, state, i, n_kb, tq, kb, q_off, s_valid, qk, vt_of):
    n_blk = _n_key_blocks(i, tq, kb, q_off, s_valid)
    n_plain = _n_full_blocks(i, tq, kb, q_off, s_valid) & -2
    limit = _key_limit(i, tq, q_off, s_valid)

    def masked(c, j, st):
        return jnp.where(lax.broadcasted_iota(I32, (kb, tq), 0) < limit - j * kb, st, NEG)

    _staged_flash_t(bufs, state, 0, n_plain, n_kb, qk, lambda c, j, st: st, vt_of, last=False)
    _staged_flash_t(bufs, state, n_plain, n_blk, n_kb, qk, masked, vt_of, first=False)


def _mla_t_body(qt_ref, kn_ref, kr_ref, vt_ref, o_ref, s_ref, p_ref, a_ref, m_ref, l_ref, acc_ref,
                *, tq, kb, q_off, s_valid):
    i = pl.program_id(2)
    _flash_init(m_ref, l_ref, acc_ref)

    def qk(a, j):
        ks = pl.multiple_of(j * kb, kb)
        kcat = jnp.concatenate([kn_ref[pl.ds(ks, kb), a * 128:(a + 1) * 128], kr_ref[pl.ds(ks, kb), :]], axis=1)
        return _dot(kcat, qt_ref[a * 256:(a + 1) * 256, :])

    def vt_of(a, j):
        return vt_ref[j, a * 128:(a + 1) * 128, :]

    _causal_flash_t((s_ref, p_ref, a_ref), (m_ref, l_ref, acc_ref), i, vt_ref.shape[0], tq, kb, q_off, s_valid,
                    qk, vt_of)
    for a in range(2):
        o_ref[:, a * 128:(a + 1) * 128] = (acc_ref[a] / l_ref[a]).T.astype(BF16)


def _mla_attn_t(qt, kn, kr, vt, *, tq, kb, q_off, s_valid):
    b, _, t = qt.shape
    s_pad = kn.shape[1]
    n_kb = s_pad // kb
    return pl.pallas_call(
        functools.partial(_mla_t_body, tq=tq, kb=kb, q_off=q_off, s_valid=s_valid),
        grid=(b, B_HEADS // 2, t // tq),
        in_specs=[pl.BlockSpec((None, 512, tq), lambda bi, h, i: (bi, h, i)),
                  pl.BlockSpec((None, s_pad, 256), lambda bi, h, i: (bi, 0, h)),
                  pl.BlockSpec((None, s_pad, 128), lambda bi, h, i: (bi, 0, 0)),
                  pl.BlockSpec((None, n_kb, 256, kb), lambda bi, h, i: (bi, 0, h, 0))],
        out_specs=pl.BlockSpec((None, tq, 256), lambda bi, h, i: (bi, i, h)),
        out_shape=jax.ShapeDtypeStruct((b, t, B_HEADS * V_DIM), BF16),
        scratch_shapes=_stage_bufs(2, kb, tq) + [pltpu.VMEM((2, 1, tq), F32), pltpu.VMEM((2, 1, tq), F32),
                                                 pltpu.VMEM((2, 128, tq), F32)],
        compiler_params=_cparams(3),
        name="mla_attn_t",
    )(qt, kn, kr, vt)


def _diff_t_body(qt_ref, k_ref, vt_ref, lq1_ref, lk1_ref, lq2_ref, lk2_ref, sn_ref, o_ref, s_ref, p_ref, a_ref,
                 m_ref, l_ref, acc_ref, *, tq, kb, q_off, s_valid, lam_init):
    i = pl.program_id(2)
    _flash_init(m_ref, l_ref, acc_ref)

    def qk(p, j):
        ks = pl.multiple_of(j * kb, kb)
        return _dot(k_ref[pl.ds(ks, kb), p * 128:(p + 1) * 128], qt_ref[p * 128:(p + 1) * 128, :])

    _causal_flash_t((s_ref, p_ref, a_ref), (m_ref, l_ref, acc_ref), i, vt_ref.shape[0], tq, kb, q_off, s_valid,
                    qk, lambda p, j: vt_ref[j])
    lam = (jnp.exp(jnp.sum(lq1_ref[...] * lk1_ref[...], axis=1, keepdims=True))
           - jnp.exp(jnp.sum(lq2_ref[...] * lk2_ref[...], axis=1, keepdims=True)) + lam_init)
    ot = acc_ref[0] / l_ref[0] - lam * (acc_ref[1] / l_ref[1])
    o = jnp.concatenate([ot[0:128, :].T, ot[128:256, :].T], axis=1)
    o_ref[...] = (_rms(o, sn_ref[...], 2 * C_DIM) * (1.0 - lam_init)).astype(BF16)


def _diff_attn_t(qt, k, vt, lq1, lk1, lq2, lk2, sn, *, tq, kb, q_off, s_valid, lam_init):
    b, _, t = qt.shape
    s_pad = k.shape[1]
    n_kb = s_pad // kb
    vec = lambda c: pl.BlockSpec((1, c), lambda bi, h, i: (0, 0))
    return pl.pallas_call(
        functools.partial(_diff_t_body, tq=tq, kb=kb, q_off=q_off, s_valid=s_valid, lam_init=lam_init),
        grid=(b, C_HEADS, t // tq),
        in_specs=[pl.BlockSpec((None, 256, tq), lambda bi, h, i: (bi, h, i)),
                  pl.BlockSpec((None, s_pad, 256), lambda bi, h, i: (bi, 0, h)),
                  pl.BlockSpec((None, n_kb, 256, kb), lambda bi, h, i: (bi, 0, h, 0)),
                  vec(128), vec(128), vec(128), vec(128), vec(256)],
        out_specs=pl.BlockSpec((None, tq, 256), lambda bi, h, i: (bi, i, h)),
        out_shape=jax.ShapeDtypeStruct((b, t, C_HEADS * 2 * C_DIM), BF16),
        scratch_shapes=_stage_bufs(2, kb, tq) + [pltpu.VMEM((2, 1, tq), F32), pltpu.VMEM((2, 1, tq), F32),
                                                 pltpu.VMEM((2, 256, tq), F32)],
        compiler_params=_cparams(3),
        name="diff_attn_t",
    )(qt, k, vt, lq1, lk1, lq2, lk2, sn)


def _blocked_t(v, kb):
    b, s, c = v.shape
    return jnp.swapaxes(v.reshape(b, s // kb, kb, c), 2, 3)


def _rope_tables(pos, reps):
    p = pos.astype(F32)[:, None]
    inv64 = jnp.power(ROPE_THETA, -jnp.arange(64, dtype=F32) / 64)
    inv32 = jnp.power(ROPE_THETA, -jnp.arange(32, dtype=F32) / 32)
    c, s = jnp.cos(p * inv64), jnp.sin(p * inv64)
    c3, s3 = jnp.cos(p * inv32), jnp.sin(p * inv32)
    z = jnp.zeros_like(s3)
    tabs = {
        "c128": jnp.concatenate([c, c], axis=1),
        "s128": jnp.concatenate([-s, s], axis=1),
        "c64": jnp.concatenate([c3, c3, c3, c3], axis=1),
        "sa64": jnp.concatenate([-s3, z, -s3, z], axis=1),
        "sb64": jnp.concatenate([z, s3, z, s3], axis=1),
    }
    return {k: jnp.tile(v, (reps, 1)) for k, v in tabs.items()}


def _pad_cols(w, n):
    return jnp.pad(w, ((0, 0), (0, n - w.shape[1])))


def _pad_lanes(g, n=128):
    g = g.reshape(1, -1)
    return jnp.pad(g, ((0, 0), (0, n - g.shape[1])))


def _prep_weights(W):
    P = {}
    for nm in ("ffn1", "ffn2"):
        P[nm] = [(W[nm + "_norm"][i].reshape(1, -1), W[nm + "_wg"][i].astype(BF16), W[nm + "_wu"][i].astype(BF16),
                  W[nm + "_wd"][i].astype(BF16)) for i in range(DEPTH)]
    P["mix_norm"] = [W["mix_norm"][i].reshape(1, -1) for i in range(DEPTH)]
    P["a"] = [dict(w=_pad_cols(W["a_w_in"][j], A_IN_PAD).astype(BF16), qn=W["a_q_norm"][j].reshape(1, -1),
                   kn=W["a_k_norm"][j].reshape(1, -1), ikn=_pad_lanes(W["a_idx_k_norm"][j]),
                   wo=W["a_w_out"][j].astype(BF16)) for j in range(W["a_w_in"].shape[0])]
    P["b"] = []
    for j in range(W["b_w_in"].shape[0]):
        wuq = W["b_w_uq"][j].reshape(Q_LORA, B_HEADS, NOPE_DIM + ROPE_DIM)
        wuq_rope = jnp.pad(wuq[:, :, NOPE_DIM:], ((0, 0), (0, 0), (0, 128 - ROPE_DIM)))
        wuq = jnp.concatenate([wuq[:, :, :NOPE_DIM], wuq_rope], axis=2).reshape(Q_LORA, -1)
        P["b"].append(dict(
            w=_pad_cols(W["b_w_in"][j], B_IN_PAD).astype(BF16), qan=W["b_q_a_norm"][j].reshape(1, -1),
            kvan=W["b_kv_a_norm"][j].reshape(1, -1), krn=_pad_lanes(W["b_k_rope_norm"][j]), wuq=wuq.astype(BF16),
            qnn=W["b_q_nope_norm"][j].reshape(1, -1), qrn=_pad_lanes(W["b_q_rope_norm"][j]),
            wukv=W["b_w_ukv"][j].astype(BF16), knn=W["b_k_nope_norm"][j].reshape(1, -1),
            wo=W["b_w_out"][j].astype(BF16)))
    P["c"] = [dict(w=W["c_w_in"][j].astype(BF16), qn=W["c_q_norm"][j].reshape(1, -1),
                   kn=W["c_k_norm"][j].reshape(1, -1), lq1=W["c_lambda_q1"][j].reshape(1, -1),
                   lk1=W["c_lambda_k1"][j].reshape(1, -1), lq2=W["c_lambda_q2"][j].reshape(1, -1),
                   lk2=W["c_lambda_k2"][j].reshape(1, -1), sn=W["c_sub_norm"][j].reshape(1, -1),
                   wo=W["c_w_out"][j].astype(BF16)) for j in range(W["c_w_in"].shape[0])]
    return P


def _with_past(past, new, s_pad):
    x = new if past is None else jnp.concatenate([past.astype(new.dtype), new], axis=1)
    return x if x.shape[1] == s_pad else jnp.pad(x, ((0, 0), (0, s_pad - x.shape[1]), (0, 0)))


def _trunk(x, offset, past, P, cfg):
    b, t, _ = x.shape
    n = b * t
    tm, tq_a, tq, kb, key_major = cfg["tm"], cfg["tq_a"], cfg["tq"], cfg["kb"], cfg["key_major"]
    p_len = 0 if past is None else past["a_k"].shape[2]
    s_valid = p_len + t
    s_pad = -(-s_valid // kb) * kb
    tabs = _rope_tables(offset + jnp.arange(t, dtype=I32), tm // t if tm > t else 1)
    att = dict(kb=kb, q_off=offset, s_valid=s_valid)
    rows = {k: [] for k in ("a_k", "a_v", "a_ik", "b_ckv", "b_kr", "c_k", "c_v")}
    x = x.reshape(n, D_MODEL)
    r3 = lambda a: a.reshape(b, t, a.shape[-1])
    t3 = lambda a: jnp.swapaxes(r3(a), 1, 2)
    pj = lambda nm, j: None if past is None else past[nm][j].reshape(b, p_len, -1)
    for i in range(DEPTH):
        x = _ffn(x, *P["ffn1"][i], tm)
        kind, j = i % N_MIXERS, i // N_MIXERS
        g = P["mix_norm"][i]
        if kind == 0:
            pa = P["a"][j]
            q, k32, v32, kbf, vbf, qi, ki32, kidx, wi = _a_proj(x, g, pa["w"], pa["qn"], pa["kn"], pa["ikn"], tabs, tm)
            rows["a_k"].append(k32.reshape(b, t, A_KV_HEADS, A_HEAD_DIM))
            rows["a_v"].append(v32.reshape(b, t, A_KV_HEADS, A_HEAD_DIM))
            rows["a_ik"].append(ki32.reshape(b, t, IDX_DIM))
            pik = pj("a_ik", j)
            if pik is not None:
                z = jnp.zeros_like(pik)
                pik = jnp.concatenate([pik, z, z, pik], axis=-1)
            kidx_all = _with_past(pik, r3(kidx), s_pad)
            k_all = _with_past(pj("a_k", j), r3(kbf), s_pad)
            v_all = _with_past(pj("a_v", j), r3(vbf), s_pad)
            topk = min(TOPK_MAX, s_valid // 4)
            if key_major:
                o = _dsa_attn_t(t3(q), t3(qi), t3(wi[:, :IDX_HEADS]), kidx_all, k_all, _blocked_t(v_all, kb),
                                tq=tq_a, topk=topk, **att)
            else:
                o = _dsa_attn(r3(q), r3(qi), r3(wi), kidx_all, k_all, v_all, tq=tq_a, topk=topk, **att)
        elif kind == 1:
            pb = P["b"][j]
            q, ckv, kr32, krbf = _b_proj(x, g, pb["w"], pb["qan"], pb["kvan"], pb["krn"], pb["wuq"], pb["qnn"],
                                         pb["qrn"], tabs, tm)
            rows["b_ckv"].append(ckv.reshape(b, t, KV_LORA))
            rows["b_kr"].append(kr32.reshape(b, t, ROPE_DIM))
            ckv_all = _with_past(pj("b_ckv", j), r3(ckv), s_pad)
            pkr = pj("b_kr", j)
            if pkr is not None:
                pkr = jnp.concatenate([pkr, jnp.zeros_like(pkr)], axis=-1)
            kn, v = _kv_up(ckv_all.reshape(b * s_pad, KV_LORA), pb["wukv"], pb["knn"], min(tm, 512))
            kn, v, kr_all = kn.reshape(b, s_pad, -1), v.reshape(b, s_pad, -1), _with_past(pkr, r3(krbf), s_pad)
            if key_major:
                o = _mla_attn_t(t3(q), kn, kr_all, _blocked_t(v, kb), tq=tq, **att)
            else:
                o = _mla_attn(r3(q), kn, kr_all, v, tq=tq, **att)
        else:
            pc = P["c"][j]
            q, k32, v32, kbf, vbf = _c_proj(x, g, pc["w"], pc["qn"], pc["kn"], tabs, tm)
            rows["c_k"].append(k32.reshape(b, t, C_HEADS, 2, C_DIM))
            rows["c_v"].append(v32.reshape(b, t, C_HEADS, 2 * C_DIM))
            k_all = _with_past(pj("c_k", j), r3(kbf), s_pad)
            v_all = _with_past(pj("c_v", j), r3(vbf), s_pad)
            lam = (pc["lq1"], pc["lk1"], pc["lq2"], pc["lk2"], pc["sn"])
            lam_init = 0.8 - 0.6 * math.exp(-0.3 * i)
            if key_major:
                att_c = dict(att, kb=cfg["kb_diff"])
                o = _diff_attn_t(t3(q), k_all, _blocked_t(v_all, cfg["kb_diff"]), *lam, tq=tq, lam_init=lam_init,
                                 **att_c)
            else:
                o = _diff_attn(r3(q), k_all, v_all, *lam, tq=tq, lam_init=lam_init, **att)
        wo = (P["a"], P["b"], P["c"])[kind][j]["wo"]
        x = _ffn(x, *P["ffn2"][i], tm, attn=o.reshape(n, -1), wo=wo)
    order = ("a_k", "a_v", "a_ik", "b_ckv", "b_kr", "c_k", "c_v")
    stack = lambda rs: rs[0][None] if len(rs) == 1 else jnp.stack(rs)
    return x.reshape(b, t, D_MODEL), tuple(stack(rows[k]) for k in order)


PROMPT_CFG = dict(tm=512, tq_a=128, tq=256, kb=256, kb_diff=512, key_major=True)
SAMPLE_CFG = dict(tm=128, tq_a=16, tq=16, kb=1280, key_major=False)


@jax.jit
def _forward(x_prompt, x_sample, past, W):
    P = _prep_weights(W)
    y_p, rows_p = _trunk(x_prompt, 0, None, P, PROMPT_CFG)
    y_s, rows_s = _trunk(x_sample, past["a_k"].shape[2], past, P, SAMPLE_CFG)
    return (y_p, y_s) + rows_p + rows_s


def kernel(x_prompt, x_sample, cache_a_k, cache_a_v, cache_a_idx_k, cache_b_ckv, cache_b_krope, cache_c_k, cache_c_v, ffn1_norm, ffn1_wg, ffn1_wu, ffn1_wd, mix_norm, ffn2_norm, ffn2_wg, ffn2_wu, ffn2_wd, a_w_in, a_q_norm, a_k_norm, a_idx_k_norm, a_w_out, b_w_in, b_q_a_norm, b_kv_a_norm, b_w_uq, b_w_ukv, b_q_nope_norm, b_q_rope_norm, b_k_nope_norm, b_k_rope_norm, b_w_out, c_w_in, c_q_norm, c_k_norm, c_lambda_q1, c_lambda_k1, c_lambda_q2, c_lambda_k2, c_sub_norm, c_w_out):
    W = dict(ffn1_norm=ffn1_norm, ffn1_wg=ffn1_wg, ffn1_wu=ffn1_wu, ffn1_wd=ffn1_wd, mix_norm=mix_norm,
             ffn2_norm=ffn2_norm, ffn2_wg=ffn2_wg, ffn2_wu=ffn2_wu, ffn2_wd=ffn2_wd,
             a_w_in=a_w_in, a_q_norm=a_q_norm, a_k_norm=a_k_norm, a_idx_k_norm=a_idx_k_norm, a_w_out=a_w_out,
             b_w_in=b_w_in, b_q_a_norm=b_q_a_norm, b_kv_a_norm=b_kv_a_norm, b_w_uq=b_w_uq, b_w_ukv=b_w_ukv,
             b_q_nope_norm=b_q_nope_norm, b_q_rope_norm=b_q_rope_norm, b_k_nope_norm=b_k_nope_norm,
             b_k_rope_norm=b_k_rope_norm, b_w_out=b_w_out,
             c_w_in=c_w_in, c_q_norm=c_q_norm, c_k_norm=c_k_norm, c_lambda_q1=c_lambda_q1,
             c_lambda_k1=c_lambda_k1, c_lambda_q2=c_lambda_q2, c_lambda_k2=c_lambda_k2, c_sub_norm=c_sub_norm,
             c_w_out=c_w_out)
    past = dict(a_k=cache_a_k, a_v=cache_a_v, a_ik=cache_a_idx_k, b_ckv=cache_b_ckv, b_kr=cache_b_krope,
                c_k=cache_c_k, c_v=cache_c_v)
    return _forward(x_prompt, x_sample, past, W)
```

```python
import functools
import math

import jax
import jax.numpy as jnp
from jax import lax
from jax.experimental import pallas as pl
from jax.experimental.pallas import tpu as pltpu

F32 = jnp.float32
BF16 = jnp.bfloat16
I32 = jnp.int32

D_MODEL = 1024
DEPTH = 4
CHUNK_SHIFT = 6
N_MIXERS = 3
ROPE_THETA = 10000.0
EPS = 1e-6
D_FF = 2816

A_HEADS = 8
A_KV_HEADS = 2
A_GROUP = A_HEADS // A_KV_HEADS
A_HEAD_DIM = 128
IDX_HEADS = 8
IDX_DIM = 64
TOPK_MAX = 256
LOG2E = math.log2(math.e)
A_SCALE = A_HEAD_DIM ** -0.5 * LOG2E
IDX_W_SCALE = (IDX_HEADS * IDX_DIM) ** -0.5
A_IN = 2120
A_IN_PAD = 2176

B_HEADS = 8
Q_LORA = 384
KV_LORA = 256
NOPE_DIM = 128
ROPE_DIM = 64
V_DIM = 128
B_SCALE = (NOPE_DIM + ROPE_DIM) ** -0.5 * LOG2E
B_IN = 704
B_IN_PAD = 768

C_HEADS = 4
C_DIM = 128
C_SCALE = C_DIM ** -0.5 * LOG2E

LANES = 128
NEG = -1e30
INT_MIN = -(2 ** 31)
VMEM_LIMIT = 56 * 1024 * 1024


def _cparams(n_axes):
    return pltpu.CompilerParams(dimension_semantics=("arbitrary",) * n_axes, vmem_limit_bytes=VMEM_LIMIT)


def _dot(a, b):
    return jnp.dot(a, b, preferred_element_type=F32)


def _dot_nt(a, b):
    return lax.dot_general(a, b, (((1,), (1,)), ((), ())), preferred_element_type=F32)


def _rms(x, g, n):
    ms = jnp.sum(x * x, axis=-1, keepdims=True) * (1.0 / n)
    return x * lax.rsqrt(ms + EPS) * g


def _rope128(x, c, s):
    return x * c + pltpu.roll(x, 64, 1) * s


def _rope64(x, c, sa, sb):
    return x * c + pltpu.roll(x, 96, 1) * sa + pltpu.roll(x, 32, 1) * sb


def _ffn_body(*refs, fc, mixed):
    if mixed:
        attn_ref, wo_ref, x_ref, g_ref, wg_ref, wu_ref, wd_ref, o_ref = refs
        x = x_ref[...] + _dot(attn_ref[...], wo_ref[...])
    else:
        x_ref, g_ref, wg_ref, wu_ref, wd_ref, o_ref = refs
        x = x_ref[...]
    h = _rms(x, g_ref[...], D_MODEL).astype(BF16)
    y = None
    for c in range(D_FF // fc):
        a = _dot(h, wg_ref[:, c * fc:(c + 1) * fc])
        u = _dot(h, wu_ref[:, c * fc:(c + 1) * fc])
        act = (a * jax.nn.sigmoid(a) * u).astype(BF16)
        part = _dot(act, wd_ref[c * fc:(c + 1) * fc, :])
        y = part if y is None else y + part
    o_ref[...] = x + 0.5 * y


def _const_spec(shape):
    nd = len(shape)
    return pl.BlockSpec(shape, lambda *_: (0,) * nd, pipeline_mode=pl.Buffered(1))


def _row_spec(tm, n):
    return pl.BlockSpec((tm, n), lambda i: (i, 0))


def _ffn(x, g, wg, wu, wd, tm, attn=None, wo=None):
    n = x.shape[0]
    mixed = attn is not None
    pre_specs = [_row_spec(tm, attn.shape[1]), _const_spec(wo.shape)] if mixed else []
    return pl.pallas_call(
        functools.partial(_ffn_body, fc=D_FF // 2, mixed=mixed),
        grid=(n // tm,),
        in_specs=pre_specs + [_row_spec(tm, D_MODEL), _const_spec((1, D_MODEL)), _const_spec((D_MODEL, D_FF)),
                              _const_spec((D_MODEL, D_FF)), _const_spec((D_FF, D_MODEL))],
        out_specs=_row_spec(tm, D_MODEL),
        out_shape=jax.ShapeDtypeStruct((n, D_MODEL), F32),
        compiler_params=_cparams(1),
        name="ffn_mixed" if mixed else "ffn",
    )(*((attn, wo) if mixed else ()), x, g, wg, wu, wd)


def _a_proj_body(x_ref, g_ref, w_ref, qn_ref, kn_ref, ikn_ref, c128_ref, s128_ref, c64_ref, sa64_ref, sb64_ref,
                 q_ref, k32_ref, v32_ref, kbf_ref, vbf_ref, qi_ref, ki32_ref, kidx_ref, wi_ref):
    h = _rms(x_ref[...], g_ref[...], D_MODEL).astype(BF16)
    slab = lambda c0, n=256: _dot(h, w_ref[:, c0:c0 + n])
    c128, s128 = c128_ref[...], s128_ref[...]
    c64, sa64, sb64 = c64_ref[...], sa64_ref[...], sb64_ref[...]
    for pr in range(A_HEADS // 2):
        y = slab(pr * 256)
        for a in range(2):
            qh = _rope128(_rms(y[:, a * 128:(a + 1) * 128], qn_ref[...], A_HEAD_DIM), c128, s128)
            q_ref[:, (2 * pr + a) * 128:(2 * pr + a + 1) * 128] = (qh * A_SCALE).astype(BF16)
    y = slab(1024)
    for hd in range(A_KV_HEADS):
        sl = slice(hd * 128, (hd + 1) * 128)
        kh = _rope128(_rms(y[:, sl], kn_ref[...], A_HEAD_DIM), c128, s128)
        k32_ref[:, sl] = kh
        kbf_ref[:, sl] = kh.astype(BF16)
    v = slab(1280)
    v32_ref[...] = v
    vbf_ref[...] = v.astype(BF16)
    for pr in range(IDX_HEADS // 4):
        y = slab(1536 + pr * 256)
        for a in range(2):
            qi_ref[:, (2 * pr + a) * 128:(2 * pr + a + 1) * 128] = _rope64(
                y[:, a * 128:(a + 1) * 128], c64, sa64, sb64).astype(BF16)
    tail = slab(2048, 128)
    lane = lax.broadcasted_iota(I32, tail.shape, 1)
    low = lane < IDX_DIM
    kin = jnp.where(low, tail, 0.0)
    ki = _rope64(_rms(kin, ikn_ref[...], IDX_DIM), c64, sa64, sb64)
    ki = jnp.where(low, ki, 0.0)
    ki32_ref[...] = ki[:, :IDX_DIM]
    kidx_ref[:, 0:128] = ki.astype(BF16)
    kidx_ref[:, 128:256] = pltpu.roll(ki, 64, 1).astype(BF16)
    wi_ref[...] = pltpu.roll(tail, 64, 1) * IDX_W_SCALE


def _tab_spec(tm, t):
    nt = t // tm
    return pl.BlockSpec((tm, LANES), lambda i: (i % nt, 0))


def _a_proj(x, g, w, qn, kn, ikn, tabs, tm):
    n = x.shape[0]
    t = tabs["c128"].shape[0]
    outs = [(1024, BF16), (256, F32), (256, F32), (256, BF16), (256, BF16), (512, BF16), (IDX_DIM, F32),
            (256, BF16), (LANES, F32)]
    return pl.pallas_call(
        _a_proj_body,
        grid=(n // tm,),
        in_specs=[_row_spec(tm, D_MODEL), _const_spec((1, D_MODEL)), _const_spec((D_MODEL, A_IN_PAD)),
                  _const_spec((1, 128)), _const_spec((1, 128)), _const_spec((1, 128))]
        + [_tab_spec(tm, t)] * 5,
        out_specs=[_row_spec(tm, c) for c, _ in outs],
        out_shape=[jax.ShapeDtypeStruct((n, c), dt) for c, dt in outs],
        compiler_params=_cparams(1),
        name="a_proj",
    )(x, g, w, qn, kn, ikn, tabs["c128"], tabs["s128"], tabs["c64"], tabs["sa64"], tabs["sb64"])


def _b_proj_body(x_ref, g_ref, w_ref, qan_ref, kvan_ref, krn_ref, wuq_ref, qnn_ref, qrn_ref,
                 c64_ref, sa64_ref, sb64_ref, q_ref, ckv_ref, kr32_ref, krbf_ref):
    h = _rms(x_ref[...], g_ref[...], D_MODEL).astype(BF16)
    y = _dot(h, w_ref[...])
    c64, sa64, sb64 = c64_ref[...], sa64_ref[...], sb64_ref[...]
    ckv_ref[...] = _rms(y[:, 384:640], kvan_ref[...], KV_LORA)
    kr = _rope64(_rms(y[:, 640:768], krn_ref[...], ROPE_DIM), c64, sa64, sb64)
    kr32_ref[...] = kr[:, :ROPE_DIM]
    krbf_ref[...] = kr.astype(BF16)
    cq = _rms(y[:, 0:384], qan_ref[...], Q_LORA).astype(BF16)
    for hd in range(B_HEADS):
        qq = _dot(cq, wuq_ref[:, hd * 256:(hd + 1) * 256])
        qn = _rms(qq[:, 0:128], qnn_ref[...], NOPE_DIM)
        qr = _rope64(_rms(qq[:, 128:256], qrn_ref[...], ROPE_DIM), c64, sa64, sb64)
        q_ref[:, hd * 256:hd * 256 + 128] = (qn * B_SCALE).astype(BF16)
        q_ref[:, hd * 256 + 128:(hd + 1) * 256] = (qr * B_SCALE).astype(BF16)


def _b_proj(x, g, w, qan, kvan, krn, wuq, qnn, qrn, tabs, tm):
    n = x.shape[0]
    t = tabs["c64"].shape[0]
    outs = [(B_HEADS * 256, BF16), (KV_LORA, F32), (ROPE_DIM, F32), (LANES, BF16)]
    return pl.pallas_call(
        _b_proj_body,
        grid=(n // tm,),
        in_specs=[_row_spec(tm, D_MODEL), _const_spec((1, D_MODEL)), _const_spec((D_MODEL, B_IN_PAD)),
                  _const_spec((1, Q_LORA)), _const_spec((1, KV_LORA)), _const_spec((1, 128)),
                  _const_spec((Q_LORA, 2048)), _const_spec((1, 128)), _const_spec((1, 128))]
        + [_tab_spec(tm, t)] * 3,
        out_specs=[_row_spec(tm, c) for c, _ in outs],
        out_shape=[jax.ShapeDtypeStruct((n, c), dt) for c, dt in outs],
        compiler_params=_cparams(1),
        name="b_proj",
    )(x, g, w, qan, kvan, krn, wuq, qnn, qrn, tabs["c64"], tabs["sa64"], tabs["sb64"])


def _kv_up_body(ckv_ref, w_ref, knn_ref, kn_ref, v_ref):
    y = _dot(ckv_ref[...].astype(BF16), w_ref[...])
    for hd in range(B_HEADS):
        kn_ref[:, hd * 128:(hd + 1) * 128] = _rms(y[:, hd * 256:hd * 256 + 128], knn_ref[...], NOPE_DIM).astype(BF16)
        v_ref[:, hd * 128:(hd + 1) * 128] = y[:, hd * 256 + 128:(hd + 1) * 256].astype(BF16)


def _kv_up(ckv, w, knn, tm):
    n = ckv.shape[0]
    return pl.pallas_call(
        _kv_up_body,
        grid=(n // tm,),
        in_specs=[_row_spec(tm, KV_LORA), _const_spec((KV_LORA, 2048)), _const_spec((1, 128))],
        out_specs=[_row_spec(tm, 1024), _row_spec(tm, 1024)],
        out_shape=[jax.ShapeDtypeStruct((n, 1024), BF16)] * 2,
        compiler_params=_cparams(1),
        name="kv_up",
    )(ckv, w, knn)


def _c_proj_body(x_ref, g_ref, w_ref, qn_ref, kn_ref, c128_ref, s128_ref,
                 q_ref, k32_ref, v32_ref, kbf_ref, vbf_ref):
    h = _rms(x_ref[...], g_ref[...], D_MODEL).astype(BF16)
    c128, s128 = c128_ref[...], s128_ref[...]
    for hd in range(C_HEADS):
        cols = slice(hd * 256, (hd + 1) * 256)
        yq = _dot(h, w_ref[:, hd * 256:(hd + 1) * 256])
        yk = _dot(h, w_ref[:, 1024 + hd * 256:1024 + (hd + 1) * 256])
        for a in range(2):
            sl = slice(hd * 256 + a * 128, hd * 256 + (a + 1) * 128)
            qh = _rope128(_rms(yq[:, a * 128:(a + 1) * 128], qn_ref[...], C_DIM), c128, s128)
            q_ref[:, sl] = (qh * C_SCALE).astype(BF16)
            kh = _rope128(_rms(yk[:, a * 128:(a + 1) * 128], kn_ref[...], C_DIM), c128, s128)
            k32_ref[:, sl] = kh
            kbf_ref[:, sl] = kh.astype(BF16)
        v = _dot(h, w_ref[:, 2048 + hd * 256:2048 + (hd + 1) * 256])
        v32_ref[:, cols] = v
        vbf_ref[:, cols] = v.astype(BF16)


def _c_proj(x, g, w, qn, kn, tabs, tm):
    n = x.shape[0]
    t = tabs["c128"].shape[0]
    outs = [(1024, BF16), (1024, F32), (1024, F32), (1024, BF16), (1024, BF16)]
    return pl.pallas_call(
        _c_proj_body,
        grid=(n // tm,),
        in_specs=[_row_spec(tm, D_MODEL), _const_spec((1, D_MODEL)), _const_spec((D_MODEL, 3072)),
                  _const_spec((1, 128)), _const_spec((1, 128))] + [_tab_spec(tm, t)] * 2,
        out_specs=[_row_spec(tm, c) for c, _ in outs],
        out_shape=[jax.ShapeDtypeStruct((n, c), dt) for c, dt in outs],
        compiler_params=_cparams(1),
        name="c_proj",
    )(x, g, w, qn, kn, tabs["c128"], tabs["s128"])


def _n_key_blocks(i, tq, kb, q_off, s_valid):
    last_chunk = lax.shift_right_logical(q_off + (i + 1) * tq - 1, CHUNK_SHIFT)
    kend = jnp.minimum((last_chunk + 1) << CHUNK_SHIFT, s_valid)
    return lax.div(kend + kb - 1, jnp.int32(kb))


def _q_chunk(i, tq, q_off):
    row = lax.broadcasted_iota(I32, (tq, 1), 0)
    return lax.shift_right_logical(q_off + i * tq + row, CHUNK_SHIFT)


def _k_chunk(ks, kb, s_valid):
    kpos = ks + lax.broadcasted_iota(I32, (1, kb), 1)
    return jnp.where(kpos < s_valid, lax.shift_right_logical(kpos, CHUNK_SHIFT), 2 ** 30)


def _flash_step(s, v, m_ref, l_ref, acc_ref):
    m_prev = m_ref[...]
    m_new = jnp.maximum(m_prev, jnp.max(s, axis=1, keepdims=True))
    alpha = jnp.exp2(m_prev - m_new)
    p = jnp.exp2(s - m_new)
    l_ref[...] = alpha * l_ref[...] + jnp.sum(p, axis=1, keepdims=True)
    acc_ref[...] = alpha * acc_ref[...] + _dot(p.astype(BF16), v)
    m_ref[...] = m_new


def _key_limit(i, tq, q_off, s_valid):
    qpos = q_off + i * tq + lax.broadcasted_iota(I32, (1, tq), 1)
    return jnp.minimum((lax.shift_right_logical(qpos, CHUNK_SHIFT) + 1) << CHUNK_SHIFT, s_valid)


def _n_full_blocks(i, tq, kb, q_off, s_valid):
    first_chunk = lax.shift_right_logical(q_off + i * tq, CHUNK_SHIFT)
    kend = jnp.minimum((first_chunk + 1) << CHUNK_SHIFT, s_valid)
    return lax.shift_right_logical(kend, int(math.log2(kb)))


def _flash_init(m_ref, l_ref, acc_ref):
    m_ref[...] = jnp.full(m_ref.shape, NEG, F32)
    l_ref[...] = jnp.zeros(l_ref.shape, F32)
    acc_ref[...] = jnp.zeros(acc_ref.shape, F32)


def _dsa_body(q_ref, qi_ref, wi_ref, kidx_ref, k_ref, v_ref, o_ref, keys_ref, m_ref, l_ref, acc_ref,
              *, tq, kb, q_off, s_valid, topk):
    i = pl.program_id(1)
    n_blk = _n_key_blocks(i, tq, kb, q_off, s_valid)
    qc = _q_chunk(i, tq, q_off)
    wi = wi_ref[...]
    wcol = [wi[:, hd:hd + 1] for hd in range(IDX_HEADS)]

    def score_block(j, carry):
        ks = pl.multiple_of(j * kb, kb)
        k_lo = kidx_ref[pl.ds(ks, kb), 0:128]
        k_hi = kidx_ref[pl.ds(ks, kb), 128:256]
        sc = jnp.zeros((tq, kb), F32)
        for p in range(IDX_HEADS // 2):
            qp = qi_ref[:, p * 128:(p + 1) * 128]
            sc = sc + wcol[2 * p] * jnp.maximum(_dot_nt(qp, k_lo), 0.0)
            sc = sc + wcol[2 * p + 1] * jnp.maximum(_dot_nt(qp, k_hi), 0.0)
        bits = lax.bitcast_convert_type(sc, I32)
        key = bits ^ ((bits >> 31) & 0x7FFFFFFF)
        keys_ref[j] = jnp.where(_k_chunk(ks, kb, s_valid) <= qc, key, INT_MIN)
        return carry

    lax.fori_loop(0, n_blk, score_block, 0)

    def count(pred):
        def blk(j, c):
            x = jnp.where(pred(keys_ref[j], j * kb), 1.0, 0.0)
            part = x[:, 0:LANES]
            for g in range(1, kb // LANES):
                part = part + x[:, g * LANES:(g + 1) * LANES]
            return c + part
        c = lax.fori_loop(0, n_blk, blk, jnp.zeros((tq, LANES), F32))
        return jnp.sum(c, axis=1, keepdims=True)

    kf = float(topk)

    def bit_step(b, t):
        cand = t + jnp.left_shift(jnp.int32(1), 31 - b)
        cnt = count(lambda kblk, _: kblk >= cand)
        return jnp.where(cnt >= kf, cand, t)

    thr = lax.fori_loop(0, 32, bit_step, jnp.full((tq, 1), INT_MIN, I32))
    need = kf - count(lambda kblk, _: kblk > thr)
    n_eq = count(lambda kblk, _: kblk == thr)
    partial = jnp.logical_and(n_eq > need, thr != INT_MIN)

    @pl.when(jnp.max(jnp.where(partial, 1.0, 0.0)) > 0.0)
    def _():
        lane = lax.broadcasted_iota(I32, (1, kb), 1)

        def idx_step(b, x):
            cand = x + jnp.left_shift(jnp.int32(1), b)
            cnt = count(lambda kblk, k0: jnp.logical_and(kblk == thr, k0 + lane < cand))
            return jnp.where(cnt < need, cand, x)

        nbits = max(1, int(s_valid - 1).bit_length())
        cut = lax.fori_loop(0, nbits, lambda b, x: idx_step(nbits - 1 - b, x), jnp.zeros((tq, 1), I32))

        def demote(j, carry):
            kblk = keys_ref[j]
            drop = jnp.logical_and(jnp.logical_and(kblk == thr, j * kb + lane > cut), partial)
            keys_ref[j] = jnp.where(drop, kblk - 1, kblk)
            return carry

        lax.fori_loop(0, n_blk, demote, 0)

    thr_sel = jnp.maximum(thr, INT_MIN + 1)

    _flash_init(m_ref, l_ref, acc_ref)

    def attend(j, carry):
        ks = pl.multiple_of(j * kb, kb)
        bias = jnp.where(keys_ref[j] >= thr_sel, 0.0, NEG)
        for g in range(A_KV_HEADS):
            qg = jnp.concatenate([q_ref[:, (g * A_GROUP + a) * 128:(g * A_GROUP + a + 1) * 128]
                                  for a in range(A_GROUP)], axis=0)
            s = _dot_nt(qg, k_ref[pl.ds(ks, kb), g * 128:(g + 1) * 128])
            s = (s.reshape(A_GROUP, tq, kb) + bias[None]).reshape(A_GROUP * tq, kb)
            _flash_step(s, v_ref[pl.ds(ks, kb), g * 128:(g + 1) * 128], m_ref.at[g], l_ref.at[g], acc_ref.at[g])
        return carry

    lax.fori_loop(0, n_blk, attend, 0)
    for g in range(A_KV_HEADS):
        o = acc_ref[g] / l_ref[g]
        for a in range(A_GROUP):
            hd = g * A_GROUP + a
            o_ref[:, hd * 128:(hd + 1) * 128] = o[a * tq:(a + 1) * tq].astype(BF16)


def _dsa_attn(q, qi, wi, kidx, k, v, *, tq, kb, q_off, s_valid, topk):
    b, t, _ = q.shape
    s_pad = k.shape[1]
    qspec = lambda c: pl.BlockSpec((None, tq, c), lambda bi, i: (bi, i, 0))
    kspec = lambda c: pl.BlockSpec((None, s_pad, c), lambda bi, i: (bi, 0, 0))
    rows = A_GROUP * tq
    return pl.pallas_call(
        functools.partial(_dsa_body, tq=tq, kb=kb, q_off=q_off, s_valid=s_valid, topk=topk),
        grid=(b, t // tq),
        in_specs=[qspec(1024), qspec(512), qspec(LANES), kspec(256), kspec(256), kspec(256)],
        out_specs=qspec(1024),
        out_shape=jax.ShapeDtypeStruct((b, t, 1024), BF16),
        scratch_shapes=[pltpu.VMEM((s_pad // kb, tq, kb), I32), pltpu.VMEM((A_KV_HEADS, rows, 1), F32),
                        pltpu.VMEM((A_KV_HEADS, rows, 1), F32), pltpu.VMEM((A_KV_HEADS, rows, 128), F32)],
        compiler_params=_cparams(2),
        name="dsa_attn",
    )(q, qi, wi, kidx, k, v)


def _mla_body(q_ref, kn_ref, kr_ref, v_ref, o_ref, m_ref, l_ref, acc_ref, *, tq, kb, q_off, s_valid):
    i = pl.program_id(2)
    n_blk = _n_key_blocks(i, tq, kb, q_off, s_valid)
    qc = _q_chunk(i, tq, q_off)
    _flash_init(m_ref, l_ref, acc_ref)

    def attend(j, carry):
        ks = pl.multiple_of(j * kb, kb)
        kcat = jnp.concatenate([kn_ref[pl.ds(ks, kb), :], kr_ref[pl.ds(ks, kb), :]], axis=1)
        s = _dot_nt(q_ref[...], kcat)
        s = jnp.where(_k_chunk(ks, kb, s_valid) <= qc, s, NEG)
        _flash_step(s, v_ref[pl.ds(ks, kb), :], m_ref, l_ref, acc_ref)
        return carry

    lax.fori_loop(0, n_blk, attend, 0)
    o_ref[...] = (acc_ref[...] / l_ref[...]).astype(BF16)


def _mla_attn(q, kn, kr, v, *, tq, kb, q_off, s_valid):
    b, t, _ = q.shape
    s_pad = kn.shape[1]
    return pl.pallas_call(
        functools.partial(_mla_body, tq=tq, kb=kb, q_off=q_off, s_valid=s_valid),
        grid=(b, B_HEADS, t // tq),
        in_specs=[pl.BlockSpec((None, tq, 256), lambda bi, h, i: (bi, i, h)),
                  pl.BlockSpec((None, s_pad, 128), lambda bi, h, i: (bi, 0, h)),
                  pl.BlockSpec((None, s_pad, 128), lambda bi, h, i: (bi, 0, 0)),
                  pl.BlockSpec((None, s_pad, 128), lambda bi, h, i: (bi, 0, h))],
        out_specs=pl.BlockSpec((None, tq, 128), lambda bi, h, i: (bi, i, h)),
        out_shape=jax.ShapeDtypeStruct((b, t, B_HEADS * V_DIM), BF16),
        scratch_shapes=[pltpu.VMEM((tq, 1), F32), pltpu.VMEM((tq, 1), F32), pltpu.VMEM((tq, 128), F32)],
        compiler_params=_cparams(3),
        name="mla_attn",
    )(q, kn, kr, v)


def _diff_body(q_ref, k_ref, v_ref, lq1_ref, lk1_ref, lq2_ref, lk2_ref, sn_ref, o_ref, m_ref, l_ref, acc_ref,
               *, tq, kb, q_off, s_valid, lam_init):
    i = pl.program_id(2)
    n_blk = _n_key_blocks(i, tq, kb, q_off, s_valid)
    qc = _q_chunk(i, tq, q_off)
    _flash_init(m_ref, l_ref, acc_ref)

    def attend(j, carry):
        ks = pl.multiple_of(j * kb, kb)
        ok = _k_chunk(ks, kb, s_valid) <= qc
        v = v_ref[pl.ds(ks, kb), :]
        for p in range(2):
            s = _dot_nt(q_ref[:, p * 128:(p + 1) * 128], k_ref[pl.ds(ks, kb), p * 128:(p + 1) * 128])
            s = jnp.where(ok, s, NEG)
            _flash_step(s, v, m_ref.at[p], l_ref.at[p], acc_ref.at[p])
        return carry

    lax.fori_loop(0, n_blk, attend, 0)
    lam = (jnp.exp(jnp.sum(lq1_ref[...] * lk1_ref[...], axis=1, keepdims=True))
           - jnp.exp(jnp.sum(lq2_ref[...] * lk2_ref[...], axis=1, keepdims=True)) + lam_init)
    o = acc_ref[0] / l_ref[0] - lam * (acc_ref[1] / l_ref[1])
    o_ref[...] = (_rms(o, sn_ref[...], 2 * C_DIM) * (1.0 - lam_init)).astype(BF16)


def _diff_attn(q, k, v, lq1, lk1, lq2, lk2, sn, *, tq, kb, q_off, s_valid, lam_init):
    b, t, _ = q.shape
    s_pad = k.shape[1]
    vec = lambda c: pl.BlockSpec((1, c), lambda bi, h, i: (0, 0))
    return pl.pallas_call(
        functools.partial(_diff_body, tq=tq, kb=kb, q_off=q_off, s_valid=s_valid, lam_init=lam_init),
        grid=(b, C_HEADS, t // tq),
        in_specs=[pl.BlockSpec((None, tq, 256), lambda bi, h, i: (bi, i, h)),
                  pl.BlockSpec((None, s_pad, 256), lambda bi, h, i: (bi, 0, h)),
                  pl.BlockSpec((None, s_pad, 256), lambda bi, h, i: (bi, 0, h)),
                  vec(128), vec(128), vec(128), vec(128), vec(256)],
        out_specs=pl.BlockSpec((None, tq, 256), lambda bi, h, i: (bi, i, h)),
        out_shape=jax.ShapeDtypeStruct((b, t, C_HEADS * 2 * C_DIM), BF16),
        scratch_shapes=[pltpu.VMEM((2, tq, 1), F32), pltpu.VMEM((2, tq, 1), F32), pltpu.VMEM((2, tq, 256), F32)],
        compiler_params=_cparams(3),
        name="diff_attn",
    )(q, k, v, lq1, lk1, lq2, lk2, sn)


SCORE_BLOCKS = 2
SEARCH_BLOCKS = 4
SEARCH_BITS_UNCHECKED = 16
SEARCH_BITS_PER_CHECK = 4


def _stage_bufs(n_chain, kb, r):
    return [pltpu.VMEM((n_chain, 2, kb, r), F32), pltpu.VMEM((n_chain, 2, kb, r), BF16),
            pltpu.VMEM((n_chain, 2, 1, r), F32)]


def _staged_flash_t(bufs, state, lo, hi, n_kb, qk, prep, vt_of, first=True, last=True):
    s_ref, p_ref, a_ref = bufs
    m_ref, l_ref, acc_ref = state
    n_chain = s_ref.shape[0]
    clamp = lambda j: jnp.clip(j, 0, n_kb - 1)

    def softmax(c, j, slot):
        st = prep(c, j, s_ref[c, slot])
        m_prev = m_ref[c]
        m_new = jnp.maximum(m_prev, jnp.max(st, axis=0, keepdims=True))
        alpha = jnp.exp2(m_prev - m_new)
        p = jnp.exp2(st - m_new)
        l_ref[c] = alpha * l_ref[c] + jnp.sum(p, axis=0, keepdims=True)
        m_ref[c] = m_new
        p_ref[c, slot] = p.astype(BF16)
        a_ref[c, slot] = alpha

    def values(c, j, slot):
        acc_ref[c] = a_ref[c, slot] * acc_ref[c] + _dot(vt_of(c, clamp(j)), p_ref[c, slot])

    if first:
        for c in range(n_chain):
            s_ref[c, 0] = qk(c, clamp(lo))
            p_ref[c, 1] = jnp.zeros(p_ref.shape[2:], BF16)
            a_ref[c, 1] = jnp.ones(a_ref.shape[2:], F32)

    def turn(t, carry):
        j0 = lo + 2 * t
        for c in range(n_chain):
            s_ref[c, 1] = qk(c, clamp(j0 + 1))
        for c in range(n_chain):
            softmax(c, j0, 0)
        for c in range(n_chain):
            values(c, j0 - 1, 1)
        for c in range(n_chain):
            s_ref[c, 0] = qk(c, clamp(j0 + 2))
        for c in range(n_chain):
            softmax(c, j0 + 1, 1)
        for c in range(n_chain):
            values(c, j0, 0)
        return carry

    n_turn = lax.shift_right_logical(hi - lo + 1, 1)
    lax.fori_loop(0, n_turn, turn, 0)
    if last:
        for c in range(n_chain):
            values(c, lo + 2 * n_turn - 1, 1)


def _dsa_t_body(q_ref, qi_ref, wi_ref, kidx_ref, k_ref, vt_ref, o_ref, keys_ref, s_ref, p_ref, a_ref,
                m_ref, l_ref, acc_ref, *, tq, kb, q_off, s_valid, topk):
    i = pl.program_id(1)
    n_blk = _n_key_blocks(i, tq, kb, q_off, s_valid)
    n_kb = keys_ref.shape[0]
    n_sb = lax.shift_right_logical(n_blk + SEARCH_BLOCKS - 1, int(math.log2(SEARCH_BLOCKS)))
    limit = _key_limit(i, tq, q_off, s_valid)
    kidx0 = lax.broadcasted_iota(I32, (kb, tq), 0)
    wit = wi_ref[...].T
    wrow = [wit[hd:hd + 1, :] for hd in range(IDX_HEADS)]

    def score_blocks(jj, carry):
        ks = pl.multiple_of(jj * (SCORE_BLOCKS * kb), SCORE_BLOCKS * kb)
        rows = pl.ds(ks, SCORE_BLOCKS * kb)
        q_all = jnp.concatenate([qi_ref[:, p * 128:(p + 1) * 128] for p in range(IDX_HEADS // 2)], axis=0)
        s_lo = _dot_nt(kidx_ref[rows, 0:128], q_all)
        s_hi = _dot_nt(kidx_ref[rows, 128:256], q_all)
        for u in range(SCORE_BLOCKS):
            blk = slice(u * kb, (u + 1) * kb)
            sc = jnp.zeros((kb, tq), F32)
            for p in range(IDX_HEADS // 2):
                sc = sc + wrow[2 * p] * jnp.maximum(s_lo[blk, p * tq:(p + 1) * tq], 0.0)
                sc = sc + wrow[2 * p + 1] * jnp.maximum(s_hi[blk, p * tq:(p + 1) * tq], 0.0)
            bits = lax.bitcast_convert_type(sc, I32)
            key = bits ^ ((bits >> 31) & 0x7FFFFFFF)
            keys_ref[jj * SCORE_BLOCKS + u] = jnp.where(kidx0 < limit - (ks + u * kb), key, INT_MIN)
        return carry

    def pad_block(j, carry):
        keys_ref[j] = jnp.full((kb, tq), INT_MIN, I32)
        return carry

    n_scored = lax.shift_right_logical(n_blk + SCORE_BLOCKS - 1, int(math.log2(SCORE_BLOCKS)))
    lax.fori_loop(0, n_scored, score_blocks, 0)
    lax.fori_loop(n_scored * SCORE_BLOCKS, n_sb * SEARCH_BLOCKS, pad_block, 0)

    def count(pred):
        def group(jj, cs):
            out = []
            for u in range(SEARCH_BLOCKS):
                j = jj * SEARCH_BLOCKS + u
                x = jnp.where(pred(keys_ref[j], j * kb), 1.0, 0.0)
                out.append(cs[u] + jnp.sum(x.reshape(kb // 8, 8, tq), axis=0))
            return tuple(out)
        cs = lax.fori_loop(0, n_sb, group, tuple(jnp.zeros((8, tq), F32) for _ in range(SEARCH_BLOCKS)))
        return jnp.sum(functools.reduce(lambda a, b: a + b, cs), axis=0, keepdims=True)

    kf = float(topk)

    def bit_step(b, state):
        t, n_ge = state
        cand = t + jnp.left_shift(jnp.int32(1), 31 - b)
        cnt = count(lambda kblk, _: kblk >= cand)
        take = cnt >= kf
        return jnp.where(take, cand, t), jnp.where(take, cnt, n_ge)

    def unsettled(n_ge):
        open_ = jnp.logical_and(n_ge != kf, limit.astype(F32) > kf)
        return jnp.max(jnp.where(open_, 1.0, 0.0)) > 0.0

    def more_bits(state):
        b, _, n_ge = state
        return jnp.logical_and(b < 32, unsettled(n_ge))

    def four_bits(state):
        b, t, n_ge = state
        for u in range(SEARCH_BITS_PER_CHECK):
            t, n_ge = bit_step(b + u, (t, n_ge))
        return b + SEARCH_BITS_PER_CHECK, t, n_ge

    start = (jnp.full((1, tq), INT_MIN, I32), jnp.full((1, tq), float(n_kb * kb), F32))
    head = lax.fori_loop(0, SEARCH_BITS_UNCHECKED, bit_step, start)
    _, thr, _ = lax.while_loop(more_bits, four_bits, (jnp.int32(SEARCH_BITS_UNCHECKED),) + head)
    need = kf - count(lambda kblk, _: kblk > thr)
    n_eq = count(lambda kblk, _: kblk == thr)
    partial = jnp.logical_and(n_eq > need, thr != INT_MIN)

    @pl.when(jnp.max(jnp.where(partial, 1.0, 0.0)) > 0.0)
    def _():
        def idx_step(b, x):
            cand = x + jnp.left_shift(jnp.int32(1), b)
            cnt = count(lambda kblk, k0: jnp.logical_and(kblk == thr, kidx0 < cand - k0))
            return jnp.where(cnt < need, cand, x)

        nbits = max(1, int(s_valid - 1).bit_length())
        cut = lax.fori_loop(0, nbits, lambda b, x: idx_step(nbits - 1 - b, x), jnp.zeros((1, tq), I32))

        def demote(j, carry):
            kblk = keys_ref[j]
            drop = jnp.logical_and(jnp.logical_and(kblk == thr, kidx0 > cut - j * kb), partial)
            keys_ref[j] = jnp.where(drop, kblk - 1, kblk)
            return carry

        lax.fori_loop(0, n_blk, demote, 0)

    thr_sel = jnp.maximum(thr, INT_MIN + 1)

    _flash_init(m_ref, l_ref, acc_ref)
    n_pair = A_HEADS // 2

    def to_bias(j, carry):
        keys_ref[j] = lax.bitcast_convert_type(jnp.where(keys_ref[j] >= thr_sel, 0.0, NEG), I32)
        return carry

    lax.fori_loop(0, n_sb * SEARCH_BLOCKS, to_bias, 0)
    for g in range(A_KV_HEADS):
        def qk(c, j, g=g):
            p = g * (A_GROUP // 2) + c
            qp = jnp.concatenate([q_ref[:, (2 * p) * 128:(2 * p + 1) * 128],
                                  q_ref[:, (2 * p + 1) * 128:(2 * p + 2) * 128]], axis=0)
            return _dot_nt(k_ref[pl.ds(pl.multiple_of(j * kb, kb), kb), g * 128:(g + 1) * 128], qp)

        def prep(c, j, st):
            bias = lax.bitcast_convert_type(keys_ref[j], F32)
            return st + jnp.concatenate([bias, bias], axis=1)

        def vt_of(c, j, g=g):
            return vt_ref[j, g * 128:(g + 1) * 128, :]

        chains = pl.ds(g * (A_GROUP // 2), A_GROUP // 2)
        _staged_flash_t((s_ref.at[chains], p_ref.at[chains], a_ref.at[chains]),
                        (m_ref.at[chains], l_ref.at[chains], acc_ref.at[chains]), 0, n_blk, n_kb, qk, prep, vt_of)
    for p in range(n_pair):
        ot = acc_ref[p] / l_ref[p]
        for a in range(2):
            hd = 2 * p + a
            o_ref[:, hd * 128:(hd + 1) * 128] = ot[:, a * tq:(a + 1) * tq].T.astype(BF16)


def _dsa_attn_t(q, qi, wi, kidx, k, vt, *, tq, kb, q_off, s_valid, topk):
    b, t, _ = q.shape
    s_pad = k.shape[1]
    n_kb = s_pad // kb
    qspec = lambda c: pl.BlockSpec((None, tq, c), lambda bi, i: (bi, i, 0))
    kspec = lambda c: pl.BlockSpec((None, s_pad, c), lambda bi, i: (bi, 0, 0))
    n_pair = A_HEADS // 2
    return pl.pallas_call(
        functools.partial(_dsa_t_body, tq=tq, kb=kb, q_off=q_off, s_valid=s_valid, topk=topk),
        grid=(b, t // tq),
        in_specs=[qspec(1024), qspec(512), qspec(LANES),
                  kspec(256), kspec(256), pl.BlockSpec((None, n_kb, 256, kb), lambda bi, i: (bi, 0, 0, 0))],
        out_specs=qspec(1024),
        out_shape=jax.ShapeDtypeStruct((b, t, 1024), BF16),
        scratch_shapes=[pltpu.VMEM((n_kb, kb, tq), I32)] + _stage_bufs(n_pair, kb, 2 * tq)
        + [pltpu.VMEM((n_pair, 1, 2 * tq), F32), pltpu.VMEM((n_pair, 1, 2 * tq), F32),
           pltpu.VMEM((n_pair, 128, 2 * tq), F32)],
        compiler_params=_cparams(2),
        name="dsa_attn_t",
    )(q, qi, wi, kidx, k, vt)


def _causal_flash_t(bufs, state, i, n_kb, tq, kb, q_off, s_valid, qk, vt_of):
    n_blk = _n_key_blocks(i, tq, kb, q_off, s_valid)
    n_plain = _n_full_blocks(i, tq, kb, q_off, s_valid) & -2
    limit = _key_limit(i, tq, q_off, s_valid)

    def masked(c, j, st):
        return jnp.where(lax.broadcasted_iota(I32, (kb, tq), 0) < limit - j * kb, st, NEG)

    _staged_flash_t(bufs, state, 0, n_plain, n_kb, qk, lambda c, j, st: st, vt_of, last=False)
    _staged_flash_t(bufs, state, n_plain, n_blk, n_kb, qk, masked, vt_of, first=False)


def _mla_t_body(qt_ref, kn_ref, kr_ref, vt_ref, o_ref, s_ref, p_ref, a_ref, m_ref, l_ref, acc_ref,
                *, tq, kb, q_off, s_valid):
    i = pl.program_id(2)
    _flash_init(m_ref, l_ref, acc_ref)

    def qk(a, j):
        ks = pl.multiple_of(j * kb, kb)
        kcat = jnp.concatenate([kn_ref[pl.ds(ks, kb), a * 128:(a + 1) * 128], kr_ref[pl.ds(ks, kb), :]], axis=1)
        return _dot(kcat, qt_ref[a * 256:(a + 1) * 256, :])

    def vt_of(a, j):
        return vt_ref[j, a * 128:(a + 1) * 128, :]

    _causal_flash_t((s_ref, p_ref, a_ref), (m_ref, l_ref, acc_ref), i, vt_ref.shape[0], tq, kb, q_off, s_valid,
                    qk, vt_of)
    for a in range(2):
        o_ref[:, a * 128:(a + 1) * 128] = (acc_ref[a] / l_ref[a]).T.astype(BF16)


def _mla_attn_t(qt, kn, kr, vt, *, tq, kb, q_off, s_valid):
    b, _, t = qt.shape
    s_pad = kn.shape[1]
    n_kb = s_pad // kb
    return pl.pallas_call(
        functools.partial(_mla_t_body, tq=tq, kb=kb, q_off=q_off, s_valid=s_valid),
        grid=(b, B_HEADS // 2, t // tq),
        in_specs=[pl.BlockSpec((None, 512, tq), lambda bi, h, i: (bi, h, i)),
                  pl.BlockSpec((None, s_pad, 256), lambda bi, h, i: (bi, 0, h)),
                  pl.BlockSpec((None, s_pad, 128), lambda bi, h, i: (bi, 0, 0)),
                  pl.BlockSpec((None, n_kb, 256, kb), lambda bi, h, i: (bi, 0, h, 0))],
        out_specs=pl.BlockSpec((None, tq, 256), lambda bi, h, i: (bi, i, h)),
        out_shape=jax.ShapeDtypeStruct((b, t, B_HEADS * V_DIM), BF16),
        scratch_shapes=_stage_bufs(2, kb, tq) + [pltpu.VMEM((2, 1, tq), F32), pltpu.VMEM((2, 1, tq), F32),
                                                 pltpu.VMEM((2, 128, tq), F32)],
        compiler_params=_cparams(3),
        name="mla_attn_t",
    )(qt, kn, kr, vt)


def _diff_t_body(q_ref, k_ref, vt_ref, lq1_ref, lk1_ref, lq2_ref, lk2_ref, sn_ref, o_ref, s_ref, p_ref, a_ref,
                 m_ref, l_ref, acc_ref, *, tq, kb, q_off, s_valid, lam_init):
    i = pl.program_id(2)
    _flash_init(m_ref, l_ref, acc_ref)

    def qk(p, j):
        ks = pl.multiple_of(j * kb, kb)
        return _dot_nt(k_ref[pl.ds(ks, kb), p * 128:(p + 1) * 128], q_ref[:, p * 128:(p + 1) * 128])

    _causal_flash_t((s_ref, p_ref, a_ref), (m_ref, l_ref, acc_ref), i, vt_ref.shape[0], tq, kb, q_off, s_valid,
                    qk, lambda p, j: vt_ref[j])
    lam = (jnp.exp(jnp.sum(lq1_ref[...] * lk1_ref[...], axis=1, keepdims=True))
           - jnp.exp(jnp.sum(lq2_ref[...] * lk2_ref[...], axis=1, keepdims=True)) + lam_init)
    ot = acc_ref[0] / l_ref[0] - lam * (acc_ref[1] / l_ref[1])
    o = jnp.concatenate([ot[0:128, :].T, ot[128:256, :].T], axis=1)
    o_ref[...] = (_rms(o, sn_ref[...], 2 * C_DIM) * (1.0 - lam_init)).astype(BF16)


def _diff_attn_t(q, k, vt, lq1, lk1, lq2, lk2, sn, *, tq, kb, q_off, s_valid, lam_init):
    b, t, _ = q.shape
    s_pad = k.shape[1]
    n_kb = s_pad // kb
    vec = lambda c: pl.BlockSpec((1, c), lambda bi, h, i: (0, 0))
    return pl.pallas_call(
        functools.partial(_diff_t_body, tq=tq, kb=kb, q_off=q_off, s_valid=s_valid, lam_init=lam_init),
        grid=(b, C_HEADS, t // tq),
        in_specs=[pl.BlockSpec((None, tq, 256), lambda bi, h, i: (bi, i, h)),
                  pl.BlockSpec((None, s_pad, 256), lambda bi, h, i: (bi, 0, h)),
                  pl.BlockSpec((None, n_kb, 256, kb), lambda bi, h, i: (bi, 0, h, 0)),
                  vec(128), vec(128), vec(128), vec(128), vec(256)],
        out_specs=pl.BlockSpec((None, tq, 256), lambda bi, h, i: (bi, i, h)),
        out_shape=jax.ShapeDtypeStruct((b, t, C_HEADS * 2 * C_DIM), BF16),
        scratch_shapes=_stage_bufs(2, kb, tq) + [pltpu.VMEM((2, 1, tq), F32), pltpu.VMEM((2, 1, tq), F32),
                                                 pltpu.VMEM((2, 256, tq), F32)],
        compiler_params=_cparams(3),
        name="diff_attn_t",
    )(q, k, vt, lq1, lk1, lq2, lk2, sn)


def _blocked_t(v, kb):
    b, s, c = v.shape
    return jnp.swapaxes(v.reshape(b, s // kb, kb, c), 2, 3)


def _rope_tables(pos, reps):
    p = pos.astype(F32)[:, None]
    inv64 = jnp.power(ROPE_THETA, -jnp.arange(64, dtype=F32) / 64)
    inv32 = jnp.power(ROPE_THETA, -jnp.arange(32, dtype=F32) / 32)
    c, s = jnp.cos(p * inv64), jnp.sin(p * inv64)
    c3, s3 = jnp.cos(p * inv32), jnp.sin(p * inv32)
    z = jnp.zeros_like(s3)
    tabs = {
        "c128": jnp.concatenate([c, c], axis=1),
        "s128": jnp.concatenate([-s, s], axis=1),
        "c64": jnp.concatenate([c3, c3, c3, c3], axis=1),
        "sa64": jnp.concatenate([-s3, z, -s3, z], axis=1),
        "sb64": jnp.concatenate([z, s3, z, s3], axis=1),
    }
    return {k: jnp.tile(v, (reps, 1)) for k, v in tabs.items()}


def _pad_cols(w, n):
    return jnp.pad(w, ((0, 0), (0, n - w.shape[1])))


def _pad_lanes(g, n=128):
    g = g.reshape(1, -1)
    return jnp.pad(g, ((0, 0), (0, n - g.shape[1])))


def _prep_weights(W):
    P = {}
    for nm in ("ffn1", "ffn2"):
        P[nm] = [(W[nm + "_norm"][i].reshape(1, -1), W[nm + "_wg"][i].astype(BF16), W[nm + "_wu"][i].astype(BF16),
                  W[nm + "_wd"][i].astype(BF16)) for i in range(DEPTH)]
    P["mix_norm"] = [W["mix_norm"][i].reshape(1, -1) for i in range(DEPTH)]
    P["a"] = [dict(w=_pad_cols(W["a_w_in"][j], A_IN_PAD).astype(BF16), qn=W["a_q_norm"][j].reshape(1, -1),
                   kn=W["a_k_norm"][j].reshape(1, -1), ikn=_pad_lanes(W["a_idx_k_norm"][j]),
                   wo=W["a_w_out"][j].astype(BF16)) for j in range(W["a_w_in"].shape[0])]
    P["b"] = []
    for j in range(W["b_w_in"].shape[0]):
        wuq = W["b_w_uq"][j].reshape(Q_LORA, B_HEADS, NOPE_DIM + ROPE_DIM)
        wuq_rope = jnp.pad(wuq[:, :, NOPE_DIM:], ((0, 0), (0, 0), (0, 128 - ROPE_DIM)))
        wuq = jnp.concatenate([wuq[:, :, :NOPE_DIM], wuq_rope], axis=2).reshape(Q_LORA, -1)
        P["b"].append(dict(
            w=_pad_cols(W["b_w_in"][j], B_IN_PAD).astype(BF16), qan=W["b_q_a_norm"][j].reshape(1, -1),
            kvan=W["b_kv_a_norm"][j].reshape(1, -1), krn=_pad_lanes(W["b_k_rope_norm"][j]), wuq=wuq.astype(BF16),
            qnn=W["b_q_nope_norm"][j].reshape(1, -1), qrn=_pad_lanes(W["b_q_rope_norm"][j]),
            wukv=W["b_w_ukv"][j].astype(BF16), knn=W["b_k_nope_norm"][j].reshape(1, -1),
            wo=W["b_w_out"][j].astype(BF16)))
    P["c"] = [dict(w=W["c_w_in"][j].astype(BF16), qn=W["c_q_norm"][j].reshape(1, -1),
                   kn=W["c_k_norm"][j].reshape(1, -1), lq1=W["c_lambda_q1"][j].reshape(1, -1),
                   lk1=W["c_lambda_k1"][j].reshape(1, -1), lq2=W["c_lambda_q2"][j].reshape(1, -1),
                   lk2=W["c_lambda_k2"][j].reshape(1, -1), sn=W["c_sub_norm"][j].reshape(1, -1),
                   wo=W["c_w_out"][j].astype(BF16)) for j in range(W["c_w_in"].shape[0])]
    return P


def _with_past(past, new, s_pad):
    x = new if past is None else jnp.concatenate([past.astype(new.dtype), new], axis=1)
    return x if x.shape[1] == s_pad else jnp.pad(x, ((0, 0), (0, s_pad - x.shape[1]), (0, 0)))


def _trunk(x, offset, past, P, cfg):
    b, t, _ = x.shape
    n = b * t
    tm, tq_a, tq, kb, key_major = cfg["tm"], cfg["tq_a"], cfg["tq"], cfg["kb"], cfg["key_major"]
    p_len = 0 if past is None else past["a_k"].shape[2]
    s_valid = p_len + t
    s_pad = -(-s_valid // kb) * kb
    tabs = _rope_tables(offset + jnp.arange(t, dtype=I32), tm // t if tm > t else 1)
    att = dict(kb=kb, q_off=offset, s_valid=s_valid)
    rows = {k: [] for k in ("a_k", "a_v", "a_ik", "b_ckv", "b_kr", "c_k", "c_v")}
    x = x.reshape(n, D_MODEL)
    r3 = lambda a: a.reshape(b, t, a.shape[-1])
    t3 = lambda a: jnp.swapaxes(r3(a), 1, 2)
    pj = lambda nm, j: None if past is None else past[nm][j].reshape(b, p_len, -1)
    for i in range(DEPTH):
        x = _ffn(x, *P["ffn1"][i], tm)
        kind, j = i % N_MIXERS, i // N_MIXERS
        g = P["mix_norm"][i]
        if kind == 0:
            pa = P["a"][j]
            q, k32, v32, kbf, vbf, qi, ki32, kidx, wi = _a_proj(x, g, pa["w"], pa["qn"], pa["kn"], pa["ikn"], tabs, tm)
            rows["a_k"].append(k32.reshape(b, t, A_KV_HEADS, A_HEAD_DIM))
            rows["a_v"].append(v32.reshape(b, t, A_KV_HEADS, A_HEAD_DIM))
            rows["a_ik"].append(ki32.reshape(b, t, IDX_DIM))
            pik = pj("a_ik", j)
            if pik is not None:
                z = jnp.zeros_like(pik)
                pik = jnp.concatenate([pik, z, z, pik], axis=-1)
            kidx_all = _with_past(pik, r3(kidx), s_pad)
            k_all = _with_past(pj("a_k", j), r3(kbf), s_pad)
            v_all = _with_past(pj("a_v", j), r3(vbf), s_pad)
            topk = min(TOPK_MAX, s_valid // 4)
            if key_major:
                o = _dsa_attn_t(r3(q), r3(qi), r3(wi), kidx_all, k_all, _blocked_t(v_all, kb),
                                tq=tq_a, topk=topk, **att)
            else:
                o = _dsa_attn(r3(q), r3(qi), r3(wi), kidx_all, k_all, v_all, tq=tq_a, topk=topk, **att)
        elif kind == 1:
            pb = P["b"][j]
            q, ckv, kr32, krbf = _b_proj(x, g, pb["w"], pb["qan"], pb["kvan"], pb["krn"], pb["wuq"], pb["qnn"],
                                         pb["qrn"], tabs, tm)
            rows["b_ckv"].append(ckv.reshape(b, t, KV_LORA))
            rows["b_kr"].append(kr32.reshape(b, t, ROPE_DIM))
            ckv_all = _with_past(pj("b_ckv", j), r3(ckv), s_pad)
            pkr = pj("b_kr", j)
            if pkr is not None:
                pkr = jnp.concatenate([pkr, jnp.zeros_like(pkr)], axis=-1)
            kn, v = _kv_up(ckv_all.reshape(b * s_pad, KV_LORA), pb["wukv"], pb["knn"], math.gcd(b * s_pad, 512))
            kn, v, kr_all = kn.reshape(b, s_pad, -1), v.reshape(b, s_pad, -1), _with_past(pkr, r3(krbf), s_pad)
            if key_major:
                o = _mla_attn_t(t3(q), kn, kr_all, _blocked_t(v, kb), tq=tq, **att)
            else:
                o = _mla_attn(r3(q), kn, kr_all, v, tq=tq, **att)
        else:
            pc = P["c"][j]
            q, k32, v32, kbf, vbf = _c_proj(x, g, pc["w"], pc["qn"], pc["kn"], tabs, tm)
            rows["c_k"].append(k32.reshape(b, t, C_HEADS, 2, C_DIM))
            rows["c_v"].append(v32.reshape(b, t, C_HEADS, 2 * C_DIM))
            k_all = _with_past(pj("c_k", j), r3(kbf), s_pad)
            v_all = _with_past(pj("c_v", j), r3(vbf), s_pad)
            lam = (pc["lq1"], pc["lk1"], pc["lq2"], pc["lk2"], pc["sn"])
            lam_init = 0.8 - 0.6 * math.exp(-0.3 * i)
            if key_major:
                att_c = dict(att, kb=cfg["kb_diff"])
                o = _diff_attn_t(r3(q), k_all, _blocked_t(v_all, cfg["kb_diff"]), *lam, tq=tq, lam_init=lam_init,
                                 **att_c)
            else:
                o = _diff_attn(r3(q), k_all, v_all, *lam, tq=tq, lam_init=lam_init, **att)
        wo = (P["a"], P["b"], P["c"])[kind][j]["wo"]
        x = _ffn(x, *P["ffn2"][i], tm, attn=o.reshape(n, -1), wo=wo)
    order = ("a_k", "a_v", "a_ik", "b_ckv", "b_kr", "c_k", "c_v")
    stack = lambda rs: rs[0][None] if len(rs) == 1 else jnp.stack(rs)
    return x.reshape(b, t, D_MODEL), tuple(stack(rows[k]) for k in order)


PROMPT_CFG = dict(tm=512, tq_a=128, tq=256, kb=256, kb_diff=512, key_major=True)
SAMPLE_CFG = dict(tm=128, tq_a=16, tq=16, kb=1280, key_major=False)


@jax.jit
def _forward(x_prompt, x_sample, past, W):
    P = _prep_weights(W)
    y_p, rows_p = _trunk(x_prompt, 0, None, P, PROMPT_CFG)
    y_s, rows_s = _trunk(x_sample, past["a_k"].shape[2], past, P, SAMPLE_CFG)
    return (y_p, y_s) + rows_p + rows_s


def kernel(x_prompt, x_sample, cache_a_k, cache_a_v, cache_a_idx_k, cache_b_ckv, cache_b_krope, cache_c_k, cache_c_v, ffn1_norm, ffn1_wg, ffn1_wu, ffn1_wd, mix_norm, ffn2_norm, ffn2_wg, ffn2_wu, ffn2_wd, a_w_in, a_q_norm, a_k_norm, a_idx_k_norm, a_w_out, b_w_in, b_q_a_norm, b_kv_a_norm, b_w_uq, b_w_ukv, b_q_nope_norm, b_q_rope_norm, b_k_nope_norm, b_k_rope_norm, b_w_out, c_w_in, c_q_norm, c_k_norm, c_lambda_q1, c_lambda_k1, c_lambda_q2, c_lambda_k2, c_sub_norm, c_w_out):
    W = dict(ffn1_norm=ffn1_norm, ffn1_wg=ffn1_wg, ffn1_wu=ffn1_wu, ffn1_wd=ffn1_wd, mix_norm=mix_norm,
             ffn2_norm=ffn2_norm, ffn2_wg=ffn2_wg, ffn2_wu=ffn2_wu, ffn2_wd=ffn2_wd,
             a_w_in=a_w_in, a_q_norm=a_q_norm, a_k_norm=a_k_norm, a_idx_k_norm=a_idx_k_norm, a_w_out=a_w_out,
             b_w_in=b_w_in, b_q_a_norm=b_q_a_norm, b_kv_a_norm=b_kv_a_norm, b_w_uq=b_w_uq, b_w_ukv=b_w_ukv,
             b_q_nope_norm=b_q_nope_norm, b_q_rope_norm=b_q_rope_norm, b_k_nope_norm=b_k_nope_norm,
             b_k_rope_norm=b_k_rope_norm, b_w_out=b_w_out,
             c_w_in=c_w_in, c_q_norm=c_q_norm, c_k_norm=c_k_norm, c_lambda_q1=c_lambda_q1,
             c_lambda_k1=c_lambda_k1, c_lambda_q2=c_lambda_q2, c_lambda_k2=c_lambda_k2, c_sub_norm=c_sub_norm,
             c_w_out=c_w_out)
    past = dict(a_k=cache_a_k, a_v=cache_a_v, a_ik=cache_a_idx_k, b_ckv=cache_b_ckv, b_kr=cache_b_krope,
                c_k=cache_c_k, c_v=cache_c_v)
    return _forward(x_prompt, x_sample, past, W)
```

```python
import functools
import math

import jax
import jax.numpy as jnp
from jax import lax
from jax.experimental import pallas as pl
from jax.experimental.pallas import tpu as pltpu

F32 = jnp.float32
BF16 = jnp.bfloat16
I32 = jnp.int32

D_MODEL = 1024
DEPTH = 4
CHUNK_SHIFT = 6
N_MIXERS = 3
ROPE_THETA = 10000.0
EPS = 1e-6
D_FF = 2816

A_HEADS = 8
A_KV_HEADS = 2
A_GROUP = A_HEADS // A_KV_HEADS
A_HEAD_DIM = 128
IDX_HEADS = 8
IDX_DIM = 64
TOPK_MAX = 256
LOG2E = math.log2(math.e)
A_SCALE = A_HEAD_DIM ** -0.5 * LOG2E
IDX_W_SCALE = (IDX_HEADS * IDX_DIM) ** -0.5
A_IN = 2120
A_IN_PAD = 2176

B_HEADS = 8
Q_LORA = 384
KV_LORA = 256
NOPE_DIM = 128
ROPE_DIM = 64
V_DIM = 128
B_SCALE = (NOPE_DIM + ROPE_DIM) ** -0.5 * LOG2E
B_IN = 704
B_IN_PAD = 768

C_HEADS = 4
C_DIM = 128
C_SCALE = C_DIM ** -0.5 * LOG2E

LANES = 128
NEG = -1e30
INT_MIN = -(2 ** 31)
VMEM_LIMIT = 56 * 1024 * 1024


def _cparams(n_axes):
    return pltpu.CompilerParams(dimension_semantics=("arbitrary",) * n_axes, vmem_limit_bytes=VMEM_LIMIT)


def _dot(a, b):
    return jnp.dot(a, b, preferred_element_type=F32)


def _dot_nt(a, b):
    return lax.dot_general(a, b, (((1,), (1,)), ((), ())), preferred_element_type=F32)


def _rms(x, g, n):
    ms = jnp.sum(x * x, axis=-1, keepdims=True) * (1.0 / n)
    return x * lax.rsqrt(ms + EPS) * g


def _rope128(x, c, s):
    return x * c + pltpu.roll(x, 64, 1) * s


def _rope64(x, c, sa, sb):
    return x * c + pltpu.roll(x, 96, 1) * sa + pltpu.roll(x, 32, 1) * sb


FFN_CHUNK = 256


def _ffn_body(*refs, fc, mixed):
    if mixed:
        attn_ref, wo_ref, x_ref, g_ref, wg_ref, wu_ref, wd_ref, o_ref = refs
        x = x_ref[...] + _dot(attn_ref[...], wo_ref[...])
    else:
        x_ref, g_ref, wg_ref, wu_ref, wd_ref, o_ref = refs
        x = x_ref[...]
    h = _rms(x, g_ref[...], D_MODEL).astype(BF16)
    y = None
    for c in range(D_FF // fc):
        a = _dot(h, wg_ref[:, c * fc:(c + 1) * fc])
        u = _dot(h, wu_ref[:, c * fc:(c + 1) * fc])
        act = (a * jax.nn.sigmoid(a) * u).astype(BF16)
        part = _dot(act, wd_ref[c * fc:(c + 1) * fc, :])
        y = part if y is None else y + part
    o_ref[...] = x + 0.5 * y


def _const_spec(shape):
    nd = len(shape)
    return pl.BlockSpec(shape, lambda *_: (0,) * nd, pipeline_mode=pl.Buffered(1))


def _row_spec(tm, n):
    return pl.BlockSpec((tm, n), lambda i: (i, 0))


def _ffn(x, g, wg, wu, wd, tm, attn=None, wo=None):
    n = x.shape[0]
    mixed = attn is not None
    pre_specs = [_row_spec(tm, attn.shape[1]), _const_spec(wo.shape)] if mixed else []
    return pl.pallas_call(
        functools.partial(_ffn_body, fc=FFN_CHUNK, mixed=mixed),
        grid=(n // tm,),
        in_specs=pre_specs + [_row_spec(tm, D_MODEL), _const_spec((1, D_MODEL)), _const_spec((D_MODEL, D_FF)),
                              _const_spec((D_MODEL, D_FF)), _const_spec((D_FF, D_MODEL))],
        out_specs=_row_spec(tm, D_MODEL),
        out_shape=jax.ShapeDtypeStruct((n, D_MODEL), F32),
        compiler_params=_cparams(1),
        name="ffn_mixed" if mixed else "ffn",
    )(*((attn, wo) if mixed else ()), x, g, wg, wu, wd)


PROJ_PARTS = 2

def _a_proj_body(x_ref, g_ref, w_ref, qn_ref, kn_ref, ikn_ref, c128_ref, s128_ref, c64_ref, sa64_ref, sb64_ref,
                 q_ref, k32_ref, v32_ref, kbf_ref, vbf_ref, qi_ref, ki32_ref, kidx_ref, wi_ref, y_ref):
    hm = x_ref.shape[0] // PROJ_PARTS
    for part in range(PROJ_PARTS):
        rows = slice(part * hm, (part + 1) * hm)
        y_ref[part] = _dot(_rms(x_ref[rows, :], g_ref[...], D_MODEL).astype(BF16), w_ref[...])
    for part in range(PROJ_PARTS):
        rows = slice(part * hm, (part + 1) * hm)
        c128, s128 = c128_ref[rows, :], s128_ref[rows, :]
        c64, sa64, sb64 = c64_ref[rows, :], sa64_ref[rows, :], sb64_ref[rows, :]
        for hd in range(A_HEADS):
            sl = slice(hd * 128, (hd + 1) * 128)
            qh = _rope128(_rms(y_ref[part, :, sl], qn_ref[...], A_HEAD_DIM), c128, s128)
            q_ref[rows, sl] = (qh * A_SCALE).astype(BF16)
        for hd in range(A_KV_HEADS):
            sl = slice(hd * 128, (hd + 1) * 128)
            kh = _rope128(_rms(y_ref[part, :, 1024 + hd * 128:1024 + (hd + 1) * 128], kn_ref[...], A_HEAD_DIM),
                          c128, s128)
            k32_ref[rows, sl] = kh
            kbf_ref[rows, sl] = kh.astype(BF16)
        v = y_ref[part, :, 1280:1536]
        v32_ref[rows, :] = v
        vbf_ref[rows, :] = v.astype(BF16)
        for p in range(IDX_HEADS // 2):
            sl = slice(p * 128, (p + 1) * 128)
            qi_ref[rows, sl] = _rope64(y_ref[part, :, 1536 + p * 128:1536 + (p + 1) * 128],
                                       c64, sa64, sb64).astype(BF16)
        tail = y_ref[part, :, 2048:2176]
        lane = lax.broadcasted_iota(I32, tail.shape, 1)
        low = lane < IDX_DIM
        kin = jnp.where(low, tail, 0.0)
        ki = _rope64(_rms(kin, ikn_ref[...], IDX_DIM), c64, sa64, sb64)
        ki = jnp.where(low, ki, 0.0)
        ki32_ref[rows, :] = ki[:, :IDX_DIM]
        kidx_ref[rows, 0:128] = ki.astype(BF16)
        kidx_ref[rows, 128:256] = pltpu.roll(ki, 64, 1).astype(BF16)
        wi_ref[rows, :] = pltpu.roll(tail, 64, 1) * IDX_W_SCALE


def _tab_spec(tm, t):
    nt = t // tm
    return pl.BlockSpec((tm, LANES), lambda i: (i % nt, 0))


def _a_proj(x, g, w, qn, kn, ikn, tabs, tm):
    n = x.shape[0]
    t = tabs["c128"].shape[0]
    outs = [(1024, BF16), (256, F32), (256, F32), (256, BF16), (256, BF16), (512, BF16), (IDX_DIM, F32),
            (256, BF16), (LANES, F32)]
    return pl.pallas_call(
        _a_proj_body,
        grid=(n // tm,),
        in_specs=[_row_spec(tm, D_MODEL), _const_spec((1, D_MODEL)), _const_spec((D_MODEL, A_IN_PAD)),
                  _const_spec((1, 128)), _const_spec((1, 128)), _const_spec((1, 128))]
        + [_tab_spec(tm, t)] * 5,
        out_specs=[_row_spec(tm, c) for c, _ in outs],
        out_shape=[jax.ShapeDtypeStruct((n, c), dt) for c, dt in outs],
        scratch_shapes=[pltpu.VMEM((PROJ_PARTS, tm // PROJ_PARTS, A_IN_PAD), F32)],
        compiler_params=_cparams(1),
        name="a_proj",
    )(x, g, w, qn, kn, ikn, tabs["c128"], tabs["s128"], tabs["c64"], tabs["sa64"], tabs["sb64"])


def _b_proj_body(x_ref, g_ref, w_ref, qan_ref, kvan_ref, krn_ref, wuq_ref, qnn_ref, qrn_ref,
                 c64_ref, sa64_ref, sb64_ref, q_ref, ckv_ref, kr32_ref, krbf_ref, qq_ref):
    hm = x_ref.shape[0] // PROJ_PARTS
    for part in range(PROJ_PARTS):
        rows = slice(part * hm, (part + 1) * hm)
        h = _rms(x_ref[rows, :], g_ref[...], D_MODEL).astype(BF16)
        y = _dot(h, w_ref[...])
        ckv_ref[rows, :] = _rms(y[:, 384:640], kvan_ref[...], KV_LORA)
        kr = _rope64(_rms(y[:, 640:768], krn_ref[...], ROPE_DIM), c64_ref[rows, :], sa64_ref[rows, :],
                     sb64_ref[rows, :])
        kr32_ref[rows, :] = kr[:, :ROPE_DIM]
        krbf_ref[rows, :] = kr.astype(BF16)
        qq_ref[part] = _dot(_rms(y[:, 0:384], qan_ref[...], Q_LORA).astype(BF16), wuq_ref[...])
    for part in range(PROJ_PARTS):
        rows = slice(part * hm, (part + 1) * hm)
        c64, sa64, sb64 = c64_ref[rows, :], sa64_ref[rows, :], sb64_ref[rows, :]
        for hd in range(B_HEADS):
            qn = _rms(qq_ref[part, :, hd * 256:hd * 256 + 128], qnn_ref[...], NOPE_DIM)
            qr = _rope64(_rms(qq_ref[part, :, hd * 256 + 128:(hd + 1) * 256], qrn_ref[...], ROPE_DIM),
                         c64, sa64, sb64)
            q_ref[rows, hd * 256:hd * 256 + 128] = (qn * B_SCALE).astype(BF16)
            q_ref[rows, hd * 256 + 128:(hd + 1) * 256] = (qr * B_SCALE).astype(BF16)


def _b_proj(x, g, w, qan, kvan, krn, wuq, qnn, qrn, tabs, tm):
    n = x.shape[0]
    t = tabs["c64"].shape[0]
    outs = [(B_HEADS * 256, BF16), (KV_LORA, F32), (ROPE_DIM, F32), (LANES, BF16)]
    return pl.pallas_call(
        _b_proj_body,
        grid=(n // tm,),
        in_specs=[_row_spec(tm, D_MODEL), _const_spec((1, D_MODEL)), _const_spec((D_MODEL, B_IN_PAD)),
                  _const_spec((1, Q_LORA)), _const_spec((1, KV_LORA)), _const_spec((1, 128)),
                  _const_spec((Q_LORA, 2048)), _const_spec((1, 128)), _const_spec((1, 128))]
        + [_tab_spec(tm, t)] * 3,
        out_specs=[_row_spec(tm, c) for c, _ in outs],
        out_shape=[jax.ShapeDtypeStruct((n, c), dt) for c, dt in outs],
        scratch_shapes=[pltpu.VMEM((PROJ_PARTS, tm // PROJ_PARTS, 2048), F32)],
        compiler_params=_cparams(1),
        name="b_proj",
    )(x, g, w, qan, kvan, krn, wuq, qnn, qrn, tabs["c64"], tabs["sa64"], tabs["sb64"])


def _kv_up_body(ckv_ref, w_ref, knn_ref, kn_ref, v_ref):
    y = _dot(ckv_ref[...].astype(BF16), w_ref[...])
    for hd in range(B_HEADS):
        kn_ref[:, hd * 128:(hd + 1) * 128] = _rms(y[:, hd * 256:hd * 256 + 128], knn_ref[...], NOPE_DIM).astype(BF16)
        v_ref[:, hd * 128:(hd + 1) * 128] = y[:, hd * 256 + 128:(hd + 1) * 256].astype(BF16)


def _kv_up(ckv, w, knn, tm):
    n = ckv.shape[0]
    return pl.pallas_call(
        _kv_up_body,
        grid=(n // tm,),
        in_specs=[_row_spec(tm, KV_LORA), _const_spec((KV_LORA, 2048)), _const_spec((1, 128))],
        out_specs=[_row_spec(tm, 1024), _row_spec(tm, 1024)],
        out_shape=[jax.ShapeDtypeStruct((n, 1024), BF16)] * 2,
        compiler_params=_cparams(1),
        name="kv_up",
    )(ckv, w, knn)


def _c_proj_body(x_ref, g_ref, w_ref, qn_ref, kn_ref, c128_ref, s128_ref,
                 q_ref, k32_ref, v32_ref, kbf_ref, vbf_ref, y_ref):
    hm = x_ref.shape[0] // PROJ_PARTS
    for part in range(PROJ_PARTS):
        rows = slice(part * hm, (part + 1) * hm)
        y_ref[part] = _dot(_rms(x_ref[rows, :], g_ref[...], D_MODEL).astype(BF16), w_ref[...])
    for part in range(PROJ_PARTS):
        rows = slice(part * hm, (part + 1) * hm)
        c128, s128 = c128_ref[rows, :], s128_ref[rows, :]
        for hd in range(2 * C_HEADS):
            sl = slice(hd * 128, (hd + 1) * 128)
            qh = _rope128(_rms(y_ref[part, :, sl], qn_ref[...], C_DIM), c128, s128)
            q_ref[rows, sl] = (qh * C_SCALE).astype(BF16)
            kh = _rope128(_rms(y_ref[part, :, 1024 + hd * 128:1024 + (hd + 1) * 128], kn_ref[...], C_DIM),
                          c128, s128)
            k32_ref[rows, sl] = kh
            kbf_ref[rows, sl] = kh.astype(BF16)
        v = y_ref[part, :, 2048:3072]
        v32_ref[rows, :] = v
        vbf_ref[rows, :] = v.astype(BF16)


def _c_proj(x, g, w, qn, kn, tabs, tm):
    n = x.shape[0]
    t = tabs["c128"].shape[0]
    outs = [(1024, BF16), (1024, F32), (1024, F32), (1024, BF16), (1024, BF16)]
    return pl.pallas_call(
        _c_proj_body,
        grid=(n // tm,),
        in_specs=[_row_spec(tm, D_MODEL), _const_spec((1, D_MODEL)), _const_spec((D_MODEL, 3072)),
                  _const_spec((1, 128)), _const_spec((1, 128))] + [_tab_spec(tm, t)] * 2,
        out_specs=[_row_spec(tm, c) for c, _ in outs],
        out_shape=[jax.ShapeDtypeStruct((n, c), dt) for c, dt in outs],
        scratch_shapes=[pltpu.VMEM((PROJ_PARTS, tm // PROJ_PARTS, 3072), F32)],
        compiler_params=_cparams(1),
        name="c_proj",
    )(x, g, w, qn, kn, tabs["c128"], tabs["s128"])


def _n_key_blocks(i, tq, kb, q_off, s_valid):
    last_chunk = lax.shift_right_logical(q_off + (i + 1) * tq - 1, CHUNK_SHIFT)
    kend = jnp.minimum((last_chunk + 1) << CHUNK_SHIFT, s_valid)
    return lax.div(kend + kb - 1, jnp.int32(kb))


def _q_chunk(i, tq, q_off):
    row = lax.broadcasted_iota(I32, (tq, 1), 0)
    return lax.shift_right_logical(q_off + i * tq + row, CHUNK_SHIFT)


def _k_chunk(ks, kb, s_valid):
    kpos = ks + lax.broadcasted_iota(I32, (1, kb), 1)
    return jnp.where(kpos < s_valid, lax.shift_right_logical(kpos, CHUNK_SHIFT), 2 ** 30)


def _flash_step(s, v, m_ref, l_ref, acc_ref):
    m_prev = m_ref[...]
    m_new = jnp.maximum(m_prev, jnp.max(s, axis=1, keepdims=True))
    alpha = jnp.exp2(m_prev - m_new)
    p = jnp.exp2(s - m_new)
    l_ref[...] = alpha * l_ref[...] + jnp.sum(p, axis=1, keepdims=True)
    acc_ref[...] = alpha * acc_ref[...] + _dot(p.astype(BF16), v)
    m_ref[...] = m_new


def _key_limit(i, tq, q_off, s_valid):
    qpos = q_off + i * tq + lax.broadcasted_iota(I32, (1, tq), 1)
    return jnp.minimum((lax.shift_right_logical(qpos, CHUNK_SHIFT) + 1) << CHUNK_SHIFT, s_valid)


def _n_full_blocks(i, tq, kb, q_off, s_valid):
    first_chunk = lax.shift_right_logical(q_off + i * tq, CHUNK_SHIFT)
    kend = jnp.minimum((first_chunk + 1) << CHUNK_SHIFT, s_valid)
    return lax.shift_right_logical(kend, int(math.log2(kb)))


def _flash_init(m_ref, l_ref, acc_ref):
    m_ref[...] = jnp.full(m_ref.shape, NEG, F32)
    l_ref[...] = jnp.zeros(l_ref.shape, F32)
    acc_ref[...] = jnp.zeros(acc_ref.shape, F32)


def _dsa_body(q_ref, qi_ref, wi_ref, kidx_ref, k_ref, v_ref, o_ref, keys_ref, m_ref, l_ref, acc_ref,
              *, tq, kb, q_off, s_valid, topk):
    i = pl.program_id(1)
    n_blk = _n_key_blocks(i, tq, kb, q_off, s_valid)
    qc = _q_chunk(i, tq, q_off)
    wi = wi_ref[...]
    wcol = [wi[:, hd:hd + 1] for hd in range(IDX_HEADS)]

    def score_block(j, carry):
        ks = pl.multiple_of(j * kb, kb)
        k_lo = kidx_ref[pl.ds(ks, kb), 0:128]
        k_hi = kidx_ref[pl.ds(ks, kb), 128:256]
        sc = jnp.zeros((tq, kb), F32)
        for p in range(IDX_HEADS // 2):
            qp = qi_ref[:, p * 128:(p + 1) * 128]
            sc = sc + wcol[2 * p] * jnp.maximum(_dot_nt(qp, k_lo), 0.0)
            sc = sc + wcol[2 * p + 1] * jnp.maximum(_dot_nt(qp, k_hi), 0.0)
        bits = lax.bitcast_convert_type(sc, I32)
        key = bits ^ ((bits >> 31) & 0x7FFFFFFF)
        keys_ref[j] = jnp.where(_k_chunk(ks, kb, s_valid) <= qc, key, INT_MIN)
        return carry

    lax.fori_loop(0, n_blk, score_block, 0)

    def count(pred):
        def blk(j, c):
            x = jnp.where(pred(keys_ref[j], j * kb), 1.0, 0.0)
            part = x[:, 0:LANES]
            for g in range(1, kb // LANES):
                part = part + x[:, g * LANES:(g + 1) * LANES]
            return c + part
        c = lax.fori_loop(0, n_blk, blk, jnp.zeros((tq, LANES), F32))
        return jnp.sum(c, axis=1, keepdims=True)

    kf = float(topk)

    def bit_step(b, t):
        cand = t + jnp.left_shift(jnp.int32(1), 31 - b)
        cnt = count(lambda kblk, _: kblk >= cand)
        return jnp.where(cnt >= kf, cand, t)

    thr = lax.fori_loop(0, 32, bit_step, jnp.full((tq, 1), INT_MIN, I32))
    need = kf - count(lambda kblk, _: kblk > thr)
    n_eq = count(lambda kblk, _: kblk == thr)
    partial = jnp.logical_and(n_eq > need, thr != INT_MIN)

    @pl.when(jnp.max(jnp.where(partial, 1.0, 0.0)) > 0.0)
    def _():
        lane = lax.broadcasted_iota(I32, (1, kb), 1)

        def idx_step(b, x):
            cand = x + jnp.left_shift(jnp.int32(1), b)
            cnt = count(lambda kblk, k0: jnp.logical_and(kblk == thr, k0 + lane < cand))
            return jnp.where(cnt < need, cand, x)

        nbits = max(1, int(s_valid - 1).bit_length())
        cut = lax.fori_loop(0, nbits, lambda b, x: idx_step(nbits - 1 - b, x), jnp.zeros((tq, 1), I32))

        def demote(j, carry):
            kblk = keys_ref[j]
            drop = jnp.logical_and(jnp.logical_and(kblk == thr, j * kb + lane > cut), partial)
            keys_ref[j] = jnp.where(drop, kblk - 1, kblk)
            return carry

        lax.fori_loop(0, n_blk, demote, 0)

    thr_sel = jnp.maximum(thr, INT_MIN + 1)

    _flash_init(m_ref, l_ref, acc_ref)

    def attend(j, carry):
        ks = pl.multiple_of(j * kb, kb)
        bias = jnp.where(keys_ref[j] >= thr_sel, 0.0, NEG)
        for g in range(A_KV_HEADS):
            qg = jnp.concatenate([q_ref[:, (g * A_GROUP + a) * 128:(g * A_GROUP + a + 1) * 128]
                                  for a in range(A_GROUP)], axis=0)
            s = _dot_nt(qg, k_ref[pl.ds(ks, kb), g * 128:(g + 1) * 128])
            s = (s.reshape(A_GROUP, tq, kb) + bias[None]).reshape(A_GROUP * tq, kb)
            _flash_step(s, v_ref[pl.ds(ks, kb), g * 128:(g + 1) * 128], m_ref.at[g], l_ref.at[g], acc_ref.at[g])
        return carry

    lax.fori_loop(0, n_blk, attend, 0)
    for g in range(A_KV_HEADS):
        o = acc_ref[g] / l_ref[g]
        for a in range(A_GROUP):
            hd = g * A_GROUP + a
            o_ref[:, hd * 128:(hd + 1) * 128] = o[a * tq:(a + 1) * tq].astype(BF16)


def _dsa_attn(q, qi, wi, kidx, k, v, *, tq, kb, q_off, s_valid, topk):
    b, t, _ = q.shape
    s_pad = k.shape[1]
    qspec = lambda c: pl.BlockSpec((None, tq, c), lambda bi, i: (bi, i, 0))
    kspec = lambda c: pl.BlockSpec((None, s_pad, c), lambda bi, i: (bi, 0, 0))
    rows = A_GROUP * tq
    return pl.pallas_call(
        functools.partial(_dsa_body, tq=tq, kb=kb, q_off=q_off, s_valid=s_valid, topk=topk),
        grid=(b, t // tq),
        in_specs=[qspec(1024), qspec(512), qspec(LANES), kspec(256), kspec(256), kspec(256)],
        out_specs=qspec(1024),
        out_shape=jax.ShapeDtypeStruct((b, t, 1024), BF16),
        scratch_shapes=[pltpu.VMEM((s_pad // kb, tq, kb), I32), pltpu.VMEM((A_KV_HEADS, rows, 1), F32),
                        pltpu.VMEM((A_KV_HEADS, rows, 1), F32), pltpu.VMEM((A_KV_HEADS, rows, 128), F32)],
        compiler_params=_cparams(2),
        name="dsa_attn",
    )(q, qi, wi, kidx, k, v)


def _mla_body(q_ref, kn_ref, kr_ref, v_ref, o_ref, m_ref, l_ref, acc_ref, *, tq, kb, q_off, s_valid):
    i = pl.program_id(2)
    n_blk = _n_key_blocks(i, tq, kb, q_off, s_valid)
    qc = _q_chunk(i, tq, q_off)
    _flash_init(m_ref, l_ref, acc_ref)

    def attend(j, carry):
        ks = pl.multiple_of(j * kb, kb)
        kcat = jnp.concatenate([kn_ref[pl.ds(ks, kb), :], kr_ref[pl.ds(ks, kb), :]], axis=1)
        s = _dot_nt(q_ref[...], kcat)
        s = jnp.where(_k_chunk(ks, kb, s_valid) <= qc, s, NEG)
        _flash_step(s, v_ref[pl.ds(ks, kb), :], m_ref, l_ref, acc_ref)
        return carry

    lax.fori_loop(0, n_blk, attend, 0)
    o_ref[...] = (acc_ref[...] / l_ref[...]).astype(BF16)


def _mla_attn(q, kn, kr, v, *, tq, kb, q_off, s_valid):
    b, t, _ = q.shape
    s_pad = kn.shape[1]
    return pl.pallas_call(
        functools.partial(_mla_body, tq=tq, kb=kb, q_off=q_off, s_valid=s_valid),
        grid=(b, B_HEADS, t // tq),
        in_specs=[pl.BlockSpec((None, tq, 256), lambda bi, h, i: (bi, i, h)),
                  pl.BlockSpec((None, s_pad, 128), lambda bi, h, i: (bi, 0, h)),
                  pl.BlockSpec((None, s_pad, 128), lambda bi, h, i: (bi, 0, 0)),
                  pl.BlockSpec((None, s_pad, 128), lambda bi, h, i: (bi, 0, h))],
        out_specs=pl.BlockSpec((None, tq, 128), lambda bi, h, i: (bi, i, h)),
        out_shape=jax.ShapeDtypeStruct((b, t, B_HEADS * V_DIM), BF16),
        scratch_shapes=[pltpu.VMEM((tq, 1), F32), pltpu.VMEM((tq, 1), F32), pltpu.VMEM((tq, 128), F32)],
        compiler_params=_cparams(3),
        name="mla_attn",
    )(q, kn, kr, v)


def _diff_body(q_ref, k_ref, v_ref, lq1_ref, lk1_ref, lq2_ref, lk2_ref, sn_ref, o_ref, m_ref, l_ref, acc_ref,
               *, tq, kb, q_off, s_valid, lam_init):
    i = pl.program_id(2)
    n_blk = _n_key_blocks(i, tq, kb, q_off, s_valid)
    qc = _q_chunk(i, tq, q_off)
    _flash_init(m_ref, l_ref, acc_ref)

    def attend(j, carry):
        ks = pl.multiple_of(j * kb, kb)
        ok = _k_chunk(ks, kb, s_valid) <= qc
        v = v_ref[pl.ds(ks, kb), :]
        for p in range(2):
            s = _dot_nt(q_ref[:, p * 128:(p + 1) * 128], k_ref[pl.ds(ks, kb), p * 128:(p + 1) * 128])
            s = jnp.where(ok, s, NEG)
            _flash_step(s, v, m_ref.at[p], l_ref.at[p], acc_ref.at[p])
        return carry

    lax.fori_loop(0, n_blk, attend, 0)
    lam = (jnp.exp(jnp.sum(lq1_ref[...] * lk1_ref[...], axis=1, keepdims=True))
           - jnp.exp(jnp.sum(lq2_ref[...] * lk2_ref[...], axis=1, keepdims=True)) + lam_init)
    o = acc_ref[0] / l_ref[0] - lam * (acc_ref[1] / l_ref[1])
    o_ref[...] = (_rms(o, sn_ref[...], 2 * C_DIM) * (1.0 - lam_init)).astype(BF16)


def _diff_attn(q, k, v, lq1, lk1, lq2, lk2, sn, *, tq, kb, q_off, s_valid, lam_init):
    b, t, _ = q.shape
    s_pad = k.shape[1]
    vec = lambda c: pl.BlockSpec((1, c), lambda bi, h, i: (0, 0))
    return pl.pallas_call(
        functools.partial(_diff_body, tq=tq, kb=kb, q_off=q_off, s_valid=s_valid, lam_init=lam_init),
        grid=(b, C_HEADS, t // tq),
        in_specs=[pl.BlockSpec((None, tq, 256), lambda bi, h, i: (bi, i, h)),
                  pl.BlockSpec((None, s_pad, 256), lambda bi, h, i: (bi, 0, h)),
                  pl.BlockSpec((None, s_pad, 256), lambda bi, h, i: (bi, 0, h)),
                  vec(128), vec(128), vec(128), vec(128), vec(256)],
        out_specs=pl.BlockSpec((None, tq, 256), lambda bi, h, i: (bi, i, h)),
        out_shape=jax.ShapeDtypeStruct((b, t, C_HEADS * 2 * C_DIM), BF16),
        scratch_shapes=[pltpu.VMEM((2, tq, 1), F32), pltpu.VMEM((2, tq, 1), F32), pltpu.VMEM((2, tq, 256), F32)],
        compiler_params=_cparams(3),
        name="diff_attn",
    )(q, k, v, lq1, lk1, lq2, lk2, sn)


SCORE_BLOCKS = 2
SEARCH_BLOCKS = 4
SEARCH_BITS_UNCHECKED = 16
SEARCH_BITS_PER_CHECK = 4


def _stage_bufs(n_chain, kb, r):
    return [pltpu.VMEM((n_chain, 2, kb, r), F32), pltpu.VMEM((n_chain, 2, kb, r), BF16),
            pltpu.VMEM((n_chain, 2, 1, r), F32)]


def _staged_flash_t(bufs, state, lo, hi, n_kb, qk, prep, vt_of, first=True, last=True):
    s_ref, p_ref, a_ref = bufs
    m_ref, l_ref, acc_ref = state
    n_chain = s_ref.shape[0]
    clamp = lambda j: jnp.clip(j, 0, n_kb - 1)

    def softmax(c, j, slot):
        st = prep(c, j, s_ref[c, slot])
        m_prev = m_ref[c]
        m_new = jnp.maximum(m_prev, jnp.max(st, axis=0, keepdims=True))
        alpha = jnp.exp2(m_prev - m_new)
        p = jnp.exp2(st - m_new)
        l_ref[c] = alpha * l_ref[c] + jnp.sum(p, axis=0, keepdims=True)
        m_ref[c] = m_new
        p_ref[c, slot] = p.astype(BF16)
        a_ref[c, slot] = alpha

    def values(c, j, slot):
        acc_ref[c] = a_ref[c, slot] * acc_ref[c] + _dot(vt_of(c, clamp(j)), p_ref[c, slot])

    if first:
        for c in range(n_chain):
            s_ref[c, 0] = qk(c, clamp(lo))
            p_ref[c, 1] = jnp.zeros(p_ref.shape[2:], BF16)
            a_ref[c, 1] = jnp.ones(a_ref.shape[2:], F32)

    def turn(t, carry):
        j0 = lo + 2 * t
        for c in range(n_chain):
            s_ref[c, 1] = qk(c, clamp(j0 + 1))
        for c in range(n_chain):
            softmax(c, j0, 0)
        for c in range(n_chain):
            values(c, j0 - 1, 1)
        for c in range(n_chain):
            s_ref[c, 0] = qk(c, clamp(j0 + 2))
        for c in range(n_chain):
            softmax(c, j0 + 1, 1)
        for c in range(n_chain):
            values(c, j0, 0)
        return carry

    n_turn = lax.shift_right_logical(hi - lo + 1, 1)
    lax.fori_loop(0, n_turn, turn, 0)
    if last:
        for c in range(n_chain):
            values(c, lo + 2 * n_turn - 1, 1)


def _dsa_t_body(q_ref, qi_ref, wi_ref, kidx_ref, k_ref, vt_ref, o_ref, keys_ref, s_ref, p_ref, a_ref,
                m_ref, l_ref, acc_ref, *, tq, kb, q_off, s_valid, topk):
    i = pl.program_id(1)
    n_blk = _n_key_blocks(i, tq, kb, q_off, s_valid)
    n_kb = keys_ref.shape[0]
    n_sb = lax.shift_right_logical(n_blk + SEARCH_BLOCKS - 1, int(math.log2(SEARCH_BLOCKS)))
    limit = _key_limit(i, tq, q_off, s_valid)
    kidx0 = lax.broadcasted_iota(I32, (kb, tq), 0)
    wit = wi_ref[...].T
    wrow = [wit[hd:hd + 1, :] for hd in range(IDX_HEADS)]

    def score_blocks(jj, carry):
        ks = pl.multiple_of(jj * (SCORE_BLOCKS * kb), SCORE_BLOCKS * kb)
        rows = pl.ds(ks, SCORE_BLOCKS * kb)
        q_all = jnp.concatenate([qi_ref[:, p * 128:(p + 1) * 128] for p in range(IDX_HEADS // 2)], axis=0)
        s_lo = _dot_nt(kidx_ref[rows, 0:128], q_all)
        s_hi = _dot_nt(kidx_ref[rows, 128:256], q_all)
        for u in range(SCORE_BLOCKS):
            blk = slice(u * kb, (u + 1) * kb)
            sc = jnp.zeros((kb, tq), F32)
            for p in range(IDX_HEADS // 2):
                sc = sc + wrow[2 * p] * jnp.maximum(s_lo[blk, p * tq:(p + 1) * tq], 0.0)
                sc = sc + wrow[2 * p + 1] * jnp.maximum(s_hi[blk, p * tq:(p + 1) * tq], 0.0)
            bits = lax.bitcast_convert_type(sc, I32)
            key = bits ^ ((bits >> 31) & 0x7FFFFFFF)
            keys_ref[jj * SCORE_BLOCKS + u] = jnp.where(kidx0 < limit - (ks + u * kb), key, INT_MIN)
        return carry

    def pad_block(j, carry):
        keys_ref[j] = jnp.full((kb, tq), INT_MIN, I32)
        return carry

    n_scored = lax.shift_right_logical(n_blk + SCORE_BLOCKS - 1, int(math.log2(SCORE_BLOCKS)))
    lax.fori_loop(0, n_scored, score_blocks, 0)
    lax.fori_loop(n_scored * SCORE_BLOCKS, n_sb * SEARCH_BLOCKS, pad_block, 0)

    def count(pred):
        def group(jj, cs):
            out = []
            for u in range(SEARCH_BLOCKS):
                j = jj * SEARCH_BLOCKS + u
                x = jnp.where(pred(keys_ref[j], j * kb), 1.0, 0.0)
                out.append(cs[u] + jnp.sum(x.reshape(kb // 8, 8, tq), axis=0))
            return tuple(out)
        cs = lax.fori_loop(0, n_sb, group, tuple(jnp.zeros((8, tq), F32) for _ in range(SEARCH_BLOCKS)))
        return jnp.sum(functools.reduce(lambda a, b: a + b, cs), axis=0, keepdims=True)

    kf = float(topk)

    def bit_step(b, state):
        t, n_ge = state
        cand = t + jnp.left_shift(jnp.int32(1), 31 - b)
        cnt = count(lambda kblk, _: kblk >= cand)
        take = cnt >= kf
        return jnp.where(take, cand, t), jnp.where(take, cnt, n_ge)

    def unsettled(n_ge):
        open_ = jnp.logical_and(n_ge != kf, limit.astype(F32) > kf)
        return jnp.max(jnp.where(open_, 1.0, 0.0)) > 0.0

    def more_bits(state):
        b, _, n_ge = state
        return jnp.logical_and(b < 32, unsettled(n_ge))

    def four_bits(state):
        b, t, n_ge = state
        for u in range(SEARCH_BITS_PER_CHECK):
            t, n_ge = bit_step(b + u, (t, n_ge))
        return b + SEARCH_BITS_PER_CHECK, t, n_ge

    start = (jnp.full((1, tq), INT_MIN, I32), jnp.full((1, tq), float(n_kb * kb), F32))
    head = lax.fori_loop(0, SEARCH_BITS_UNCHECKED, bit_step, start)
    _, thr, _ = lax.while_loop(more_bits, four_bits, (jnp.int32(SEARCH_BITS_UNCHECKED),) + head)
    need = kf - count(lambda kblk, _: kblk > thr)
    n_eq = count(lambda kblk, _: kblk == thr)
    partial = jnp.logical_and(n_eq > need, thr != INT_MIN)

    @pl.when(jnp.max(jnp.where(partial, 1.0, 0.0)) > 0.0)
    def _():
        def idx_step(b, x):
            cand = x + jnp.left_shift(jnp.int32(1), b)
            cnt = count(lambda kblk, k0: jnp.logical_and(kblk == thr, kidx0 < cand - k0))
            return jnp.where(cnt < need, cand, x)

        nbits = max(1, int(s_valid - 1).bit_length())
        cut = lax.fori_loop(0, nbits, lambda b, x: idx_step(nbits - 1 - b, x), jnp.zeros((1, tq), I32))

        def demote(j, carry):
            kblk = keys_ref[j]
            drop = jnp.logical_and(jnp.logical_and(kblk == thr, kidx0 > cut - j * kb), partial)
            keys_ref[j] = jnp.where(drop, kblk - 1, kblk)
            return carry

        lax.fori_loop(0, n_blk, demote, 0)

    thr_sel = jnp.maximum(thr, INT_MIN + 1)

    _flash_init(m_ref, l_ref, acc_ref)
    n_pair = A_HEADS // 2

    def to_bias(j, carry):
        keys_ref[j] = lax.bitcast_convert_type(jnp.where(keys_ref[j] >= thr_sel, 0.0, NEG), I32)
        return carry

    lax.fori_loop(0, n_sb * SEARCH_BLOCKS, to_bias, 0)
    for g in range(A_KV_HEADS):
        def qk(c, j, g=g):
            p = g * (A_GROUP // 2) + c
            qp = jnp.concatenate([q_ref[:, (2 * p) * 128:(2 * p + 1) * 128],
                                  q_ref[:, (2 * p + 1) * 128:(2 * p + 2) * 128]], axis=0)
            return _dot_nt(k_ref[pl.ds(pl.multiple_of(j * kb, kb), kb), g * 128:(g + 1) * 128], qp)

        def prep(c, j, st):
            bias = lax.bitcast_convert_type(keys_ref[j], F32)
            return st + jnp.concatenate([bias, bias], axis=1)

        def vt_of(c, j, g=g):
            return vt_ref[j, g * 128:(g + 1) * 128, :]

        chains = pl.ds(g * (A_GROUP // 2), A_GROUP // 2)
        _staged_flash_t((s_ref.at[chains], p_ref.at[chains], a_ref.at[chains]),
                        (m_ref.at[chains], l_ref.at[chains], acc_ref.at[chains]), 0, n_blk, n_kb, qk, prep, vt_of)
    for p in range(n_pair):
        ot = acc_ref[p] / l_ref[p]
        for a in range(2):
            hd = 2 * p + a
            o_ref[:, hd * 128:(hd + 1) * 128] = ot[:, a * tq:(a + 1) * tq].T.astype(BF16)


def _dsa_attn_t(q, qi, wi, kidx, k, vt, *, tq, kb, q_off, s_valid, topk):
    b, t, _ = q.shape
    s_pad = k.shape[1]
    n_kb = s_pad // kb
    qspec = lambda c: pl.BlockSpec((None, tq, c), lambda bi, i: (bi, i, 0))
    kspec = lambda c: pl.BlockSpec((None, s_pad, c), lambda bi, i: (bi, 0, 0))
    n_pair = A_HEADS // 2
    return pl.pallas_call(
        functools.partial(_dsa_t_body, tq=tq, kb=kb, q_off=q_off, s_valid=s_valid, topk=topk),
        grid=(b, t // tq),
        in_specs=[qspec(1024), qspec(512), qspec(LANES),
                  kspec(256), kspec(256), pl.BlockSpec((None, n_kb, 256, kb), lambda bi, i: (bi, 0, 0, 0))],
        out_specs=qspec(1024),
        out_shape=jax.ShapeDtypeStruct((b, t, 1024), BF16),
        scratch_shapes=[pltpu.VMEM((n_kb, kb, tq), I32)] + _stage_bufs(n_pair, kb, 2 * tq)
        + [pltpu.VMEM((n_pair, 1, 2 * tq), F32), pltpu.VMEM((n_pair, 1, 2 * tq), F32),
           pltpu.VMEM((n_pair, 128, 2 * tq), F32)],
        compiler_params=_cparams(2),
        name="dsa_attn_t",
    )(q, qi, wi, kidx, k, vt)


def _causal_flash_t(bufs, state, i, n_kb, tq, kb, q_off, s_valid, qk, vt_of):
    n_blk = _n_key_blocks(i, tq, kb, q_off, s_valid)
    n_plain = _n_full_blocks(i, tq, kb, q_off, s_valid) & -2
    limit = _key_limit(i, tq, q_off, s_valid)

    def masked(c, j, st):
        return jnp.where(lax.broadcasted_iota(I32, (kb, tq), 0) < limit - j * kb, st, NEG)

    _staged_flash_t(bufs, state, 0, n_plain, n_kb, qk, lambda c, j, st: st, vt_of, last=False)
    _staged_flash_t(bufs, state, n_plain, n_blk, n_kb, qk, masked, vt_of, first=False)


def _mla_t_body(qt_ref, kn_ref, kr_ref, vt_ref, o_ref, s_ref, p_ref, a_ref, m_ref, l_ref, acc_ref,
                *, tq, kb, q_off, s_valid):
    i = pl.program_id(2)
    _flash_init(m_ref, l_ref, acc_ref)

    def qk(a, j):
        ks = pl.multiple_of(j * kb, kb)
        kcat = jnp.concatenate([kn_ref[pl.ds(ks, kb), a * 128:(a + 1) * 128], kr_ref[pl.ds(ks, kb), :]], axis=1)
        return _dot(kcat, qt_ref[a * 256:(a + 1) * 256, :])

    def vt_of(a, j):
        return vt_ref[j, a * 128:(a + 1) * 128, :]

    _causal_flash_t((s_ref, p_ref, a_ref), (m_ref, l_ref, acc_ref), i, vt_ref.shape[0], tq, kb, q_off, s_valid,
                    qk, vt_of)
    for a in range(2):
        o_ref[:, a * 128:(a + 1) * 128] = (acc_ref[a] / l_ref[a]).T.astype(BF16)


def _mla_attn_t(qt, kn, kr, vt, *, tq, kb, q_off, s_valid):
    b, _, t = qt.shape
    s_pad = kn.shape[1]
    n_kb = s_pad // kb
    return pl.pallas_call(
        functools.partial(_mla_t_body, tq=tq, kb=kb, q_off=q_off, s_valid=s_valid),
        grid=(b, B_HEADS // 2, t // tq),
        in_specs=[pl.BlockSpec((None, 512, tq), lambda bi, h, i: (bi, h, i)),
                  pl.BlockSpec((None, s_pad, 256), lambda bi, h, i: (bi, 0, h)),
                  pl.BlockSpec((None, s_pad, 128), lambda bi, h, i: (bi, 0, 0)),
                  pl.BlockSpec((None, n_kb, 256, kb), lambda bi, h, i: (bi, 0, h, 0))],
        out_specs=pl.BlockSpec((None, tq, 256), lambda bi, h, i: (bi, i, h)),
        out_shape=jax.ShapeDtypeStruct((b, t, B_HEADS * V_DIM), BF16),
        scratch_shapes=_stage_bufs(2, kb, tq) + [pltpu.VMEM((2, 1, tq), F32), pltpu.VMEM((2, 1, tq), F32),
                                                 pltpu.VMEM((2, 128, tq), F32)],
        compiler_params=_cparams(3),
        name="mla_attn_t",
    )(qt, kn, kr, vt)


def _diff_t_body(q_ref, k_ref, vt_ref, lq1_ref, lk1_ref, lq2_ref, lk2_ref, sn_ref, o_ref, s_ref, p_ref, a_ref,
                 m_ref, l_ref, acc_ref, *, tq, kb, q_off, s_valid, lam_init):
    i = pl.program_id(2)
    _flash_init(m_ref, l_ref, acc_ref)

    def qk(p, j):
        ks = pl.multiple_of(j * kb, kb)
        return _dot_nt(k_ref[pl.ds(ks, kb), p * 128:(p + 1) * 128], q_ref[:, p * 128:(p + 1) * 128])

    _causal_flash_t((s_ref, p_ref, a_ref), (m_ref, l_ref, acc_ref), i, vt_ref.shape[0], tq, kb, q_off, s_valid,
                    qk, lambda p, j: vt_ref[j])
    lam = (jnp.exp(jnp.sum(lq1_ref[...] * lk1_ref[...], axis=1, keepdims=True))
           - jnp.exp(jnp.sum(lq2_ref[...] * lk2_ref[...], axis=1, keepdims=True)) + lam_init)
    ot = acc_ref[0] / l_ref[0] - lam * (acc_ref[1] / l_ref[1])
    o = jnp.concatenate([ot[0:128, :].T, ot[128:256, :].T], axis=1)
    o_ref[...] = (_rms(o, sn_ref[...], 2 * C_DIM) * (1.0 - lam_init)).astype(BF16)


def _diff_attn_t(q, k, vt, lq1, lk1, lq2, lk2, sn, *, tq, kb, q_off, s_valid, lam_init):
    b, t, _ = q.shape
    s_pad = k.shape[1]
    n_kb = s_pad // kb
    vec = lambda c: pl.BlockSpec((1, c), lambda bi, h, i: (0, 0))
    return pl.pallas_call(
        functools.partial(_diff_t_body, tq=tq, kb=kb, q_off=q_off, s_valid=s_valid, lam_init=lam_init),
        grid=(b, C_HEADS, t // tq),
        in_specs=[pl.BlockSpec((None, tq, 256), lambda bi, h, i: (bi, i, h)),
                  pl.BlockSpec((None, s_pad, 256), lambda bi, h, i: (bi, 0, h)),
                  pl.BlockSpec((None, n_kb, 256, kb), lambda bi, h, i: (bi, 0, h, 0)),
                  vec(128), vec(128), vec(128), vec(128), vec(256)],
        out_specs=pl.BlockSpec((None, tq, 256), lambda bi, h, i: (bi, i, h)),
        out_shape=jax.ShapeDtypeStruct((b, t, C_HEADS * 2 * C_DIM), BF16),
        scratch_shapes=_stage_bufs(2, kb, tq) + [pltpu.VMEM((2, 1, tq), F32), pltpu.VMEM((2, 1, tq), F32),
                                                 pltpu.VMEM((2, 256, tq), F32)],
        compiler_params=_cparams(3),
        name="diff_attn_t",
    )(q, k, vt, lq1, lk1, lq2, lk2, sn)


def _blocked_t(v, kb):
    b, s, c = v.shape
    return jnp.swapaxes(v.reshape(b, s // kb, kb, c), 2, 3)


def _rope_tables(pos, reps):
    p = pos.astype(F32)[:, None]
    inv64 = jnp.power(ROPE_THETA, -jnp.arange(64, dtype=F32) / 64)
    inv32 = jnp.power(ROPE_THETA, -jnp.arange(32, dtype=F32) / 32)
    c, s = jnp.cos(p * inv64), jnp.sin(p * inv64)
    c3, s3 = jnp.cos(p * inv32), jnp.sin(p * inv32)
    z = jnp.zeros_like(s3)
    tabs = {
        "c128": jnp.concatenate([c, c], axis=1),
        "s128": jnp.concatenate([-s, s], axis=1),
        "c64": jnp.concatenate([c3, c3, c3, c3], axis=1),
        "sa64": jnp.concatenate([-s3, z, -s3, z], axis=1),
        "sb64": jnp.concatenate([z, s3, z, s3], axis=1),
    }
    return {k: jnp.tile(v, (reps, 1)) for k, v in tabs.items()}


def _pad_cols(w, n):
    return jnp.pad(w, ((0, 0), (0, n - w.shape[1])))


def _pad_lanes(g, n=128):
    g = g.reshape(1, -1)
    return jnp.pad(g, ((0, 0), (0, n - g.shape[1])))


def _prep_weights(W):
    P = {}
    for nm in ("ffn1", "ffn2"):
        P[nm] = [(W[nm + "_norm"][i].reshape(1, -1), W[nm + "_wg"][i].astype(BF16), W[nm + "_wu"][i].astype(BF16),
                  W[nm + "_wd"][i].astype(BF16)) for i in range(DEPTH)]
    P["mix_norm"] = [W["mix_norm"][i].reshape(1, -1) for i in range(DEPTH)]
    P["a"] = [dict(w=_pad_cols(W["a_w_in"][j], A_IN_PAD).astype(BF16), qn=W["a_q_norm"][j].reshape(1, -1),
                   kn=W["a_k_norm"][j].reshape(1, -1), ikn=_pad_lanes(W["a_idx_k_norm"][j]),
                   wo=W["a_w_out"][j].astype(BF16)) for j in range(W["a_w_in"].shape[0])]
    P["b"] = []
    for j in range(W["b_w_in"].shape[0]):
        wuq = W["b_w_uq"][j].reshape(Q_LORA, B_HEADS, NOPE_DIM + ROPE_DIM)
        wuq_rope = jnp.pad(wuq[:, :, NOPE_DIM:], ((0, 0), (0, 0), (0, 128 - ROPE_DIM)))
        wuq = jnp.concatenate([wuq[:, :, :NOPE_DIM], wuq_rope], axis=2).reshape(Q_LORA, -1)
        P["b"].append(dict(
            w=_pad_cols(W["b_w_in"][j], B_IN_PAD).astype(BF16), qan=W["b_q_a_norm"][j].reshape(1, -1),
            kvan=W["b_kv_a_norm"][j].reshape(1, -1), krn=_pad_lanes(W["b_k_rope_norm"][j]), wuq=wuq.astype(BF16),
            qnn=W["b_q_nope_norm"][j].reshape(1, -1), qrn=_pad_lanes(W["b_q_rope_norm"][j]),
            wukv=W["b_w_ukv"][j].astype(BF16), knn=W["b_k_nope_norm"][j].reshape(1, -1),
            wo=W["b_w_out"][j].astype(BF16)))
    P["c"] = [dict(w=W["c_w_in"][j].astype(BF16), qn=W["c_q_norm"][j].reshape(1, -1),
                   kn=W["c_k_norm"][j].reshape(1, -1), lq1=W["c_lambda_q1"][j].reshape(1, -1),
                   lk1=W["c_lambda_k1"][j].reshape(1, -1), lq2=W["c_lambda_q2"][j].reshape(1, -1),
                   lk2=W["c_lambda_k2"][j].reshape(1, -1), sn=W["c_sub_norm"][j].reshape(1, -1),
                   wo=W["c_w_out"][j].astype(BF16)) for j in range(W["c_w_in"].shape[0])]
    return P


def _with_past(past, new, s_pad):
    x = new if past is None else jnp.concatenate([past.astype(new.dtype), new], axis=1)
    return x if x.shape[1] == s_pad else jnp.pad(x, ((0, 0), (0, s_pad - x.shape[1]), (0, 0)))


def _trunk(x, offset, past, P, cfg):
    b, t, _ = x.shape
    n = b * t
    tm, tq_a, tq, kb, key_major = cfg["tm"], cfg["tq_a"], cfg["tq"], cfg["kb"], cfg["key_major"]
    p_len = 0 if past is None else past["a_k"].shape[2]
    s_valid = p_len + t
    s_pad = -(-s_valid // kb) * kb
    tabs = _rope_tables(offset + jnp.arange(t, dtype=I32), tm // t if tm > t else 1)
    att = dict(kb=kb, q_off=offset, s_valid=s_valid)
    rows = {k: [] for k in ("a_k", "a_v", "a_ik", "b_ckv", "b_kr", "c_k", "c_v")}
    x = x.reshape(n, D_MODEL)
    r3 = lambda a: a.reshape(b, t, a.shape[-1])
    t3 = lambda a: jnp.swapaxes(r3(a), 1, 2)
    pj = lambda nm, j: None if past is None else past[nm][j].reshape(b, p_len, -1)
    for i in range(DEPTH):
        x = _ffn(x, *P["ffn1"][i], cfg["tm_ffn"])
        kind, j = i % N_MIXERS, i // N_MIXERS
        g = P["mix_norm"][i]
        if kind == 0:
            pa = P["a"][j]
            q, k32, v32, kbf, vbf, qi, ki32, kidx, wi = _a_proj(x, g, pa["w"], pa["qn"], pa["kn"], pa["ikn"], tabs, tm)
            rows["a_k"].append(k32.reshape(b, t, A_KV_HEADS, A_HEAD_DIM))
            rows["a_v"].append(v32.reshape(b, t, A_KV_HEADS, A_HEAD_DIM))
            rows["a_ik"].append(ki32.reshape(b, t, IDX_DIM))
            pik = pj("a_ik", j)
            if pik is not None:
                z = jnp.zeros_like(pik)
                pik = jnp.concatenate([pik, z, z, pik], axis=-1)
            kidx_all = _with_past(pik, r3(kidx), s_pad)
            k_all = _with_past(pj("a_k", j), r3(kbf), s_pad)
            v_all = _with_past(pj("a_v", j), r3(vbf), s_pad)
            topk = min(TOPK_MAX, s_valid // 4)
            if key_major:
                o = _dsa_attn_t(r3(q), r3(qi), r3(wi), kidx_all, k_all, _blocked_t(v_all, kb),
                                tq=tq_a, topk=topk, **att)
            else:
                o = _dsa_attn(r3(q), r3(qi), r3(wi), kidx_all, k_all, v_all, tq=tq_a, topk=topk, **att)
        elif kind == 1:
            pb = P["b"][j]
            q, ckv, kr32, krbf = _b_proj(x, g, pb["w"], pb["qan"], pb["kvan"], pb["krn"], pb["wuq"], pb["qnn"],
                                         pb["qrn"], tabs, tm)
            rows["b_ckv"].append(ckv.reshape(b, t, KV_LORA))
            rows["b_kr"].append(kr32.reshape(b, t, ROPE_DIM))
            ckv_all = _with_past(pj("b_ckv", j), r3(ckv), s_pad)
            pkr = pj("b_kr", j)
            if pkr is not None:
                pkr = jnp.concatenate([pkr, jnp.zeros_like(pkr)], axis=-1)
            kn, v = _kv_up(ckv_all.reshape(b * s_pad, KV_LORA), pb["wukv"], pb["knn"], math.gcd(b * s_pad, 512))
            kn, v, kr_all = kn.reshape(b, s_pad, -1), v.reshape(b, s_pad, -1), _with_past(pkr, r3(krbf), s_pad)
            if key_major:
                o = _mla_attn_t(t3(q), kn, kr_all, _blocked_t(v, kb), tq=tq, **att)
            else:
                o = _mla_attn(r3(q), kn, kr_all, v, tq=tq, **att)
        else:
            pc = P["c"][j]
            q, k32, v32, kbf, vbf = _c_proj(x, g, pc["w"], pc["qn"], pc["kn"], tabs, tm)
            rows["c_k"].append(k32.reshape(b, t, C_HEADS, 2, C_DIM))
            rows["c_v"].append(v32.reshape(b, t, C_HEADS, 2 * C_DIM))
            k_all = _with_past(pj("c_k", j), r3(kbf), s_pad)
            v_all = _with_past(pj("c_v", j), r3(vbf), s_pad)
            lam = (pc["lq1"], pc["lk1"], pc["lq2"], pc["lk2"], pc["sn"])
            lam_init = 0.8 - 0.6 * math.exp(-0.3 * i)
            if key_major:
                att_c = dict(att, kb=cfg["kb_diff"])
                o = _diff_attn_t(r3(q), k_all, _blocked_t(v_all, cfg["kb_diff"]), *lam, tq=tq, lam_init=lam_init,
                                 **att_c)
            else:
                o = _diff_attn(r3(q), k_all, v_all, *lam, tq=tq, lam_init=lam_init, **att)
        wo = (P["a"], P["b"], P["c"])[kind][j]["wo"]
        x = _ffn(x, *P["ffn2"][i], cfg["tm_ffn"], attn=o.reshape(n, -1), wo=wo)
    order = ("a_k", "a_v", "a_ik", "b_ckv", "b_kr", "c_k", "c_v")
    stack = lambda rs: rs[0][None] if len(rs) == 1 else jnp.stack(rs)
    return x.reshape(b, t, D_MODEL), tuple(stack(rows[k]) for k in order)


PROMPT_CFG = dict(tm=512, tm_ffn=1024, tq_a=128, tq=256, kb=256, kb_diff=512, key_major=True)
SAMPLE_CFG = dict(tm=128, tm_ffn=128, tq_a=16, tq=16, kb=1280, key_major=False)


@jax.jit
def _forward(x_prompt, x_sample, past, W):
    P = _prep_weights(W)
    y_p, rows_p = _trunk(x_prompt, 0, None, P, PROMPT_CFG)
    y_s, rows_s = _trunk(x_sample, past["a_k"].shape[2], past, P, SAMPLE_CFG)
    return (y_p, y_s) + rows_p + rows_s


def kernel(x_prompt, x_sample, cache_a_k, cache_a_v, cache_a_idx_k, cache_b_ckv, cache_b_krope, cache_c_k, cache_c_v, ffn1_norm, ffn1_wg, ffn1_wu, ffn1_wd, mix_norm, ffn2_norm, ffn2_wg, ffn2_wu, ffn2_wd, a_w_in, a_q_norm, a_k_norm, a_idx_k_norm, a_w_out, b_w_in, b_q_a_norm, b_kv_a_norm, b_w_uq, b_w_ukv, b_q_nope_norm, b_q_rope_norm, b_k_nope_norm, b_k_rope_norm, b_w_out, c_w_in, c_q_norm, c_k_norm, c_lambda_q1, c_lambda_k1, c_lambda_q2, c_lambda_k2, c_sub_norm, c_w_out):
    W = dict(ffn1_norm=ffn1_norm, ffn1_wg=ffn1_wg, ffn1_wu=ffn1_wu, ffn1_wd=ffn1_wd, mix_norm=mix_norm,
             ffn2_norm=ffn2_norm, ffn2_wg=ffn2_wg, ffn2_wu=ffn2_wu, ffn2_wd=ffn2_wd,
             a_w_in=a_w_in, a_q_norm=a_q_norm, a_k_norm=a_k_norm, a_idx_k_norm=a_idx_k_norm, a_w_out=a_w_out,
             b_w_in=b_w_in, b_q_a_norm=b_q_a_norm, b_kv_a_norm=b_kv_a_norm, b_w_uq=b_w_uq, b_w_ukv=b_w_ukv,
             b_q_nope_norm=b_q_nope_norm, b_q_rope_norm=b_q_rope_norm, b_k_nope_norm=b_k_nope_norm,
             b_k_rope_norm=b_k_rope_norm, b_w_out=b_w_out,
             c_w_in=c_w_in, c_q_norm=c_q_norm, c_k_norm=c_k_norm, c_lambda_q1=c_lambda_q1,
             c_lambda_k1=c_lambda_k1, c_lambda_q2=c_lambda_q2, c_lambda_k2=c_lambda_k2, c_sub_norm=c_sub_norm,
             c_w_out=c_w_out)
    past = dict(a_k=cache_a_k, a_v=cache_a_v, a_ik=cache_a_idx_k, b_ckv=cache_b_ckv, b_kr=cache_b_krope,
                c_k=cache_c_k, c_v=cache_c_v)
    return _forward(x_prompt, x_sample, past, W)
```

```python
import functools
import math

import jax
import jax.numpy as jnp
from jax import lax
from jax.experimental import pallas as pl
from jax.experimental.pallas import tpu as pltpu

F32 = jnp.float32
BF16 = jnp.bfloat16
I32 = jnp.int32

D_MODEL = 1024
DEPTH = 4
CHUNK_SHIFT = 6
N_MIXERS = 3
ROPE_THETA = 10000.0
EPS = 1e-6
D_FF = 2816

A_HEADS = 8
A_KV_HEADS = 2
A_GROUP = A_HEADS // A_KV_HEADS
A_HEAD_DIM = 128
IDX_HEADS = 8
IDX_DIM = 64
TOPK_MAX = 256
LOG2E = math.log2(math.e)
A_SCALE = A_HEAD_DIM ** -0.5 * LOG2E
IDX_W_SCALE = (IDX_HEADS * IDX_DIM) ** -0.5
A_IN = 2120
A_IN_PAD = 2176

B_HEADS = 8
Q_LORA = 384
KV_LORA = 256
NOPE_DIM = 128
ROPE_DIM = 64
V_DIM = 128
B_SCALE = (NOPE_DIM + ROPE_DIM) ** -0.5 * LOG2E
B_IN = 704
B_IN_PAD = 768

C_HEADS = 4
C_DIM = 128
C_SCALE = C_DIM ** -0.5 * LOG2E

LANES = 128
NEG = -1e30
INT_MIN = -(2 ** 31)
VMEM_LIMIT = 56 * 1024 * 1024


def _cparams(n_axes):
    return pltpu.CompilerParams(dimension_semantics=("arbitrary",) * n_axes, vmem_limit_bytes=VMEM_LIMIT)


def _dot(a, b):
    return jnp.dot(a, b, preferred_element_type=F32)


def _dot_nt(a, b):
    return lax.dot_general(a, b, (((1,), (1,)), ((), ())), preferred_element_type=F32)


def _rms(x, g, n):
    ms = jnp.sum(x * x, axis=-1, keepdims=True) * (1.0 / n)
    return x * lax.rsqrt(ms + EPS) * g


def _rope128(x, c, s):
    return x * c + pltpu.roll(x, 64, 1) * s


def _rope64(x, c, sa, sb):
    return x * c + pltpu.roll(x, 96, 1) * sa + pltpu.roll(x, 32, 1) * sb


FFN_CHUNK = 256


def _ffn_body(*refs, fc, mixed):
    if mixed:
        attn_ref, wo_ref, x_ref, g_ref, wg_ref, wu_ref, wd_ref, o_ref = refs
        x = x_ref[...] + _dot(attn_ref[...], wo_ref[...])
    else:
        x_ref, g_ref, wg_ref, wu_ref, wd_ref, o_ref = refs
        x = x_ref[...]
    h = _rms(x, g_ref[...], D_MODEL).astype(BF16)
    y = None
    for c in range(D_FF // fc):
        a = _dot(h, wg_ref[:, c * fc:(c + 1) * fc])
        u = _dot(h, wu_ref[:, c * fc:(c + 1) * fc])
        act = (a * jax.nn.sigmoid(a) * u).astype(BF16)
        part = _dot(act, wd_ref[c * fc:(c + 1) * fc, :])
        y = part if y is None else y + part
    o_ref[...] = x + 0.5 * y


def _const_spec(shape):
    nd = len(shape)
    return pl.BlockSpec(shape, lambda *_: (0,) * nd, pipeline_mode=pl.Buffered(1))


def _row_spec(tm, n):
    return pl.BlockSpec((tm, n), lambda i: (i, 0))


def _ffn(x, g, wg, wu, wd, tm, attn=None, wo=None):
    n = x.shape[0]
    mixed = attn is not None
    pre_specs = [_row_spec(tm, attn.shape[1]), _const_spec(wo.shape)] if mixed else []
    return pl.pallas_call(
        functools.partial(_ffn_body, fc=FFN_CHUNK, mixed=mixed),
        grid=(n // tm,),
        in_specs=pre_specs + [_row_spec(tm, D_MODEL), _const_spec((1, D_MODEL)), _const_spec((D_MODEL, D_FF)),
                              _const_spec((D_MODEL, D_FF)), _const_spec((D_FF, D_MODEL))],
        out_specs=_row_spec(tm, D_MODEL),
        out_shape=jax.ShapeDtypeStruct((n, D_MODEL), F32),
        compiler_params=_cparams(1),
        name="ffn_mixed" if mixed else "ffn",
    )(*((attn, wo) if mixed else ()), x, g, wg, wu, wd)


PROJ_PARTS = 2

def _a_proj_body(x_ref, g_ref, w_ref, qn_ref, kn_ref, ikn_ref, c128_ref, s128_ref, c64_ref, sa64_ref, sb64_ref,
                 q_ref, k32_ref, v32_ref, kbf_ref, vbf_ref, qi_ref, ki32_ref, kidx_ref, wi_ref, y_ref):
    hm = x_ref.shape[0] // PROJ_PARTS
    for part in range(PROJ_PARTS):
        rows = slice(part * hm, (part + 1) * hm)
        y_ref[part] = _dot(_rms(x_ref[rows, :], g_ref[...], D_MODEL).astype(BF16), w_ref[...])
    for part in range(PROJ_PARTS):
        rows = slice(part * hm, (part + 1) * hm)
        c128, s128 = c128_ref[rows, :], s128_ref[rows, :]
        c64, sa64, sb64 = c64_ref[rows, :], sa64_ref[rows, :], sb64_ref[rows, :]
        for hd in range(A_HEADS):
            sl = slice(hd * 128, (hd + 1) * 128)
            qh = _rope128(_rms(y_ref[part, :, sl], qn_ref[...], A_HEAD_DIM), c128, s128)
            q_ref[rows, sl] = (qh * A_SCALE).astype(BF16)
        for hd in range(A_KV_HEADS):
            sl = slice(hd * 128, (hd + 1) * 128)
            kh = _rope128(_rms(y_ref[part, :, 1024 + hd * 128:1024 + (hd + 1) * 128], kn_ref[...], A_HEAD_DIM),
                          c128, s128)
            k32_ref[rows, sl] = kh
            kbf_ref[rows, sl] = kh.astype(BF16)
        v = y_ref[part, :, 1280:1536]
        v32_ref[rows, :] = v
        vbf_ref[rows, :] = v.astype(BF16)
        for p in range(IDX_HEADS // 2):
            sl = slice(p * 128, (p + 1) * 128)
            qi_ref[rows, sl] = _rope64(y_ref[part, :, 1536 + p * 128:1536 + (p + 1) * 128],
                                       c64, sa64, sb64).astype(BF16)
        tail = y_ref[part, :, 2048:2176]
        lane = lax.broadcasted_iota(I32, tail.shape, 1)
        low = lane < IDX_DIM
        kin = jnp.where(low, tail, 0.0)
        ki = _rope64(_rms(kin, ikn_ref[...], IDX_DIM), c64, sa64, sb64)
        ki = jnp.where(low, ki, 0.0)
        ki32_ref[rows, :] = ki[:, :IDX_DIM]
        kidx_ref[rows, 0:128] = ki.astype(BF16)
        kidx_ref[rows, 128:256] = pltpu.roll(ki, 64, 1).astype(BF16)
        wi_ref[rows, :] = pltpu.roll(tail, 64, 1) * IDX_W_SCALE


def _tab_spec(tm, t):
    nt = t // tm
    return pl.BlockSpec((tm, LANES), lambda i: (i % nt, 0))


def _a_proj(x, g, w, qn, kn, ikn, tabs, tm):
    n = x.shape[0]
    t = tabs["c128"].shape[0]
    outs = [(1024, BF16), (256, F32), (256, F32), (256, BF16), (256, BF16), (512, BF16), (IDX_DIM, F32),
            (256, BF16), (LANES, F32)]
    return pl.pallas_call(
        _a_proj_body,
        grid=(n // tm,),
        in_specs=[_row_spec(tm, D_MODEL), _const_spec((1, D_MODEL)), _const_spec((D_MODEL, A_IN_PAD)),
                  _const_spec((1, 128)), _const_spec((1, 128)), _const_spec((1, 128))]
        + [_tab_spec(tm, t)] * 5,
        out_specs=[_row_spec(tm, c) for c, _ in outs],
        out_shape=[jax.ShapeDtypeStruct((n, c), dt) for c, dt in outs],
        scratch_shapes=[pltpu.VMEM((PROJ_PARTS, tm // PROJ_PARTS, A_IN_PAD), F32)],
        compiler_params=_cparams(1),
        name="a_proj",
    )(x, g, w, qn, kn, ikn, tabs["c128"], tabs["s128"], tabs["c64"], tabs["sa64"], tabs["sb64"])


def _b_proj_body(x_ref, g_ref, w_ref, qan_ref, kvan_ref, krn_ref, wuq_ref, qnn_ref, qrn_ref,
                 c64_ref, sa64_ref, sb64_ref, q_ref, ckv_ref, kr32_ref, krbf_ref, qq_ref):
    hm = x_ref.shape[0] // PROJ_PARTS
    for part in range(PROJ_PARTS):
        rows = slice(part * hm, (part + 1) * hm)
        h = _rms(x_ref[rows, :], g_ref[...], D_MODEL).astype(BF16)
        y = _dot(h, w_ref[...])
        ckv_ref[rows, :] = _rms(y[:, 384:640], kvan_ref[...], KV_LORA)
        kr = _rope64(_rms(y[:, 640:768], krn_ref[...], ROPE_DIM), c64_ref[rows, :], sa64_ref[rows, :],
                     sb64_ref[rows, :])
        kr32_ref[rows, :] = kr[:, :ROPE_DIM]
        krbf_ref[rows, :] = kr.astype(BF16)
        qq_ref[part] = _dot(_rms(y[:, 0:384], qan_ref[...], Q_LORA).astype(BF16), wuq_ref[...])
    for part in range(PROJ_PARTS):
        rows = slice(part * hm, (part + 1) * hm)
        c64, sa64, sb64 = c64_ref[rows, :], sa64_ref[rows, :], sb64_ref[rows, :]
        for hd in range(B_HEADS):
            qn = _rms(qq_ref[part, :, hd * 256:hd * 256 + 128], qnn_ref[...], NOPE_DIM)
            qr = _rope64(_rms(qq_ref[part, :, hd * 256 + 128:(hd + 1) * 256], qrn_ref[...], ROPE_DIM),
                         c64, sa64, sb64)
            q_ref[rows, hd * 256:hd * 256 + 128] = (qn * B_SCALE).astype(BF16)
            q_ref[rows, hd * 256 + 128:(hd + 1) * 256] = (qr * B_SCALE).astype(BF16)


def _b_proj(x, g, w, qan, kvan, krn, wuq, qnn, qrn, tabs, tm):
    n = x.shape[0]
    t = tabs["c64"].shape[0]
    outs = [(B_HEADS * 256, BF16), (KV_LORA, F32), (ROPE_DIM, F32), (LANES, BF16)]
    return pl.pallas_call(
        _b_proj_body,
        grid=(n // tm,),
        in_specs=[_row_spec(tm, D_MODEL), _const_spec((1, D_MODEL)), _const_spec((D_MODEL, B_IN_PAD)),
                  _const_spec((1, Q_LORA)), _const_spec((1, KV_LORA)), _const_spec((1, 128)),
                  _const_spec((Q_LORA, 2048)), _const_spec((1, 128)), _const_spec((1, 128))]
        + [_tab_spec(tm, t)] * 3,
        out_specs=[_row_spec(tm, c) for c, _ in outs],
        out_shape=[jax.ShapeDtypeStruct((n, c), dt) for c, dt in outs],
        scratch_shapes=[pltpu.VMEM((PROJ_PARTS, tm // PROJ_PARTS, 2048), F32)],
        compiler_params=_cparams(1),
        name="b_proj",
    )(x, g, w, qan, kvan, krn, wuq, qnn, qrn, tabs["c64"], tabs["sa64"], tabs["sb64"])


def _kv_up_body(ckv_ref, w_ref, knn_ref, kn_ref, v_ref):
    y = _dot(ckv_ref[...].astype(BF16), w_ref[...])
    for hd in range(B_HEADS):
        kn_ref[:, hd * 128:(hd + 1) * 128] = _rms(y[:, hd * 256:hd * 256 + 128], knn_ref[...], NOPE_DIM).astype(BF16)
        v_ref[:, hd * 128:(hd + 1) * 128] = y[:, hd * 256 + 128:(hd + 1) * 256].astype(BF16)


def _kv_up(ckv, w, knn, tm):
    n = ckv.shape[0]
    return pl.pallas_call(
        _kv_up_body,
        grid=(n // tm,),
        in_specs=[_row_spec(tm, KV_LORA), _const_spec((KV_LORA, 2048)), _const_spec((1, 128))],
        out_specs=[_row_spec(tm, 1024), _row_spec(tm, 1024)],
        out_shape=[jax.ShapeDtypeStruct((n, 1024), BF16)] * 2,
        compiler_params=_cparams(1),
        name="kv_up",
    )(ckv, w, knn)


def _c_proj_body(x_ref, g_ref, w_ref, qn_ref, kn_ref, c128_ref, s128_ref,
                 q_ref, k32_ref, v32_ref, kbf_ref, vbf_ref, y_ref):
    hm = x_ref.shape[0] // PROJ_PARTS
    for part in range(PROJ_PARTS):
        rows = slice(part * hm, (part + 1) * hm)
        y_ref[part] = _dot(_rms(x_ref[rows, :], g_ref[...], D_MODEL).astype(BF16), w_ref[...])
    for part in range(PROJ_PARTS):
        rows = slice(part * hm, (part + 1) * hm)
        c128, s128 = c128_ref[rows, :], s128_ref[rows, :]
        for hd in range(2 * C_HEADS):
            sl = slice(hd * 128, (hd + 1) * 128)
            qh = _rope128(_rms(y_ref[part, :, sl], qn_ref[...], C_DIM), c128, s128)
            q_ref[rows, sl] = (qh * C_SCALE).astype(BF16)
            kh = _rope128(_rms(y_ref[part, :, 1024 + hd * 128:1024 + (hd + 1) * 128], kn_ref[...], C_DIM),
                          c128, s128)
            k32_ref[rows, sl] = kh
            kbf_ref[rows, sl] = kh.astype(BF16)
        v = y_ref[part, :, 2048:3072]
        v32_ref[rows, :] = v
        vbf_ref[rows, :] = v.astype(BF16)


def _c_proj(x, g, w, qn, kn, tabs, tm):
    n = x.shape[0]
    t = tabs["c128"].shape[0]
    outs = [(1024, BF16), (1024, F32), (1024, F32), (1024, BF16), (1024, BF16)]
    return pl.pallas_call(
        _c_proj_body,
        grid=(n // tm,),
        in_specs=[_row_spec(tm, D_MODEL), _const_spec((1, D_MODEL)), _const_spec((D_MODEL, 3072)),
                  _const_spec((1, 128)), _const_spec((1, 128))] + [_tab_spec(tm, t)] * 2,
        out_specs=[_row_spec(tm, c) for c, _ in outs],
        out_shape=[jax.ShapeDtypeStruct((n, c), dt) for c, dt in outs],
        scratch_shapes=[pltpu.VMEM((PROJ_PARTS, tm // PROJ_PARTS, 3072), F32)],
        compiler_params=_cparams(1),
        name="c_proj",
    )(x, g, w, qn, kn, tabs["c128"], tabs["s128"])


def _n_key_blocks(i, tq, kb, q_off, s_valid):
    last_chunk = lax.shift_right_logical(q_off + (i + 1) * tq - 1, CHUNK_SHIFT)
    kend = jnp.minimum((last_chunk + 1) << CHUNK_SHIFT, s_valid)
    return lax.div(kend + kb - 1, jnp.int32(kb))


def _q_chunk(i, tq, q_off):
    row = lax.broadcasted_iota(I32, (tq, 1), 0)
    return lax.shift_right_logical(q_off + i * tq + row, CHUNK_SHIFT)


def _k_chunk(ks, kb, s_valid):
    kpos = ks + lax.broadcasted_iota(I32, (1, kb), 1)
    return jnp.where(kpos < s_valid, lax.shift_right_logical(kpos, CHUNK_SHIFT), 2 ** 30)


def _flash_step(s, v, m_ref, l_ref, acc_ref):
    m_prev = m_ref[...]
    m_new = jnp.maximum(m_prev, jnp.max(s, axis=1, keepdims=True))
    alpha = jnp.exp2(m_prev - m_new)
    p = jnp.exp2(s - m_new)
    l_ref[...] = alpha * l_ref[...] + jnp.sum(p, axis=1, keepdims=True)
    acc_ref[...] = alpha * acc_ref[...] + _dot(p.astype(BF16), v)
    m_ref[...] = m_new


def _key_limit(i, tq, q_off, s_valid):
    qpos = q_off + i * tq + lax.broadcasted_iota(I32, (1, tq), 1)
    return jnp.minimum((lax.shift_right_logical(qpos, CHUNK_SHIFT) + 1) << CHUNK_SHIFT, s_valid)


def _n_full_blocks(i, tq, kb, q_off, s_valid):
    first_chunk = lax.shift_right_logical(q_off + i * tq, CHUNK_SHIFT)
    kend = jnp.minimum((first_chunk + 1) << CHUNK_SHIFT, s_valid)
    return lax.shift_right_logical(kend, int(math.log2(kb)))


def _flash_init(m_ref, l_ref, acc_ref):
    m_ref[...] = jnp.full(m_ref.shape, NEG, F32)
    l_ref[...] = jnp.zeros(l_ref.shape, F32)
    acc_ref[...] = jnp.zeros(acc_ref.shape, F32)


def _dsa_body(q_ref, qi_ref, wi_ref, kidx_ref, k_ref, v_ref, o_ref, keys_ref, m_ref, l_ref, acc_ref,
              *, tq, kb, q_off, s_valid, topk):
    i = pl.program_id(1)
    n_blk = _n_key_blocks(i, tq, kb, q_off, s_valid)
    qc = _q_chunk(i, tq, q_off)
    wi = wi_ref[...]
    wcol = [wi[:, hd:hd + 1] for hd in range(IDX_HEADS)]

    def score_block(j, carry):
        ks = pl.multiple_of(j * kb, kb)
        k_lo = kidx_ref[pl.ds(ks, kb), 0:128]
        k_hi = kidx_ref[pl.ds(ks, kb), 128:256]
        sc = jnp.zeros((tq, kb), F32)
        for p in range(IDX_HEADS // 2):
            qp = qi_ref[:, p * 128:(p + 1) * 128]
            sc = sc + wcol[2 * p] * jnp.maximum(_dot_nt(qp, k_lo), 0.0)
            sc = sc + wcol[2 * p + 1] * jnp.maximum(_dot_nt(qp, k_hi), 0.0)
        bits = lax.bitcast_convert_type(sc, I32)
        key = bits ^ ((bits >> 31) & 0x7FFFFFFF)
        keys_ref[j] = jnp.where(_k_chunk(ks, kb, s_valid) <= qc, key, INT_MIN)
        return carry

    lax.fori_loop(0, n_blk, score_block, 0)

    def count(pred):
        def blk(j, c):
            x = jnp.where(pred(keys_ref[j], j * kb), 1.0, 0.0)
            part = x[:, 0:LANES]
            for g in range(1, kb // LANES):
                part = part + x[:, g * LANES:(g + 1) * LANES]
            return c + part
        c = lax.fori_loop(0, n_blk, blk, jnp.zeros((tq, LANES), F32))
        return jnp.sum(c, axis=1, keepdims=True)

    kf = float(topk)

    def bit_step(b, t):
        cand = t + jnp.left_shift(jnp.int32(1), 31 - b)
        cnt = count(lambda kblk, _: kblk >= cand)
        return jnp.where(cnt >= kf, cand, t)

    thr = lax.fori_loop(0, 32, bit_step, jnp.full((tq, 1), INT_MIN, I32))
    need = kf - count(lambda kblk, _: kblk > thr)
    n_eq = count(lambda kblk, _: kblk == thr)
    partial = jnp.logical_and(n_eq > need, thr != INT_MIN)

    @pl.when(jnp.max(jnp.where(partial, 1.0, 0.0)) > 0.0)
    def _():
        lane = lax.broadcasted_iota(I32, (1, kb), 1)

        def idx_step(b, x):
            cand = x + jnp.left_shift(jnp.int32(1), b)
            cnt = count(lambda kblk, k0: jnp.logical_and(kblk == thr, k0 + lane < cand))
            return jnp.where(cnt < need, cand, x)

        nbits = max(1, int(s_valid - 1).bit_length())
        cut = lax.fori_loop(0, nbits, lambda b, x: idx_step(nbits - 1 - b, x), jnp.zeros((tq, 1), I32))

        def demote(j, carry):
            kblk = keys_ref[j]
            drop = jnp.logical_and(jnp.logical_and(kblk == thr, j * kb + lane > cut), partial)
            keys_ref[j] = jnp.where(drop, kblk - 1, kblk)
            return carry

        lax.fori_loop(0, n_blk, demote, 0)

    thr_sel = jnp.maximum(thr, INT_MIN + 1)

    _flash_init(m_ref, l_ref, acc_ref)

    def attend(j, carry):
        ks = pl.multiple_of(j * kb, kb)
        bias = jnp.where(keys_ref[j] >= thr_sel, 0.0, NEG)
        for g in range(A_KV_HEADS):
            qg = jnp.concatenate([q_ref[:, (g * A_GROUP + a) * 128:(g * A_GROUP + a + 1) * 128]
                                  for a in range(A_GROUP)], axis=0)
            s = _dot_nt(qg, k_ref[pl.ds(ks, kb), g * 128:(g + 1) * 128])
            s = (s.reshape(A_GROUP, tq, kb) + bias[None]).reshape(A_GROUP * tq, kb)
            _flash_step(s, v_ref[pl.ds(ks, kb), g * 128:(g + 1) * 128], m_ref.at[g], l_ref.at[g], acc_ref.at[g])
        return carry

    lax.fori_loop(0, n_blk, attend, 0)
    for g in range(A_KV_HEADS):
        o = acc_ref[g] / l_ref[g]
        for a in range(A_GROUP):
            hd = g * A_GROUP + a
            o_ref[:, hd * 128:(hd + 1) * 128] = o[a * tq:(a + 1) * tq].astype(BF16)


def _dsa_attn(q, qi, wi, kidx, k, v, *, tq, kb, q_off, s_valid, topk):
    b, t, _ = q.shape
    s_pad = k.shape[1]
    qspec = lambda c: pl.BlockSpec((None, tq, c), lambda bi, i: (bi, i, 0))
    kspec = lambda c: pl.BlockSpec((None, s_pad, c), lambda bi, i: (bi, 0, 0))
    rows = A_GROUP * tq
    return pl.pallas_call(
        functools.partial(_dsa_body, tq=tq, kb=kb, q_off=q_off, s_valid=s_valid, topk=topk),
        grid=(b, t // tq),
        in_specs=[qspec(1024), qspec(512), qspec(LANES), kspec(256), kspec(256), kspec(256)],
        out_specs=qspec(1024),
        out_shape=jax.ShapeDtypeStruct((b, t, 1024), BF16),
        scratch_shapes=[pltpu.VMEM((s_pad // kb, tq, kb), I32), pltpu.VMEM((A_KV_HEADS, rows, 1), F32),
                        pltpu.VMEM((A_KV_HEADS, rows, 1), F32), pltpu.VMEM((A_KV_HEADS, rows, 128), F32)],
        compiler_params=_cparams(2),
        name="dsa_attn",
    )(q, qi, wi, kidx, k, v)


def _mla_body(q_ref, kn_ref, kr_ref, v_ref, o_ref, m_ref, l_ref, acc_ref, *, tq, kb, q_off, s_valid):
    i = pl.program_id(2)
    n_blk = _n_key_blocks(i, tq, kb, q_off, s_valid)
    qc = _q_chunk(i, tq, q_off)
    _flash_init(m_ref, l_ref, acc_ref)

    def attend(j, carry):
        ks = pl.multiple_of(j * kb, kb)
        kcat = jnp.concatenate([kn_ref[pl.ds(ks, kb), :], kr_ref[pl.ds(ks, kb), :]], axis=1)
        s = _dot_nt(q_ref[...], kcat)
        s = jnp.where(_k_chunk(ks, kb, s_valid) <= qc, s, NEG)
        _flash_step(s, v_ref[pl.ds(ks, kb), :], m_ref, l_ref, acc_ref)
        return carry

    lax.fori_loop(0, n_blk, attend, 0)
    o_ref[...] = (acc_ref[...] / l_ref[...]).astype(BF16)


def _mla_attn(q, kn, kr, v, *, tq, kb, q_off, s_valid):
    b, t, _ = q.shape
    s_pad = kn.shape[1]
    return pl.pallas_call(
        functools.partial(_mla_body, tq=tq, kb=kb, q_off=q_off, s_valid=s_valid),
        grid=(b, B_HEADS, t // tq),
        in_specs=[pl.BlockSpec((None, tq, 256), lambda bi, h, i: (bi, i, h)),
                  pl.BlockSpec((None, s_pad, 128), lambda bi, h, i: (bi, 0, h)),
                  pl.BlockSpec((None, s_pad, 128), lambda bi, h, i: (bi, 0, 0)),
                  pl.BlockSpec((None, s_pad, 128), lambda bi, h, i: (bi, 0, h))],
        out_specs=pl.BlockSpec((None, tq, 128), lambda bi, h, i: (bi, i, h)),
        out_shape=jax.ShapeDtypeStruct((b, t, B_HEADS * V_DIM), BF16),
        scratch_shapes=[pltpu.VMEM((tq, 1), F32), pltpu.VMEM((tq, 1), F32), pltpu.VMEM((tq, 128), F32)],
        compiler_params=_cparams(3),
        name="mla_attn",
    )(q, kn, kr, v)


def _diff_body(q_ref, k_ref, v_ref, lq1_ref, lk1_ref, lq2_ref, lk2_ref, sn_ref, o_ref, m_ref, l_ref, acc_ref,
               *, tq, kb, q_off, s_valid, lam_init):
    i = pl.program_id(2)
    n_blk = _n_key_blocks(i, tq, kb, q_off, s_valid)
    qc = _q_chunk(i, tq, q_off)
    _flash_init(m_ref, l_ref, acc_ref)

    def attend(j, carry):
        ks = pl.multiple_of(j * kb, kb)
        ok = _k_chunk(ks, kb, s_valid) <= qc
        v = v_ref[pl.ds(ks, kb), :]
        for p in range(2):
            s = _dot_nt(q_ref[:, p * 128:(p + 1) * 128], k_ref[pl.ds(ks, kb), p * 128:(p + 1) * 128])
            s = jnp.where(ok, s, NEG)
            _flash_step(s, v, m_ref.at[p], l_ref.at[p], acc_ref.at[p])
        return carry

    lax.fori_loop(0, n_blk, attend, 0)
    lam = (jnp.exp(jnp.sum(lq1_ref[...] * lk1_ref[...], axis=1, keepdims=True))
           - jnp.exp(jnp.sum(lq2_ref[...] * lk2_ref[...], axis=1, keepdims=True)) + lam_init)
    o = acc_ref[0] / l_ref[0] - lam * (acc_ref[1] / l_ref[1])
    o_ref[...] = (_rms(o, sn_ref[...], 2 * C_DIM) * (1.0 - lam_init)).astype(BF16)


def _diff_attn(q, k, v, lq1, lk1, lq2, lk2, sn, *, tq, kb, q_off, s_valid, lam_init):
    b, t, _ = q.shape
    s_pad = k.shape[1]
    vec = lambda c: pl.BlockSpec((1, c), lambda bi, h, i: (0, 0))
    return pl.pallas_call(
        functools.partial(_diff_body, tq=tq, kb=kb, q_off=q_off, s_valid=s_valid, lam_init=lam_init),
        grid=(b, C_HEADS, t // tq),
        in_specs=[pl.BlockSpec((None, tq, 256), lambda bi, h, i: (bi, i, h)),
                  pl.BlockSpec((None, s_pad, 256), lambda bi, h, i: (bi, 0, h)),
                  pl.BlockSpec((None, s_pad, 256), lambda bi, h, i: (bi, 0, h)),
                  vec(128), vec(128), vec(128), vec(128), vec(256)],
        out_specs=pl.BlockSpec((None, tq, 256), lambda bi, h, i: (bi, i, h)),
        out_shape=jax.ShapeDtypeStruct((b, t, C_HEADS * 2 * C_DIM), BF16),
        scratch_shapes=[pltpu.VMEM((2, tq, 1), F32), pltpu.VMEM((2, tq, 1), F32), pltpu.VMEM((2, tq, 256), F32)],
        compiler_params=_cparams(3),
        name="diff_attn",
    )(q, k, v, lq1, lk1, lq2, lk2, sn)


SCORE_BLOCKS = 4
SEARCH_BLOCKS = 4
SEARCH_BITS_UNCHECKED = 24
SEARCH_BITS_PER_CHECK = 2


def _stage_bufs(n_chain, kb, r):
    return [pltpu.VMEM((n_chain, 2, kb, r), F32), pltpu.VMEM((n_chain, 2, kb, r), BF16),
            pltpu.VMEM((n_chain, 2, 1, r), F32)]


def _staged_flash_t(bufs, state, lo, hi, n_kb, qk, prep, vt_of, first=True, last=True):
    s_ref, p_ref, a_ref = bufs
    m_ref, l_ref, acc_ref = state
    n_chain = s_ref.shape[0]
    clamp = lambda j: jnp.clip(j, 0, n_kb - 1)

    def softmax(c, j, slot):
        st = prep(c, j, s_ref[c, slot])
        m_prev = m_ref[c]
        m_new = jnp.maximum(m_prev, jnp.max(st, axis=0, keepdims=True))
        alpha = jnp.exp2(m_prev - m_new)
        p = jnp.exp2(st - m_new)
        l_ref[c] = alpha * l_ref[c] + jnp.sum(p, axis=0, keepdims=True)
        m_ref[c] = m_new
        p_ref[c, slot] = p.astype(BF16)
        a_ref[c, slot] = alpha

    def values(c, j, slot):
        acc_ref[c] = a_ref[c, slot] * acc_ref[c] + _dot(vt_of(c, clamp(j)), p_ref[c, slot])

    if first:
        for c in range(n_chain):
            s_ref[c, 0] = qk(c, clamp(lo))
            p_ref[c, 1] = jnp.zeros(p_ref.shape[2:], BF16)
            a_ref[c, 1] = jnp.ones(a_ref.shape[2:], F32)

    def turn(t, carry):
        j0 = lo + 2 * t
        for c in range(n_chain):
            s_ref[c, 1] = qk(c, clamp(j0 + 1))
        for c in range(n_chain):
            softmax(c, j0, 0)
        for c in range(n_chain):
            values(c, j0 - 1, 1)
        for c in range(n_chain):
            s_ref[c, 0] = qk(c, clamp(j0 + 2))
        for c in range(n_chain):
            softmax(c, j0 + 1, 1)
        for c in range(n_chain):
            values(c, j0, 0)
        return carry

    n_turn = lax.shift_right_logical(hi - lo + 1, 1)
    lax.fori_loop(0, n_turn, turn, 0)
    if last:
        for c in range(n_chain):
            values(c, lo + 2 * n_turn - 1, 1)


def _dsa_t_body(q_ref, qi_ref, wi_ref, kidx_ref, k_ref, vt_ref, o_ref, keys_ref, sc_ref, s_ref, p_ref, a_ref,
                m_ref, l_ref, acc_ref, *, tq, kb, q_off, s_valid, topk):
    i = pl.program_id(1)
    n_blk = _n_key_blocks(i, tq, kb, q_off, s_valid)
    n_kb = keys_ref.shape[0]
    n_sb = lax.shift_right_logical(n_blk + SEARCH_BLOCKS - 1, int(math.log2(SEARCH_BLOCKS)))
    limit = _key_limit(i, tq, q_off, s_valid)
    kidx0 = lax.broadcasted_iota(I32, (kb, tq), 0)
    wit = wi_ref[...].T
    wrow = [wit[hd:hd + 1, :] for hd in range(IDX_HEADS)]

    def score_blocks(jj, carry):
        ks = pl.multiple_of(jj * (SCORE_BLOCKS * kb), SCORE_BLOCKS * kb)
        half = SCORE_BLOCKS // 2
        q_all = jnp.concatenate([qi_ref[:, p * 128:(p + 1) * 128] for p in range(IDX_HEADS // 2)], axis=0)
        for hf in range(2):
            rows = pl.ds(ks + hf * half * kb, half * kb)
            sc_ref[hf, 0] = _dot_nt(kidx_ref[rows, 0:128], q_all)
            sc_ref[hf, 1] = _dot_nt(kidx_ref[rows, 128:256], q_all)
        for hf in range(2):
            for u in range(half):
                blk = slice(u * kb, (u + 1) * kb)
                sc = jnp.zeros((kb, tq), F32)
                for p in range(IDX_HEADS // 2):
                    sc = sc + wrow[2 * p] * jnp.maximum(sc_ref[hf, 0, blk, p * tq:(p + 1) * tq], 0.0)
                    sc = sc + wrow[2 * p + 1] * jnp.maximum(sc_ref[hf, 1, blk, p * tq:(p + 1) * tq], 0.0)
                bits = lax.bitcast_convert_type(sc, I32)
                key = bits ^ ((bits >> 31) & 0x7FFFFFFF)
                j = jj * SCORE_BLOCKS + hf * half + u
                keys_ref[j] = jnp.where(kidx0 < limit - j * kb, key, INT_MIN)
        return carry

    def pad_block(j, carry):
        keys_ref[j] = jnp.full((kb, tq), INT_MIN, I32)
        return carry

    n_scored = lax.shift_right_logical(n_blk + SCORE_BLOCKS - 1, int(math.log2(SCORE_BLOCKS)))
    lax.fori_loop(0, n_scored, score_blocks, 0)
    lax.fori_loop(n_scored * SCORE_BLOCKS, n_sb * SEARCH_BLOCKS, pad_block, 0)

    def count(pred):
        def group(jj, cs):
            out = []
            for u in range(SEARCH_BLOCKS):
                j = jj * SEARCH_BLOCKS + u
                x = jnp.where(pred(keys_ref[j], j * kb), 1.0, 0.0).reshape(kb // 8, 8, tq)
                h = kb // 16
                out.append(cs[u] + (jnp.sum(x[:h], axis=0) + jnp.sum(x[h:], axis=0)))
            return tuple(out)
        cs = lax.fori_loop(0, n_sb, group, tuple(jnp.zeros((8, tq), F32) for _ in range(SEARCH_BLOCKS)))
        return jnp.sum(functools.reduce(lambda a, b: a + b, cs), axis=0, keepdims=True)

    kf = float(topk)

    def bit_step(b, state):
        t, n_ge = state
        cand = t + jnp.left_shift(jnp.int32(1), 31 - b)
        cnt = count(lambda kblk, _: kblk >= cand)
        take = cnt >= kf
        return jnp.where(take, cand, t), jnp.where(take, cnt, n_ge)

    def unsettled(n_ge):
        open_ = jnp.logical_and(n_ge != kf, limit.astype(F32) > kf)
        return jnp.max(jnp.where(open_, 1.0, 0.0)) > 0.0

    def more_bits(state):
        b, _, n_ge = state
        return jnp.logical_and(b < 32, unsettled(n_ge))

    def four_bits(state):
        b, t, n_ge = state
        for u in range(SEARCH_BITS_PER_CHECK):
            t, n_ge = bit_step(b + u, (t, n_ge))
        return b + SEARCH_BITS_PER_CHECK, t, n_ge

    start = (jnp.full((1, tq), INT_MIN, I32), jnp.full((1, tq), float(n_kb * kb), F32))
    head = lax.fori_loop(0, SEARCH_BITS_UNCHECKED, bit_step, start)
    _, thr, _ = lax.while_loop(more_bits, four_bits, (jnp.int32(SEARCH_BITS_UNCHECKED),) + head)
    need = kf - count(lambda kblk, _: kblk > thr)
    n_eq = count(lambda kblk, _: kblk == thr)
    partial = jnp.logical_and(n_eq > need, thr != INT_MIN)

    @pl.when(jnp.max(jnp.where(partial, 1.0, 0.0)) > 0.0)
    def _():
        def idx_step(b, x):
            cand = x + jnp.left_shift(jnp.int32(1), b)
            cnt = count(lambda kblk, k0: jnp.logical_and(kblk == thr, kidx0 < cand - k0))
            return jnp.where(cnt < need, cand, x)

        nbits = max(1, int(s_valid - 1).bit_length())
        cut = lax.fori_loop(0, nbits, lambda b, x: idx_step(nbits - 1 - b, x), jnp.zeros((1, tq), I32))

        def demote(j, carry):
            kblk = keys_ref[j]
            drop = jnp.logical_and(jnp.logical_and(kblk == thr, kidx0 > cut - j * kb), partial)
            keys_ref[j] = jnp.where(drop, kblk - 1, kblk)
            return carry

        lax.fori_loop(0, n_blk, demote, 0)

    thr_sel = jnp.maximum(thr, INT_MIN + 1)

    _flash_init(m_ref, l_ref, acc_ref)
    n_pair = A_HEADS // 2

    def to_bias(j, carry):
        keys_ref[j] = lax.bitcast_convert_type(jnp.where(keys_ref[j] >= thr_sel, 0.0, NEG), I32)
        return carry

    lax.fori_loop(0, n_sb * SEARCH_BLOCKS, to_bias, 0)
    for g in range(A_KV_HEADS):
        def qk(c, j, g=g):
            p = g * (A_GROUP // 2) + c
            qp = jnp.concatenate([q_ref[:, (2 * p) * 128:(2 * p + 1) * 128],
                                  q_ref[:, (2 * p + 1) * 128:(2 * p + 2) * 128]], axis=0)
            return _dot_nt(k_ref[pl.ds(pl.multiple_of(j * kb, kb), kb), g * 128:(g + 1) * 128], qp)

        def prep(c, j, st):
            bias = lax.bitcast_convert_type(keys_ref[j], F32)
            return st + jnp.concatenate([bias, bias], axis=1)

        def vt_of(c, j, g=g):
            return vt_ref[j, g * 128:(g + 1) * 128, :]

        chains = pl.ds(g * (A_GROUP // 2), A_GROUP // 2)
        _staged_flash_t((s_ref.at[chains], p_ref.at[chains], a_ref.at[chains]),
                        (m_ref.at[chains], l_ref.at[chains], acc_ref.at[chains]), 0, n_blk, n_kb, qk, prep, vt_of)
    for p in range(n_pair):
        ot = acc_ref[p] / l_ref[p]
        for a in range(2):
            hd = 2 * p + a
            o_ref[:, hd * 128:(hd + 1) * 128] = ot[:, a * tq:(a + 1) * tq].T.astype(BF16)


def _dsa_attn_t(q, qi, wi, kidx, k, vt, *, tq, kb, q_off, s_valid, topk):
    b, t, _ = q.shape
    s_pad = k.shape[1]
    n_kb = s_pad // kb
    qspec = lambda c: pl.BlockSpec((None, tq, c), lambda bi, i: (bi, i, 0))
    kspec = lambda c: pl.BlockSpec((None, s_pad, c), lambda bi, i: (bi, 0, 0))
    n_pair = A_HEADS // 2
    return pl.pallas_call(
        functools.partial(_dsa_t_body, tq=tq, kb=kb, q_off=q_off, s_valid=s_valid, topk=topk),
        grid=(b, t // tq),
        in_specs=[qspec(1024), qspec(512), qspec(LANES),
                  kspec(256), kspec(256), pl.BlockSpec((None, n_kb, 256, kb), lambda bi, i: (bi, 0, 0, 0))],
        out_specs=qspec(1024),
        out_shape=jax.ShapeDtypeStruct((b, t, 1024), BF16),
        scratch_shapes=[pltpu.VMEM((n_kb, kb, tq), I32),
                        pltpu.VMEM((2, 2, SCORE_BLOCKS // 2 * kb, IDX_HEADS // 2 * tq), F32)]
        + _stage_bufs(n_pair, kb, 2 * tq)
        + [pltpu.VMEM((n_pair, 1, 2 * tq), F32), pltpu.VMEM((n_pair, 1, 2 * tq), F32),
           pltpu.VMEM((n_pair, 128, 2 * tq), F32)],
        compiler_params=_cparams(2),
        name="dsa_attn_t",
    )(q, qi, wi, kidx, k, vt)


def _causal_flash_t(bufs, state, i, n_kb, tq, kb, q_off, s_valid, qk, vt_of):
    n_blk = _n_key_blocks(i, tq, kb, q_off, s_valid)
    n_plain = _n_full_blocks(i, tq, kb, q_off, s_valid) & -2
    limit = _key_limit(i, tq, q_off, s_valid)

    def masked(c, j, st):
        return jnp.where(lax.broadcasted_iota(I32, (kb, tq), 0) < limit - j * kb, st, NEG)

    _staged_flash_t(bufs, state, 0, n_plain, n_kb, qk, lambda c, j, st: st, vt_of, last=False)
    _staged_flash_t(bufs, state, n_plain, n_blk, n_kb, qk, masked, vt_of, first=False)


def _mla_t_body(qt_ref, kn_ref, kr_ref, vt_ref, o_ref, s_ref, p_ref, a_ref, m_ref, l_ref, acc_ref,
                *, tq, kb, q_off, s_valid):
    i = pl.program_id(2)
    _flash_init(m_ref, l_ref, acc_ref)

    def qk(a, j):
        ks = pl.multiple_of(j * kb, kb)
        kcat = jnp.concatenate([kn_ref[pl.ds(ks, kb), a * 128:(a + 1) * 128], kr_ref[pl.ds(ks, kb), :]], axis=1)
        return _dot(kcat, qt_ref[a * 256:(a + 1) * 256, :])

    def vt_of(a, j):
        return vt_ref[j, a * 128:(a + 1) * 128, :]

    _causal_flash_t((s_ref, p_ref, a_ref), (m_ref, l_ref, acc_ref), i, vt_ref.shape[0], tq, kb, q_off, s_valid,
                    qk, vt_of)
    for a in range(2):
        o_ref[:, a * 128:(a + 1) * 128] = (acc_ref[a] / l_ref[a]).T.astype(BF16)


def _mla_attn_t(qt, kn, kr, vt, *, tq, kb, q_off, s_valid):
    b, _, t = qt.shape
    s_pad = kn.shape[1]
    n_kb = s_pad // kb
    return pl.pallas_call(
        functools.partial(_mla_t_body, tq=tq, kb=kb, q_off=q_off, s_valid=s_valid),
        grid=(b, B_HEADS // 2, t // tq),
        in_specs=[pl.BlockSpec((None, 512, tq), lambda bi, h, i: (bi, h, i)),
                  pl.BlockSpec((None, s_pad, 256), lambda bi, h, i: (bi, 0, h)),
                  pl.BlockSpec((None, s_pad, 128), lambda bi, h, i: (bi, 0, 0)),
                  pl.BlockSpec((None, n_kb, 256, kb), lambda bi, h, i: (bi, 0, h, 0))],
        out_specs=pl.BlockSpec((None, tq, 256), lambda bi, h, i: (bi, i, h)),
        out_shape=jax.ShapeDtypeStruct((b, t, B_HEADS * V_DIM), BF16),
        scratch_shapes=_stage_bufs(2, kb, tq) + [pltpu.VMEM((2, 1, tq), F32), pltpu.VMEM((2, 1, tq), F32),
                                                 pltpu.VMEM((2, 128, tq), F32)],
        compiler_params=_cparams(3),
        name="mla_attn_t",
    )(qt, kn, kr, vt)


def _diff_t_body(q_ref, k_ref, vt_ref, lq1_ref, lk1_ref, lq2_ref, lk2_ref, sn_ref, o_ref, s_ref, p_ref, a_ref,
                 m_ref, l_ref, acc_ref, *, tq, kb, q_off, s_valid, lam_init):
    i = pl.program_id(2)
    _flash_init(m_ref, l_ref, acc_ref)

    def qk(p, j):
        ks = pl.multiple_of(j * kb, kb)
        return _dot_nt(k_ref[pl.ds(ks, kb), p * 128:(p + 1) * 128], q_ref[:, p * 128:(p + 1) * 128])

    _causal_flash_t((s_ref, p_ref, a_ref), (m_ref, l_ref, acc_ref), i, vt_ref.shape[0], tq, kb, q_off, s_valid,
                    qk, lambda p, j: vt_ref[j])
    lam = (jnp.exp(jnp.sum(lq1_ref[...] * lk1_ref[...], axis=1, keepdims=True))
           - jnp.exp(jnp.sum(lq2_ref[...] * lk2_ref[...], axis=1, keepdims=True)) + lam_init)
    ot = acc_ref[0] / l_ref[0] - lam * (acc_ref[1] / l_ref[1])
    o = jnp.concatenate([ot[0:128, :].T, ot[128:256, :].T], axis=1)
    o_ref[...] = (_rms(o, sn_ref[...], 2 * C_DIM) * (1.0 - lam_init)).astype(BF16)


def _diff_attn_t(q, k, vt, lq1, lk1, lq2, lk2, sn, *, tq, kb, q_off, s_valid, lam_init):
    b, t, _ = q.shape
    s_pad = k.shape[1]
    n_kb = s_pad // kb
    vec = lambda c: pl.BlockSpec((1, c), lambda bi, h, i: (0, 0))
    return pl.pallas_call(
        functools.partial(_diff_t_body, tq=tq, kb=kb, q_off=q_off, s_valid=s_valid, lam_init=lam_init),
        grid=(b, C_HEADS, t // tq),
        in_specs=[pl.BlockSpec((None, tq, 256), lambda bi, h, i: (bi, i, h)),
                  pl.BlockSpec((None, s_pad, 256), lambda bi, h, i: (bi, 0, h)),
                  pl.BlockSpec((None, n_kb, 256, kb), lambda bi, h, i: (bi, 0, h, 0)),
                  vec(128), vec(128), vec(128), vec(128), vec(256)],
        out_specs=pl.BlockSpec((None, tq, 256), lambda bi, h, i: (bi, i, h)),
        out_shape=jax.ShapeDtypeStruct((b, t, C_HEADS * 2 * C_DIM), BF16),
        scratch_shapes=_stage_bufs(2, kb, tq) + [pltpu.VMEM((2, 1, tq), F32), pltpu.VMEM((2, 1, tq), F32),
                                                 pltpu.VMEM((2, 256, tq), F32)],
        compiler_params=_cparams(3),
        name="diff_attn_t",
    )(q, k, vt, lq1, lk1, lq2, lk2, sn)


def _blocked_t(v, kb):
    b, s, c = v.shape
    return jnp.swapaxes(v.reshape(b, s // kb, kb, c), 2, 3)


def _rope_tables(pos, reps):
    p = pos.astype(F32)[:, None]
    inv64 = jnp.power(ROPE_THETA, -jnp.arange(64, dtype=F32) / 64)
    inv32 = jnp.power(ROPE_THETA, -jnp.arange(32, dtype=F32) / 32)
    c, s = jnp.cos(p * inv64), jnp.sin(p * inv64)
    c3, s3 = jnp.cos(p * inv32), jnp.sin(p * inv32)
    z = jnp.zeros_like(s3)
    tabs = {
        "c128": jnp.concatenate([c, c], axis=1),
        "s128": jnp.concatenate([-s, s], axis=1),
        "c64": jnp.concatenate([c3, c3, c3, c3], axis=1),
        "sa64": jnp.concatenate([-s3, z, -s3, z], axis=1),
        "sb64": jnp.concatenate([z, s3, z, s3], axis=1),
    }
    return {k: jnp.tile(v, (reps, 1)) for k, v in tabs.items()}


def _pad_cols(w, n):
    return jnp.pad(w, ((0, 0), (0, n - w.shape[1])))


def _pad_lanes(g, n=128):
    g = g.reshape(1, -1)
    return jnp.pad(g, ((0, 0), (0, n - g.shape[1])))


def _prep_weights(W):
    P = {}
    for nm in ("ffn1", "ffn2"):
        P[nm] = [(W[nm + "_norm"][i].reshape(1, -1), W[nm + "_wg"][i].astype(BF16), W[nm + "_wu"][i].astype(BF16),
                  W[nm + "_wd"][i].astype(BF16)) for i in range(DEPTH)]
    P["mix_norm"] = [W["mix_norm"][i].reshape(1, -1) for i in range(DEPTH)]
    P["a"] = [dict(w=_pad_cols(W["a_w_in"][j], A_IN_PAD).astype(BF16), qn=W["a_q_norm"][j].reshape(1, -1),
                   kn=W["a_k_norm"][j].reshape(1, -1), ikn=_pad_lanes(W["a_idx_k_norm"][j]),
                   wo=W["a_w_out"][j].astype(BF16)) for j in range(W["a_w_in"].shape[0])]
    P["b"] = []
    for j in range(W["b_w_in"].shape[0]):
        wuq = W["b_w_uq"][j].reshape(Q_LORA, B_HEADS, NOPE_DIM + ROPE_DIM)
        wuq_rope = jnp.pad(wuq[:, :, NOPE_DIM:], ((0, 0), (0, 0), (0, 128 - ROPE_DIM)))
        wuq = jnp.concatenate([wuq[:, :, :NOPE_DIM], wuq_rope], axis=2).reshape(Q_LORA, -1)
        P["b"].append(dict(
            w=_pad_cols(W["b_w_in"][j], B_IN_PAD).astype(BF16), qan=W["b_q_a_norm"][j].reshape(1, -1),
            kvan=W["b_kv_a_norm"][j].reshape(1, -1), krn=_pad_lanes(W["b_k_rope_norm"][j]), wuq=wuq.astype(BF16),
            qnn=W["b_q_nope_norm"][j].reshape(1, -1), qrn=_pad_lanes(W["b_q_rope_norm"][j]),
            wukv=W["b_w_ukv"][j].astype(BF16), knn=W["b_k_nope_norm"][j].reshape(1, -1),
            wo=W["b_w_out"][j].astype(BF16)))
    P["c"] = [dict(w=W["c_w_in"][j].astype(BF16), qn=W["c_q_norm"][j].reshape(1, -1),
                   kn=W["c_k_norm"][j].reshape(1, -1), lq1=W["c_lambda_q1"][j].reshape(1, -1),
                   lk1=W["c_lambda_k1"][j].reshape(1, -1), lq2=W["c_lambda_q2"][j].reshape(1, -1),
                   lk2=W["c_lambda_k2"][j].reshape(1, -1), sn=W["c_sub_norm"][j].reshape(1, -1),
                   wo=W["c_w_out"][j].astype(BF16)) for j in range(W["c_w_in"].shape[0])]
    return P


def _with_past(past, new, s_pad):
    x = new if past is None else jnp.concatenate([past.astype(new.dtype), new], axis=1)
    return x if x.shape[1] == s_pad else jnp.pad(x, ((0, 0), (0, s_pad - x.shape[1]), (0, 0)))


def _trunk(x, offset, past, P, cfg):
    b, t, _ = x.shape
    n = b * t
    tm, tq_a, tq, kb, key_major = cfg["tm"], cfg["tq_a"], cfg["tq"], cfg["kb"], cfg["key_major"]
    p_len = 0 if past is None else past["a_k"].shape[2]
    s_valid = p_len + t
    s_pad = -(-s_valid // kb) * kb
    tabs = _rope_tables(offset + jnp.arange(t, dtype=I32), tm // t if tm > t else 1)
    att = dict(kb=kb, q_off=offset, s_valid=s_valid)
    rows = {k: [] for k in ("a_k", "a_v", "a_ik", "b_ckv", "b_kr", "c_k", "c_v")}
    x = x.reshape(n, D_MODEL)
    r3 = lambda a: a.reshape(b, t, a.shape[-1])
    t3 = lambda a: jnp.swapaxes(r3(a), 1, 2)
    pj = lambda nm, j: None if past is None else past[nm][j].reshape(b, p_len, -1)
    for i in range(DEPTH):
        x = _ffn(x, *P["ffn1"][i], cfg["tm_ffn"])
        kind, j = i % N_MIXERS, i // N_MIXERS
        g = P["mix_norm"][i]
        if kind == 0:
            pa = P["a"][j]
            q, k32, v32, kbf, vbf, qi, ki32, kidx, wi = _a_proj(x, g, pa["w"], pa["qn"], pa["kn"], pa["ikn"], tabs, tm)
            rows["a_k"].append(k32.reshape(b, t, A_KV_HEADS, A_HEAD_DIM))
            rows["a_v"].append(v32.reshape(b, t, A_KV_HEADS, A_HEAD_DIM))
            rows["a_ik"].append(ki32.reshape(b, t, IDX_DIM))
            pik = pj("a_ik", j)
            if pik is not None:
                z = jnp.zeros_like(pik)
                pik = jnp.concatenate([pik, z, z, pik], axis=-1)
            kidx_all = _with_past(pik, r3(kidx), s_pad)
            k_all = _with_past(pj("a_k", j), r3(kbf), s_pad)
            v_all = _with_past(pj("a_v", j), r3(vbf), s_pad)
            topk = min(TOPK_MAX, s_valid // 4)
            if key_major:
                o = _dsa_attn_t(r3(q), r3(qi), r3(wi), kidx_all, k_all, _blocked_t(v_all, kb),
                                tq=tq_a, topk=topk, **att)
            else:
                o = _dsa_attn(r3(q), r3(qi), r3(wi), kidx_all, k_all, v_all, tq=tq_a, topk=topk, **att)
        elif kind == 1:
            pb = P["b"][j]
            q, ckv, kr32, krbf = _b_proj(x, g, pb["w"], pb["qan"], pb["kvan"], pb["krn"], pb["wuq"], pb["qnn"],
                                         pb["qrn"], tabs, tm)
            rows["b_ckv"].append(ckv.reshape(b, t, KV_LORA))
            rows["b_kr"].append(kr32.reshape(b, t, ROPE_DIM))
            ckv_all = _with_past(pj("b_ckv", j), r3(ckv), s_pad)
            pkr = pj("b_kr", j)
            if pkr is not None:
                pkr = jnp.concatenate([pkr, jnp.zeros_like(pkr)], axis=-1)
            kn, v = _kv_up(ckv_all.reshape(b * s_pad, KV_LORA), pb["wukv"], pb["knn"], math.gcd(b * s_pad, 512))
            kn, v, kr_all = kn.reshape(b, s_pad, -1), v.reshape(b, s_pad, -1), _with_past(pkr, r3(krbf), s_pad)
            if key_major:
                o = _mla_attn_t(t3(q), kn, kr_all, _blocked_t(v, kb), tq=tq, **att)
            else:
                o = _mla_attn(r3(q), kn, kr_all, v, tq=tq, **att)
        else:
            pc = P["c"][j]
            q, k32, v32, kbf, vbf = _c_proj(x, g, pc["w"], pc["qn"], pc["kn"], tabs, tm)
            rows["c_k"].append(k32.reshape(b, t, C_HEADS, 2, C_DIM))
            rows["c_v"].append(v32.reshape(b, t, C_HEADS, 2 * C_DIM))
            k_all = _with_past(pj("c_k", j), r3(kbf), s_pad)
            v_all = _with_past(pj("c_v", j), r3(vbf), s_pad)
            lam = (pc["lq1"], pc["lk1"], pc["lq2"], pc["lk2"], pc["sn"])
            lam_init = 0.8 - 0.6 * math.exp(-0.3 * i)
            if key_major:
                att_c = dict(att, kb=cfg["kb_diff"])
                o = _diff_attn_t(r3(q), k_all, _blocked_t(v_all, cfg["kb_diff"]), *lam, tq=tq, lam_init=lam_init,
                                 **att_c)
            else:
                o = _diff_attn(r3(q), k_all, v_all, *lam, tq=tq, lam_init=lam_init, **att)
        wo = (P["a"], P["b"], P["c"])[kind][j]["wo"]
        x = _ffn(x, *P["ffn2"][i], cfg["tm_ffn"], attn=o.reshape(n, -1), wo=wo)
    order = ("a_k", "a_v", "a_ik", "b_ckv", "b_kr", "c_k", "c_v")
    stack = lambda rs: rs[0][None] if len(rs) == 1 else jnp.stack(rs)
    return x.reshape(b, t, D_MODEL), tuple(stack(rows[k]) for k in order)


PROMPT_CFG = dict(tm=512, tm_ffn=1024, tq_a=128, tq=256, kb=256, kb_diff=512, key_major=True)
SAMPLE_CFG = dict(tm=128, tm_ffn=128, tq_a=16, tq=16, kb=1280, key_major=False)


@jax.jit
def _forward(x_prompt, x_sample, past, W):
    P = _prep_weights(W)
    y_p, rows_p = _trunk(x_prompt, 0, None, P, PROMPT_CFG)
    y_s, rows_s = _trunk(x_sample, past["a_k"].shape[2], past, P, SAMPLE_CFG)
    return (y_p, y_s) + rows_p + rows_s


def kernel(x_prompt, x_sample, cache_a_k, cache_a_v, cache_a_idx_k, cache_b_ckv, cache_b_krope, cache_c_k, cache_c_v, ffn1_norm, ffn1_wg, ffn1_wu, ffn1_wd, mix_norm, ffn2_norm, ffn2_wg, ffn2_wu, ffn2_wd, a_w_in, a_q_norm, a_k_norm, a_idx_k_norm, a_w_out, b_w_in, b_q_a_norm, b_kv_a_norm, b_w_uq, b_w_ukv, b_q_nope_norm, b_q_rope_norm, b_k_nope_norm, b_k_rope_norm, b_w_out, c_w_in, c_q_norm, c_k_norm, c_lambda_q1, c_lambda_k1, c_lambda_q2, c_lambda_k2, c_sub_norm, c_w_out):
    W = dict(ffn1_norm=ffn1_norm, ffn1_wg=ffn1_wg, ffn1_wu=ffn1_wu, ffn1_wd=ffn1_wd, mix_norm=mix_norm,
             ffn2_norm=ffn2_norm, ffn2_wg=ffn2_wg, ffn2_wu=ffn2_wu, ffn2_wd=ffn2_wd,
             a_w_in=a_w_in, a_q_norm=a_q_norm, a_k_norm=a_k_norm, a_idx_k_norm=a_idx_k_norm, a_w_out=a_w_out,
             b_w_in=b_w_in, b_q_a_norm=b_q_a_norm, b_kv_a_norm=b_kv_a_norm, b_w_uq=b_w_uq, b_w_ukv=b_w_ukv,
             b_q_nope_norm=b_q_nope_norm, b_q_rope_norm=b_q_rope_norm, b_k_nope_norm=b_k_nope_norm,
             b_k_rope_norm=b_k_rope_norm, b_w_out=b_w_out,
             c_w_in=c_w_in, c_q_norm=c_q_norm, c_k_norm=c_k_norm, c_lambda_q1=c_lambda_q1,
             c_lambda_k1=c_lambda_k1, c_lambda_q2=c_lambda_q2, c_lambda_k2=c_lambda_k2, c_sub_norm=c_sub_norm,
             c_w_out=c_w_out)
    past = dict(a_k=cache_a_k, a_v=cache_a_v, a_ik=cache_a_idx_k, b_ckv=cache_b_ckv, b_kr=cache_b_krope,
                c_k=cache_c_k, c_v=cache_c_v)
    return _forward(x_prompt, x_sample, past, W)
```

```python
import functools
import math

import jax
import jax.numpy as jnp
from jax import lax
from jax.experimental import pallas as pl
from jax.experimental.pallas import tpu as pltpu

F32 = jnp.float32
BF16 = jnp.bfloat16
I32 = jnp.int32

D_MODEL = 1024
DEPTH = 4
CHUNK_SHIFT = 6
N_MIXERS = 3
ROPE_THETA = 10000.0
EPS = 1e-6
D_FF = 2816

A_HEADS = 8
A_KV_HEADS = 2
A_GROUP = A_HEADS // A_KV_HEADS
A_HEAD_DIM = 128
IDX_HEADS = 8
IDX_DIM = 64
TOPK_MAX = 256
LOG2E = math.log2(math.e)
A_SCALE = A_HEAD_DIM ** -0.5 * LOG2E
IDX_W_SCALE = (IDX_HEADS * IDX_DIM) ** -0.5
A_IN = 2120
A_IN_PAD = 2176

B_HEADS = 8
Q_LORA = 384
KV_LORA = 256
NOPE_DIM = 128
ROPE_DIM = 64
V_DIM = 128
B_SCALE = (NOPE_DIM + ROPE_DIM) ** -0.5 * LOG2E
B_IN = 704
B_IN_PAD = 768

C_HEADS = 4
C_DIM = 128
C_SCALE = C_DIM ** -0.5 * LOG2E

LANES = 128
NEG = -1e30
INT_MIN = -(2 ** 31)
VMEM_LIMIT = 56 * 1024 * 1024


def _cparams(n_axes):
    return pltpu.CompilerParams(dimension_semantics=("arbitrary",) * n_axes, vmem_limit_bytes=VMEM_LIMIT)


def _dot(a, b):
    return jnp.dot(a, b, preferred_element_type=F32)


def _dot_nt(a, b):
    return lax.dot_general(a, b, (((1,), (1,)), ((), ())), preferred_element_type=F32)


def _rms(x, g, n):
    ms = jnp.sum(x * x, axis=-1, keepdims=True) * (1.0 / n)
    return x * lax.rsqrt(ms + EPS) * g


def _rope128(x, c, s):
    return x * c + pltpu.roll(x, 64, 1) * s


def _rope64(x, c, sa, sb):
    return x * c + pltpu.roll(x, 96, 1) * sa + pltpu.roll(x, 32, 1) * sb


FFN_CHUNK = 256


def _ffn_body(*refs, fc, mixed):
    if mixed:
        attn_ref, wo_ref, x_ref, g_ref, wg_ref, wu_ref, wd_ref, o_ref = refs
        x = x_ref[...] + _dot(attn_ref[...], wo_ref[...])
    else:
        x_ref, g_ref, wg_ref, wu_ref, wd_ref, o_ref = refs
        x = x_ref[...]
    h = _rms(x, g_ref[...], D_MODEL).astype(BF16)
    y = None
    for c in range(D_FF // fc):
        a = _dot(h, wg_ref[:, c * fc:(c + 1) * fc])
        u = _dot(h, wu_ref[:, c * fc:(c + 1) * fc])
        act = (a * jax.nn.sigmoid(a) * u).astype(BF16)
        part = _dot(act, wd_ref[c * fc:(c + 1) * fc, :])
        y = part if y is None else y + part
    o_ref[...] = x + 0.5 * y


def _const_spec(shape):
    nd = len(shape)
    return pl.BlockSpec(shape, lambda *_: (0,) * nd, pipeline_mode=pl.Buffered(1))


def _row_spec(tm, n):
    return pl.BlockSpec((tm, n), lambda i: (i, 0))


def _ffn(x, g, wg, wu, wd, tm, attn=None, wo=None):
    n = x.shape[0]
    mixed = attn is not None
    pre_specs = [_row_spec(tm, attn.shape[1]), _const_spec(wo.shape)] if mixed else []
    return pl.pallas_call(
        functools.partial(_ffn_body, fc=FFN_CHUNK, mixed=mixed),
        grid=(n // tm,),
        in_specs=pre_specs + [_row_spec(tm, D_MODEL), _const_spec((1, D_MODEL)), _const_spec((D_MODEL, D_FF)),
                              _const_spec((D_MODEL, D_FF)), _const_spec((D_FF, D_MODEL))],
        out_specs=_row_spec(tm, D_MODEL),
        out_shape=jax.ShapeDtypeStruct((n, D_MODEL), F32),
        compiler_params=_cparams(1),
        name="ffn_mixed" if mixed else "ffn",
    )(*((attn, wo) if mixed else ()), x, g, wg, wu, wd)


PROJ_PARTS = 2

def _a_proj_body(x_ref, g_ref, w_ref, qn_ref, kn_ref, ikn_ref, c128_ref, s128_ref, c64_ref, sa64_ref, sb64_ref,
                 q_ref, k32_ref, v32_ref, kbf_ref, vbf_ref, qi_ref, ki32_ref, kidx_ref, wi_ref, y_ref):
    hm = x_ref.shape[0] // PROJ_PARTS
    for part in range(PROJ_PARTS):
        rows = slice(part * hm, (part + 1) * hm)
        y_ref[part] = _dot(_rms(x_ref[rows, :], g_ref[...], D_MODEL).astype(BF16), w_ref[...])
    for part in range(PROJ_PARTS):
        rows = slice(part * hm, (part + 1) * hm)
        c128, s128 = c128_ref[rows, :], s128_ref[rows, :]
        c64, sa64, sb64 = c64_ref[rows, :], sa64_ref[rows, :], sb64_ref[rows, :]
        for hd in range(A_HEADS):
            sl = slice(hd * 128, (hd + 1) * 128)
            qh = _rope128(_rms(y_ref[part, :, sl], qn_ref[...], A_HEAD_DIM), c128, s128)
            q_ref[rows, sl] = (qh * A_SCALE).astype(BF16)
        for hd in range(A_KV_HEADS):
            sl = slice(hd * 128, (hd + 1) * 128)
            kh = _rope128(_rms(y_ref[part, :, 1024 + hd * 128:1024 + (hd + 1) * 128], kn_ref[...], A_HEAD_DIM),
                          c128, s128)
            k32_ref[rows, sl] = kh
            kbf_ref[rows, sl] = kh.astype(BF16)
        v = y_ref[part, :, 1280:1536]
        v32_ref[rows, :] = v
        vbf_ref[rows, :] = v.astype(BF16)
        for p in range(IDX_HEADS // 2):
            sl = slice(p * 128, (p + 1) * 128)
            qi_ref[rows, sl] = _rope64(y_ref[part, :, 1536 + p * 128:1536 + (p + 1) * 128],
                                       c64, sa64, sb64).astype(BF16)
        tail = y_ref[part, :, 2048:2176]
        lane = lax.broadcasted_iota(I32, tail.shape, 1)
        low = lane < IDX_DIM
        kin = jnp.where(low, tail, 0.0)
        ki = _rope64(_rms(kin, ikn_ref[...], IDX_DIM), c64, sa64, sb64)
        ki = jnp.where(low, ki, 0.0)
        ki32_ref[rows, :] = ki[:, :IDX_DIM]
        kidx_ref[rows, 0:128] = ki.astype(BF16)
        kidx_ref[rows, 128:256] = pltpu.roll(ki, 64, 1).astype(BF16)
        wi_ref[rows, :] = pltpu.roll(tail, 64, 1) * IDX_W_SCALE


def _tab_spec(tm, t):
    nt = t // tm
    return pl.BlockSpec((tm, LANES), lambda i: (i % nt, 0))


def _a_proj(x, g, w, qn, kn, ikn, tabs, tm):
    n = x.shape[0]
    t = tabs["c128"].shape[0]
    outs = [(1024, BF16), (256, F32), (256, F32), (256, BF16), (256, BF16), (512, BF16), (IDX_DIM, F32),
            (256, BF16), (LANES, F32)]
    return pl.pallas_call(
        _a_proj_body,
        grid=(n // tm,),
        in_specs=[_row_spec(tm, D_MODEL), _const_spec((1, D_MODEL)), _const_spec((D_MODEL, A_IN_PAD)),
                  _const_spec((1, 128)), _const_spec((1, 128)), _const_spec((1, 128))]
        + [_tab_spec(tm, t)] * 5,
        out_specs=[_row_spec(tm, c) for c, _ in outs],
        out_shape=[jax.ShapeDtypeStruct((n, c), dt) for c, dt in outs],
        scratch_shapes=[pltpu.VMEM((PROJ_PARTS, tm // PROJ_PARTS, A_IN_PAD), F32)],
        compiler_params=_cparams(1),
        name="a_proj",
    )(x, g, w, qn, kn, ikn, tabs["c128"], tabs["s128"], tabs["c64"], tabs["sa64"], tabs["sb64"])


def _b_proj_body(x_ref, g_ref, w_ref, qan_ref, kvan_ref, krn_ref, wuq_ref, qnn_ref, qrn_ref,
                 c64_ref, sa64_ref, sb64_ref, q_ref, ckv_ref, kr32_ref, krbf_ref, qq_ref):
    hm = x_ref.shape[0] // PROJ_PARTS
    for part in range(PROJ_PARTS):
        rows = slice(part * hm, (part + 1) * hm)
        h = _rms(x_ref[rows, :], g_ref[...], D_MODEL).astype(BF16)
        y = _dot(h, w_ref[...])
        ckv_ref[rows, :] = _rms(y[:, 384:640], kvan_ref[...], KV_LORA)
        kr = _rope64(_rms(y[:, 640:768], krn_ref[...], ROPE_DIM), c64_ref[rows, :], sa64_ref[rows, :],
                     sb64_ref[rows, :])
        kr32_ref[rows, :] = kr[:, :ROPE_DIM]
        krbf_ref[rows, :] = kr.astype(BF16)
        qq_ref[part] = _dot(_rms(y[:, 0:384], qan_ref[...], Q_LORA).astype(BF16), wuq_ref[...])
    for part in range(PROJ_PARTS):
        rows = slice(part * hm, (part + 1) * hm)
        c64, sa64, sb64 = c64_ref[rows, :], sa64_ref[rows, :], sb64_ref[rows, :]
        for hd in range(B_HEADS):
            qn = _rms(qq_ref[part, :, hd * 256:hd * 256 + 128], qnn_ref[...], NOPE_DIM)
            qr = _rope64(_rms(qq_ref[part, :, hd * 256 + 128:(hd + 1) * 256], qrn_ref[...], ROPE_DIM),
                         c64, sa64, sb64)
            q_ref[rows, hd * 256:hd * 256 + 128] = (qn * B_SCALE).astype(BF16)
            q_ref[rows, hd * 256 + 128:(hd + 1) * 256] = (qr * B_SCALE).astype(BF16)


def _b_proj(x, g, w, qan, kvan, krn, wuq, qnn, qrn, tabs, tm):
    n = x.shape[0]
    t = tabs["c64"].shape[0]
    outs = [(B_HEADS * 256, BF16), (KV_LORA, F32), (ROPE_DIM, F32), (LANES, BF16)]
    return pl.pallas_call(
        _b_proj_body,
        grid=(n // tm,),
        in_specs=[_row_spec(tm, D_MODEL), _const_spec((1, D_MODEL)), _const_spec((D_MODEL, B_IN_PAD)),
                  _const_spec((1, Q_LORA)), _const_spec((1, KV_LORA)), _const_spec((1, 128)),
                  _const_spec((Q_LORA, 2048)), _const_spec((1, 128)), _const_spec((1, 128))]
        + [_tab_spec(tm, t)] * 3,
        out_specs=[_row_spec(tm, c) for c, _ in outs],
        out_shape=[jax.ShapeDtypeStruct((n, c), dt) for c, dt in outs],
        scratch_shapes=[pltpu.VMEM((PROJ_PARTS, tm // PROJ_PARTS, 2048), F32)],
        compiler_params=_cparams(1),
        name="b_proj",
    )(x, g, w, qan, kvan, krn, wuq, qnn, qrn, tabs["c64"], tabs["sa64"], tabs["sb64"])


def _kv_up_body(ckv_ref, w_ref, knn_ref, kn_ref, v_ref):
    y = _dot(ckv_ref[...].astype(BF16), w_ref[...])
    for hd in range(B_HEADS):
        kn_ref[:, hd * 128:(hd + 1) * 128] = _rms(y[:, hd * 256:hd * 256 + 128], knn_ref[...], NOPE_DIM).astype(BF16)
        v_ref[:, hd * 128:(hd + 1) * 128] = y[:, hd * 256 + 128:(hd + 1) * 256].astype(BF16)


def _kv_up(ckv, w, knn, tm):
    n = ckv.shape[0]
    return pl.pallas_call(
        _kv_up_body,
        grid=(n // tm,),
        in_specs=[_row_spec(tm, KV_LORA), _const_spec((KV_LORA, 2048)), _const_spec((1, 128))],
        out_specs=[_row_spec(tm, 1024), _row_spec(tm, 1024)],
        out_shape=[jax.ShapeDtypeStruct((n, 1024), BF16)] * 2,
        compiler_params=_cparams(1),
        name="kv_up",
    )(ckv, w, knn)


def _c_proj_body(x_ref, g_ref, w_ref, qn_ref, kn_ref, c128_ref, s128_ref,
                 q_ref, k32_ref, v32_ref, kbf_ref, vbf_ref, y_ref):
    hm = x_ref.shape[0] // PROJ_PARTS
    for part in range(PROJ_PARTS):
        rows = slice(part * hm, (part + 1) * hm)
        y_ref[part] = _dot(_rms(x_ref[rows, :], g_ref[...], D_MODEL).astype(BF16), w_ref[...])
    for part in range(PROJ_PARTS):
        rows = slice(part * hm, (part + 1) * hm)
        c128, s128 = c128_ref[rows, :], s128_ref[rows, :]
        for hd in range(2 * C_HEADS):
            sl = slice(hd * 128, (hd + 1) * 128)
            qh = _rope128(_rms(y_ref[part, :, sl], qn_ref[...], C_DIM), c128, s128)
            q_ref[rows, sl] = (qh * C_SCALE).astype(BF16)
            kh = _rope128(_rms(y_ref[part, :, 1024 + hd * 128:1024 + (hd + 1) * 128], kn_ref[...], C_DIM),
                          c128, s128)
            k32_ref[rows, sl] = kh
            kbf_ref[rows, sl] = kh.astype(BF16)
        v = y_ref[part, :, 2048:3072]
        v32_ref[rows, :] = v
        vbf_ref[rows, :] = v.astype(BF16)


def _c_proj(x, g, w, qn, kn, tabs, tm):
    n = x.shape[0]
    t = tabs["c128"].shape[0]
    outs = [(1024, BF16), (1024, F32), (1024, F32), (1024, BF16), (1024, BF16)]
    return pl.pallas_call(
        _c_proj_body,
        grid=(n // tm,),
        in_specs=[_row_spec(tm, D_MODEL), _const_spec((1, D_MODEL)), _const_spec((D_MODEL, 3072)),
                  _const_spec((1, 128)), _const_spec((1, 128))] + [_tab_spec(tm, t)] * 2,
        out_specs=[_row_spec(tm, c) for c, _ in outs],
        out_shape=[jax.ShapeDtypeStruct((n, c), dt) for c, dt in outs],
        scratch_shapes=[pltpu.VMEM((PROJ_PARTS, tm // PROJ_PARTS, 3072), F32)],
        compiler_params=_cparams(1),
        name="c_proj",
    )(x, g, w, qn, kn, tabs["c128"], tabs["s128"])


def _n_key_blocks(i, tq, kb, q_off, s_valid):
    last_chunk = lax.shift_right_logical(q_off + (i + 1) * tq - 1, CHUNK_SHIFT)
    kend = jnp.minimum((last_chunk + 1) << CHUNK_SHIFT, s_valid)
    return lax.div(kend + kb - 1, jnp.int32(kb))


def _q_chunk(i, tq, q_off):
    row = lax.broadcasted_iota(I32, (tq, 1), 0)
    return lax.shift_right_logical(q_off + i * tq + row, CHUNK_SHIFT)


def _k_chunk(ks, kb, s_valid):
    kpos = ks + lax.broadcasted_iota(I32, (1, kb), 1)
    return jnp.where(kpos < s_valid, lax.shift_right_logical(kpos, CHUNK_SHIFT), 2 ** 30)


def _flash_step(s, v, m_ref, l_ref, acc_ref):
    m_prev = m_ref[...]
    m_new = jnp.maximum(m_prev, jnp.max(s, axis=1, keepdims=True))
    alpha = jnp.exp2(m_prev - m_new)
    p = jnp.exp2(s - m_new)
    l_ref[...] = alpha * l_ref[...] + jnp.sum(p, axis=1, keepdims=True)
    acc_ref[...] = alpha * acc_ref[...] + _dot(p.astype(BF16), v)
    m_ref[...] = m_new


def _key_limit(i, tq, q_off, s_valid):
    qpos = q_off + i * tq + lax.broadcasted_iota(I32, (1, tq), 1)
    return jnp.minimum((lax.shift_right_logical(qpos, CHUNK_SHIFT) + 1) << CHUNK_SHIFT, s_valid)


def _n_full_blocks(i, tq, kb, q_off, s_valid):
    first_chunk = lax.shift_right_logical(q_off + i * tq, CHUNK_SHIFT)
    kend = jnp.minimum((first_chunk + 1) << CHUNK_SHIFT, s_valid)
    return lax.shift_right_logical(kend, int(math.log2(kb)))


def _flash_init(m_ref, l_ref, acc_ref):
    m_ref[...] = jnp.full(m_ref.shape, NEG, F32)
    l_ref[...] = jnp.zeros(l_ref.shape, F32)
    acc_ref[...] = jnp.zeros(acc_ref.shape, F32)


def _dsa_body(q_ref, qi_ref, wi_ref, kidx_ref, k_ref, v_ref, o_ref, keys_ref, m_ref, l_ref, acc_ref,
              *, tq, kb, q_off, s_valid, topk):
    i = pl.program_id(1)
    n_blk = _n_key_blocks(i, tq, kb, q_off, s_valid)
    qc = _q_chunk(i, tq, q_off)
    wi = wi_ref[...]
    wcol = [wi[:, hd:hd + 1] for hd in range(IDX_HEADS)]

    def score_block(j, carry):
        ks = pl.multiple_of(j * kb, kb)
        k_lo = kidx_ref[pl.ds(ks, kb), 0:128]
        k_hi = kidx_ref[pl.ds(ks, kb), 128:256]
        sc = jnp.zeros((tq, kb), F32)
        for p in range(IDX_HEADS // 2):
            qp = qi_ref[:, p * 128:(p + 1) * 128]
            sc = sc + wcol[2 * p] * jnp.maximum(_dot_nt(qp, k_lo), 0.0)
            sc = sc + wcol[2 * p + 1] * jnp.maximum(_dot_nt(qp, k_hi), 0.0)
        bits = lax.bitcast_convert_type(sc, I32)
        key = bits ^ ((bits >> 31) & 0x7FFFFFFF)
        keys_ref[j] = jnp.where(_k_chunk(ks, kb, s_valid) <= qc, key, INT_MIN)
        return carry

    lax.fori_loop(0, n_blk, score_block, 0)

    def count(pred):
        def blk(j, c):
            x = jnp.where(pred(keys_ref[j], j * kb), 1.0, 0.0)
            part = x[:, 0:LANES]
            for g in range(1, kb // LANES):
                part = part + x[:, g * LANES:(g + 1) * LANES]
            return c + part
        c = lax.fori_loop(0, n_blk, blk, jnp.zeros((tq, LANES), F32))
        return jnp.sum(c, axis=1, keepdims=True)

    kf = float(topk)

    def bit_step(b, t):
        cand = t + jnp.left_shift(jnp.int32(1), 31 - b)
        cnt = count(lambda kblk, _: kblk >= cand)
        return jnp.where(cnt >= kf, cand, t)

    thr = lax.fori_loop(0, 32, bit_step, jnp.full((tq, 1), INT_MIN, I32))
    need = kf - count(lambda kblk, _: kblk > thr)
    n_eq = count(lambda kblk, _: kblk == thr)
    partial = jnp.logical_and(n_eq > need, thr != INT_MIN)

    @pl.when(jnp.max(jnp.where(partial, 1.0, 0.0)) > 0.0)
    def _():
        lane = lax.broadcasted_iota(I32, (1, kb), 1)

        def idx_step(b, x):
            cand = x + jnp.left_shift(jnp.int32(1), b)
            cnt = count(lambda kblk, k0: jnp.logical_and(kblk == thr, k0 + lane < cand))
            return jnp.where(cnt < need, cand, x)

        nbits = max(1, int(s_valid - 1).bit_length())
        cut = lax.fori_loop(0, nbits, lambda b, x: idx_step(nbits - 1 - b, x), jnp.zeros((tq, 1), I32))

        def demote(j, carry):
            kblk = keys_ref[j]
            drop = jnp.logical_and(jnp.logical_and(kblk == thr, j * kb + lane > cut), partial)
            keys_ref[j] = jnp.where(drop, kblk - 1, kblk)
            return carry

        lax.fori_loop(0, n_blk, demote, 0)

    thr_sel = jnp.maximum(thr, INT_MIN + 1)

    _flash_init(m_ref, l_ref, acc_ref)

    def attend(j, carry):
        ks = pl.multiple_of(j * kb, kb)
        bias = jnp.where(keys_ref[j] >= thr_sel, 0.0, NEG)
        for g in range(A_KV_HEADS):
            qg = jnp.concatenate([q_ref[:, (g * A_GROUP + a) * 128:(g * A_GROUP + a + 1) * 128]
                                  for a in range(A_GROUP)], axis=0)
            s = _dot_nt(qg, k_ref[pl.ds(ks, kb), g * 128:(g + 1) * 128])
            s = (s.reshape(A_GROUP, tq, kb) + bias[None]).reshape(A_GROUP * tq, kb)
            _flash_step(s, v_ref[pl.ds(ks, kb), g * 128:(g + 1) * 128], m_ref.at[g], l_ref.at[g], acc_ref.at[g])
        return carry

    lax.fori_loop(0, n_blk, attend, 0)
    for g in range(A_KV_HEADS):
        o = acc_ref[g] / l_ref[g]
        for a in range(A_GROUP):
            hd = g * A_GROUP + a
            o_ref[:, hd * 128:(hd + 1) * 128] = o[a * tq:(a + 1) * tq].astype(BF16)


def _dsa_attn(q, qi, wi, kidx, k, v, *, tq, kb, q_off, s_valid, topk):
    b, t, _ = q.shape
    s_pad = k.shape[1]
    qspec = lambda c: pl.BlockSpec((None, tq, c), lambda bi, i: (bi, i, 0))
    kspec = lambda c: pl.BlockSpec((None, s_pad, c), lambda bi, i: (bi, 0, 0))
    rows = A_GROUP * tq
    return pl.pallas_call(
        functools.partial(_dsa_body, tq=tq, kb=kb, q_off=q_off, s_valid=s_valid, topk=topk),
        grid=(b, t // tq),
        in_specs=[qspec(1024), qspec(512), qspec(LANES), kspec(256), kspec(256), kspec(256)],
        out_specs=qspec(1024),
        out_shape=jax.ShapeDtypeStruct((b, t, 1024), BF16),
        scratch_shapes=[pltpu.VMEM((s_pad // kb, tq, kb), I32), pltpu.VMEM((A_KV_HEADS, rows, 1), F32),
                        pltpu.VMEM((A_KV_HEADS, rows, 1), F32), pltpu.VMEM((A_KV_HEADS, rows, 128), F32)],
        compiler_params=_cparams(2),
        name="dsa_attn",
    )(q, qi, wi, kidx, k, v)


def _mla_body(q_ref, kn_ref, kr_ref, v_ref, o_ref, m_ref, l_ref, acc_ref, *, tq, kb, q_off, s_valid):
    i = pl.program_id(2)
    n_blk = _n_key_blocks(i, tq, kb, q_off, s_valid)
    qc = _q_chunk(i, tq, q_off)
    _flash_init(m_ref, l_ref, acc_ref)

    def attend(j, carry):
        ks = pl.multiple_of(j * kb, kb)
        kcat = jnp.concatenate([kn_ref[pl.ds(ks, kb), :], kr_ref[pl.ds(ks, kb), :]], axis=1)
        s = _dot_nt(q_ref[...], kcat)
        s = jnp.where(_k_chunk(ks, kb, s_valid) <= qc, s, NEG)
        _flash_step(s, v_ref[pl.ds(ks, kb), :], m_ref, l_ref, acc_ref)
        return carry

    lax.fori_loop(0, n_blk, attend, 0)
    o_ref[...] = (acc_ref[...] / l_ref[...]).astype(BF16)


def _mla_attn(q, kn, kr, v, *, tq, kb, q_off, s_valid):
    b, t, _ = q.shape
    s_pad = kn.shape[1]
    return pl.pallas_call(
        functools.partial(_mla_body, tq=tq, kb=kb, q_off=q_off, s_valid=s_valid),
        grid=(b, B_HEADS, t // tq),
        in_specs=[pl.BlockSpec((None, tq, 256), lambda bi, h, i: (bi, i, h)),
                  pl.BlockSpec((None, s_pad, 128), lambda bi, h, i: (bi, 0, h)),
                  pl.BlockSpec((None, s_pad, 128), lambda bi, h, i: (bi, 0, 0)),
                  pl.BlockSpec((None, s_pad, 128), lambda bi, h, i: (bi, 0, h))],
        out_specs=pl.BlockSpec((None, tq, 128), lambda bi, h, i: (bi, i, h)),
        out_shape=jax.ShapeDtypeStruct((b, t, B_HEADS * V_DIM), BF16),
        scratch_shapes=[pltpu.VMEM((tq, 1), F32), pltpu.VMEM((tq, 1), F32), pltpu.VMEM((tq, 128), F32)],
        compiler_params=_cparams(3),
        name="mla_attn",
    )(q, kn, kr, v)


def _diff_body(q_ref, k_ref, v_ref, lq1_ref, lk1_ref, lq2_ref, lk2_ref, sn_ref, o_ref, m_ref, l_ref, acc_ref,
               *, tq, kb, q_off, s_valid, lam_init):
    i = pl.program_id(2)
    n_blk = _n_key_blocks(i, tq, kb, q_off, s_valid)
    qc = _q_chunk(i, tq, q_off)
    _flash_init(m_ref, l_ref, acc_ref)

    def attend(j, carry):
        ks = pl.multiple_of(j * kb, kb)
        ok = _k_chunk(ks, kb, s_valid) <= qc
        v = v_ref[pl.ds(ks, kb), :]
        for p in range(2):
            s = _dot_nt(q_ref[:, p * 128:(p + 1) * 128], k_ref[pl.ds(ks, kb), p * 128:(p + 1) * 128])
            s = jnp.where(ok, s, NEG)
            _flash_step(s, v, m_ref.at[p], l_ref.at[p], acc_ref.at[p])
        return carry

    lax.fori_loop(0, n_blk, attend, 0)
    lam = (jnp.exp(jnp.sum(lq1_ref[...] * lk1_ref[...], axis=1, keepdims=True))
           - jnp.exp(jnp.sum(lq2_ref[...] * lk2_ref[...], axis=1, keepdims=True)) + lam_init)
    o = acc_ref[0] / l_ref[0] - lam * (acc_ref[1] / l_ref[1])
    o_ref[...] = (_rms(o, sn_ref[...], 2 * C_DIM) * (1.0 - lam_init)).astype(BF16)


def _diff_attn(q, k, v, lq1, lk1, lq2, lk2, sn, *, tq, kb, q_off, s_valid, lam_init):
    b, t, _ = q.shape
    s_pad = k.shape[1]
    vec = lambda c: pl.BlockSpec((1, c), lambda bi, h, i: (0, 0))
    return pl.pallas_call(
        functools.partial(_diff_body, tq=tq, kb=kb, q_off=q_off, s_valid=s_valid, lam_init=lam_init),
        grid=(b, C_HEADS, t // tq),
        in_specs=[pl.BlockSpec((None, tq, 256), lambda bi, h, i: (bi, i, h)),
                  pl.BlockSpec((None, s_pad, 256), lambda bi, h, i: (bi, 0, h)),
                  pl.BlockSpec((None, s_pad, 256), lambda bi, h, i: (bi, 0, h)),
                  vec(128), vec(128), vec(128), vec(128), vec(256)],
        out_specs=pl.BlockSpec((None, tq, 256), lambda bi, h, i: (bi, i, h)),
        out_shape=jax.ShapeDtypeStruct((b, t, C_HEADS * 2 * C_DIM), BF16),
        scratch_shapes=[pltpu.VMEM((2, tq, 1), F32), pltpu.VMEM((2, tq, 1), F32), pltpu.VMEM((2, tq, 256), F32)],
        compiler_params=_cparams(3),
        name="diff_attn",
    )(q, k, v, lq1, lk1, lq2, lk2, sn)


SCORE_BLOCKS = 4
SEARCH_BLOCKS = 4
SEARCH_BITS_UNCHECKED = 22
SEARCH_BITS_PER_CHECK = 2


def _stage_bufs(n_chain, kb, r):
    return [pltpu.VMEM((n_chain, 2, kb, r), F32), pltpu.VMEM((n_chain, 2, kb, r), BF16),
            pltpu.VMEM((n_chain, 2, 1, r), F32)]


def _staged_flash_t(bufs, state, lo, hi, n_kb, qk, prep, vt_of, first=True, last=True):
    s_ref, p_ref, a_ref = bufs
    m_ref, l_ref, acc_ref = state
    n_chain = s_ref.shape[0]
    clamp = lambda j: jnp.clip(j, 0, n_kb - 1)

    def softmax(c, j, slot):
        st = prep(c, j, s_ref[c, slot])
        m_prev = m_ref[c]
        m_new = jnp.maximum(m_prev, jnp.max(st, axis=0, keepdims=True))
        alpha = jnp.exp2(m_prev - m_new)
        p = jnp.exp2(st - m_new)
        l_ref[c] = alpha * l_ref[c] + jnp.sum(p, axis=0, keepdims=True)
        m_ref[c] = m_new
        p_ref[c, slot] = p.astype(BF16)
        a_ref[c, slot] = alpha

    def values(c, j, slot):
        acc_ref[c] = a_ref[c, slot] * acc_ref[c] + _dot(vt_of(c, clamp(j)), p_ref[c, slot])

    if first:
        for c in range(n_chain):
            s_ref[c, 0] = qk(c, clamp(lo))
            p_ref[c, 1] = jnp.zeros(p_ref.shape[2:], BF16)
            a_ref[c, 1] = jnp.ones(a_ref.shape[2:], F32)

    def turn(t, carry):
        j0 = lo + 2 * t
        for c in range(n_chain):
            s_ref[c, 1] = qk(c, clamp(j0 + 1))
        for c in range(n_chain):
            softmax(c, j0, 0)
        for c in range(n_chain):
            values(c, j0 - 1, 1)
        for c in range(n_chain):
            s_ref[c, 0] = qk(c, clamp(j0 + 2))
        for c in range(n_chain):
            softmax(c, j0 + 1, 1)
        for c in range(n_chain):
            values(c, j0, 0)
        return carry

    n_turn = lax.shift_right_logical(hi - lo + 1, 1)
    lax.fori_loop(0, n_turn, turn, 0)
    if last:
        for c in range(n_chain):
            values(c, lo + 2 * n_turn - 1, 1)


def _dsa_t_body(q_ref, qi_ref, wi_ref, kidx_ref, k_ref, vt_ref, o_ref, keys_ref, sc_ref, s_ref, p_ref, a_ref,
                m_ref, l_ref, acc_ref, *, tq, kb, q_off, s_valid, topk):
    i = pl.program_id(1)
    n_blk = _n_key_blocks(i, tq, kb, q_off, s_valid)
    n_kb = keys_ref.shape[0]
    n_sb = lax.shift_right_logical(n_blk + SEARCH_BLOCKS - 1, int(math.log2(SEARCH_BLOCKS)))
    limit = _key_limit(i, tq, q_off, s_valid)
    kidx0 = lax.broadcasted_iota(I32, (kb, tq), 0)
    wit = wi_ref[...].T
    wrow = [wit[hd:hd + 1, :] for hd in range(IDX_HEADS)]

    def score_blocks(jj, carry):
        ks = pl.multiple_of(jj * (SCORE_BLOCKS * kb), SCORE_BLOCKS * kb)
        half = SCORE_BLOCKS // 2
        q_all = jnp.concatenate([qi_ref[:, p * 128:(p + 1) * 128] for p in range(IDX_HEADS // 2)], axis=0)
        for hf in range(2):
            rows = pl.ds(ks + hf * half * kb, half * kb)
            sc_ref[hf, 0] = _dot_nt(kidx_ref[rows, 0:128], q_all)
            sc_ref[hf, 1] = _dot_nt(kidx_ref[rows, 128:256], q_all)
        for hf in range(2):
            for u in range(half):
                blk = slice(u * kb, (u + 1) * kb)
                sc = jnp.zeros((kb, tq), F32)
                for p in range(IDX_HEADS // 2):
                    sc = sc + wrow[2 * p] * jnp.maximum(sc_ref[hf, 0, blk, p * tq:(p + 1) * tq], 0.0)
                    sc = sc + wrow[2 * p + 1] * jnp.maximum(sc_ref[hf, 1, blk, p * tq:(p + 1) * tq], 0.0)
                bits = lax.bitcast_convert_type(sc, I32)
                key = bits ^ ((bits >> 31) & 0x7FFFFFFF)
                j = jj * SCORE_BLOCKS + hf * half + u
                keys_ref[j] = jnp.where(kidx0 < limit - j * kb, key, INT_MIN)
        return carry

    def pad_block(j, carry):
        keys_ref[j] = jnp.full((kb, tq), INT_MIN, I32)
        return carry

    n_scored = lax.shift_right_logical(n_blk + SCORE_BLOCKS - 1, int(math.log2(SCORE_BLOCKS)))
    lax.fori_loop(0, n_scored, score_blocks, 0)
    lax.fori_loop(n_scored * SCORE_BLOCKS, n_sb * SEARCH_BLOCKS, pad_block, 0)

    def count(pred):
        def group(jj, cs):
            out = []
            for u in range(SEARCH_BLOCKS):
                j = jj * SEARCH_BLOCKS + u
                x = jnp.where(pred(keys_ref[j], j * kb), 1.0, 0.0).reshape(kb // 8, 8, tq)
                h = kb // 16
                out.append(cs[u] + (jnp.sum(x[:h], axis=0) + jnp.sum(x[h:], axis=0)))
            return tuple(out)
        cs = lax.fori_loop(0, n_sb, group, tuple(jnp.zeros((8, tq), F32) for _ in range(SEARCH_BLOCKS)))
        return jnp.sum(functools.reduce(lambda a, b: a + b, cs), axis=0, keepdims=True)

    kf = float(topk)

    def bit_step(b, state):
        t, n_ge = state
        cand = t + jnp.left_shift(jnp.int32(1), 31 - b)
        cnt = count(lambda kblk, _: kblk >= cand)
        take = cnt >= kf
        return jnp.where(take, cand, t), jnp.where(take, cnt, n_ge)

    def unsettled(n_ge):
        open_ = jnp.logical_and(n_ge != kf, limit.astype(F32) > kf)
        return jnp.max(jnp.where(open_, 1.0, 0.0)) > 0.0

    def more_bits(state):
        b, _, n_ge = state
        return jnp.logical_and(b < 32, unsettled(n_ge))

    def four_bits(state):
        b, t, n_ge = state
        for u in range(SEARCH_BITS_PER_CHECK):
            t, n_ge = bit_step(b + u, (t, n_ge))
        return b + SEARCH_BITS_PER_CHECK, t, n_ge

    start = (jnp.full((1, tq), INT_MIN, I32), jnp.full((1, tq), float(n_kb * kb), F32))
    head = lax.fori_loop(0, SEARCH_BITS_UNCHECKED, bit_step, start)
    _, thr, _ = lax.while_loop(more_bits, four_bits, (jnp.int32(SEARCH_BITS_UNCHECKED),) + head)
    need = kf - count(lambda kblk, _: kblk > thr)
    n_eq = count(lambda kblk, _: kblk == thr)
    partial = jnp.logical_and(n_eq > need, thr != INT_MIN)

    @pl.when(jnp.max(jnp.where(partial, 1.0, 0.0)) > 0.0)
    def _():
        def idx_step(b, x):
            cand = x + jnp.left_shift(jnp.int32(1), b)
            cnt = count(lambda kblk, k0: jnp.logical_and(kblk == thr, kidx0 < cand - k0))
            return jnp.where(cnt < need, cand, x)

        nbits = max(1, int(s_valid - 1).bit_length())
        cut = lax.fori_loop(0, nbits, lambda b, x: idx_step(nbits - 1 - b, x), jnp.zeros((1, tq), I32))

        def demote(j, carry):
            kblk = keys_ref[j]
            drop = jnp.logical_and(jnp.logical_and(kblk == thr, kidx0 > cut - j * kb), partial)
            keys_ref[j] = jnp.where(drop, kblk - 1, kblk)
            return carry

        lax.fori_loop(0, n_blk, demote, 0)

    thr_sel = jnp.maximum(thr, INT_MIN + 1)

    _flash_init(m_ref, l_ref, acc_ref)
    n_pair = A_HEADS // 2

    def to_bias(j, carry):
        keys_ref[j] = lax.bitcast_convert_type(jnp.where(keys_ref[j] >= thr_sel, 0.0, NEG), I32)
        return carry

    lax.fori_loop(0, n_sb * SEARCH_BLOCKS, to_bias, 0)
    for g in range(A_KV_HEADS):
        def qk(c, j, g=g):
            p = g * (A_GROUP // 2) + c
            qp = jnp.concatenate([q_ref[:, (2 * p) * 128:(2 * p + 1) * 128],
                                  q_ref[:, (2 * p + 1) * 128:(2 * p + 2) * 128]], axis=0)
            return _dot_nt(k_ref[pl.ds(pl.multiple_of(j * kb, kb), kb), g * 128:(g + 1) * 128], qp)

        def prep(c, j, st):
            bias = lax.bitcast_convert_type(keys_ref[j], F32)
            return st + jnp.concatenate([bias, bias], axis=1)

        def vt_of(c, j, g=g):
            return vt_ref[j, g * 128:(g + 1) * 128, :]

        chains = pl.ds(g * (A_GROUP // 2), A_GROUP // 2)
        _staged_flash_t((s_ref.at[chains], p_ref.at[chains], a_ref.at[chains]),
                        (m_ref.at[chains], l_ref.at[chains], acc_ref.at[chains]), 0, n_blk, n_kb, qk, prep, vt_of)
    for p in range(n_pair):
        ot = acc_ref[p] / l_ref[p]
        for a in range(2):
            hd = 2 * p + a
            o_ref[:, hd * 128:(hd + 1) * 128] = ot[:, a * tq:(a + 1) * tq].T.astype(BF16)


def _dsa_attn_t(q, qi, wi, kidx, k, vt, *, tq, kb, q_off, s_valid, topk):
    b, t, _ = q.shape
    s_pad = k.shape[1]
    n_kb = s_pad // kb
    qspec = lambda c: pl.BlockSpec((None, tq, c), lambda bi, i: (bi, i, 0))
    kspec = lambda c: pl.BlockSpec((None, s_pad, c), lambda bi, i: (bi, 0, 0))
    n_pair = A_HEADS // 2
    return pl.pallas_call(
        functools.partial(_dsa_t_body, tq=tq, kb=kb, q_off=q_off, s_valid=s_valid, topk=topk),
        grid=(b, t // tq),
        in_specs=[qspec(1024), qspec(512), qspec(LANES),
                  kspec(256), kspec(256), pl.BlockSpec((None, n_kb, 256, kb), lambda bi, i: (bi, 0, 0, 0))],
        out_specs=qspec(1024),
        out_shape=jax.ShapeDtypeStruct((b, t, 1024), BF16),
        scratch_shapes=[pltpu.VMEM((n_kb, kb, tq), I32),
                        pltpu.VMEM((2, 2, SCORE_BLOCKS // 2 * kb, IDX_HEADS // 2 * tq), F32)]
        + _stage_bufs(n_pair, kb, 2 * tq)
        + [pltpu.VMEM((n_pair, 1, 2 * tq), F32), pltpu.VMEM((n_pair, 1, 2 * tq), F32),
           pltpu.VMEM((n_pair, 128, 2 * tq), F32)],
        compiler_params=_cparams(2),
        name="dsa_attn_t",
    )(q, qi, wi, kidx, k, vt)


def _causal_flash_t(bufs, state, i, n_kb, tq, kb, q_off, s_valid, qk, vt_of):
    n_blk = _n_key_blocks(i, tq, kb, q_off, s_valid)
    n_plain = _n_full_blocks(i, tq, kb, q_off, s_valid) & -2
    limit = _key_limit(i, tq, q_off, s_valid)

    def masked(c, j, st):
        return jnp.where(lax.broadcasted_iota(I32, (kb, tq), 0) < limit - j * kb, st, NEG)

    _staged_flash_t(bufs, state, 0, n_plain, n_kb, qk, lambda c, j, st: st, vt_of, last=False)
    _staged_flash_t(bufs, state, n_plain, n_blk, n_kb, qk, masked, vt_of, first=False)


def _mla_t_body(qt_ref, kn_ref, kr_ref, vt_ref, o_ref, s_ref, p_ref, a_ref, m_ref, l_ref, acc_ref,
                *, tq, kb, q_off, s_valid):
    i = pl.program_id(2)
    _flash_init(m_ref, l_ref, acc_ref)

    def qk(a, j):
        ks = pl.multiple_of(j * kb, kb)
        kcat = jnp.concatenate([kn_ref[pl.ds(ks, kb), a * 128:(a + 1) * 128], kr_ref[pl.ds(ks, kb), :]], axis=1)
        return _dot(kcat, qt_ref[a * 256:(a + 1) * 256, :])

    def vt_of(a, j):
        return vt_ref[j, a * 128:(a + 1) * 128, :]

    _causal_flash_t((s_ref, p_ref, a_ref), (m_ref, l_ref, acc_ref), i, vt_ref.shape[0], tq, kb, q_off, s_valid,
                    qk, vt_of)
    for a in range(2):
        o_ref[:, a * 128:(a + 1) * 128] = (acc_ref[a] / l_ref[a]).T.astype(BF16)


def _mla_attn_t(qt, kn, kr, vt, *, tq, kb, q_off, s_valid):
    b, _, t = qt.shape
    s_pad = kn.shape[1]
    n_kb = s_pad // kb
    return pl.pallas_call(
        functools.partial(_mla_t_body, tq=tq, kb=kb, q_off=q_off, s_valid=s_valid),
        grid=(b, B_HEADS // 2, t // tq),
        in_specs=[pl.BlockSpec((None, 512, tq), lambda bi, h, i: (bi, h, i)),
                  pl.BlockSpec((None, s_pad, 256), lambda bi, h, i: (bi, 0, h)),
                  pl.BlockSpec((None, s_pad, 128), lambda bi, h, i: (bi, 0, 0)),
                  pl.BlockSpec((None, n_kb, 256, kb), lambda bi, h, i: (bi, 0, h, 0))],
        out_specs=pl.BlockSpec((None, tq, 256), lambda bi, h, i: (bi, i, h)),
        out_shape=jax.ShapeDtypeStruct((b, t, B_HEADS * V_DIM), BF16),
        scratch_shapes=_stage_bufs(2, kb, tq) + [pltpu.VMEM((2, 1, tq), F32), pltpu.VMEM((2, 1, tq), F32),
                                                 pltpu.VMEM((2, 128, tq), F32)],
        compiler_params=_cparams(3),
        name="mla_attn_t",
    )(qt, kn, kr, vt)


def _diff_t_body(q_ref, k_ref, vt_ref, lq1_ref, lk1_ref, lq2_ref, lk2_ref, sn_ref, o_ref, s_ref, p_ref, a_ref,
                 m_ref, l_ref, acc_ref, *, tq, kb, q_off, s_valid, lam_init):
    i = pl.program_id(2)
    _flash_init(m_ref, l_ref, acc_ref)

    def qk(p, j):
        ks = pl.multiple_of(j * kb, kb)
        return _dot_nt(k_ref[pl.ds(ks, kb), p * 128:(p + 1) * 128], q_ref[:, p * 128:(p + 1) * 128])

    _causal_flash_t((s_ref, p_ref, a_ref), (m_ref, l_ref, acc_ref), i, vt_ref.shape[0], tq, kb, q_off, s_valid,
                    qk, lambda p, j: vt_ref[j])
    lam = (jnp.exp(jnp.sum(lq1_ref[...] * lk1_ref[...], axis=1, keepdims=True))
           - jnp.exp(jnp.sum(lq2_ref[...] * lk2_ref[...], axis=1, keepdims=True)) + lam_init)
    ot = acc_ref[0] / l_ref[0] - lam * (acc_ref[1] / l_ref[1])
    o = jnp.concatenate([ot[0:128, :].T, ot[128:256, :].T], axis=1)
    o_ref[...] = (_rms(o, sn_ref[...], 2 * C_DIM) * (1.0 - lam_init)).astype(BF16)


def _diff_attn_t(q, k, vt, lq1, lk1, lq2, lk2, sn, *, tq, kb, q_off, s_valid, lam_init):
    b, t, _ = q.shape
    s_pad = k.shape[1]
    n_kb = s_pad // kb
    vec = lambda c: pl.BlockSpec((1, c), lambda bi, h, i: (0, 0))
    return pl.pallas_call(
        functools.partial(_diff_t_body, tq=tq, kb=kb, q_off=q_off, s_valid=s_valid, lam_init=lam_init),
        grid=(b, C_HEADS, t // tq),
        in_specs=[pl.BlockSpec((None, tq, 256), lambda bi, h, i: (bi, i, h)),
                  pl.BlockSpec((None, s_pad, 256), lambda bi, h, i: (bi, 0, h)),
                  pl.BlockSpec((None, n_kb, 256, kb), lambda bi, h, i: (bi, 0, h, 0)),
                  vec(128), vec(128), vec(128), vec(128), vec(256)],
        out_specs=pl.BlockSpec((None, tq, 256), lambda bi, h, i: (bi, i, h)),
        out_shape=jax.ShapeDtypeStruct((b, t, C_HEADS * 2 * C_DIM), BF16),
        scratch_shapes=_stage_bufs(2, kb, tq) + [pltpu.VMEM((2, 1, tq), F32), pltpu.VMEM((2, 1, tq), F32),
                                                 pltpu.VMEM((2, 256, tq), F32)],
        compiler_params=_cparams(3),
        name="diff_attn_t",
    )(q, k, vt, lq1, lk1, lq2, lk2, sn)


def _blocked_t(v, kb):
    b, s, c = v.shape
    return jnp.swapaxes(v.reshape(b, s // kb, kb, c), 2, 3)


def _rope_tables(pos, reps):
    p = pos.astype(F32)[:, None]
    inv64 = jnp.power(ROPE_THETA, -jnp.arange(64, dtype=F32) / 64)
    inv32 = jnp.power(ROPE_THETA, -jnp.arange(32, dtype=F32) / 32)
    c, s = jnp.cos(p * inv64), jnp.sin(p * inv64)
    c3, s3 = jnp.cos(p * inv32), jnp.sin(p * inv32)
    z = jnp.zeros_like(s3)
    tabs = {
        "c128": jnp.concatenate([c, c], axis=1),
        "s128": jnp.concatenate([-s, s], axis=1),
        "c64": jnp.concatenate([c3, c3, c3, c3], axis=1),
        "sa64": jnp.concatenate([-s3, z, -s3, z], axis=1),
        "sb64": jnp.concatenate([z, s3, z, s3], axis=1),
    }
    return {k: jnp.tile(v, (reps, 1)) for k, v in tabs.items()}


def _pad_cols(w, n):
    return jnp.pad(w, ((0, 0), (0, n - w.shape[1])))


def _pad_lanes(g, n=128):
    g = g.reshape(1, -1)
    return jnp.pad(g, ((0, 0), (0, n - g.shape[1])))


def _prep_weights(W):
    P = {}
    for nm in ("ffn1", "ffn2"):
        P[nm] = [(W[nm + "_norm"][i].reshape(1, -1), W[nm + "_wg"][i].astype(BF16), W[nm + "_wu"][i].astype(BF16),
                  W[nm + "_wd"][i].astype(BF16)) for i in range(DEPTH)]
    P["mix_norm"] = [W["mix_norm"][i].reshape(1, -1) for i in range(DEPTH)]
    P["a"] = [dict(w=_pad_cols(W["a_w_in"][j], A_IN_PAD).astype(BF16), qn=W["a_q_norm"][j].reshape(1, -1),
                   kn=W["a_k_norm"][j].reshape(1, -1), ikn=_pad_lanes(W["a_idx_k_norm"][j]),
                   wo=W["a_w_out"][j].astype(BF16)) for j in range(W["a_w_in"].shape[0])]
    P["b"] = []
    for j in range(W["b_w_in"].shape[0]):
        wuq = W["b_w_uq"][j].reshape(Q_LORA, B_HEADS, NOPE_DIM + ROPE_DIM)
        wuq_rope = jnp.pad(wuq[:, :, NOPE_DIM:], ((0, 0), (0, 0), (0, 128 - ROPE_DIM)))
        wuq = jnp.concatenate([wuq[:, :, :NOPE_DIM], wuq_rope], axis=2).reshape(Q_LORA, -1)
        P["b"].append(dict(
            w=_pad_cols(W["b_w_in"][j], B_IN_PAD).astype(BF16), qan=W["b_q_a_norm"][j].reshape(1, -1),
            kvan=W["b_kv_a_norm"][j].reshape(1, -1), krn=_pad_lanes(W["b_k_rope_norm"][j]), wuq=wuq.astype(BF16),
            qnn=W["b_q_nope_norm"][j].reshape(1, -1), qrn=_pad_lanes(W["b_q_rope_norm"][j]),
            wukv=W["b_w_ukv"][j].astype(BF16), knn=W["b_k_nope_norm"][j].reshape(1, -1),
            wo=W["b_w_out"][j].astype(BF16)))
    P["c"] = [dict(w=W["c_w_in"][j].astype(BF16), qn=W["c_q_norm"][j].reshape(1, -1),
                   kn=W["c_k_norm"][j].reshape(1, -1), lq1=W["c_lambda_q1"][j].reshape(1, -1),
                   lk1=W["c_lambda_k1"][j].reshape(1, -1), lq2=W["c_lambda_q2"][j].reshape(1, -1),
                   lk2=W["c_lambda_k2"][j].reshape(1, -1), sn=W["c_sub_norm"][j].reshape(1, -1),
                   wo=W["c_w_out"][j].astype(BF16)) for j in range(W["c_w_in"].shape[0])]
    return P


def _with_past(past, new, s_pad):
    x = new if past is None else jnp.concatenate([past.astype(new.dtype), new], axis=1)
    return x if x.shape[1] == s_pad else jnp.pad(x, ((0, 0), (0, s_pad - x.shape[1]), (0, 0)))


def _trunk(x, offset, past, P, cfg):
    b, t, _ = x.shape
    n = b * t
    tm, tq_a, tq, kb, key_major = cfg["tm"], cfg["tq_a"], cfg["tq"], cfg["kb"], cfg["key_major"]
    p_len = 0 if past is None else past["a_k"].shape[2]
    s_valid = p_len + t
    s_pad = -(-s_valid // kb) * kb
    tabs = _rope_tables(offset + jnp.arange(t, dtype=I32), tm // t if tm > t else 1)
    att = dict(kb=kb, q_off=offset, s_valid=s_valid)
    rows = {k: [] for k in ("a_k", "a_v", "a_ik", "b_ckv", "b_kr", "c_k", "c_v")}
    x = x.reshape(n, D_MODEL)
    r3 = lambda a: a.reshape(b, t, a.shape[-1])
    t3 = lambda a: jnp.swapaxes(r3(a), 1, 2)
    pj = lambda nm, j: None if past is None else past[nm][j].reshape(b, p_len, -1)
    for i in range(DEPTH):
        x = _ffn(x, *P["ffn1"][i], cfg["tm_ffn"])
        kind, j = i % N_MIXERS, i // N_MIXERS
        g = P["mix_norm"][i]
        if kind == 0:
            pa = P["a"][j]
            q, k32, v32, kbf, vbf, qi, ki32, kidx, wi = _a_proj(x, g, pa["w"], pa["qn"], pa["kn"], pa["ikn"], tabs, tm)
            rows["a_k"].append(k32.reshape(b, t, A_KV_HEADS, A_HEAD_DIM))
            rows["a_v"].append(v32.reshape(b, t, A_KV_HEADS, A_HEAD_DIM))
            rows["a_ik"].append(ki32.reshape(b, t, IDX_DIM))
            pik = pj("a_ik", j)
            if pik is not None:
                z = jnp.zeros_like(pik)
                pik = jnp.concatenate([pik, z, z, pik], axis=-1)
            kidx_all = _with_past(pik, r3(kidx), s_pad)
            k_all = _with_past(pj("a_k", j), r3(kbf), s_pad)
            v_all = _with_past(pj("a_v", j), r3(vbf), s_pad)
            topk = min(TOPK_MAX, s_valid // 4)
            if key_major:
                o = _dsa_attn_t(r3(q), r3(qi), r3(wi), kidx_all, k_all, _blocked_t(v_all, kb),
                                tq=tq_a, topk=topk, **att)
            else:
                o = _dsa_attn(r3(q), r3(qi), r3(wi), kidx_all, k_all, v_all, tq=tq_a, topk=topk, **att)
        elif kind == 1:
            pb = P["b"][j]
            q, ckv, kr32, krbf = _b_proj(x, g, pb["w"], pb["qan"], pb["kvan"], pb["krn"], pb["wuq"], pb["qnn"],
                                         pb["qrn"], tabs, tm)
            rows["b_ckv"].append(ckv.reshape(b, t, KV_LORA))
            rows["b_kr"].append(kr32.reshape(b, t, ROPE_DIM))
            ckv_all = _with_past(pj("b_ckv", j), r3(ckv), s_pad)
            pkr = pj("b_kr", j)
            if pkr is not None:
                pkr = jnp.concatenate([pkr, jnp.zeros_like(pkr)], axis=-1)
            kn, v = _kv_up(ckv_all.reshape(b * s_pad, KV_LORA), pb["wukv"], pb["knn"], math.gcd(b * s_pad, 512))
            kn, v, kr_all = kn.reshape(b, s_pad, -1), v.reshape(b, s_pad, -1), _with_past(pkr, r3(krbf), s_pad)
            if key_major:
                o = _mla_attn_t(t3(q), kn, kr_all, _blocked_t(v, kb), tq=cfg["tq_mla"], **att)
            else:
                o = _mla_attn(r3(q), kn, kr_all, v, tq=tq, **att)
        else:
            pc = P["c"][j]
            q, k32, v32, kbf, vbf = _c_proj(x, g, pc["w"], pc["qn"], pc["kn"], tabs, tm)
            rows["c_k"].append(k32.reshape(b, t, C_HEADS, 2, C_DIM))
            rows["c_v"].append(v32.reshape(b, t, C_HEADS, 2 * C_DIM))
            k_all = _with_past(pj("c_k", j), r3(kbf), s_pad)
            v_all = _with_past(pj("c_v", j), r3(vbf), s_pad)
            lam = (pc["lq1"], pc["lk1"], pc["lq2"], pc["lk2"], pc["sn"])
            lam_init = 0.8 - 0.6 * math.exp(-0.3 * i)
            if key_major:
                att_c = dict(att, kb=cfg["kb_diff"])
                o = _diff_attn_t(r3(q), k_all, _blocked_t(v_all, cfg["kb_diff"]), *lam, tq=tq, lam_init=lam_init,
                                 **att_c)
            else:
                o = _diff_attn(r3(q), k_all, v_all, *lam, tq=tq, lam_init=lam_init, **att)
        wo = (P["a"], P["b"], P["c"])[kind][j]["wo"]
        x = _ffn(x, *P["ffn2"][i], cfg["tm_ffn"], attn=o.reshape(n, -1), wo=wo)
    order = ("a_k", "a_v", "a_ik", "b_ckv", "b_kr", "c_k", "c_v")
    stack = lambda rs: rs[0][None] if len(rs) == 1 else jnp.stack(rs)
    return x.reshape(b, t, D_MODEL), tuple(stack(rows[k]) for k in order)


PROMPT_CFG = dict(tm=512, tm_ffn=1024, tq_a=128, tq=256, tq_mla=512, kb=256, kb_diff=512, key_major=True)
SAMPLE_CFG = dict(tm=128, tm_ffn=128, tq_a=16, tq=16, kb=1280, key_major=False)


@jax.jit
def _forward(x_prompt, x_sample, past, W):
    P = _prep_weights(W)
    y_p, rows_p = _trunk(x_prompt, 0, None, P, PROMPT_CFG)
    y_s, rows_s = _trunk(x_sample, past["a_k"].shape[2], past, P, SAMPLE_CFG)
    return (y_p, y_s) + rows_p + rows_s


def kernel(x_prompt, x_sample, cache_a_k, cache_a_v, cache_a_idx_k, cache_b_ckv, cache_b_krope, cache_c_k, cache_c_v, ffn1_norm, ffn1_wg, ffn1_wu, ffn1_wd, mix_norm, ffn2_norm, ffn2_wg, ffn2_wu, ffn2_wd, a_w_in, a_q_norm, a_k_norm, a_idx_k_norm, a_w_out, b_w_in, b_q_a_norm, b_kv_a_norm, b_w_uq, b_w_ukv, b_q_nope_norm, b_q_rope_norm, b_k_nope_norm, b_k_rope_norm, b_w_out, c_w_in, c_q_norm, c_k_norm, c_lambda_q1, c_lambda_k1, c_lambda_q2, c_lambda_k2, c_sub_norm, c_w_out):
    W = dict(ffn1_norm=ffn1_norm, ffn1_wg=ffn1_wg, ffn1_wu=ffn1_wu, ffn1_wd=ffn1_wd, mix_norm=mix_norm,
             ffn2_norm=ffn2_norm, ffn2_wg=ffn2_wg, ffn2_wu=ffn2_wu, ffn2_wd=ffn2_wd,
             a_w_in=a_w_in, a_q_norm=a_q_norm, a_k_norm=a_k_norm, a_idx_k_norm=a_idx_k_norm, a_w_out=a_w_out,
             b_w_in=b_w_in, b_q_a_norm=b_q_a_norm, b_kv_a_norm=b_kv_a_norm, b_w_uq=b_w_uq, b_w_ukv=b_w_ukv,
             b_q_nope_norm=b_q_nope_norm, b_q_rope_norm=b_q_rope_norm, b_k_nope_norm=b_k_nope_norm,
             b_k_rope_norm=b_k_rope_norm, b_w_out=b_w_out,
             c_w_in=c_w_in, c_q_norm=c_q_norm, c_k_norm=c_k_norm, c_lambda_q1=c_lambda_q1,
             c_lambda_k1=c_lambda_k1, c_lambda_q2=c_lambda_q2, c_lambda_k2=c_lambda_k2, c_sub_norm=c_sub_norm,
             c_w_out=c_w_out)
    past = dict(a_k=cache_a_k, a_v=cache_a_v, a_ik=cache_a_idx_k, b_ckv=cache_b_ckv, b_kr=cache_b_krope,
                c_k=cache_c_k, c_v=cache_c_v)
    return _forward(x_prompt, x_sample, past, W)
```

```python
import functools
import math

import jax
import jax.numpy as jnp
from jax import lax
from jax.experimental import pallas as pl
from jax.experimental.pallas import tpu as pltpu

F32 = jnp.float32
BF16 = jnp.bfloat16
I32 = jnp.int32

D_MODEL = 1024
DEPTH = 4
CHUNK_SHIFT = 6
N_MIXERS = 3
ROPE_THETA = 10000.0
EPS = 1e-6
D_FF = 2816

A_HEADS = 8
A_KV_HEADS = 2
A_GROUP = A_HEADS // A_KV_HEADS
A_HEAD_DIM = 128
IDX_HEADS = 8
IDX_DIM = 64
TOPK_MAX = 256
LOG2E = math.log2(math.e)
A_SCALE = A_HEAD_DIM ** -0.5 * LOG2E
IDX_W_SCALE = (IDX_HEADS * IDX_DIM) ** -0.5
A_IN = 2120
A_IN_PAD = 2176

B_HEADS = 8
Q_LORA = 384
KV_LORA = 256
NOPE_DIM = 128
ROPE_DIM = 64
V_DIM = 128
B_SCALE = (NOPE_DIM + ROPE_DIM) ** -0.5 * LOG2E
B_IN = 704
B_IN_PAD = 768

C_HEADS = 4
C_DIM = 128
C_SCALE = C_DIM ** -0.5 * LOG2E

LANES = 128
NEG = -1e30
INT_MIN = -(2 ** 31)
VMEM_LIMIT = 56 * 1024 * 1024


def _cparams(n_axes):
    return pltpu.CompilerParams(dimension_semantics=("arbitrary",) * n_axes, vmem_limit_bytes=VMEM_LIMIT)


def _dot(a, b):
    return jnp.dot(a, b, preferred_element_type=F32)


def _dot_nt(a, b):
    return lax.dot_general(a, b, (((1,), (1,)), ((), ())), preferred_element_type=F32)


def _rms(x, g, n):
    ms = jnp.sum(x * x, axis=-1, keepdims=True) * (1.0 / n)
    return x * lax.rsqrt(ms + EPS) * g


def _rope128(x, c, s):
    return x * c + pltpu.roll(x, 64, 1) * s


def _rope64(x, c, sa, sb):
    return x * c + pltpu.roll(x, 96, 1) * sa + pltpu.roll(x, 32, 1) * sb


FFN_CHUNK = 256


def _ffn_body(*refs, fc, mixed):
    if mixed:
        attn_ref, wo_ref, x_ref, g_ref, wg_ref, wu_ref, wd_ref, o_ref = refs
        x = x_ref[...] + _dot(attn_ref[...], wo_ref[...])
    else:
        x_ref, g_ref, wg_ref, wu_ref, wd_ref, o_ref = refs
        x = x_ref[...]
    h = _rms(x, g_ref[...], D_MODEL).astype(BF16)
    y = None
    for c in range(D_FF // fc):
        a = _dot(h, wg_ref[:, c * fc:(c + 1) * fc])
        u = _dot(h, wu_ref[:, c * fc:(c + 1) * fc])
        act = (a * jax.nn.sigmoid(a) * u).astype(BF16)
        part = _dot(act, wd_ref[c * fc:(c + 1) * fc, :])
        y = part if y is None else y + part
    o_ref[...] = x + 0.5 * y


def _const_spec(shape):
    nd = len(shape)
    return pl.BlockSpec(shape, lambda *_: (0,) * nd, pipeline_mode=pl.Buffered(1))


def _row_spec(tm, n):
    return pl.BlockSpec((tm, n), lambda i: (i, 0))


def _ffn(x, g, wg, wu, wd, tm, attn=None, wo=None):
    n = x.shape[0]
    mixed = attn is not None
    pre_specs = [_row_spec(tm, attn.shape[1]), _const_spec(wo.shape)] if mixed else []
    return pl.pallas_call(
        functools.partial(_ffn_body, fc=FFN_CHUNK, mixed=mixed),
        grid=(n // tm,),
        in_specs=pre_specs + [_row_spec(tm, D_MODEL), _const_spec((1, D_MODEL)), _const_spec((D_MODEL, D_FF)),
                              _const_spec((D_MODEL, D_FF)), _const_spec((D_FF, D_MODEL))],
        out_specs=_row_spec(tm, D_MODEL),
        out_shape=jax.ShapeDtypeStruct((n, D_MODEL), F32),
        compiler_params=_cparams(1),
        name="ffn_mixed" if mixed else "ffn",
    )(*((attn, wo) if mixed else ()), x, g, wg, wu, wd)


PROJ_PARTS = 2

def _a_proj_body(x_ref, g_ref, w_ref, qn_ref, kn_ref, ikn_ref, c128_ref, s128_ref, c64_ref, sa64_ref, sb64_ref,
                 q_ref, k32_ref, v32_ref, kbf_ref, vbf_ref, qi_ref, ki32_ref, kidx_ref, wi_ref, y_ref):
    hm = x_ref.shape[0] // PROJ_PARTS
    for part in range(PROJ_PARTS):
        rows = slice(part * hm, (part + 1) * hm)
        y_ref[part] = _dot(_rms(x_ref[rows, :], g_ref[...], D_MODEL).astype(BF16), w_ref[...])
    for part in range(PROJ_PARTS):
        rows = slice(part * hm, (part + 1) * hm)
        c128, s128 = c128_ref[rows, :], s128_ref[rows, :]
        c64, sa64, sb64 = c64_ref[rows, :], sa64_ref[rows, :], sb64_ref[rows, :]
        for hd in range(A_HEADS):
            sl = slice(hd * 128, (hd + 1) * 128)
            qh = _rope128(_rms(y_ref[part, :, sl], qn_ref[...], A_HEAD_DIM), c128, s128)
            q_ref[rows, sl] = (qh * A_SCALE).astype(BF16)
        for hd in range(A_KV_HEADS):
            sl = slice(hd * 128, (hd + 1) * 128)
            kh = _rope128(_rms(y_ref[part, :, 1024 + hd * 128:1024 + (hd + 1) * 128], kn_ref[...], A_HEAD_DIM),
                          c128, s128)
            k32_ref[rows, sl] = kh
            kbf_ref[rows, sl] = kh.astype(BF16)
        v = y_ref[part, :, 1280:1536]
        v32_ref[rows, :] = v
        vbf_ref[rows, :] = v.astype(BF16)
        for p in range(IDX_HEADS // 2):
            sl = slice(p * 128, (p + 1) * 128)
            qi_ref[rows, sl] = _rope64(y_ref[part, :, 1536 + p * 128:1536 + (p + 1) * 128],
                                       c64, sa64, sb64).astype(BF16)
        tail = y_ref[part, :, 2048:2176]
        lane = lax.broadcasted_iota(I32, tail.shape, 1)
        low = lane < IDX_DIM
        kin = jnp.where(low, tail, 0.0)
        ki = _rope64(_rms(kin, ikn_ref[...], IDX_DIM), c64, sa64, sb64)
        ki = jnp.where(low, ki, 0.0)
        ki32_ref[rows, :] = ki[:, :IDX_DIM]
        kidx_ref[rows, 0:128] = ki.astype(BF16)
        kidx_ref[rows, 128:256] = pltpu.roll(ki, 64, 1).astype(BF16)
        wi_ref[rows, :] = pltpu.roll(tail, 64, 1) * IDX_W_SCALE


def _tab_spec(tm, t):
    nt = t // tm
    return pl.BlockSpec((tm, LANES), lambda i: (i % nt, 0))


def _a_proj(x, g, w, qn, kn, ikn, tabs, tm):
    n = x.shape[0]
    t = tabs["c128"].shape[0]
    outs = [(1024, BF16), (256, F32), (256, F32), (256, BF16), (256, BF16), (512, BF16), (IDX_DIM, F32),
            (256, BF16), (LANES, F32)]
    return pl.pallas_call(
        _a_proj_body,
        grid=(n // tm,),
        in_specs=[_row_spec(tm, D_MODEL), _const_spec((1, D_MODEL)), _const_spec((D_MODEL, A_IN_PAD)),
                  _const_spec((1, 128)), _const_spec((1, 128)), _const_spec((1, 128))]
        + [_tab_spec(tm, t)] * 5,
        out_specs=[_row_spec(tm, c) for c, _ in outs],
        out_shape=[jax.ShapeDtypeStruct((n, c), dt) for c, dt in outs],
        scratch_shapes=[pltpu.VMEM((PROJ_PARTS, tm // PROJ_PARTS, A_IN_PAD), F32)],
        compiler_params=_cparams(1),
        name="a_proj",
    )(x, g, w, qn, kn, ikn, tabs["c128"], tabs["s128"], tabs["c64"], tabs["sa64"], tabs["sb64"])


def _b_proj_body(x_ref, g_ref, w_ref, qan_ref, kvan_ref, krn_ref, wuq_ref, qnn_ref, qrn_ref,
                 c64_ref, sa64_ref, sb64_ref, q_ref, ckv_ref, kr32_ref, krbf_ref, qq_ref):
    hm = x_ref.shape[0] // PROJ_PARTS
    for part in range(PROJ_PARTS):
        rows = slice(part * hm, (part + 1) * hm)
        h = _rms(x_ref[rows, :], g_ref[...], D_MODEL).astype(BF16)
        y = _dot(h, w_ref[...])
        ckv_ref[rows, :] = _rms(y[:, 384:640], kvan_ref[...], KV_LORA)
        kr = _rope64(_rms(y[:, 640:768], krn_ref[...], ROPE_DIM), c64_ref[rows, :], sa64_ref[rows, :],
                     sb64_ref[rows, :])
        kr32_ref[rows, :] = kr[:, :ROPE_DIM]
        krbf_ref[rows, :] = kr.astype(BF16)
        qq_ref[part] = _dot(_rms(y[:, 0:384], qan_ref[...], Q_LORA).astype(BF16), wuq_ref[...])
    for part in range(PROJ_PARTS):
        rows = slice(part * hm, (part + 1) * hm)
        c64, sa64, sb64 = c64_ref[rows, :], sa64_ref[rows, :], sb64_ref[rows, :]
        for hd in range(B_HEADS):
            qn = _rms(qq_ref[part, :, hd * 256:hd * 256 + 128], qnn_ref[...], NOPE_DIM)
            qr = _rope64(_rms(qq_ref[part, :, hd * 256 + 128:(hd + 1) * 256], qrn_ref[...], ROPE_DIM),
                         c64, sa64, sb64)
            q_ref[rows, hd * 256:hd * 256 + 128] = (qn * B_SCALE).astype(BF16)
            q_ref[rows, hd * 256 + 128:(hd + 1) * 256] = (qr * B_SCALE).astype(BF16)


def _b_proj(x, g, w, qan, kvan, krn, wuq, qnn, qrn, tabs, tm):
    n = x.shape[0]
    t = tabs["c64"].shape[0]
    outs = [(B_HEADS * 256, BF16), (KV_LORA, F32), (ROPE_DIM, F32), (LANES, BF16)]
    return pl.pallas_call(
        _b_proj_body,
        grid=(n // tm,),
        in_specs=[_row_spec(tm, D_MODEL), _const_spec((1, D_MODEL)), _const_spec((D_MODEL, B_IN_PAD)),
                  _const_spec((1, Q_LORA)), _const_spec((1, KV_LORA)), _const_spec((1, 128)),
                  _const_spec((Q_LORA, 2048)), _const_spec((1, 128)), _const_spec((1, 128))]
        + [_tab_spec(tm, t)] * 3,
        out_specs=[_row_spec(tm, c) for c, _ in outs],
        out_shape=[jax.ShapeDtypeStruct((n, c), dt) for c, dt in outs],
        scratch_shapes=[pltpu.VMEM((PROJ_PARTS, tm // PROJ_PARTS, 2048), F32)],
        compiler_params=_cparams(1),
        name="b_proj",
    )(x, g, w, qan, kvan, krn, wuq, qnn, qrn, tabs["c64"], tabs["sa64"], tabs["sb64"])


def _kv_up_body(ckv_ref, w_ref, knn_ref, kn_ref, v_ref):
    y = _dot(ckv_ref[...].astype(BF16), w_ref[...])
    for hd in range(B_HEADS):
        kn_ref[:, hd * 128:(hd + 1) * 128] = _rms(y[:, hd * 256:hd * 256 + 128], knn_ref[...], NOPE_DIM).astype(BF16)
        v_ref[:, hd * 128:(hd + 1) * 128] = y[:, hd * 256 + 128:(hd + 1) * 256].astype(BF16)


def _kv_up(ckv, w, knn, tm):
    n = ckv.shape[0]
    return pl.pallas_call(
        _kv_up_body,
        grid=(n // tm,),
        in_specs=[_row_spec(tm, KV_LORA), _const_spec((KV_LORA, 2048)), _const_spec((1, 128))],
        out_specs=[_row_spec(tm, 1024), _row_spec(tm, 1024)],
        out_shape=[jax.ShapeDtypeStruct((n, 1024), BF16)] * 2,
        compiler_params=_cparams(1),
        name="kv_up",
    )(ckv, w, knn)


def _c_proj_body(x_ref, g_ref, w_ref, qn_ref, kn_ref, c128_ref, s128_ref,
                 q_ref, k32_ref, v32_ref, kbf_ref, vbf_ref, y_ref):
    hm = x_ref.shape[0] // PROJ_PARTS
    for part in range(PROJ_PARTS):
        rows = slice(part * hm, (part + 1) * hm)
        y_ref[part] = _dot(_rms(x_ref[rows, :], g_ref[...], D_MODEL).astype(BF16), w_ref[...])
    for part in range(PROJ_PARTS):
        rows = slice(part * hm, (part + 1) * hm)
        c128, s128 = c128_ref[rows, :], s128_ref[rows, :]
        for hd in range(2 * C_HEADS):
            sl = slice(hd * 128, (hd + 1) * 128)
            qh = _rope128(_rms(y_ref[part, :, sl], qn_ref[...], C_DIM), c128, s128)
            q_ref[rows, sl] = (qh * C_SCALE).astype(BF16)
            kh = _rope128(_rms(y_ref[part, :, 1024 + hd * 128:1024 + (hd + 1) * 128], kn_ref[...], C_DIM),
                          c128, s128)
            k32_ref[rows, sl] = kh
            kbf_ref[rows, sl] = kh.astype(BF16)
        v = y_ref[part, :, 2048:3072]
        v32_ref[rows, :] = v
        vbf_ref[rows, :] = v.astype(BF16)


def _c_proj(x, g, w, qn, kn, tabs, tm):
    n = x.shape[0]
    t = tabs["c128"].shape[0]
    outs = [(1024, BF16), (1024, F32), (1024, F32), (1024, BF16), (1024, BF16)]
    return pl.pallas_call(
        _c_proj_body,
        grid=(n // tm,),
        in_specs=[_row_spec(tm, D_MODEL), _const_spec((1, D_MODEL)), _const_spec((D_MODEL, 3072)),
                  _const_spec((1, 128)), _const_spec((1, 128))] + [_tab_spec(tm, t)] * 2,
        out_specs=[_row_spec(tm, c) for c, _ in outs],
        out_shape=[jax.ShapeDtypeStruct((n, c), dt) for c, dt in outs],
        scratch_shapes=[pltpu.VMEM((PROJ_PARTS, tm // PROJ_PARTS, 3072), F32)],
        compiler_params=_cparams(1),
        name="c_proj",
    )(x, g, w, qn, kn, tabs["c128"], tabs["s128"])


def _n_key_blocks(i, tq, kb, q_off, s_valid):
    last_chunk = lax.shift_right_logical(q_off + (i + 1) * tq - 1, CHUNK_SHIFT)
    kend = jnp.minimum((last_chunk + 1) << CHUNK_SHIFT, s_valid)
    return lax.div(kend + kb - 1, jnp.int32(kb))


def _q_chunk(i, tq, q_off):
    row = lax.broadcasted_iota(I32, (tq, 1), 0)
    return lax.shift_right_logical(q_off + i * tq + row, CHUNK_SHIFT)


def _k_chunk(ks, kb, s_valid):
    kpos = ks + lax.broadcasted_iota(I32, (1, kb), 1)
    return jnp.where(kpos < s_valid, lax.shift_right_logical(kpos, CHUNK_SHIFT), 2 ** 30)


def _flash_step(s, v, m_ref, l_ref, acc_ref):
    m_prev = m_ref[...]
    m_new = jnp.maximum(m_prev, jnp.max(s, axis=1, keepdims=True))
    alpha = jnp.exp2(m_prev - m_new)
    p = jnp.exp2(s - m_new)
    l_ref[...] = alpha * l_ref[...] + jnp.sum(p, axis=1, keepdims=True)
    acc_ref[...] = alpha * acc_ref[...] + _dot(p.astype(BF16), v)
    m_ref[...] = m_new


def _key_limit(i, tq, q_off, s_valid):
    qpos = q_off + i * tq + lax.broadcasted_iota(I32, (1, tq), 1)
    return jnp.minimum((lax.shift_right_logical(qpos, CHUNK_SHIFT) + 1) << CHUNK_SHIFT, s_valid)


def _n_full_blocks(i, tq, kb, q_off, s_valid):
    first_chunk = lax.shift_right_logical(q_off + i * tq, CHUNK_SHIFT)
    kend = jnp.minimum((first_chunk + 1) << CHUNK_SHIFT, s_valid)
    return lax.shift_right_logical(kend, int(math.log2(kb)))


def _flash_init(m_ref, l_ref, acc_ref):
    m_ref[...] = jnp.full(m_ref.shape, NEG, F32)
    l_ref[...] = jnp.zeros(l_ref.shape, F32)
    acc_ref[...] = jnp.zeros(acc_ref.shape, F32)


def _dsa_body(q_ref, qi_ref, wi_ref, kidx_ref, k_ref, v_ref, o_ref, keys_ref, m_ref, l_ref, acc_ref,
              *, tq, kb, q_off, s_valid, topk):
    i = pl.program_id(1)
    n_blk = _n_key_blocks(i, tq, kb, q_off, s_valid)
    qc = _q_chunk(i, tq, q_off)
    wi = wi_ref[...]
    wcol = [wi[:, hd:hd + 1] for hd in range(IDX_HEADS)]

    def score_block(j, carry):
        ks = pl.multiple_of(j * kb, kb)
        k_lo = kidx_ref[pl.ds(ks, kb), 0:128]
        k_hi = kidx_ref[pl.ds(ks, kb), 128:256]
        sc = jnp.zeros((tq, kb), F32)
        for p in range(IDX_HEADS // 2):
            qp = qi_ref[:, p * 128:(p + 1) * 128]
            sc = sc + wcol[2 * p] * jnp.maximum(_dot_nt(qp, k_lo), 0.0)
            sc = sc + wcol[2 * p + 1] * jnp.maximum(_dot_nt(qp, k_hi), 0.0)
        bits = lax.bitcast_convert_type(sc, I32)
        key = bits ^ ((bits >> 31) & 0x7FFFFFFF)
        keys_ref[j] = jnp.where(_k_chunk(ks, kb, s_valid) <= qc, key, INT_MIN)
        return carry

    lax.fori_loop(0, n_blk, score_block, 0)

    def count(pred):
        def blk(j, c):
            x = jnp.where(pred(keys_ref[j], j * kb), 1.0, 0.0)
            part = x[:, 0:LANES]
            for g in range(1, kb // LANES):
                part = part + x[:, g * LANES:(g + 1) * LANES]
            return c + part
        c = lax.fori_loop(0, n_blk, blk, jnp.zeros((tq, LANES), F32))
        return jnp.sum(c, axis=1, keepdims=True)

    kf = float(topk)

    def bit_step(b, t):
        cand = t + jnp.left_shift(jnp.int32(1), 31 - b)
        cnt = count(lambda kblk, _: kblk >= cand)
        return jnp.where(cnt >= kf, cand, t)

    thr = lax.fori_loop(0, 32, bit_step, jnp.full((tq, 1), INT_MIN, I32))
    need = kf - count(lambda kblk, _: kblk > thr)
    n_eq = count(lambda kblk, _: kblk == thr)
    partial = jnp.logical_and(n_eq > need, thr != INT_MIN)

    @pl.when(jnp.max(jnp.where(partial, 1.0, 0.0)) > 0.0)
    def _():
        lane = lax.broadcasted_iota(I32, (1, kb), 1)

        def idx_step(b, x):
            cand = x + jnp.left_shift(jnp.int32(1), b)
            cnt = count(lambda kblk, k0: jnp.logical_and(kblk == thr, k0 + lane < cand))
            return jnp.where(cnt < need, cand, x)

        nbits = max(1, int(s_valid - 1).bit_length())
        cut = lax.fori_loop(0, nbits, lambda b, x: idx_step(nbits - 1 - b, x), jnp.zeros((tq, 1), I32))

        def demote(j, carry):
            kblk = keys_ref[j]
            drop = jnp.logical_and(jnp.logical_and(kblk == thr, j * kb + lane > cut), partial)
            keys_ref[j] = jnp.where(drop, kblk - 1, kblk)
            return carry

        lax.fori_loop(0, n_blk, demote, 0)

    thr_sel = jnp.maximum(thr, INT_MIN + 1)

    _flash_init(m_ref, l_ref, acc_ref)

    def attend(j, carry):
        ks = pl.multiple_of(j * kb, kb)
        bias = jnp.where(keys_ref[j] >= thr_sel, 0.0, NEG)
        for g in range(A_KV_HEADS):
            qg = jnp.concatenate([q_ref[:, (g * A_GROUP + a) * 128:(g * A_GROUP + a + 1) * 128]
                                  for a in range(A_GROUP)], axis=0)
            s = _dot_nt(qg, k_ref[pl.ds(ks, kb), g * 128:(g + 1) * 128])
            s = (s.reshape(A_GROUP, tq, kb) + bias[None]).reshape(A_GROUP * tq, kb)
            _flash_step(s, v_ref[pl.ds(ks, kb), g * 128:(g + 1) * 128], m_ref.at[g], l_ref.at[g], acc_ref.at[g])
        return carry

    lax.fori_loop(0, n_blk, attend, 0)
    for g in range(A_KV_HEADS):
        o = acc_ref[g] / l_ref[g]
        for a in range(A_GROUP):
            hd = g * A_GROUP + a
            o_ref[:, hd * 128:(hd + 1) * 128] = o[a * tq:(a + 1) * tq].astype(BF16)


def _dsa_attn(q, qi, wi, kidx, k, v, *, tq, kb, q_off, s_valid, topk):
    b, t, _ = q.shape
    s_pad = k.shape[1]
    qspec = lambda c: pl.BlockSpec((None, tq, c), lambda bi, i: (bi, i, 0))
    kspec = lambda c: pl.BlockSpec((None, s_pad, c), lambda bi, i: (bi, 0, 0))
    rows = A_GROUP * tq
    return pl.pallas_call(
        functools.partial(_dsa_body, tq=tq, kb=kb, q_off=q_off, s_valid=s_valid, topk=topk),
        grid=(b, t // tq),
        in_specs=[qspec(1024), qspec(512), qspec(LANES), kspec(256), kspec(256), kspec(256)],
        out_specs=qspec(1024),
        out_shape=jax.ShapeDtypeStruct((b, t, 1024), BF16),
        scratch_shapes=[pltpu.VMEM((s_pad // kb, tq, kb), I32), pltpu.VMEM((A_KV_HEADS, rows, 1), F32),
                        pltpu.VMEM((A_KV_HEADS, rows, 1), F32), pltpu.VMEM((A_KV_HEADS, rows, 128), F32)],
        compiler_params=_cparams(2),
        name="dsa_attn",
    )(q, qi, wi, kidx, k, v)


def _mla_body(q_ref, kn_ref, kr_ref, v_ref, o_ref, m_ref, l_ref, acc_ref, *, tq, kb, q_off, s_valid):
    i = pl.program_id(2)
    n_blk = _n_key_blocks(i, tq, kb, q_off, s_valid)
    qc = _q_chunk(i, tq, q_off)
    _flash_init(m_ref, l_ref, acc_ref)

    def attend(j, carry):
        ks = pl.multiple_of(j * kb, kb)
        kcat = jnp.concatenate([kn_ref[pl.ds(ks, kb), :], kr_ref[pl.ds(ks, kb), :]], axis=1)
        s = _dot_nt(q_ref[...], kcat)
        s = jnp.where(_k_chunk(ks, kb, s_valid) <= qc, s, NEG)
        _flash_step(s, v_ref[pl.ds(ks, kb), :], m_ref, l_ref, acc_ref)
        return carry

    lax.fori_loop(0, n_blk, attend, 0)
    o_ref[...] = (acc_ref[...] / l_ref[...]).astype(BF16)


def _mla_attn(q, kn, kr, v, *, tq, kb, q_off, s_valid):
    b, t, _ = q.shape
    s_pad = kn.shape[1]
    return pl.pallas_call(
        functools.partial(_mla_body, tq=tq, kb=kb, q_off=q_off, s_valid=s_valid),
        grid=(b, B_HEADS, t // tq),
        in_specs=[pl.BlockSpec((None, tq, 256), lambda bi, h, i: (bi, i, h)),
                  pl.BlockSpec((None, s_pad, 128), lambda bi, h, i: (bi, 0, h)),
                  pl.BlockSpec((None, s_pad, 128), lambda bi, h, i: (bi, 0, 0)),
                  pl.BlockSpec((None, s_pad, 128), lambda bi, h, i: (bi, 0, h))],
        out_specs=pl.BlockSpec((None, tq, 128), lambda bi, h, i: (bi, i, h)),
        out_shape=jax.ShapeDtypeStruct((b, t, B_HEADS * V_DIM), BF16),
        scratch_shapes=[pltpu.VMEM((tq, 1), F32), pltpu.VMEM((tq, 1), F32), pltpu.VMEM((tq, 128), F32)],
        compiler_params=_cparams(3),
        name="mla_attn",
    )(q, kn, kr, v)


def _diff_body(q_ref, k_ref, v_ref, lq1_ref, lk1_ref, lq2_ref, lk2_ref, sn_ref, o_ref, m_ref, l_ref, acc_ref,
               *, tq, kb, q_off, s_valid, lam_init):
    i = pl.program_id(2)
    n_blk = _n_key_blocks(i, tq, kb, q_off, s_valid)
    qc = _q_chunk(i, tq, q_off)
    _flash_init(m_ref, l_ref, acc_ref)

    def attend(j, carry):
        ks = pl.multiple_of(j * kb, kb)
        ok = _k_chunk(ks, kb, s_valid) <= qc
        v = v_ref[pl.ds(ks, kb), :]
        for p in range(2):
            s = _dot_nt(q_ref[:, p * 128:(p + 1) * 128], k_ref[pl.ds(ks, kb), p * 128:(p + 1) * 128])
            s = jnp.where(ok, s, NEG)
            _flash_step(s, v, m_ref.at[p], l_ref.at[p], acc_ref.at[p])
        return carry

    lax.fori_loop(0, n_blk, attend, 0)
    lam = (jnp.exp(jnp.sum(lq1_ref[...] * lk1_ref[...], axis=1, keepdims=True))
           - jnp.exp(jnp.sum(lq2_ref[...] * lk2_ref[...], axis=1, keepdims=True)) + lam_init)
    o = acc_ref[0] / l_ref[0] - lam * (acc_ref[1] / l_ref[1])
    o_ref[...] = (_rms(o, sn_ref[...], 2 * C_DIM) * (1.0 - lam_init)).astype(BF16)


def _diff_attn(q, k, v, lq1, lk1, lq2, lk2, sn, *, tq, kb, q_off, s_valid, lam_init):
    b, t, _ = q.shape
    s_pad = k.shape[1]
    vec = lambda c: pl.BlockSpec((1, c), lambda bi, h, i: (0, 0))
    return pl.pallas_call(
        functools.partial(_diff_body, tq=tq, kb=kb, q_off=q_off, s_valid=s_valid, lam_init=lam_init),
        grid=(b, C_HEADS, t // tq),
        in_specs=[pl.BlockSpec((None, tq, 256), lambda bi, h, i: (bi, i, h)),
                  pl.BlockSpec((None, s_pad, 256), lambda bi, h, i: (bi, 0, h)),
                  pl.BlockSpec((None, s_pad, 256), lambda bi, h, i: (bi, 0, h)),
                  vec(128), vec(128), vec(128), vec(128), vec(256)],
        out_specs=pl.BlockSpec((None, tq, 256), lambda bi, h, i: (bi, i, h)),
        out_shape=jax.ShapeDtypeStruct((b, t, C_HEADS * 2 * C_DIM), BF16),
        scratch_shapes=[pltpu.VMEM((2, tq, 1), F32), pltpu.VMEM((2, tq, 1), F32), pltpu.VMEM((2, tq, 256), F32)],
        compiler_params=_cparams(3),
        name="diff_attn",
    )(q, k, v, lq1, lk1, lq2, lk2, sn)


SCORE_BLOCKS = 4
SEARCH_BLOCKS = 4
SEARCH_BITS_UNCHECKED = 26
SEARCH_BITS_PER_CHECK = 2


def _stage_bufs(n_chain, kb, r):
    return [pltpu.VMEM((n_chain, 2, kb, r), F32), pltpu.VMEM((n_chain, 2, kb, r), BF16),
            pltpu.VMEM((n_chain, 2, 1, r), F32)]


def _staged_flash_t(bufs, state, lo, hi, n_kb, qk, prep, vt_of, first=True, last=True):
    s_ref, p_ref, a_ref = bufs
    m_ref, l_ref, acc_ref = state
    n_chain = s_ref.shape[0]
    clamp = lambda j: jnp.clip(j, 0, n_kb - 1)

    def softmax(c, j, slot):
        st = prep(c, j, s_ref[c, slot])
        m_prev = m_ref[c]
        m_new = jnp.maximum(m_prev, jnp.max(st, axis=0, keepdims=True))
        alpha = jnp.exp2(m_prev - m_new)
        p = jnp.exp2(st - m_new)
        l_ref[c] = alpha * l_ref[c] + jnp.sum(p, axis=0, keepdims=True)
        m_ref[c] = m_new
        p_ref[c, slot] = p.astype(BF16)
        a_ref[c, slot] = alpha

    def values(c, j, slot):
        acc_ref[c] = a_ref[c, slot] * acc_ref[c] + _dot(vt_of(c, clamp(j)), p_ref[c, slot])

    if first:
        for c in range(n_chain):
            s_ref[c, 0] = qk(c, clamp(lo))
            p_ref[c, 1] = jnp.zeros(p_ref.shape[2:], BF16)
            a_ref[c, 1] = jnp.ones(a_ref.shape[2:], F32)

    def turn(t, carry):
        j0 = lo + 2 * t
        for c in range(n_chain):
            s_ref[c, 1] = qk(c, clamp(j0 + 1))
        for c in range(n_chain):
            softmax(c, j0, 0)
        for c in range(n_chain):
            values(c, j0 - 1, 1)
        for c in range(n_chain):
            s_ref[c, 0] = qk(c, clamp(j0 + 2))
        for c in range(n_chain):
            softmax(c, j0 + 1, 1)
        for c in range(n_chain):
            values(c, j0, 0)
        return carry

    n_turn = lax.shift_right_logical(hi - lo + 1, 1)
    lax.fori_loop(0, n_turn, turn, 0)
    if last:
        for c in range(n_chain):
            values(c, lo + 2 * n_turn - 1, 1)


def _dsa_t_body(q_ref, qi_ref, wi_ref, kidx_ref, k_ref, vt_ref, o_ref, keys_ref, sc_ref, s_ref, p_ref, a_ref,
                m_ref, l_ref, acc_ref, *, tq, kb, q_off, s_valid, topk):
    i = pl.program_id(1)
    n_blk = _n_key_blocks(i, tq, kb, q_off, s_valid)
    n_kb = keys_ref.shape[0]
    n_sb = lax.shift_right_logical(n_blk + SEARCH_BLOCKS - 1, int(math.log2(SEARCH_BLOCKS)))
    limit = _key_limit(i, tq, q_off, s_valid)
    kidx0 = lax.broadcasted_iota(I32, (kb, tq), 0)
    wit = wi_ref[...].T
    wrow = [wit[hd:hd + 1, :] for hd in range(IDX_HEADS)]

    def score_blocks(jj, carry):
        ks = pl.multiple_of(jj * (SCORE_BLOCKS * kb), SCORE_BLOCKS * kb)
        half = SCORE_BLOCKS // 2
        q_all = jnp.concatenate([qi_ref[:, p * 128:(p + 1) * 128] for p in range(IDX_HEADS // 2)], axis=0)
        for hf in range(2):
            rows = pl.ds(ks + hf * half * kb, half * kb)
            sc_ref[hf, 0] = _dot_nt(kidx_ref[rows, 0:128], q_all)
            sc_ref[hf, 1] = _dot_nt(kidx_ref[rows, 128:256], q_all)
        for hf in range(2):
            for u in range(half):
                blk = slice(u * kb, (u + 1) * kb)
                sc = jnp.zeros((kb, tq), F32)
                for p in range(IDX_HEADS // 2):
                    sc = sc + wrow[2 * p] * jnp.maximum(sc_ref[hf, 0, blk, p * tq:(p + 1) * tq], 0.0)
                    sc = sc + wrow[2 * p + 1] * jnp.maximum(sc_ref[hf, 1, blk, p * tq:(p + 1) * tq], 0.0)
                bits = lax.bitcast_convert_type(sc, I32)
                key = bits ^ ((bits >> 31) & 0x7FFFFFFF)
                j = jj * SCORE_BLOCKS + hf * half + u
                keys_ref[j] = jnp.where(kidx0 < limit - j * kb, key, INT_MIN)
        return carry

    def pad_block(j, carry):
        keys_ref[j] = jnp.full((kb, tq), INT_MIN, I32)
        return carry

    n_scored = lax.shift_right_logical(n_blk + SCORE_BLOCKS - 1, int(math.log2(SCORE_BLOCKS)))
    lax.fori_loop(0, n_scored, score_blocks, 0)
    lax.fori_loop(n_scored * SCORE_BLOCKS, n_sb * SEARCH_BLOCKS, pad_block, 0)

    def count(pred):
        def group(jj, cs):
            out = []
            for u in range(SEARCH_BLOCKS):
                j = jj * SEARCH_BLOCKS + u
                x = jnp.where(pred(keys_ref[j], j * kb), 1.0, 0.0).reshape(kb // 8, 8, tq)
                h = kb // 32
                out.append(cs[u] + ((jnp.sum(x[:h], axis=0) + jnp.sum(x[h:2 * h], axis=0))
                                    + (jnp.sum(x[2 * h:3 * h], axis=0) + jnp.sum(x[3 * h:], axis=0))))
            return tuple(out)
        cs = lax.fori_loop(0, n_sb, group, tuple(jnp.zeros((8, tq), F32) for _ in range(SEARCH_BLOCKS)))
        return jnp.sum(functools.reduce(lambda a, b: a + b, cs), axis=0, keepdims=True)

    kf = float(topk)

    def bit_step(b, state):
        t, n_ge = state
        cand = t + jnp.left_shift(jnp.int32(1), 31 - b)
        cnt = count(lambda kblk, _: kblk >= cand)
        take = cnt >= kf
        return jnp.where(take, cand, t), jnp.where(take, cnt, n_ge)

    def unsettled(n_ge):
        open_ = jnp.logical_and(n_ge != kf, limit.astype(F32) > kf)
        return jnp.max(jnp.where(open_, 1.0, 0.0)) > 0.0

    def more_bits(state):
        b, _, n_ge = state
        return jnp.logical_and(b < 32, unsettled(n_ge))

    def four_bits(state):
        b, t, n_ge = state
        for u in range(SEARCH_BITS_PER_CHECK):
            t, n_ge = bit_step(b + u, (t, n_ge))
        return b + SEARCH_BITS_PER_CHECK, t, n_ge

    start = (jnp.full((1, tq), INT_MIN, I32), jnp.full((1, tq), float(n_kb * kb), F32))
    head = lax.fori_loop(0, SEARCH_BITS_UNCHECKED, bit_step, start)
    _, thr, _ = lax.while_loop(more_bits, four_bits, (jnp.int32(SEARCH_BITS_UNCHECKED),) + head)
    need = kf - count(lambda kblk, _: kblk > thr)
    n_eq = count(lambda kblk, _: kblk == thr)
    partial = jnp.logical_and(n_eq > need, thr != INT_MIN)

    @pl.when(jnp.max(jnp.where(partial, 1.0, 0.0)) > 0.0)
    def _():
        def idx_step(b, x):
            cand = x + jnp.left_shift(jnp.int32(1), b)
            cnt = count(lambda kblk, k0: jnp.logical_and(kblk == thr, kidx0 < cand - k0))
            return jnp.where(cnt < need, cand, x)

        nbits = max(1, int(s_valid - 1).bit_length())
        cut = lax.fori_loop(0, nbits, lambda b, x: idx_step(nbits - 1 - b, x), jnp.zeros((1, tq), I32))

        def demote(j, carry):
            kblk = keys_ref[j]
            drop = jnp.logical_and(jnp.logical_and(kblk == thr, kidx0 > cut - j * kb), partial)
            keys_ref[j] = jnp.where(drop, kblk - 1, kblk)
            return carry

        lax.fori_loop(0, n_blk, demote, 0)

    thr_sel = jnp.maximum(thr, INT_MIN + 1)

    _flash_init(m_ref, l_ref, acc_ref)
    n_pair = A_HEADS // 2

    def to_bias(j, carry):
        keys_ref[j] = lax.bitcast_convert_type(jnp.where(keys_ref[j] >= thr_sel, 0.0, NEG), I32)
        return carry

    lax.fori_loop(0, n_sb * SEARCH_BLOCKS, to_bias, 0)
    for g in range(A_KV_HEADS):
        def qk(c, j, g=g):
            p = g * (A_GROUP // 2) + c
            qp = jnp.concatenate([q_ref[:, (2 * p) * 128:(2 * p + 1) * 128],
                                  q_ref[:, (2 * p + 1) * 128:(2 * p + 2) * 128]], axis=0)
            return _dot_nt(k_ref[pl.ds(pl.multiple_of(j * kb, kb), kb), g * 128:(g + 1) * 128], qp)

        def prep(c, j, st):
            bias = lax.bitcast_convert_type(keys_ref[j], F32)
            return st + jnp.concatenate([bias, bias], axis=1)

        def vt_of(c, j, g=g):
            return vt_ref[j, g * 128:(g + 1) * 128, :]

        chains = pl.ds(g * (A_GROUP // 2), A_GROUP // 2)
        _staged_flash_t((s_ref.at[chains], p_ref.at[chains], a_ref.at[chains]),
                        (m_ref.at[chains], l_ref.at[chains], acc_ref.at[chains]), 0, n_blk, n_kb, qk, prep, vt_of)
    for p in range(n_pair):
        ot = acc_ref[p] / l_ref[p]
        for a in range(2):
            hd = 2 * p + a
            o_ref[:, hd * 128:(hd + 1) * 128] = ot[:, a * tq:(a + 1) * tq].T.astype(BF16)


def _dsa_attn_t(q, qi, wi, kidx, k, vt, *, tq, kb, q_off, s_valid, topk):
    b, t, _ = q.shape
    s_pad = k.shape[1]
    n_kb = s_pad // kb
    qspec = lambda c: pl.BlockSpec((None, tq, c), lambda bi, i: (bi, i, 0))
    kspec = lambda c: pl.BlockSpec((None, s_pad, c), lambda bi, i: (bi, 0, 0))
    n_pair = A_HEADS // 2
    return pl.pallas_call(
        functools.partial(_dsa_t_body, tq=tq, kb=kb, q_off=q_off, s_valid=s_valid, topk=topk),
        grid=(b, t // tq),
        in_specs=[qspec(1024), qspec(512), qspec(LANES),
                  kspec(256), kspec(256), pl.BlockSpec((None, n_kb, 256, kb), lambda bi, i: (bi, 0, 0, 0))],
        out_specs=qspec(1024),
        out_shape=jax.ShapeDtypeStruct((b, t, 1024), BF16),
        scratch_shapes=[pltpu.VMEM((n_kb, kb, tq), I32),
                        pltpu.VMEM((2, 2, SCORE_BLOCKS // 2 * kb, IDX_HEADS // 2 * tq), F32)]
        + _stage_bufs(n_pair, kb, 2 * tq)
        + [pltpu.VMEM((n_pair, 1, 2 * tq), F32), pltpu.VMEM((n_pair, 1, 2 * tq), F32),
           pltpu.VMEM((n_pair, 128, 2 * tq), F32)],
        compiler_params=_cparams(2),
        name="dsa_attn_t",
    )(q, qi, wi, kidx, k, vt)


def _causal_flash_t(bufs, state, i, n_kb, tq, kb, q_off, s_valid, qk, vt_of):
    n_blk = _n_key_blocks(i, tq, kb, q_off, s_valid)
    n_plain = _n_full_blocks(i, tq, kb, q_off, s_valid) & -2
    limit = _key_limit(i, tq, q_off, s_valid)

    def masked(c, j, st):
        return jnp.where(lax.broadcasted_iota(I32, (kb, tq), 0) < limit - j * kb, st, NEG)

    _staged_flash_t(bufs, state, 0, n_plain, n_kb, qk, lambda c, j, st: st, vt_of, last=False)
    _staged_flash_t(bufs, state, n_plain, n_blk, n_kb, qk, masked, vt_of, first=False)


def _mla_t_body(qt_ref, kn_ref, kr_ref, vt_ref, o_ref, s_ref, p_ref, a_ref, m_ref, l_ref, acc_ref,
                *, tq, kb, q_off, s_valid):
    i = pl.program_id(2)
    _flash_init(m_ref, l_ref, acc_ref)

    def qk(a, j):
        ks = pl.multiple_of(j * kb, kb)
        kcat = jnp.concatenate([kn_ref[pl.ds(ks, kb), a * 128:(a + 1) * 128], kr_ref[pl.ds(ks, kb), :]], axis=1)
        return _dot(kcat, qt_ref[a * 256:(a + 1) * 256, :])

    def vt_of(a, j):
        return vt_ref[j, a * 128:(a + 1) * 128, :]

    _causal_flash_t((s_ref, p_ref, a_ref), (m_ref, l_ref, acc_ref), i, vt_ref.shape[0], tq, kb, q_off, s_valid,
                    qk, vt_of)
    for a in range(2):
        o_ref[:, a * 128:(a + 1) * 128] = (acc_ref[a] / l_ref[a]).T.astype(BF16)


def _mla_attn_t(qt, kn, kr, vt, *, tq, kb, q_off, s_valid):
    b, _, t = qt.shape
    s_pad = kn.shape[1]
    n_kb = s_pad // kb
    return pl.pallas_call(
        functools.partial(_mla_t_body, tq=tq, kb=kb, q_off=q_off, s_valid=s_valid),
        grid=(b, B_HEADS // 2, t // tq),
        in_specs=[pl.BlockSpec((None, 512, tq), lambda bi, h, i: (bi, h, i)),
                  pl.BlockSpec((None, s_pad, 256), lambda bi, h, i: (bi, 0, h)),
                  pl.BlockSpec((None, s_pad, 128), lambda bi, h, i: (bi, 0, 0)),
                  pl.BlockSpec((None, n_kb, 256, kb), lambda bi, h, i: (bi, 0, h, 0))],
        out_specs=pl.BlockSpec((None, tq, 256), lambda bi, h, i: (bi, i, h)),
        out_shape=jax.ShapeDtypeStruct((b, t, B_HEADS * V_DIM), BF16),
        scratch_shapes=_stage_bufs(2, kb, tq) + [pltpu.VMEM((2, 1, tq), F32), pltpu.VMEM((2, 1, tq), F32),
                                                 pltpu.VMEM((2, 128, tq), F32)],
        compiler_params=_cparams(3),
        name="mla_attn_t",
    )(qt, kn, kr, vt)


def _diff_t_body(q_ref, k_ref, vt_ref, lq1_ref, lk1_ref, lq2_ref, lk2_ref, sn_ref, o_ref, s_ref, p_ref, a_ref,
                 m_ref, l_ref, acc_ref, *, tq, kb, q_off, s_valid, lam_init):
    i = pl.program_id(2)
    _flash_init(m_ref, l_ref, acc_ref)

    def qk(p, j):
        ks = pl.multiple_of(j * kb, kb)
        return _dot_nt(k_ref[pl.ds(ks, kb), p * 128:(p + 1) * 128], q_ref[:, p * 128:(p + 1) * 128])

    _causal_flash_t((s_ref, p_ref, a_ref), (m_ref, l_ref, acc_ref), i, vt_ref.shape[0], tq, kb, q_off, s_valid,
                    qk, lambda p, j: vt_ref[j])
    lam = (jnp.exp(jnp.sum(lq1_ref[...] * lk1_ref[...], axis=1, keepdims=True))
           - jnp.exp(jnp.sum(lq2_ref[...] * lk2_ref[...], axis=1, keepdims=True)) + lam_init)
    ot = acc_ref[0] / l_ref[0] - lam * (acc_ref[1] / l_ref[1])
    o = jnp.concatenate([ot[0:128, :].T, ot[128:256, :].T], axis=1)
    o_ref[...] = (_rms(o, sn_ref[...], 2 * C_DIM) * (1.0 - lam_init)).astype(BF16)


def _diff_attn_t(q, k, vt, lq1, lk1, lq2, lk2, sn, *, tq, kb, q_off, s_valid, lam_init):
    b, t, _ = q.shape
    s_pad = k.shape[1]
    n_kb = s_pad // kb
    vec = lambda c: pl.BlockSpec((1, c), lambda bi, h, i: (0, 0))
    return pl.pallas_call(
        functools.partial(_diff_t_body, tq=tq, kb=kb, q_off=q_off, s_valid=s_valid, lam_init=lam_init),
        grid=(b, C_HEADS, t // tq),
        in_specs=[pl.BlockSpec((None, tq, 256), lambda bi, h, i: (bi, i, h)),
                  pl.BlockSpec((None, s_pad, 256), lambda bi, h, i: (bi, 0, h)),
                  pl.BlockSpec((None, n_kb, 256, kb), lambda bi, h, i: (bi, 0, h, 0)),
                  vec(128), vec(128), vec(128), vec(128), vec(256)],
        out_specs=pl.BlockSpec((None, tq, 256), lambda bi, h, i: (bi, i, h)),
        out_shape=jax.ShapeDtypeStruct((b, t, C_HEADS * 2 * C_DIM), BF16),
        scratch_shapes=_stage_bufs(2, kb, tq) + [pltpu.VMEM((2, 1, tq), F32), pltpu.VMEM((2, 1, tq), F32),
                                                 pltpu.VMEM((2, 256, tq), F32)],
        compiler_params=_cparams(3),
        name="diff_attn_t",
    )(q, k, vt, lq1, lk1, lq2, lk2, sn)


def _blocked_t(v, kb):
    b, s, c = v.shape
    return jnp.swapaxes(v.reshape(b, s // kb, kb, c), 2, 3)


def _rope_tables(pos, reps):
    p = pos.astype(F32)[:, None]
    inv64 = jnp.power(ROPE_THETA, -jnp.arange(64, dtype=F32) / 64)
    inv32 = jnp.power(ROPE_THETA, -jnp.arange(32, dtype=F32) / 32)
    c, s = jnp.cos(p * inv64), jnp.sin(p * inv64)
    c3, s3 = jnp.cos(p * inv32), jnp.sin(p * inv32)
    z = jnp.zeros_like(s3)
    tabs = {
        "c128": jnp.concatenate([c, c], axis=1),
        "s128": jnp.concatenate([-s, s], axis=1),
        "c64": jnp.concatenate([c3, c3, c3, c3], axis=1),
        "sa64": jnp.concatenate([-s3, z, -s3, z], axis=1),
        "sb64": jnp.concatenate([z, s3, z, s3], axis=1),
    }
    return {k: jnp.tile(v, (reps, 1)) for k, v in tabs.items()}


def _pad_cols(w, n):
    return jnp.pad(w, ((0, 0), (0, n - w.shape[1])))


def _pad_lanes(g, n=128):
    g = g.reshape(1, -1)
    return jnp.pad(g, ((0, 0), (0, n - g.shape[1])))


def _prep_weights(W):
    P = {}
    for nm in ("ffn1", "ffn2"):
        P[nm] = [(W[nm + "_norm"][i].reshape(1, -1), W[nm + "_wg"][i].astype(BF16), W[nm + "_wu"][i].astype(BF16),
                  W[nm + "_wd"][i].astype(BF16)) for i in range(DEPTH)]
    P["mix_norm"] = [W["mix_norm"][i].reshape(1, -1) for i in range(DEPTH)]
    P["a"] = [dict(w=_pad_cols(W["a_w_in"][j], A_IN_PAD).astype(BF16), qn=W["a_q_norm"][j].reshape(1, -1),
                   kn=W["a_k_norm"][j].reshape(1, -1), ikn=_pad_lanes(W["a_idx_k_norm"][j]),
                   wo=W["a_w_out"][j].astype(BF16)) for j in range(W["a_w_in"].shape[0])]
    P["b"] = []
    for j in range(W["b_w_in"].shape[0]):
        wuq = W["b_w_uq"][j].reshape(Q_LORA, B_HEADS, NOPE_DIM + ROPE_DIM)
        wuq_rope = jnp.pad(wuq[:, :, NOPE_DIM:], ((0, 0), (0, 0), (0, 128 - ROPE_DIM)))
        wuq = jnp.concatenate([wuq[:, :, :NOPE_DIM], wuq_rope], axis=2).reshape(Q_LORA, -1)
        P["b"].append(dict(
            w=_pad_cols(W["b_w_in"][j], B_IN_PAD).astype(BF16), qan=W["b_q_a_norm"][j].reshape(1, -1),
            kvan=W["b_kv_a_norm"][j].reshape(1, -1), krn=_pad_lanes(W["b_k_rope_norm"][j]), wuq=wuq.astype(BF16),
            qnn=W["b_q_nope_norm"][j].reshape(1, -1), qrn=_pad_lanes(W["b_q_rope_norm"][j]),
            wukv=W["b_w_ukv"][j].astype(BF16), knn=W["b_k_nope_norm"][j].reshape(1, -1),
            wo=W["b_w_out"][j].astype(BF16)))
    P["c"] = [dict(w=W["c_w_in"][j].astype(BF16), qn=W["c_q_norm"][j].reshape(1, -1),
                   kn=W["c_k_norm"][j].reshape(1, -1), lq1=W["c_lambda_q1"][j].reshape(1, -1),
                   lk1=W["c_lambda_k1"][j].reshape(1, -1), lq2=W["c_lambda_q2"][j].reshape(1, -1),
                   lk2=W["c_lambda_k2"][j].reshape(1, -1), sn=W["c_sub_norm"][j].reshape(1, -1),
                   wo=W["c_w_out"][j].astype(BF16)) for j in range(W["c_w_in"].shape[0])]
    return P


def _with_past(past, new, s_pad):
    x = new if past is None else jnp.concatenate([past.astype(new.dtype), new], axis=1)
    return x if x.shape[1] == s_pad else jnp.pad(x, ((0, 0), (0, s_pad - x.shape[1]), (0, 0)))


def _trunk(x, offset, past, P, cfg):
    b, t, _ = x.shape
    n = b * t
    tm, tq_a, tq, kb, key_major = cfg["tm"], cfg["tq_a"], cfg["tq"], cfg["kb"], cfg["key_major"]
    p_len = 0 if past is None else past["a_k"].shape[2]
    s_valid = p_len + t
    s_pad = -(-s_valid // kb) * kb
    tabs = _rope_tables(offset + jnp.arange(t, dtype=I32), tm // t if tm > t else 1)
    att = dict(kb=kb, q_off=offset, s_valid=s_valid)
    rows = {k: [] for k in ("a_k", "a_v", "a_ik", "b_ckv", "b_kr", "c_k", "c_v")}
    x = x.reshape(n, D_MODEL)
    r3 = lambda a: a.reshape(b, t, a.shape[-1])
    t3 = lambda a: jnp.swapaxes(r3(a), 1, 2)
    pj = lambda nm, j: None if past is None else past[nm][j].reshape(b, p_len, -1)
    for i in range(DEPTH):
        x = _ffn(x, *P["ffn1"][i], cfg["tm_ffn"])
        kind, j = i % N_MIXERS, i // N_MIXERS
        g = P["mix_norm"][i]
        if kind == 0:
            pa = P["a"][j]
            q, k32, v32, kbf, vbf, qi, ki32, kidx, wi = _a_proj(x, g, pa["w"], pa["qn"], pa["kn"], pa["ikn"], tabs, tm)
            rows["a_k"].append(k32.reshape(b, t, A_KV_HEADS, A_HEAD_DIM))
            rows["a_v"].append(v32.reshape(b, t, A_KV_HEADS, A_HEAD_DIM))
            rows["a_ik"].append(ki32.reshape(b, t, IDX_DIM))
            pik = pj("a_ik", j)
            if pik is not None:
                z = jnp.zeros_like(pik)
                pik = jnp.concatenate([pik, z, z, pik], axis=-1)
            kidx_all = _with_past(pik, r3(kidx), s_pad)
            k_all = _with_past(pj("a_k", j), r3(kbf), s_pad)
            v_all = _with_past(pj("a_v", j), r3(vbf), s_pad)
            topk = min(TOPK_MAX, s_valid // 4)
            if key_major:
                o = _dsa_attn_t(r3(q), r3(qi), r3(wi), kidx_all, k_all, _blocked_t(v_all, kb),
                                tq=tq_a, topk=topk, **att)
            else:
                o = _dsa_attn(r3(q), r3(qi), r3(wi), kidx_all, k_all, v_all, tq=tq_a, topk=topk, **att)
        elif kind == 1:
            pb = P["b"][j]
            q, ckv, kr32, krbf = _b_proj(x, g, pb["w"], pb["qan"], pb["kvan"], pb["krn"], pb["wuq"], pb["qnn"],
                                         pb["qrn"], tabs, tm)
            rows["b_ckv"].append(ckv.reshape(b, t, KV_LORA))
            rows["b_kr"].append(kr32.reshape(b, t, ROPE_DIM))
            ckv_all = _with_past(pj("b_ckv", j), r3(ckv), s_pad)
            pkr = pj("b_kr", j)
            if pkr is not None:
                pkr = jnp.concatenate([pkr, jnp.zeros_like(pkr)], axis=-1)
            kn, v = _kv_up(ckv_all.reshape(b * s_pad, KV_LORA), pb["wukv"], pb["knn"], math.gcd(b * s_pad, 512))
            kn, v, kr_all = kn.reshape(b, s_pad, -1), v.reshape(b, s_pad, -1), _with_past(pkr, r3(krbf), s_pad)
            if key_major:
                o = _mla_attn_t(t3(q), kn, kr_all, _blocked_t(v, kb), tq=cfg["tq_mla"], **att)
            else:
                o = _mla_attn(r3(q), kn, kr_all, v, tq=tq, **att)
        else:
            pc = P["c"][j]
            q, k32, v32, kbf, vbf = _c_proj(x, g, pc["w"], pc["qn"], pc["kn"], tabs, tm)
            rows["c_k"].append(k32.reshape(b, t, C_HEADS, 2, C_DIM))
            rows["c_v"].append(v32.reshape(b, t, C_HEADS, 2 * C_DIM))
            k_all = _with_past(pj("c_k", j), r3(kbf), s_pad)
            v_all = _with_past(pj("c_v", j), r3(vbf), s_pad)
            lam = (pc["lq1"], pc["lk1"], pc["lq2"], pc["lk2"], pc["sn"])
            lam_init = 0.8 - 0.6 * math.exp(-0.3 * i)
            if key_major:
                att_c = dict(att, kb=cfg["kb_diff"])
                o = _diff_attn_t(r3(q), k_all, _blocked_t(v_all, cfg["kb_diff"]), *lam, tq=tq, lam_init=lam_init,
                                 **att_c)
            else:
                o = _diff_attn(r3(q), k_all, v_all, *lam, tq=tq, lam_init=lam_init, **att)
        wo = (P["a"], P["b"], P["c"])[kind][j]["wo"]
        x = _ffn(x, *P["ffn2"][i], cfg["tm_ffn"], attn=o.reshape(n, -1), wo=wo)
    order = ("a_k", "a_v", "a_ik", "b_ckv", "b_kr", "c_k", "c_v")
    stack = lambda rs: rs[0][None] if len(rs) == 1 else jnp.stack(rs)
    return x.reshape(b, t, D_MODEL), tuple(stack(rows[k]) for k in order)


PROMPT_CFG = dict(tm=512, tm_ffn=1024, tq_a=128, tq=256, tq_mla=512, kb=256, kb_diff=512, key_major=True)
SAMPLE_CFG = dict(tm=128, tm_ffn=128, tq_a=16, tq=16, kb=1280, key_major=False)


@jax.jit
def _forward(x_prompt, x_sample, past, W):
    P = _prep_weights(W)
    y_p, rows_p = _trunk(x_prompt, 0, None, P, PROMPT_CFG)
    y_s, rows_s = _trunk(x_sample, past["a_k"].shape[2], past, P, SAMPLE_CFG)
    return (y_p, y_s) + rows_p + rows_s


def kernel(x_prompt, x_sample, cache_a_k, cache_a_v, cache_a_idx_k, cache_b_ckv, cache_b_krope, cache_c_k, cache_c_v, ffn1_norm, ffn1_wg, ffn1_wu, ffn1_wd, mix_norm, ffn2_norm, ffn2_wg, ffn2_wu, ffn2_wd, a_w_in, a_q_norm, a_k_norm, a_idx_k_norm, a_w_out, b_w_in, b_q_a_norm, b_kv_a_norm, b_w_uq, b_w_ukv, b_q_nope_norm, b_q_rope_norm, b_k_nope_norm, b_k_rope_norm, b_w_out, c_w_in, c_q_norm, c_k_norm, c_lambda_q1, c_lambda_k1, c_lambda_q2, c_lambda_k2, c_sub_norm, c_w_out):
    W = dict(ffn1_norm=ffn1_norm, ffn1_wg=ffn1_wg, ffn1_wu=ffn1_wu, ffn1_wd=ffn1_wd, mix_norm=mix_norm,
             ffn2_norm=ffn2_norm, ffn2_wg=ffn2_wg, ffn2_wu=ffn2_wu, ffn2_wd=ffn2_wd,
             a_w_in=a_w_in, a_q_norm=a_q_norm, a_k_norm=a_k_norm, a_idx_k_norm=a_idx_k_norm, a_w_out=a_w_out,
             b_w_in=b_w_in, b_q_a_norm=b_q_a_norm, b_kv_a_norm=b_kv_a_norm, b_w_uq=b_w_uq, b_w_ukv=b_w_ukv,
             b_q_nope_norm=b_q_nope_norm, b_q_rope_norm=b_q_rope_norm, b_k_nope_norm=b_k_nope_norm,
             b_k_rope_norm=b_k_rope_norm, b_w_out=b_w_out,
             c_w_in=c_w_in, c_q_norm=c_q_norm, c_k_norm=c_k_norm, c_lambda_q1=c_lambda_q1,
             c_lambda_k1=c_lambda_k1, c_lambda_q2=c_lambda_q2, c_lambda_k2=c_lambda_k2, c_sub_norm=c_sub_norm,
             c_w_out=c_w_out)
    past = dict(a_k=cache_a_k, a_v=cache_a_v, a_ik=cache_a_idx_k, b_ckv=cache_b_ckv, b_kr=cache_b_krope,
                c_k=cache_c_k, c_v=cache_c_v)
    return _forward(x_prompt, x_sample, past, W)
```

```python
import functools
import math

import jax
import jax.numpy as jnp
from jax import lax
from jax.experimental import pallas as pl
from jax.experimental.pallas import tpu as pltpu

F32 = jnp.float32
BF16 = jnp.bfloat16
I32 = jnp.int32

D_MODEL = 1024
DEPTH = 4
CHUNK_SHIFT = 6
N_MIXERS = 3
ROPE_THETA = 10000.0
EPS = 1e-6
D_FF = 2816

A_HEADS = 8
A_KV_HEADS = 2
A_GROUP = A_HEADS // A_KV_HEADS
A_HEAD_DIM = 128
IDX_HEADS = 8
IDX_DIM = 64
TOPK_MAX = 256
LOG2E = math.log2(math.e)
A_SCALE = A_HEAD_DIM ** -0.5 * LOG2E
IDX_W_SCALE = (IDX_HEADS * IDX_DIM) ** -0.5
A_IN = 2120
A_IN_PAD = 2176

B_HEADS = 8
Q_LORA = 384
KV_LORA = 256
NOPE_DIM = 128
ROPE_DIM = 64
V_DIM = 128
B_SCALE = (NOPE_DIM + ROPE_DIM) ** -0.5 * LOG2E
B_IN = 704
B_IN_PAD = 768

C_HEADS = 4
C_DIM = 128
C_SCALE = C_DIM ** -0.5 * LOG2E

LANES = 128
NEG = -1e30
INT_MIN = -(2 ** 31)
VMEM_LIMIT = 56 * 1024 * 1024


def _cparams(n_axes):
    return pltpu.CompilerParams(dimension_semantics=("arbitrary",) * n_axes, vmem_limit_bytes=VMEM_LIMIT)


def _dot(a, b):
    return jnp.dot(a, b, preferred_element_type=F32)


def _dot_nt(a, b):
    return lax.dot_general(a, b, (((1,), (1,)), ((), ())), preferred_element_type=F32)


def _rms(x, g, n):
    ms = jnp.sum(x * x, axis=-1, keepdims=True) * (1.0 / n)
    return x * lax.rsqrt(ms + EPS) * g


def _rope128(x, c, s):
    return x * c + pltpu.roll(x, 64, 1) * s


def _rope64(x, c, sa, sb):
    return x * c + pltpu.roll(x, 96, 1) * sa + pltpu.roll(x, 32, 1) * sb


FFN_CHUNK = 256


def _ffn_body(*refs, fc, mixed):
    if mixed:
        attn_ref, wo_ref, x_ref, g_ref, wg_ref, wu_ref, wd_ref, o_ref = refs
        x = x_ref[...] + _dot(attn_ref[...], wo_ref[...])
    else:
        x_ref, g_ref, wg_ref, wu_ref, wd_ref, o_ref = refs
        x = x_ref[...]
    h = _rms(x, g_ref[...], D_MODEL).astype(BF16)
    y = None
    for c in range(D_FF // fc):
        a = _dot(h, wg_ref[:, c * fc:(c + 1) * fc])
        u = _dot(h, wu_ref[:, c * fc:(c + 1) * fc])
        act = (a * jax.nn.sigmoid(a) * u).astype(BF16)
        part = _dot(act, wd_ref[c * fc:(c + 1) * fc, :])
        y = part if y is None else y + part
    o_ref[...] = x + 0.5 * y


def _const_spec(shape):
    nd = len(shape)
    return pl.BlockSpec(shape, lambda *_: (0,) * nd, pipeline_mode=pl.Buffered(1))


def _row_spec(tm, n):
    return pl.BlockSpec((tm, n), lambda i: (i, 0))


def _ffn(x, g, wg, wu, wd, tm, attn=None, wo=None):
    n = x.shape[0]
    mixed = attn is not None
    pre_specs = [_row_spec(tm, attn.shape[1]), _const_spec(wo.shape)] if mixed else []
    return pl.pallas_call(
        functools.partial(_ffn_body, fc=FFN_CHUNK, mixed=mixed),
        grid=(n // tm,),
        in_specs=pre_specs + [_row_spec(tm, D_MODEL), _const_spec((1, D_MODEL)), _const_spec((D_MODEL, D_FF)),
                              _const_spec((D_MODEL, D_FF)), _const_spec((D_FF, D_MODEL))],
        out_specs=_row_spec(tm, D_MODEL),
        out_shape=jax.ShapeDtypeStruct((n, D_MODEL), F32),
        compiler_params=_cparams(1),
        name="ffn_mixed" if mixed else "ffn",
    )(*((attn, wo) if mixed else ()), x, g, wg, wu, wd)


PROJ_PARTS = 2

def _a_proj_body(x_ref, g_ref, w_ref, qn_ref, kn_ref, ikn_ref, c128_ref, s128_ref, c64_ref, sa64_ref, sb64_ref,
                 q_ref, k32_ref, v32_ref, kbf_ref, vbf_ref, qi_ref, ki32_ref, kidx_ref, wi_ref, y_ref):
    hm = x_ref.shape[0] // PROJ_PARTS
    for part in range(PROJ_PARTS):
        rows = slice(part * hm, (part + 1) * hm)
        y_ref[part] = _dot(_rms(x_ref[rows, :], g_ref[...], D_MODEL).astype(BF16), w_ref[...])
    for part in range(PROJ_PARTS):
        rows = slice(part * hm, (part + 1) * hm)
        c128, s128 = c128_ref[rows, :], s128_ref[rows, :]
        c64, sa64, sb64 = c64_ref[rows, :], sa64_ref[rows, :], sb64_ref[rows, :]
        for hd in range(A_HEADS):
            sl = slice(hd * 128, (hd + 1) * 128)
            qh = _rope128(_rms(y_ref[part, :, sl], qn_ref[...], A_HEAD_DIM), c128, s128)
            q_ref[rows, sl] = (qh * A_SCALE).astype(BF16)
        for hd in range(A_KV_HEADS):
            sl = slice(hd * 128, (hd + 1) * 128)
            kh = _rope128(_rms(y_ref[part, :, 1024 + hd * 128:1024 + (hd + 1) * 128], kn_ref[...], A_HEAD_DIM),
                          c128, s128)
            k32_ref[rows, sl] = kh
            kbf_ref[rows, sl] = kh.astype(BF16)
        v = y_ref[part, :, 1280:1536]
        v32_ref[rows, :] = v
        vbf_ref[rows, :] = v.astype(BF16)
        for p in range(IDX_HEADS // 2):
            sl = slice(p * 128, (p + 1) * 128)
            qi_ref[rows, sl] = _rope64(y_ref[part, :, 1536 + p * 128:1536 + (p + 1) * 128],
                                       c64, sa64, sb64).astype(BF16)
        tail = y_ref[part, :, 2048:2176]
        lane = lax.broadcasted_iota(I32, tail.shape, 1)
        low = lane < IDX_DIM
        kin = jnp.where(low, tail, 0.0)
        ki = _rope64(_rms(kin, ikn_ref[...], IDX_DIM), c64, sa64, sb64)
        ki = jnp.where(low, ki, 0.0)
        ki32_ref[rows, :] = ki[:, :IDX_DIM]
        kidx_ref[rows, 0:128] = ki.astype(BF16)
        kidx_ref[rows, 128:256] = pltpu.roll(ki, 64, 1).astype(BF16)
        wi_ref[rows, :] = pltpu.roll(tail, 64, 1) * IDX_W_SCALE


def _tab_spec(tm, t):
    nt = t // tm
    return pl.BlockSpec((tm, LANES), lambda i: (i % nt, 0))


def _a_proj(x, g, w, qn, kn, ikn, tabs, tm):
    n = x.shape[0]
    t = tabs["c128"].shape[0]
    outs = [(1024, BF16), (256, F32), (256, F32), (256, BF16), (256, BF16), (512, BF16), (IDX_DIM, F32),
            (256, BF16), (LANES, F32)]
    return pl.pallas_call(
        _a_proj_body,
        grid=(n // tm,),
        in_specs=[_row_spec(tm, D_MODEL), _const_spec((1, D_MODEL)), _const_spec((D_MODEL, A_IN_PAD)),
                  _const_spec((1, 128)), _const_spec((1, 128)), _const_spec((1, 128))]
        + [_tab_spec(tm, t)] * 5,
        out_specs=[_row_spec(tm, c) for c, _ in outs],
        out_shape=[jax.ShapeDtypeStruct((n, c), dt) for c, dt in outs],
        scratch_shapes=[pltpu.VMEM((PROJ_PARTS, tm // PROJ_PARTS, A_IN_PAD), F32)],
        compiler_params=_cparams(1),
        name="a_proj",
    )(x, g, w, qn, kn, ikn, tabs["c128"], tabs["s128"], tabs["c64"], tabs["sa64"], tabs["sb64"])


def _b_proj_body(x_ref, g_ref, w_ref, qan_ref, kvan_ref, krn_ref, wuq_ref, qnn_ref, qrn_ref,
                 c64_ref, sa64_ref, sb64_ref, q_ref, ckv_ref, kr32_ref, krbf_ref, qq_ref):
    hm = x_ref.shape[0] // PROJ_PARTS
    for part in range(PROJ_PARTS):
        rows = slice(part * hm, (part + 1) * hm)
        h = _rms(x_ref[rows, :], g_ref[...], D_MODEL).astype(BF16)
        y = _dot(h, w_ref[...])
        ckv_ref[rows, :] = _rms(y[:, 384:640], kvan_ref[...], KV_LORA)
        kr = _rope64(_rms(y[:, 640:768], krn_ref[...], ROPE_DIM), c64_ref[rows, :], sa64_ref[rows, :],
                     sb64_ref[rows, :])
        kr32_ref[rows, :] = kr[:, :ROPE_DIM]
        krbf_ref[rows, :] = kr.astype(BF16)
        qq_ref[part] = _dot(_rms(y[:, 0:384], qan_ref[...], Q_LORA).astype(BF16), wuq_ref[...])
    for part in range(PROJ_PARTS):
        rows = slice(part * hm, (part + 1) * hm)
        c64, sa64, sb64 = c64_ref[rows, :], sa64_ref[rows, :], sb64_ref[rows, :]
        for hd in range(B_HEADS):
            qn = _rms(qq_ref[part, :, hd * 256:hd * 256 + 128], qnn_ref[...], NOPE_DIM)
            qr = _rope64(_rms(qq_ref[part, :, hd * 256 + 128:(hd + 1) * 256], qrn_ref[...], ROPE_DIM),
                         c64, sa64, sb64)
            q_ref[rows, hd * 256:hd * 256 + 128] = (qn * B_SCALE).astype(BF16)
            q_ref[rows, hd * 256 + 128:(hd + 1) * 256] = (qr * B_SCALE).astype(BF16)


def _b_proj(x, g, w, qan, kvan, krn, wuq, qnn, qrn, tabs, tm):
    n = x.shape[0]
    t = tabs["c64"].shape[0]
    outs = [(B_HEADS * 256, BF16), (KV_LORA, F32), (ROPE_DIM, F32), (LANES, BF16)]
    return pl.pallas_call(
        _b_proj_body,
        grid=(n // tm,),
        in_specs=[_row_spec(tm, D_MODEL), _const_spec((1, D_MODEL)), _const_spec((D_MODEL, B_IN_PAD)),
                  _const_spec((1, Q_LORA)), _const_spec((1, KV_LORA)), _const_spec((1, 128)),
                  _const_spec((Q_LORA, 2048)), _const_spec((1, 128)), _const_spec((1, 128))]
        + [_tab_spec(tm, t)] * 3,
        out_specs=[_row_spec(tm, c) for c, _ in outs],
        out_shape=[jax.ShapeDtypeStruct((n, c), dt) for c, dt in outs],
        scratch_shapes=[pltpu.VMEM((PROJ_PARTS, tm // PROJ_PARTS, 2048), F32)],
        compiler_params=_cparams(1),
        name="b_proj",
    )(x, g, w, qan, kvan, krn, wuq, qnn, qrn, tabs["c64"], tabs["sa64"], tabs["sb64"])


def _kv_up_body(ckv_ref, w_ref, knn_ref, kn_ref, v_ref):
    y = _dot(ckv_ref[...].astype(BF16), w_ref[...])
    for hd in range(B_HEADS):
        kn_ref[:, hd * 128:(hd + 1) * 128] = _rms(y[:, hd * 256:hd * 256 + 128], knn_ref[...], NOPE_DIM).astype(BF16)
        v_ref[:, hd * 128:(hd + 1) * 128] = y[:, hd * 256 + 128:(hd + 1) * 256].astype(BF16)


def _kv_up(ckv, w, knn, tm):
    n = ckv.shape[0]
    return pl.pallas_call(
        _kv_up_body,
        grid=(n // tm,),
        in_specs=[_row_spec(tm, KV_LORA), _const_spec((KV_LORA, 2048)), _const_spec((1, 128))],
        out_specs=[_row_spec(tm, 1024), _row_spec(tm, 1024)],
        out_shape=[jax.ShapeDtypeStruct((n, 1024), BF16)] * 2,
        compiler_params=_cparams(1),
        name="kv_up",
    )(ckv, w, knn)


def _c_proj_body(x_ref, g_ref, w_ref, qn_ref, kn_ref, c128_ref, s128_ref,
                 q_ref, k32_ref, v32_ref, kbf_ref, vbf_ref, y_ref):
    hm = x_ref.shape[0] // PROJ_PARTS
    for part in range(PROJ_PARTS):
        rows = slice(part * hm, (part + 1) * hm)
        y_ref[part] = _dot(_rms(x_ref[rows, :], g_ref[...], D_MODEL).astype(BF16), w_ref[...])
    for part in range(PROJ_PARTS):
        rows = slice(part * hm, (part + 1) * hm)
        c128, s128 = c128_ref[rows, :], s128_ref[rows, :]
        for hd in range(2 * C_HEADS):
            sl = slice(hd * 128, (hd + 1) * 128)
            qh = _rope128(_rms(y_ref[part, :, sl], qn_ref[...], C_DIM), c128, s128)
            q_ref[rows, sl] = (qh * C_SCALE).astype(BF16)
            kh = _rope128(_rms(y_ref[part, :, 1024 + hd * 128:1024 + (hd + 1) * 128], kn_ref[...], C_DIM),
                          c128, s128)
            k32_ref[rows, sl] = kh
            kbf_ref[rows, sl] = kh.astype(BF16)
        v = y_ref[part, :, 2048:3072]
        v32_ref[rows, :] = v
        vbf_ref[rows, :] = v.astype(BF16)


def _c_proj(x, g, w, qn, kn, tabs, tm):
    n = x.shape[0]
    t = tabs["c128"].shape[0]
    outs = [(1024, BF16), (1024, F32), (1024, F32), (1024, BF16), (1024, BF16)]
    return pl.pallas_call(
        _c_proj_body,
        grid=(n // tm,),
        in_specs=[_row_spec(tm, D_MODEL), _const_spec((1, D_MODEL)), _const_spec((D_MODEL, 3072)),
                  _const_spec((1, 128)), _const_spec((1, 128))] + [_tab_spec(tm, t)] * 2,
        out_specs=[_row_spec(tm, c) for c, _ in outs],
        out_shape=[jax.ShapeDtypeStruct((n, c), dt) for c, dt in outs],
        scratch_shapes=[pltpu.VMEM((PROJ_PARTS, tm // PROJ_PARTS, 3072), F32)],
        compiler_params=_cparams(1),
        name="c_proj",
    )(x, g, w, qn, kn, tabs["c128"], tabs["s128"])


def _n_key_blocks(i, tq, kb, q_off, s_valid):
    last_chunk = lax.shift_right_logical(q_off + (i + 1) * tq - 1, CHUNK_SHIFT)
    kend = jnp.minimum((last_chunk + 1) << CHUNK_SHIFT, s_valid)
    return lax.div(kend + kb - 1, jnp.int32(kb))


def _q_chunk(i, tq, q_off):
    row = lax.broadcasted_iota(I32, (tq, 1), 0)
    return lax.shift_right_logical(q_off + i * tq + row, CHUNK_SHIFT)


def _k_chunk(ks, kb, s_valid):
    kpos = ks + lax.broadcasted_iota(I32, (1, kb), 1)
    return jnp.where(kpos < s_valid, lax.shift_right_logical(kpos, CHUNK_SHIFT), 2 ** 30)


def _flash_step(s, v, m_ref, l_ref, acc_ref):
    m_prev = m_ref[...]
    m_new = jnp.maximum(m_prev, jnp.max(s, axis=1, keepdims=True))
    alpha = jnp.exp2(m_prev - m_new)
    p = jnp.exp2(s - m_new)
    l_ref[...] = alpha * l_ref[...] + jnp.sum(p, axis=1, keepdims=True)
    acc_ref[...] = alpha * acc_ref[...] + _dot(p.astype(BF16), v)
    m_ref[...] = m_new


def _key_limit(i, tq, q_off, s_valid):
    qpos = q_off + i * tq + lax.broadcasted_iota(I32, (1, tq), 1)
    return jnp.minimum((lax.shift_right_logical(qpos, CHUNK_SHIFT) + 1) << CHUNK_SHIFT, s_valid)


def _n_full_blocks(i, tq, kb, q_off, s_valid):
    first_chunk = lax.shift_right_logical(q_off + i * tq, CHUNK_SHIFT)
    kend = jnp.minimum((first_chunk + 1) << CHUNK_SHIFT, s_valid)
    return lax.shift_right_logical(kend, int(math.log2(kb)))


def _flash_init(m_ref, l_ref, acc_ref):
    m_ref[...] = jnp.full(m_ref.shape, NEG, F32)
    l_ref[...] = jnp.zeros(l_ref.shape, F32)
    acc_ref[...] = jnp.zeros(acc_ref.shape, F32)


def _dsa_body(q_ref, qi_ref, wi_ref, kidx_ref, k_ref, v_ref, o_ref, keys_ref, m_ref, l_ref, acc_ref,
              *, tq, kb, q_off, s_valid, topk):
    i = pl.program_id(1)
    n_blk = _n_key_blocks(i, tq, kb, q_off, s_valid)
    qc = _q_chunk(i, tq, q_off)
    wi = wi_ref[...]
    wcol = [wi[:, hd:hd + 1] for hd in range(IDX_HEADS)]

    def score_block(j, carry):
        ks = pl.multiple_of(j * kb, kb)
        k_lo = kidx_ref[pl.ds(ks, kb), 0:128]
        k_hi = kidx_ref[pl.ds(ks, kb), 128:256]
        sc = jnp.zeros((tq, kb), F32)
        for p in range(IDX_HEADS // 2):
            qp = qi_ref[:, p * 128:(p + 1) * 128]
            sc = sc + wcol[2 * p] * jnp.maximum(_dot_nt(qp, k_lo), 0.0)
            sc = sc + wcol[2 * p + 1] * jnp.maximum(_dot_nt(qp, k_hi), 0.0)
        bits = lax.bitcast_convert_type(sc, I32)
        key = bits ^ ((bits >> 31) & 0x7FFFFFFF)
        keys_ref[j] = jnp.where(_k_chunk(ks, kb, s_valid) <= qc, key, INT_MIN)
        return carry

    lax.fori_loop(0, n_blk, score_block, 0)

    def count(pred):
        def blk(j, c):
            x = jnp.where(pred(keys_ref[j], j * kb), 1.0, 0.0)
            part = x[:, 0:LANES]
            for g in range(1, kb // LANES):
                part = part + x[:, g * LANES:(g + 1) * LANES]
            return c + part
        c = lax.fori_loop(0, n_blk, blk, jnp.zeros((tq, LANES), F32))
        return jnp.sum(c, axis=1, keepdims=True)

    kf = float(topk)

    def bit_step(b, t):
        cand = t + jnp.left_shift(jnp.int32(1), 31 - b)
        cnt = count(lambda kblk, _: kblk >= cand)
        return jnp.where(cnt >= kf, cand, t)

    thr = lax.fori_loop(0, 32, bit_step, jnp.full((tq, 1), INT_MIN, I32))
    need = kf - count(lambda kblk, _: kblk > thr)
    n_eq = count(lambda kblk, _: kblk == thr)
    partial = jnp.logical_and(n_eq > need, thr != INT_MIN)

    @pl.when(jnp.max(jnp.where(partial, 1.0, 0.0)) > 0.0)
    def _():
        lane = lax.broadcasted_iota(I32, (1, kb), 1)

        def idx_step(b, x):
            cand = x + jnp.left_shift(jnp.int32(1), b)
            cnt = count(lambda kblk, k0: jnp.logical_and(kblk == thr, k0 + lane < cand))
            return jnp.where(cnt < need, cand, x)

        nbits = max(1, int(s_valid - 1).bit_length())
        cut = lax.fori_loop(0, nbits, lambda b, x: idx_step(nbits - 1 - b, x), jnp.zeros((tq, 1), I32))

        def demote(j, carry):
            kblk = keys_ref[j]
            drop = jnp.logical_and(jnp.logical_and(kblk == thr, j * kb + lane > cut), partial)
            keys_ref[j] = jnp.where(drop, kblk - 1, kblk)
            return carry

        lax.fori_loop(0, n_blk, demote, 0)

    thr_sel = jnp.maximum(thr, INT_MIN + 1)

    _flash_init(m_ref, l_ref, acc_ref)

    def attend(j, carry):
        ks = pl.multiple_of(j * kb, kb)
        bias = jnp.where(keys_ref[j] >= thr_sel, 0.0, NEG)
        for g in range(A_KV_HEADS):
            qg = jnp.concatenate([q_ref[:, (g * A_GROUP + a) * 128:(g * A_GROUP + a + 1) * 128]
                                  for a in range(A_GROUP)], axis=0)
            s = _dot_nt(qg, k_ref[pl.ds(ks, kb), g * 128:(g + 1) * 128])
            s = (s.reshape(A_GROUP, tq, kb) + bias[None]).reshape(A_GROUP * tq, kb)
            _flash_step(s, v_ref[pl.ds(ks, kb), g * 128:(g + 1) * 128], m_ref.at[g], l_ref.at[g], acc_ref.at[g])
        return carry

    lax.fori_loop(0, n_blk, attend, 0)
    for g in range(A_KV_HEADS):
        o = acc_ref[g] / l_ref[g]
        for a in range(A_GROUP):
            hd = g * A_GROUP + a
            o_ref[:, hd * 128:(hd + 1) * 128] = o[a * tq:(a + 1) * tq].astype(BF16)


def _dsa_attn(q, qi, wi, kidx, k, v, *, tq, kb, q_off, s_valid, topk):
    b, t, _ = q.shape
    s_pad = k.shape[1]
    qspec = lambda c: pl.BlockSpec((None, tq, c), lambda bi, i: (bi, i, 0))
    kspec = lambda c: pl.BlockSpec((None, s_pad, c), lambda bi, i: (bi, 0, 0))
    rows = A_GROUP * tq
    return pl.pallas_call(
        functools.partial(_dsa_body, tq=tq, kb=kb, q_off=q_off, s_valid=s_valid, topk=topk),
        grid=(b, t // tq),
        in_specs=[qspec(1024), qspec(512), qspec(LANES), kspec(256), kspec(256), kspec(256)],
        out_specs=qspec(1024),
        out_shape=jax.ShapeDtypeStruct((b, t, 1024), BF16),
        scratch_shapes=[pltpu.VMEM((s_pad // kb, tq, kb), I32), pltpu.VMEM((A_KV_HEADS, rows, 1), F32),
                        pltpu.VMEM((A_KV_HEADS, rows, 1), F32), pltpu.VMEM((A_KV_HEADS, rows, 128), F32)],
        compiler_params=_cparams(2),
        name="dsa_attn",
    )(q, qi, wi, kidx, k, v)


def _mla_body(q_ref, kn_ref, kr_ref, v_ref, o_ref, m_ref, l_ref, acc_ref, *, tq, kb, q_off, s_valid):
    i = pl.program_id(1)
    n_blk = _n_key_blocks(i, tq, kb, q_off, s_valid)
    qc = _q_chunk(i, tq, q_off)
    _flash_init(m_ref, l_ref, acc_ref)

    def attend(j, carry):
        ks = pl.multiple_of(j * kb, kb)
        kr = kr_ref[pl.ds(ks, kb), :]
        ok = _k_chunk(ks, kb, s_valid) <= qc
        for hd in range(B_HEADS):
            kcat = jnp.concatenate([kn_ref[pl.ds(ks, kb), hd * 128:(hd + 1) * 128], kr], axis=1)
            s = jnp.where(ok, _dot_nt(q_ref[:, hd * 256:(hd + 1) * 256], kcat), NEG)
            _flash_step(s, v_ref[pl.ds(ks, kb), hd * 128:(hd + 1) * 128], m_ref.at[hd], l_ref.at[hd], acc_ref.at[hd])
        return carry

    lax.fori_loop(0, n_blk, attend, 0)
    for hd in range(B_HEADS):
        o_ref[:, hd * 128:(hd + 1) * 128] = (acc_ref[hd] / l_ref[hd]).astype(BF16)


def _mla_attn(q, kn, kr, v, *, tq, kb, q_off, s_valid):
    b, t, _ = q.shape
    s_pad = kn.shape[1]
    return pl.pallas_call(
        functools.partial(_mla_body, tq=tq, kb=kb, q_off=q_off, s_valid=s_valid),
        grid=(b, t // tq),
        in_specs=[pl.BlockSpec((None, tq, B_HEADS * 256), lambda bi, i: (bi, i, 0)),
                  pl.BlockSpec((None, s_pad, B_HEADS * 128), lambda bi, i: (bi, 0, 0)),
                  pl.BlockSpec((None, s_pad, 128), lambda bi, i: (bi, 0, 0)),
                  pl.BlockSpec((None, s_pad, B_HEADS * 128), lambda bi, i: (bi, 0, 0))],
        out_specs=pl.BlockSpec((None, tq, B_HEADS * V_DIM), lambda bi, i: (bi, i, 0)),
        out_shape=jax.ShapeDtypeStruct((b, t, B_HEADS * V_DIM), BF16),
        scratch_shapes=[pltpu.VMEM((B_HEADS, tq, 1), F32), pltpu.VMEM((B_HEADS, tq, 1), F32),
                        pltpu.VMEM((B_HEADS, tq, 128), F32)],
        compiler_params=_cparams(2),
        name="mla_attn",
    )(q, kn, kr, v)


def _diff_body(q_ref, k_ref, v_ref, lq1_ref, lk1_ref, lq2_ref, lk2_ref, sn_ref, o_ref, m_ref, l_ref, acc_ref,
               *, tq, kb, q_off, s_valid, lam_init):
    i = pl.program_id(1)
    n_blk = _n_key_blocks(i, tq, kb, q_off, s_valid)
    qc = _q_chunk(i, tq, q_off)
    _flash_init(m_ref, l_ref, acc_ref)

    def attend(j, carry):
        ks = pl.multiple_of(j * kb, kb)
        ok = _k_chunk(ks, kb, s_valid) <= qc
        for hd in range(C_HEADS):
            v = v_ref[pl.ds(ks, kb), hd * 256:(hd + 1) * 256]
            for p in range(2):
                c = 2 * hd + p
                s = _dot_nt(q_ref[:, c * 128:(c + 1) * 128], k_ref[pl.ds(ks, kb), c * 128:(c + 1) * 128])
                _flash_step(jnp.where(ok, s, NEG), v, m_ref.at[c], l_ref.at[c], acc_ref.at[c])
        return carry

    lax.fori_loop(0, n_blk, attend, 0)
    lam = (jnp.exp(jnp.sum(lq1_ref[...] * lk1_ref[...], axis=1, keepdims=True))
           - jnp.exp(jnp.sum(lq2_ref[...] * lk2_ref[...], axis=1, keepdims=True)) + lam_init)
    for hd in range(C_HEADS):
        o = acc_ref[2 * hd] / l_ref[2 * hd] - lam * (acc_ref[2 * hd + 1] / l_ref[2 * hd + 1])
        o_ref[:, hd * 256:(hd + 1) * 256] = (_rms(o, sn_ref[...], 2 * C_DIM) * (1.0 - lam_init)).astype(BF16)


def _diff_attn(q, k, v, lq1, lk1, lq2, lk2, sn, *, tq, kb, q_off, s_valid, lam_init):
    b, t, c = q.shape
    s_pad = k.shape[1]
    vec = lambda n: pl.BlockSpec((1, n), lambda bi, i: (0, 0))
    return pl.pallas_call(
        functools.partial(_diff_body, tq=tq, kb=kb, q_off=q_off, s_valid=s_valid, lam_init=lam_init),
        grid=(b, t // tq),
        in_specs=[pl.BlockSpec((None, tq, c), lambda bi, i: (bi, i, 0)),
                  pl.BlockSpec((None, s_pad, c), lambda bi, i: (bi, 0, 0)),
                  pl.BlockSpec((None, s_pad, c), lambda bi, i: (bi, 0, 0)),
                  vec(128), vec(128), vec(128), vec(128), vec(256)],
        out_specs=pl.BlockSpec((None, tq, c), lambda bi, i: (bi, i, 0)),
        out_shape=jax.ShapeDtypeStruct((b, t, c), BF16),
        scratch_shapes=[pltpu.VMEM((2 * C_HEADS, tq, 1), F32), pltpu.VMEM((2 * C_HEADS, tq, 1), F32),
                        pltpu.VMEM((2 * C_HEADS, tq, 256), F32)],
        compiler_params=_cparams(2),
        name="diff_attn",
    )(q, k, v, lq1, lk1, lq2, lk2, sn)


SCORE_BLOCKS = 4
SEARCH_BLOCKS = 4
SEARCH_BITS_UNCHECKED = 28
SEARCH_BITS_PER_CHECK = 2


def _stage_bufs(n_chain, kb, r):
    return [pltpu.VMEM((n_chain, 2, kb, r), F32), pltpu.VMEM((n_chain, 2, kb, r), BF16),
            pltpu.VMEM((n_chain, 2, 1, r), F32)]


def _staged_flash_t(bufs, state, lo, hi, n_kb, qk, prep, vt_of, first=True, last=True):
    s_ref, p_ref, a_ref = bufs
    m_ref, l_ref, acc_ref = state
    n_chain = s_ref.shape[0]
    clamp = lambda j: jnp.clip(j, 0, n_kb - 1)

    def softmax(c, j, slot):
        st = prep(c, j, s_ref[c, slot])
        m_prev = m_ref[c]
        m_new = jnp.maximum(m_prev, jnp.max(st, axis=0, keepdims=True))
        alpha = jnp.exp2(m_prev - m_new)
        p = jnp.exp2(st - m_new)
        l_ref[c] = alpha * l_ref[c] + jnp.sum(p, axis=0, keepdims=True)
        m_ref[c] = m_new
        p_ref[c, slot] = p.astype(BF16)
        a_ref[c, slot] = alpha

    def values(c, j, slot):
        acc_ref[c] = a_ref[c, slot] * acc_ref[c] + _dot(vt_of(c, clamp(j)), p_ref[c, slot])

    if first:
        for c in range(n_chain):
            s_ref[c, 0] = qk(c, clamp(lo))
            p_ref[c, 1] = jnp.zeros(p_ref.shape[2:], BF16)
            a_ref[c, 1] = jnp.ones(a_ref.shape[2:], F32)

    def turn(t, carry):
        j0 = lo + 2 * t
        for c in range(n_chain):
            s_ref[c, 1] = qk(c, clamp(j0 + 1))
        for c in range(n_chain):
            softmax(c, j0, 0)
        for c in range(n_chain):
            values(c, j0 - 1, 1)
        for c in range(n_chain):
            s_ref[c, 0] = qk(c, clamp(j0 + 2))
        for c in range(n_chain):
            softmax(c, j0 + 1, 1)
        for c in range(n_chain):
            values(c, j0, 0)
        return carry

    n_turn = lax.shift_right_logical(hi - lo + 1, 1)
    lax.fori_loop(0, n_turn, turn, 0)
    if last:
        for c in range(n_chain):
            values(c, lo + 2 * n_turn - 1, 1)


def _dsa_t_body(q_ref, qi_ref, wi_ref, kidx_ref, k_ref, vt_ref, o_ref, keys_ref, sc_ref, s_ref, p_ref, a_ref,
                m_ref, l_ref, acc_ref, *, tq, kb, q_off, s_valid, topk):
    i = pl.program_id(1)
    n_blk = _n_key_blocks(i, tq, kb, q_off, s_valid)
    n_kb = keys_ref.shape[0]
    n_sb = lax.shift_right_logical(n_blk + SEARCH_BLOCKS - 1, int(math.log2(SEARCH_BLOCKS)))
    limit = _key_limit(i, tq, q_off, s_valid)
    kidx0 = lax.broadcasted_iota(I32, (kb, tq), 0)
    wit = wi_ref[...].T
    wrow = [wit[hd:hd + 1, :] for hd in range(IDX_HEADS)]

    def score_blocks(jj, carry):
        ks = pl.multiple_of(jj * (SCORE_BLOCKS * kb), SCORE_BLOCKS * kb)
        half = SCORE_BLOCKS // 2
        q_all = jnp.concatenate([qi_ref[:, p * 128:(p + 1) * 128] for p in range(IDX_HEADS // 2)], axis=0)
        for hf in range(2):
            rows = pl.ds(ks + hf * half * kb, half * kb)
            sc_ref[hf, 0] = _dot_nt(kidx_ref[rows, 0:128], q_all)
            sc_ref[hf, 1] = _dot_nt(kidx_ref[rows, 128:256], q_all)
        for hf in range(2):
            for u in range(half):
                blk = slice(u * kb, (u + 1) * kb)
                sc = jnp.zeros((kb, tq), F32)
                for p in range(IDX_HEADS // 2):
                    sc = sc + wrow[2 * p] * jnp.maximum(sc_ref[hf, 0, blk, p * tq:(p + 1) * tq], 0.0)
                    sc = sc + wrow[2 * p + 1] * jnp.maximum(sc_ref[hf, 1, blk, p * tq:(p + 1) * tq], 0.0)
                bits = lax.bitcast_convert_type(sc, I32)
                key = bits ^ ((bits >> 31) & 0x7FFFFFFF)
                j = jj * SCORE_BLOCKS + hf * half + u
                keys_ref[j] = jnp.where(kidx0 < limit - j * kb, key, INT_MIN)
        return carry

    def pad_block(j, carry):
        keys_ref[j] = jnp.full((kb, tq), INT_MIN, I32)
        return carry

    n_scored = lax.shift_right_logical(n_blk + SCORE_BLOCKS - 1, int(math.log2(SCORE_BLOCKS)))
    lax.fori_loop(0, n_scored, score_blocks, 0)
    lax.fori_loop(n_scored * SCORE_BLOCKS, n_sb * SEARCH_BLOCKS, pad_block, 0)

    def count(pred):
        def group(jj, cs):
            out = []
            for u in range(SEARCH_BLOCKS):
                j = jj * SEARCH_BLOCKS + u
                x = jnp.where(pred(keys_ref[j], j * kb), 1.0, 0.0).reshape(kb // 8, 8, tq)
                h = kb // 32
                out.append(cs[u] + ((jnp.sum(x[:h], axis=0) + jnp.sum(x[h:2 * h], axis=0))
                                    + (jnp.sum(x[2 * h:3 * h], axis=0) + jnp.sum(x[3 * h:], axis=0))))
            return tuple(out)
        cs = lax.fori_loop(0, n_sb, group, tuple(jnp.zeros((8, tq), F32) for _ in range(SEARCH_BLOCKS)))
        return jnp.sum(functools.reduce(lambda a, b: a + b, cs), axis=0, keepdims=True)

    kf = float(topk)

    def bit_step(b, state):
        t, n_ge = state
        cand = t + jnp.left_shift(jnp.int32(1), 31 - b)
        cnt = count(lambda kblk, _: kblk >= cand)
        take = cnt >= kf
        return jnp.where(take, cand, t), jnp.where(take, cnt, n_ge)

    def unsettled(n_ge):
        open_ = jnp.logical_and(n_ge != kf, limit.astype(F32) > kf)
        return jnp.max(jnp.where(open_, 1.0, 0.0)) > 0.0

    def more_bits(state):
        b, _, n_ge = state
        return jnp.logical_and(b < 32, unsettled(n_ge))

    def four_bits(state):
        b, t, n_ge = state
        for u in range(SEARCH_BITS_PER_CHECK):
            t, n_ge = bit_step(b + u, (t, n_ge))
        return b + SEARCH_BITS_PER_CHECK, t, n_ge

    start = (jnp.full((1, tq), INT_MIN, I32), jnp.full((1, tq), float(n_kb * kb), F32))
    head = lax.fori_loop(0, SEARCH_BITS_UNCHECKED, bit_step, start)
    _, thr, _ = lax.while_loop(more_bits, four_bits, (jnp.int32(SEARCH_BITS_UNCHECKED),) + head)
    need = kf - count(lambda kblk, _: kblk > thr)
    n_eq = count(lambda kblk, _: kblk == thr)
    partial = jnp.logical_and(n_eq > need, thr != INT_MIN)

    @pl.when(jnp.max(jnp.where(partial, 1.0, 0.0)) > 0.0)
    def _():
        def idx_step(b, x):
            cand = x + jnp.left_shift(jnp.int32(1), b)
            cnt = count(lambda kblk, k0: jnp.logical_and(kblk == thr, kidx0 < cand - k0))
            return jnp.where(cnt < need, cand, x)

        nbits = max(1, int(s_valid - 1).bit_length())
        cut = lax.fori_loop(0, nbits, lambda b, x: idx_step(nbits - 1 - b, x), jnp.zeros((1, tq), I32))

        def demote(j, carry):
            kblk = keys_ref[j]
            drop = jnp.logical_and(jnp.logical_and(kblk == thr, kidx0 > cut - j * kb), partial)
            keys_ref[j] = jnp.where(drop, kblk - 1, kblk)
            return carry

        lax.fori_loop(0, n_blk, demote, 0)

    thr_sel = jnp.maximum(thr, INT_MIN + 1)

    _flash_init(m_ref, l_ref, acc_ref)
    n_pair = A_HEADS // 2

    def to_bias(j, carry):
        keys_ref[j] = lax.bitcast_convert_type(jnp.where(keys_ref[j] >= thr_sel, 0.0, NEG), I32)
        return carry

    lax.fori_loop(0, n_sb * SEARCH_BLOCKS, to_bias, 0)
    for g in range(A_KV_HEADS):
        def qk(c, j, g=g):
            p = g * (A_GROUP // 2) + c
            qp = jnp.concatenate([q_ref[:, (2 * p) * 128:(2 * p + 1) * 128],
                                  q_ref[:, (2 * p + 1) * 128:(2 * p + 2) * 128]], axis=0)
            return _dot_nt(k_ref[pl.ds(pl.multiple_of(j * kb, kb), kb), g * 128:(g + 1) * 128], qp)

        def prep(c, j, st):
            bias = lax.bitcast_convert_type(keys_ref[j], F32)
            return st + jnp.concatenate([bias, bias], axis=1)

        def vt_of(c, j, g=g):
            return vt_ref[j, g * 128:(g + 1) * 128, :]

        chains = pl.ds(g * (A_GROUP // 2), A_GROUP // 2)
        _staged_flash_t((s_ref.at[chains], p_ref.at[chains], a_ref.at[chains]),
                        (m_ref.at[chains], l_ref.at[chains], acc_ref.at[chains]), 0, n_blk, n_kb, qk, prep, vt_of)
    for p in range(n_pair):
        ot = acc_ref[p] / l_ref[p]
        for a in range(2):
            hd = 2 * p + a
            o_ref[:, hd * 128:(hd + 1) * 128] = ot[:, a * tq:(a + 1) * tq].T.astype(BF16)


def _dsa_attn_t(q, qi, wi, kidx, k, vt, *, tq, kb, q_off, s_valid, topk):
    b, t, _ = q.shape
    s_pad = k.shape[1]
    n_kb = s_pad // kb
    qspec = lambda c: pl.BlockSpec((None, tq, c), lambda bi, i: (bi, i, 0))
    kspec = lambda c: pl.BlockSpec((None, s_pad, c), lambda bi, i: (bi, 0, 0))
    n_pair = A_HEADS // 2
    return pl.pallas_call(
        functools.partial(_dsa_t_body, tq=tq, kb=kb, q_off=q_off, s_valid=s_valid, topk=topk),
        grid=(b, t // tq),
        in_specs=[qspec(1024), qspec(512), qspec(LANES),
                  kspec(256), kspec(256), pl.BlockSpec((None, n_kb, 256, kb), lambda bi, i: (bi, 0, 0, 0))],
        out_specs=qspec(1024),
        out_shape=jax.ShapeDtypeStruct((b, t, 1024), BF16),
        scratch_shapes=[pltpu.VMEM((n_kb, kb, tq), I32),
                        pltpu.VMEM((2, 2, SCORE_BLOCKS // 2 * kb, IDX_HEADS // 2 * tq), F32)]
        + _stage_bufs(n_pair, kb, 2 * tq)
        + [pltpu.VMEM((n_pair, 1, 2 * tq), F32), pltpu.VMEM((n_pair, 1, 2 * tq), F32),
           pltpu.VMEM((n_pair, 128, 2 * tq), F32)],
        compiler_params=_cparams(2),
        name="dsa_attn_t",
    )(q, qi, wi, kidx, k, vt)


def _causal_flash_t(bufs, state, i, n_kb, tq, kb, q_off, s_valid, qk, vt_of):
    n_blk = _n_key_blocks(i, tq, kb, q_off, s_valid)
    n_plain = _n_full_blocks(i, tq, kb, q_off, s_valid) & -2
    limit = _key_limit(i, tq, q_off, s_valid)

    def masked(c, j, st):
        return jnp.where(lax.broadcasted_iota(I32, (kb, tq), 0) < limit - j * kb, st, NEG)

    _staged_flash_t(bufs, state, 0, n_plain, n_kb, qk, lambda c, j, st: st, vt_of, last=False)
    _staged_flash_t(bufs, state, n_plain, n_blk, n_kb, qk, masked, vt_of, first=False)


def _mla_t_body(qt_ref, kn_ref, kr_ref, vt_ref, o_ref, s_ref, p_ref, a_ref, m_ref, l_ref, acc_ref,
                *, tq, kb, q_off, s_valid):
    i = pl.program_id(2)
    _flash_init(m_ref, l_ref, acc_ref)

    def qk(a, j):
        ks = pl.multiple_of(j * kb, kb)
        kcat = jnp.concatenate([kn_ref[pl.ds(ks, kb), a * 128:(a + 1) * 128], kr_ref[pl.ds(ks, kb), :]], axis=1)
        return _dot(kcat, qt_ref[a * 256:(a + 1) * 256, :])

    def vt_of(a, j):
        return vt_ref[j, a * 128:(a + 1) * 128, :]

    _causal_flash_t((s_ref, p_ref, a_ref), (m_ref, l_ref, acc_ref), i, vt_ref.shape[0], tq, kb, q_off, s_valid,
                    qk, vt_of)
    for a in range(2):
        o_ref[:, a * 128:(a + 1) * 128] = (acc_ref[a] / l_ref[a]).T.astype(BF16)


def _mla_attn_t(qt, kn, kr, vt, *, tq, kb, q_off, s_valid):
    b, _, t = qt.shape
    s_pad = kn.shape[1]
    n_kb = s_pad // kb
    return pl.pallas_call(
        functools.partial(_mla_t_body, tq=tq, kb=kb, q_off=q_off, s_valid=s_valid),
        grid=(b, B_HEADS // 2, t // tq),
        in_specs=[pl.BlockSpec((None, 512, tq), lambda bi, h, i: (bi, h, i)),
                  pl.BlockSpec((None, s_pad, 256), lambda bi, h, i: (bi, 0, h)),
                  pl.BlockSpec((None, s_pad, 128), lambda bi, h, i: (bi, 0, 0)),
                  pl.BlockSpec((None, n_kb, 256, kb), lambda bi, h, i: (bi, 0, h, 0))],
        out_specs=pl.BlockSpec((None, tq, 256), lambda bi, h, i: (bi, i, h)),
        out_shape=jax.ShapeDtypeStruct((b, t, B_HEADS * V_DIM), BF16),
        scratch_shapes=_stage_bufs(2, kb, tq) + [pltpu.VMEM((2, 1, tq), F32), pltpu.VMEM((2, 1, tq), F32),
                                                 pltpu.VMEM((2, 128, tq), F32)],
        compiler_params=_cparams(3),
        name="mla_attn_t",
    )(qt, kn, kr, vt)


def _diff_t_body(q_ref, k_ref, vt_ref, lq1_ref, lk1_ref, lq2_ref, lk2_ref, sn_ref, o_ref, s_ref, p_ref, a_ref,
                 m_ref, l_ref, acc_ref, *, tq, kb, q_off, s_valid, lam_init):
    i = pl.program_id(2)
    _flash_init(m_ref, l_ref, acc_ref)

    def qk(p, j):
        ks = pl.multiple_of(j * kb, kb)
        return _dot_nt(k_ref[pl.ds(ks, kb), p * 128:(p + 1) * 128], q_ref[:, p * 128:(p + 1) * 128])

    _causal_flash_t((s_ref, p_ref, a_ref), (m_ref, l_ref, acc_ref), i, vt_ref.shape[0], tq, kb, q_off, s_valid,
                    qk, lambda p, j: vt_ref[j])
    lam = (jnp.exp(jnp.sum(lq1_ref[...] * lk1_ref[...], axis=1, keepdims=True))
           - jnp.exp(jnp.sum(lq2_ref[...] * lk2_ref[...], axis=1, keepdims=True)) + lam_init)
    ot = acc_ref[0] / l_ref[0] - lam * (acc_ref[1] / l_ref[1])
    o = jnp.concatenate([ot[0:128, :].T, ot[128:256, :].T], axis=1)
    o_ref[...] = (_rms(o, sn_ref[...], 2 * C_DIM) * (1.0 - lam_init)).astype(BF16)


def _diff_attn_t(q, k, vt, lq1, lk1, lq2, lk2, sn, *, tq, kb, q_off, s_valid, lam_init):
    b, t, _ = q.shape
    s_pad = k.shape[1]
    n_kb = s_pad // kb
    vec = lambda c: pl.BlockSpec((1, c), lambda bi, h, i: (0, 0))
    return pl.pallas_call(
        functools.partial(_diff_t_body, tq=tq, kb=kb, q_off=q_off, s_valid=s_valid, lam_init=lam_init),
        grid=(b, C_HEADS, t // tq),
        in_specs=[pl.BlockSpec((None, tq, 256), lambda bi, h, i: (bi, i, h)),
                  pl.BlockSpec((None, s_pad, 256), lambda bi, h, i: (bi, 0, h)),
                  pl.BlockSpec((None, n_kb, 256, kb), lambda bi, h, i: (bi, 0, h, 0)),
                  vec(128), vec(128), vec(128), vec(128), vec(256)],
        out_specs=pl.BlockSpec((None, tq, 256), lambda bi, h, i: (bi, i, h)),
        out_shape=jax.ShapeDtypeStruct((b, t, C_HEADS * 2 * C_DIM), BF16),
        scratch_shapes=_stage_bufs(2, kb, tq) + [pltpu.VMEM((2, 1, tq), F32), pltpu.VMEM((2, 1, tq), F32),
                                                 pltpu.VMEM((2, 256, tq), F32)],
        compiler_params=_cparams(3),
        name="diff_attn_t",
    )(q, k, vt, lq1, lk1, lq2, lk2, sn)


def _blocked_t(v, kb):
    b, s, c = v.shape
    return jnp.swapaxes(v.reshape(b, s // kb, kb, c), 2, 3)


def _rope_tables(pos, reps):
    p = pos.astype(F32)[:, None]
    inv64 = jnp.power(ROPE_THETA, -jnp.arange(64, dtype=F32) / 64)
    inv32 = jnp.power(ROPE_THETA, -jnp.arange(32, dtype=F32) / 32)
    c, s = jnp.cos(p * inv64), jnp.sin(p * inv64)
    c3, s3 = jnp.cos(p * inv32), jnp.sin(p * inv32)
    z = jnp.zeros_like(s3)
    tabs = {
        "c128": jnp.concatenate([c, c], axis=1),
        "s128": jnp.concatenate([-s, s], axis=1),
        "c64": jnp.concatenate([c3, c3, c3, c3], axis=1),
        "sa64": jnp.concatenate([-s3, z, -s3, z], axis=1),
        "sb64": jnp.concatenate([z, s3, z, s3], axis=1),
    }
    return {k: jnp.tile(v, (reps, 1)) for k, v in tabs.items()}


def _pad_cols(w, n):
    return jnp.pad(w, ((0, 0), (0, n - w.shape[1])))


def _pad_lanes(g, n=128):
    g = g.reshape(1, -1)
    return jnp.pad(g, ((0, 0), (0, n - g.shape[1])))


def _prep_weights(W):
    P = {}
    for nm in ("ffn1", "ffn2"):
        P[nm] = [(W[nm + "_norm"][i].reshape(1, -1), W[nm + "_wg"][i].astype(BF16), W[nm + "_wu"][i].astype(BF16),
                  W[nm + "_wd"][i].astype(BF16)) for i in range(DEPTH)]
    P["mix_norm"] = [W["mix_norm"][i].reshape(1, -1) for i in range(DEPTH)]
    P["a"] = [dict(w=_pad_cols(W["a_w_in"][j], A_IN_PAD).astype(BF16), qn=W["a_q_norm"][j].reshape(1, -1),
                   kn=W["a_k_norm"][j].reshape(1, -1), ikn=_pad_lanes(W["a_idx_k_norm"][j]),
                   wo=W["a_w_out"][j].astype(BF16)) for j in range(W["a_w_in"].shape[0])]
    P["b"] = []
    for j in range(W["b_w_in"].shape[0]):
        wuq = W["b_w_uq"][j].reshape(Q_LORA, B_HEADS, NOPE_DIM + ROPE_DIM)
        wuq_rope = jnp.pad(wuq[:, :, NOPE_DIM:], ((0, 0), (0, 0), (0, 128 - ROPE_DIM)))
        wuq = jnp.concatenate([wuq[:, :, :NOPE_DIM], wuq_rope], axis=2).reshape(Q_LORA, -1)
        P["b"].append(dict(
            w=_pad_cols(W["b_w_in"][j], B_IN_PAD).astype(BF16), qan=W["b_q_a_norm"][j].reshape(1, -1),
            kvan=W["b_kv_a_norm"][j].reshape(1, -1), krn=_pad_lanes(W["b_k_rope_norm"][j]), wuq=wuq.astype(BF16),
            qnn=W["b_q_nope_norm"][j].reshape(1, -1), qrn=_pad_lanes(W["b_q_rope_norm"][j]),
            wukv=W["b_w_ukv"][j].astype(BF16), knn=W["b_k_nope_norm"][j].reshape(1, -1),
            wo=W["b_w_out"][j].astype(BF16)))
    P["c"] = [dict(w=W["c_w_in"][j].astype(BF16), qn=W["c_q_norm"][j].reshape(1, -1),
                   kn=W["c_k_norm"][j].reshape(1, -1), lq1=W["c_lambda_q1"][j].reshape(1, -1),
                   lk1=W["c_lambda_k1"][j].reshape(1, -1), lq2=W["c_lambda_q2"][j].reshape(1, -1),
                   lk2=W["c_lambda_k2"][j].reshape(1, -1), sn=W["c_sub_norm"][j].reshape(1, -1),
                   wo=W["c_w_out"][j].astype(BF16)) for j in range(W["c_w_in"].shape[0])]
    return P


def _with_past(past, new, s_pad):
    x = new if past is None else jnp.concatenate([past.astype(new.dtype), new], axis=1)
    return x if x.shape[1] == s_pad else jnp.pad(x, ((0, 0), (0, s_pad - x.shape[1]), (0, 0)))


def _trunk(x, offset, past, P, cfg):
    b, t, _ = x.shape
    n = b * t
    tm, tq_a, tq, kb, key_major = cfg["tm"], cfg["tq_a"], cfg["tq"], cfg["kb"], cfg["key_major"]
    p_len = 0 if past is None else past["a_k"].shape[2]
    s_valid = p_len + t
    s_pad = -(-s_valid // kb) * kb
    tabs = _rope_tables(offset + jnp.arange(t, dtype=I32), tm // t if tm > t else 1)
    att = dict(kb=kb, q_off=offset, s_valid=s_valid)
    rows = {k: [] for k in ("a_k", "a_v", "a_ik", "b_ckv", "b_kr", "c_k", "c_v")}
    x = x.reshape(n, D_MODEL)
    r3 = lambda a: a.reshape(b, t, a.shape[-1])
    t3 = lambda a: jnp.swapaxes(r3(a), 1, 2)
    pj = lambda nm, j: None if past is None else past[nm][j].reshape(b, p_len, -1)
    for i in range(DEPTH):
        x = _ffn(x, *P["ffn1"][i], cfg["tm_ffn"])
        kind, j = i % N_MIXERS, i // N_MIXERS
        g = P["mix_norm"][i]
        if kind == 0:
            pa = P["a"][j]
            q, k32, v32, kbf, vbf, qi, ki32, kidx, wi = _a_proj(x, g, pa["w"], pa["qn"], pa["kn"], pa["ikn"], tabs, tm)
            rows["a_k"].append(k32.reshape(b, t, A_KV_HEADS, A_HEAD_DIM))
            rows["a_v"].append(v32.reshape(b, t, A_KV_HEADS, A_HEAD_DIM))
            rows["a_ik"].append(ki32.reshape(b, t, IDX_DIM))
            pik = pj("a_ik", j)
            if pik is not None:
                z = jnp.zeros_like(pik)
                pik = jnp.concatenate([pik, z, z, pik], axis=-1)
            kidx_all = _with_past(pik, r3(kidx), s_pad)
            k_all = _with_past(pj("a_k", j), r3(kbf), s_pad)
            v_all = _with_past(pj("a_v", j), r3(vbf), s_pad)
            topk = min(TOPK_MAX, s_valid // 4)
            if key_major:
                o = _dsa_attn_t(r3(q), r3(qi), r3(wi), kidx_all, k_all, _blocked_t(v_all, kb),
                                tq=tq_a, topk=topk, **att)
            else:
                o = _dsa_attn(r3(q), r3(qi), r3(wi), kidx_all, k_all, v_all, tq=tq_a, topk=topk, **att)
        elif kind == 1:
            pb = P["b"][j]
            q, ckv, kr32, krbf = _b_proj(x, g, pb["w"], pb["qan"], pb["kvan"], pb["krn"], pb["wuq"], pb["qnn"],
                                         pb["qrn"], tabs, tm)
            rows["b_ckv"].append(ckv.reshape(b, t, KV_LORA))
            rows["b_kr"].append(kr32.reshape(b, t, ROPE_DIM))
            ckv_all = _with_past(pj("b_ckv", j), r3(ckv), s_pad)
            pkr = pj("b_kr", j)
            if pkr is not None:
                pkr = jnp.concatenate([pkr, jnp.zeros_like(pkr)], axis=-1)
            kn, v = _kv_up(ckv_all.reshape(b * s_pad, KV_LORA), pb["wukv"], pb["knn"], math.gcd(b * s_pad, 512))
            kn, v, kr_all = kn.reshape(b, s_pad, -1), v.reshape(b, s_pad, -1), _with_past(pkr, r3(krbf), s_pad)
            if key_major:
                o = _mla_attn_t(t3(q), kn, kr_all, _blocked_t(v, kb), tq=cfg["tq_mla"], **att)
            else:
                o = _mla_attn(r3(q), kn, kr_all, v, tq=tq, **att)
        else:
            pc = P["c"][j]
            q, k32, v32, kbf, vbf = _c_proj(x, g, pc["w"], pc["qn"], pc["kn"], tabs, tm)
            rows["c_k"].append(k32.reshape(b, t, C_HEADS, 2, C_DIM))
            rows["c_v"].append(v32.reshape(b, t, C_HEADS, 2 * C_DIM))
            k_all = _with_past(pj("c_k", j), r3(kbf), s_pad)
            v_all = _with_past(pj("c_v", j), r3(vbf), s_pad)
            lam = (pc["lq1"], pc["lk1"], pc["lq2"], pc["lk2"], pc["sn"])
            lam_init = 0.8 - 0.6 * math.exp(-0.3 * i)
            if key_major:
                att_c = dict(att, kb=cfg["kb_diff"])
                o = _diff_attn_t(r3(q), k_all, _blocked_t(v_all, cfg["kb_diff"]), *lam, tq=tq, lam_init=lam_init,
                                 **att_c)
            else:
                o = _diff_attn(r3(q), k_all, v_all, *lam, tq=tq, lam_init=lam_init, **att)
        wo = (P["a"], P["b"], P["c"])[kind][j]["wo"]
        x = _ffn(x, *P["ffn2"][i], cfg["tm_ffn"], attn=o.reshape(n, -1), wo=wo)
    order = ("a_k", "a_v", "a_ik", "b_ckv", "b_kr", "c_k", "c_v")
    stack = lambda rs: rs[0][None] if len(rs) == 1 else jnp.stack(rs)
    return x.reshape(b, t, D_MODEL), tuple(stack(rows[k]) for k in order)


PROMPT_CFG = dict(tm=512, tm_ffn=1024, tq_a=128, tq=256, tq_mla=512, kb=256, kb_diff=512, key_major=True)
SAMPLE_CFG = dict(tm=128, tm_ffn=128, tq_a=16, tq=16, kb=1280, key_major=False)


@jax.jit
def _forward(x_prompt, x_sample, past, W):
    P = _prep_weights(W)
    y_p, rows_p = _trunk(x_prompt, 0, None, P, PROMPT_CFG)
    y_s, rows_s = _trunk(x_sample, past["a_k"].shape[2], past, P, SAMPLE_CFG)
    return (y_p, y_s) + rows_p + rows_s


def kernel(x_prompt, x_sample, cache_a_k, cache_a_v, cache_a_idx_k, cache_b_ckv, cache_b_krope, cache_c_k, cache_c_v, ffn1_norm, ffn1_wg, ffn1_wu, ffn1_wd, mix_norm, ffn2_norm, ffn2_wg, ffn2_wu, ffn2_wd, a_w_in, a_q_norm, a_k_norm, a_idx_k_norm, a_w_out, b_w_in, b_q_a_norm, b_kv_a_norm, b_w_uq, b_w_ukv, b_q_nope_norm, b_q_rope_norm, b_k_nope_norm, b_k_rope_norm, b_w_out, c_w_in, c_q_norm, c_k_norm, c_lambda_q1, c_lambda_k1, c_lambda_q2, c_lambda_k2, c_sub_norm, c_w_out):
    W = dict(ffn1_norm=ffn1_norm, ffn1_wg=ffn1_wg, ffn1_wu=ffn1_wu, ffn1_wd=ffn1_wd, mix_norm=mix_norm,
             ffn2_norm=ffn2_norm, ffn2_wg=ffn2_wg, ffn2_wu=ffn2_wu, ffn2_wd=ffn2_wd,
             a_w_in=a_w_in, a_q_norm=a_q_norm, a_k_norm=a_k_norm, a_idx_k_norm=a_idx_k_norm, a_w_out=a_w_out,
             b_w_in=b_w_in, b_q_a_norm=b_q_a_norm, b_kv_a_norm=b_kv_a_norm, b_w_uq=b_w_uq, b_w_ukv=b_w_ukv,
             b_q_nope_norm=b_q_nope_norm, b_q_rope_norm=b_q_rope_norm, b_k_nope_norm=b_k_nope_norm,
             b_k_rope_norm=b_k_rope_norm, b_w_out=b_w_out,
             c_w_in=c_w_in, c_q_norm=c_q_norm, c_k_norm=c_k_norm, c_lambda_q1=c_lambda_q1,
             c_lambda_k1=c_lambda_k1, c_lambda_q2=c_lambda_q2, c_lambda_k2=c_lambda_k2, c_sub_norm=c_sub_norm,
             c_w_out=c_w_out)
    past = dict(a_k=cache_a_k, a_v=cache_a_v, a_ik=cache_a_idx_k, b_ckv=cache_b_ckv, b_kr=cache_b_krope,
                c_k=cache_c_k, c_v=cache_c_v)
    return _forward(x_prompt, x_sample, past, W)
```

```python
import functools
import math

import jax
import jax.numpy as jnp
from jax import lax
from jax.experimental import pallas as pl
from jax.experimental.pallas import tpu as pltpu

F32 = jnp.float32
BF16 = jnp.bfloat16
I32 = jnp.int32

D_MODEL = 1024
DEPTH = 4
CHUNK_SHIFT = 6
N_MIXERS = 3
ROPE_THETA = 10000.0
EPS = 1e-6
D_FF = 2816

A_HEADS = 8
A_KV_HEADS = 2
A_GROUP = A_HEADS // A_KV_HEADS
A_HEAD_DIM = 128
IDX_HEADS = 8
IDX_DIM = 64
TOPK_MAX = 256
LOG2E = math.log2(math.e)
A_SCALE = A_HEAD_DIM ** -0.5 * LOG2E
IDX_W_SCALE = (IDX_HEADS * IDX_DIM) ** -0.5
A_IN = 2120
A_IN_PAD = 2176

B_HEADS = 8
Q_LORA = 384
KV_LORA = 256
NOPE_DIM = 128
ROPE_DIM = 64
V_DIM = 128
B_SCALE = (NOPE_DIM + ROPE_DIM) ** -0.5 * LOG2E
B_IN = 704
B_IN_PAD = 768

C_HEADS = 4
C_DIM = 128
C_SCALE = C_DIM ** -0.5 * LOG2E

LANES = 128
NEG = -1e30
INT_MIN = -(2 ** 31)
VMEM_LIMIT = 56 * 1024 * 1024


def _cparams(n_axes):
    return pltpu.CompilerParams(dimension_semantics=("arbitrary",) * n_axes, vmem_limit_bytes=VMEM_LIMIT)


def _dot(a, b):
    return jnp.dot(a, b, preferred_element_type=F32)


def _dot_nt(a, b):
    return lax.dot_general(a, b, (((1,), (1,)), ((), ())), preferred_element_type=F32)


def _rms(x, g, n):
    ms = jnp.sum(x * x, axis=-1, keepdims=True) * (1.0 / n)
    return x * lax.rsqrt(ms + EPS) * g


def _rope128(x, c, s):
    return x * c + pltpu.roll(x, 64, 1) * s


def _rope64(x, c, sa, sb):
    return x * c + pltpu.roll(x, 96, 1) * sa + pltpu.roll(x, 32, 1) * sb


FFN_CHUNK = 256


def _ffn_body(*refs, fc, mixed):
    if mixed:
        attn_ref, wo_ref, x_ref, g_ref, wg_ref, wu_ref, wd_ref, o_ref = refs
        x = x_ref[...] + _dot(attn_ref[...], wo_ref[...])
    else:
        x_ref, g_ref, wg_ref, wu_ref, wd_ref, o_ref = refs
        x = x_ref[...]
    h = _rms(x, g_ref[...], D_MODEL).astype(BF16)
    y = None
    for c in range(D_FF // fc):
        a = _dot(h, wg_ref[:, c * fc:(c + 1) * fc])
        u = _dot(h, wu_ref[:, c * fc:(c + 1) * fc])
        act = (a * jax.nn.sigmoid(a) * u).astype(BF16)
        part = _dot(act, wd_ref[c * fc:(c + 1) * fc, :])
        y = part if y is None else y + part
    o_ref[...] = x + 0.5 * y


def _const_spec(shape):
    nd = len(shape)
    return pl.BlockSpec(shape, lambda *_: (0,) * nd, pipeline_mode=pl.Buffered(1))


def _row_spec(tm, n):
    return pl.BlockSpec((tm, n), lambda i: (i, 0))


def _ffn(x, g, wg, wu, wd, tm, attn=None, wo=None):
    n = x.shape[0]
    mixed = attn is not None
    pre_specs = [_row_spec(tm, attn.shape[1]), _const_spec(wo.shape)] if mixed else []
    return pl.pallas_call(
        functools.partial(_ffn_body, fc=FFN_CHUNK, mixed=mixed),
        grid=(n // tm,),
        in_specs=pre_specs + [_row_spec(tm, D_MODEL), _const_spec((1, D_MODEL)), _const_spec((D_MODEL, D_FF)),
                              _const_spec((D_MODEL, D_FF)), _const_spec((D_FF, D_MODEL))],
        out_specs=_row_spec(tm, D_MODEL),
        out_shape=jax.ShapeDtypeStruct((n, D_MODEL), F32),
        compiler_params=_cparams(1),
        name="ffn_mixed" if mixed else "ffn",
    )(*((attn, wo) if mixed else ()), x, g, wg, wu, wd)


PROJ_PARTS = 2

def _a_proj_body(x_ref, g_ref, w_ref, qn_ref, kn_ref, ikn_ref, c128_ref, s128_ref, c64_ref, sa64_ref, sb64_ref,
                 q_ref, k32_ref, v32_ref, kbf_ref, vbf_ref, qi_ref, ki32_ref, kidx_ref, wi_ref, y_ref):
    hm = x_ref.shape[0] // PROJ_PARTS
    for part in range(PROJ_PARTS):
        rows = slice(part * hm, (part + 1) * hm)
        y_ref[part] = _dot(_rms(x_ref[rows, :], g_ref[...], D_MODEL).astype(BF16), w_ref[...])
    for part in range(PROJ_PARTS):
        rows = slice(part * hm, (part + 1) * hm)
        c128, s128 = c128_ref[rows, :], s128_ref[rows, :]
        c64, sa64, sb64 = c64_ref[rows, :], sa64_ref[rows, :], sb64_ref[rows, :]
        for hd in range(A_HEADS):
            sl = slice(hd * 128, (hd + 1) * 128)
            qh = _rope128(_rms(y_ref[part, :, sl], qn_ref[...], A_HEAD_DIM), c128, s128)
            q_ref[rows, sl] = (qh * A_SCALE).astype(BF16)
        for hd in range(A_KV_HEADS):
            sl = slice(hd * 128, (hd + 1) * 128)
            kh = _rope128(_rms(y_ref[part, :, 1024 + hd * 128:1024 + (hd + 1) * 128], kn_ref[...], A_HEAD_DIM),
                          c128, s128)
            k32_ref[rows, sl] = kh
            kbf_ref[rows, sl] = kh.astype(BF16)
        v = y_ref[part, :, 1280:1536]
        v32_ref[rows, :] = v
        vbf_ref[rows, :] = v.astype(BF16)
        for p in range(IDX_HEADS // 2):
            sl = slice(p * 128, (p + 1) * 128)
            qi_ref[rows, sl] = _rope64(y_ref[part, :, 1536 + p * 128:1536 + (p + 1) * 128],
                                       c64, sa64, sb64).astype(BF16)
        tail = y_ref[part, :, 2048:2176]
        lane = lax.broadcasted_iota(I32, tail.shape, 1)
        low = lane < IDX_DIM
        kin = jnp.where(low, tail, 0.0)
        ki = _rope64(_rms(kin, ikn_ref[...], IDX_DIM), c64, sa64, sb64)
        ki = jnp.where(low, ki, 0.0)
        ki32_ref[rows, :] = ki[:, :IDX_DIM]
        kidx_ref[rows, 0:128] = ki.astype(BF16)
        kidx_ref[rows, 128:256] = pltpu.roll(ki, 64, 1).astype(BF16)
        wi_ref[rows, :] = pltpu.roll(tail, 64, 1) * IDX_W_SCALE


def _tab_spec(tm, t):
    nt = t // tm
    return pl.BlockSpec((tm, LANES), lambda i: (i % nt, 0))


def _a_proj(x, g, w, qn, kn, ikn, tabs, tm):
    n = x.shape[0]
    t = tabs["c128"].shape[0]
    outs = [(1024, BF16), (256, F32), (256, F32), (256, BF16), (256, BF16), (512, BF16), (IDX_DIM, F32),
            (256, BF16), (LANES, F32)]
    return pl.pallas_call(
        _a_proj_body,
        grid=(n // tm,),
        in_specs=[_row_spec(tm, D_MODEL), _const_spec((1, D_MODEL)), _const_spec((D_MODEL, A_IN_PAD)),
                  _const_spec((1, 128)), _const_spec((1, 128)), _const_spec((1, 128))]
        + [_tab_spec(tm, t)] * 5,
        out_specs=[_row_spec(tm, c) for c, _ in outs],
        out_shape=[jax.ShapeDtypeStruct((n, c), dt) for c, dt in outs],
        scratch_shapes=[pltpu.VMEM((PROJ_PARTS, tm // PROJ_PARTS, A_IN_PAD), F32)],
        compiler_params=_cparams(1),
        name="a_proj",
    )(x, g, w, qn, kn, ikn, tabs["c128"], tabs["s128"], tabs["c64"], tabs["sa64"], tabs["sb64"])


def _b_proj_body(x_ref, g_ref, w_ref, qan_ref, kvan_ref, krn_ref, wuq_ref, qnn_ref, qrn_ref,
                 c64_ref, sa64_ref, sb64_ref, q_ref, ckv_ref, kr32_ref, krbf_ref, qq_ref):
    hm = x_ref.shape[0] // PROJ_PARTS
    for part in range(PROJ_PARTS):
        rows = slice(part * hm, (part + 1) * hm)
        h = _rms(x_ref[rows, :], g_ref[...], D_MODEL).astype(BF16)
        y = _dot(h, w_ref[...])
        ckv_ref[rows, :] = _rms(y[:, 384:640], kvan_ref[...], KV_LORA)
        kr = _rope64(_rms(y[:, 640:768], krn_ref[...], ROPE_DIM), c64_ref[rows, :], sa64_ref[rows, :],
                     sb64_ref[rows, :])
        kr32_ref[rows, :] = kr[:, :ROPE_DIM]
        krbf_ref[rows, :] = kr.astype(BF16)
        qq_ref[part] = _dot(_rms(y[:, 0:384], qan_ref[...], Q_LORA).astype(BF16), wuq_ref[...])
    for part in range(PROJ_PARTS):
        rows = slice(part * hm, (part + 1) * hm)
        c64, sa64, sb64 = c64_ref[rows, :], sa64_ref[rows, :], sb64_ref[rows, :]
        for hd in range(B_HEADS):
            qn = _rms(qq_ref[part, :, hd * 256:hd * 256 + 128], qnn_ref[...], NOPE_DIM)
            qr = _rope64(_rms(qq_ref[part, :, hd * 256 + 128:(hd + 1) * 256], qrn_ref[...], ROPE_DIM),
                         c64, sa64, sb64)
            q_ref[rows, hd * 256:hd * 256 + 128] = (qn * B_SCALE).astype(BF16)
            q_ref[rows, hd * 256 + 128:(hd + 1) * 256] = (qr * B_SCALE).astype(BF16)


def _b_proj(x, g, w, qan, kvan, krn, wuq, qnn, qrn, tabs, tm):
    n = x.shape[0]
    t = tabs["c64"].shape[0]
    outs = [(B_HEADS * 256, BF16), (KV_LORA, F32), (ROPE_DIM, F32), (LANES, BF16)]
    return pl.pallas_call(
        _b_proj_body,
        grid=(n // tm,),
        in_specs=[_row_spec(tm, D_MODEL), _const_spec((1, D_MODEL)), _const_spec((D_MODEL, B_IN_PAD)),
                  _const_spec((1, Q_LORA)), _const_spec((1, KV_LORA)), _const_spec((1, 128)),
                  _const_spec((Q_LORA, 2048)), _const_spec((1, 128)), _const_spec((1, 128))]
        + [_tab_spec(tm, t)] * 3,
        out_specs=[_row_spec(tm, c) for c, _ in outs],
        out_shape=[jax.ShapeDtypeStruct((n, c), dt) for c, dt in outs],
        scratch_shapes=[pltpu.VMEM((PROJ_PARTS, tm // PROJ_PARTS, 2048), F32)],
        compiler_params=_cparams(1),
        name="b_proj",
    )(x, g, w, qan, kvan, krn, wuq, qnn, qrn, tabs["c64"], tabs["sa64"], tabs["sb64"])


def _kv_up_body(ckv_ref, w_ref, knn_ref, kn_ref, v_ref):
    y = _dot(ckv_ref[...].astype(BF16), w_ref[...])
    for hd in range(B_HEADS):
        kn_ref[:, hd * 128:(hd + 1) * 128] = _rms(y[:, hd * 256:hd * 256 + 128], knn_ref[...], NOPE_DIM).astype(BF16)
        v_ref[:, hd * 128:(hd + 1) * 128] = y[:, hd * 256 + 128:(hd + 1) * 256].astype(BF16)


def _kv_up(ckv, w, knn, tm):
    n = ckv.shape[0]
    return pl.pallas_call(
        _kv_up_body,
        grid=(n // tm,),
        in_specs=[_row_spec(tm, KV_LORA), _const_spec((KV_LORA, 2048)), _const_spec((1, 128))],
        out_specs=[_row_spec(tm, 1024), _row_spec(tm, 1024)],
        out_shape=[jax.ShapeDtypeStruct((n, 1024), BF16)] * 2,
        compiler_params=_cparams(1),
        name="kv_up",
    )(ckv, w, knn)


def _c_proj_body(x_ref, g_ref, w_ref, qn_ref, kn_ref, c128_ref, s128_ref,
                 q_ref, k32_ref, v32_ref, kbf_ref, vbf_ref, y_ref):
    hm = x_ref.shape[0] // PROJ_PARTS
    for part in range(PROJ_PARTS):
        rows = slice(part * hm, (part + 1) * hm)
        y_ref[part] = _dot(_rms(x_ref[rows, :], g_ref[...], D_MODEL).astype(BF16), w_ref[...])
    for part in range(PROJ_PARTS):
        rows = slice(part * hm, (part + 1) * hm)
        c128, s128 = c128_ref[rows, :], s128_ref[rows, :]
        for hd in range(2 * C_HEADS):
            sl = slice(hd * 128, (hd + 1) * 128)
            qh = _rope128(_rms(y_ref[part, :, sl], qn_ref[...], C_DIM), c128, s128)
            q_ref[rows, sl] = (qh * C_SCALE).astype(BF16)
            kh = _rope128(_rms(y_ref[part, :, 1024 + hd * 128:1024 + (hd + 1) * 128], kn_ref[...], C_DIM),
                          c128, s128)
            k32_ref[rows, sl] = kh
            kbf_ref[rows, sl] = kh.astype(BF16)
        v = y_ref[part, :, 2048:3072]
        v32_ref[rows, :] = v
        vbf_ref[rows, :] = v.astype(BF16)


def _c_proj(x, g, w, qn, kn, tabs, tm):
    n = x.shape[0]
    t = tabs["c128"].shape[0]
    outs = [(1024, BF16), (1024, F32), (1024, F32), (1024, BF16), (1024, BF16)]
    return pl.pallas_call(
        _c_proj_body,
        grid=(n // tm,),
        in_specs=[_row_spec(tm, D_MODEL), _const_spec((1, D_MODEL)), _const_spec((D_MODEL, 3072)),
                  _const_spec((1, 128)), _const_spec((1, 128))] + [_tab_spec(tm, t)] * 2,
        out_specs=[_row_spec(tm, c) for c, _ in outs],
        out_shape=[jax.ShapeDtypeStruct((n, c), dt) for c, dt in outs],
        scratch_shapes=[pltpu.VMEM((PROJ_PARTS, tm // PROJ_PARTS, 3072), F32)],
        compiler_params=_cparams(1),
        name="c_proj",
    )(x, g, w, qn, kn, tabs["c128"], tabs["s128"])


def _n_key_blocks(i, tq, kb, q_off, s_valid):
    last_chunk = lax.shift_right_logical(q_off + (i + 1) * tq - 1, CHUNK_SHIFT)
    kend = jnp.minimum((last_chunk + 1) << CHUNK_SHIFT, s_valid)
    return lax.div(kend + kb - 1, jnp.int32(kb))


def _q_chunk(i, tq, q_off):
    row = lax.broadcasted_iota(I32, (tq, 1), 0)
    return lax.shift_right_logical(q_off + i * tq + row, CHUNK_SHIFT)


def _k_chunk(ks, kb, s_valid):
    kpos = ks + lax.broadcasted_iota(I32, (1, kb), 1)
    return jnp.where(kpos < s_valid, lax.shift_right_logical(kpos, CHUNK_SHIFT), 2 ** 30)


def _flash_step(s, v, m_ref, l_ref, acc_ref):
    m_prev = m_ref[...]
    m_new = jnp.maximum(m_prev, jnp.max(s, axis=1, keepdims=True))
    alpha = jnp.exp2(m_prev - m_new)
    p = jnp.exp2(s - m_new)
    l_ref[...] = alpha * l_ref[...] + jnp.sum(p, axis=1, keepdims=True)
    acc_ref[...] = alpha * acc_ref[...] + _dot(p.astype(BF16), v)
    m_ref[...] = m_new


def _key_limit(i, tq, q_off, s_valid):
    qpos = q_off + i * tq + lax.broadcasted_iota(I32, (1, tq), 1)
    return jnp.minimum((lax.shift_right_logical(qpos, CHUNK_SHIFT) + 1) << CHUNK_SHIFT, s_valid)


def _n_full_blocks(i, tq, kb, q_off, s_valid):
    first_chunk = lax.shift_right_logical(q_off + i * tq, CHUNK_SHIFT)
    kend = jnp.minimum((first_chunk + 1) << CHUNK_SHIFT, s_valid)
    return lax.shift_right_logical(kend, int(math.log2(kb)))


def _flash_init(m_ref, l_ref, acc_ref):
    m_ref[...] = jnp.full(m_ref.shape, NEG, F32)
    l_ref[...] = jnp.zeros(l_ref.shape, F32)
    acc_ref[...] = jnp.zeros(acc_ref.shape, F32)


def _dsa_body(q_ref, qi_ref, wi_ref, kidx_ref, k_ref, v_ref, o_ref, keys_ref, m_ref, l_ref, acc_ref,
              *, tq, kb, q_off, s_valid, topk):
    i = pl.program_id(1)
    n_blk = _n_key_blocks(i, tq, kb, q_off, s_valid)
    qc = _q_chunk(i, tq, q_off)
    wi = wi_ref[...]
    wcol = [wi[:, hd:hd + 1] for hd in range(IDX_HEADS)]

    def score_block(j, carry):
        ks = pl.multiple_of(j * kb, kb)
        k_lo = kidx_ref[pl.ds(ks, kb), 0:128]
        k_hi = kidx_ref[pl.ds(ks, kb), 128:256]
        sc = jnp.zeros((tq, kb), F32)
        for p in range(IDX_HEADS // 2):
            qp = qi_ref[:, p * 128:(p + 1) * 128]
            sc = sc + wcol[2 * p] * jnp.maximum(_dot_nt(qp, k_lo), 0.0)
            sc = sc + wcol[2 * p + 1] * jnp.maximum(_dot_nt(qp, k_hi), 0.0)
        bits = lax.bitcast_convert_type(sc, I32)
        key = bits ^ ((bits >> 31) & 0x7FFFFFFF)
        keys_ref[j] = jnp.where(_k_chunk(ks, kb, s_valid) <= qc, key, INT_MIN)
        return carry

    lax.fori_loop(0, n_blk, score_block, 0)

    def count(pred):
        def blk(j, c):
            x = jnp.where(pred(keys_ref[j], j * kb), 1.0, 0.0)
            part = x[:, 0:LANES]
            for g in range(1, kb // LANES):
                part = part + x[:, g * LANES:(g + 1) * LANES]
            return c + part
        c = lax.fori_loop(0, n_blk, blk, jnp.zeros((tq, LANES), F32))
        return jnp.sum(c, axis=1, keepdims=True)

    kf = float(topk)

    def bit_step(b, t):
        cand = t + jnp.left_shift(jnp.int32(1), 31 - b)
        cnt = count(lambda kblk, _: kblk >= cand)
        return jnp.where(cnt >= kf, cand, t)

    thr = lax.fori_loop(0, 32, bit_step, jnp.full((tq, 1), INT_MIN, I32))
    need = kf - count(lambda kblk, _: kblk > thr)
    n_eq = count(lambda kblk, _: kblk == thr)
    partial = jnp.logical_and(n_eq > need, thr != INT_MIN)

    @pl.when(jnp.max(jnp.where(partial, 1.0, 0.0)) > 0.0)
    def _():
        lane = lax.broadcasted_iota(I32, (1, kb), 1)

        def idx_step(b, x):
            cand = x + jnp.left_shift(jnp.int32(1), b)
            cnt = count(lambda kblk, k0: jnp.logical_and(kblk == thr, k0 + lane < cand))
            return jnp.where(cnt < need, cand, x)

        nbits = max(1, int(s_valid - 1).bit_length())
        cut = lax.fori_loop(0, nbits, lambda b, x: idx_step(nbits - 1 - b, x), jnp.zeros((tq, 1), I32))

        def demote(j, carry):
            kblk = keys_ref[j]
            drop = jnp.logical_and(jnp.logical_and(kblk == thr, j * kb + lane > cut), partial)
            keys_ref[j] = jnp.where(drop, kblk - 1, kblk)
            return carry

        lax.fori_loop(0, n_blk, demote, 0)

    thr_sel = jnp.maximum(thr, INT_MIN + 1)

    _flash_init(m_ref, l_ref, acc_ref)

    def attend(j, carry):
        ks = pl.multiple_of(j * kb, kb)
        bias = jnp.where(keys_ref[j] >= thr_sel, 0.0, NEG)
        for g in range(A_KV_HEADS):
            qg = jnp.concatenate([q_ref[:, (g * A_GROUP + a) * 128:(g * A_GROUP + a + 1) * 128]
                                  for a in range(A_GROUP)], axis=0)
            s = _dot_nt(qg, k_ref[pl.ds(ks, kb), g * 128:(g + 1) * 128])
            s = (s.reshape(A_GROUP, tq, kb) + bias[None]).reshape(A_GROUP * tq, kb)
            _flash_step(s, v_ref[pl.ds(ks, kb), g * 128:(g + 1) * 128], m_ref.at[g], l_ref.at[g], acc_ref.at[g])
        return carry

    lax.fori_loop(0, n_blk, attend, 0)
    for g in range(A_KV_HEADS):
        o = acc_ref[g] / l_ref[g]
        for a in range(A_GROUP):
            hd = g * A_GROUP + a
            o_ref[:, hd * 128:(hd + 1) * 128] = o[a * tq:(a + 1) * tq].astype(BF16)


def _dsa_attn(q, qi, wi, kidx, k, v, *, tq, kb, q_off, s_valid, topk):
    b, t, _ = q.shape
    s_pad = k.shape[1]
    qspec = lambda c: pl.BlockSpec((None, tq, c), lambda bi, i: (bi, i, 0))
    kspec = lambda c: pl.BlockSpec((None, s_pad, c), lambda bi, i: (bi, 0, 0))
    rows = A_GROUP * tq
    return pl.pallas_call(
        functools.partial(_dsa_body, tq=tq, kb=kb, q_off=q_off, s_valid=s_valid, topk=topk),
        grid=(b, t // tq),
        in_specs=[qspec(1024), qspec(512), qspec(LANES), kspec(256), kspec(256), kspec(256)],
        out_specs=qspec(1024),
        out_shape=jax.ShapeDtypeStruct((b, t, 1024), BF16),
        scratch_shapes=[pltpu.VMEM((s_pad // kb, tq, kb), I32), pltpu.VMEM((A_KV_HEADS, rows, 1), F32),
                        pltpu.VMEM((A_KV_HEADS, rows, 1), F32), pltpu.VMEM((A_KV_HEADS, rows, 128), F32)],
        compiler_params=_cparams(2),
        name="dsa_attn",
    )(q, qi, wi, kidx, k, v)


def _mla_body(q_ref, kn_ref, kr_ref, v_ref, o_ref, m_ref, l_ref, acc_ref, *, tq, kb, q_off, s_valid):
    i = pl.program_id(1)
    n_blk = _n_key_blocks(i, tq, kb, q_off, s_valid)
    qc = _q_chunk(i, tq, q_off)
    _flash_init(m_ref, l_ref, acc_ref)

    def attend(j, carry):
        ks = pl.multiple_of(j * kb, kb)
        kr = kr_ref[pl.ds(ks, kb), :]
        ok = _k_chunk(ks, kb, s_valid) <= qc
        for hd in range(B_HEADS):
            kcat = jnp.concatenate([kn_ref[pl.ds(ks, kb), hd * 128:(hd + 1) * 128], kr], axis=1)
            s = jnp.where(ok, _dot_nt(q_ref[:, hd * 256:(hd + 1) * 256], kcat), NEG)
            _flash_step(s, v_ref[pl.ds(ks, kb), hd * 128:(hd + 1) * 128], m_ref.at[hd], l_ref.at[hd], acc_ref.at[hd])
        return carry

    lax.fori_loop(0, n_blk, attend, 0)
    for hd in range(B_HEADS):
        o_ref[:, hd * 128:(hd + 1) * 128] = (acc_ref[hd] / l_ref[hd]).astype(BF16)


def _mla_attn(q, kn, kr, v, *, tq, kb, q_off, s_valid):
    b, t, _ = q.shape
    s_pad = kn.shape[1]
    return pl.pallas_call(
        functools.partial(_mla_body, tq=tq, kb=kb, q_off=q_off, s_valid=s_valid),
        grid=(b, t // tq),
        in_specs=[pl.BlockSpec((None, tq, B_HEADS * 256), lambda bi, i: (bi, i, 0)),
                  pl.BlockSpec((None, s_pad, B_HEADS * 128), lambda bi, i: (bi, 0, 0)),
                  pl.BlockSpec((None, s_pad, 128), lambda bi, i: (bi, 0, 0)),
                  pl.BlockSpec((None, s_pad, B_HEADS * 128), lambda bi, i: (bi, 0, 0))],
        out_specs=pl.BlockSpec((None, tq, B_HEADS * V_DIM), lambda bi, i: (bi, i, 0)),
        out_shape=jax.ShapeDtypeStruct((b, t, B_HEADS * V_DIM), BF16),
        scratch_shapes=[pltpu.VMEM((B_HEADS, tq, 1), F32), pltpu.VMEM((B_HEADS, tq, 1), F32),
                        pltpu.VMEM((B_HEADS, tq, 128), F32)],
        compiler_params=_cparams(2),
        name="mla_attn",
    )(q, kn, kr, v)


def _diff_body(q_ref, k_ref, v_ref, lq1_ref, lk1_ref, lq2_ref, lk2_ref, sn_ref, o_ref, m_ref, l_ref, acc_ref,
               *, tq, kb, q_off, s_valid, lam_init):
    i = pl.program_id(1)
    n_blk = _n_key_blocks(i, tq, kb, q_off, s_valid)
    qc = _q_chunk(i, tq, q_off)
    _flash_init(m_ref, l_ref, acc_ref)

    def attend(j, carry):
        ks = pl.multiple_of(j * kb, kb)
        ok = _k_chunk(ks, kb, s_valid) <= qc
        for hd in range(C_HEADS):
            v = v_ref[pl.ds(ks, kb), hd * 256:(hd + 1) * 256]
            for p in range(2):
                c = 2 * hd + p
                s = _dot_nt(q_ref[:, c * 128:(c + 1) * 128], k_ref[pl.ds(ks, kb), c * 128:(c + 1) * 128])
                _flash_step(jnp.where(ok, s, NEG), v, m_ref.at[c], l_ref.at[c], acc_ref.at[c])
        return carry

    lax.fori_loop(0, n_blk, attend, 0)
    lam = (jnp.exp(jnp.sum(lq1_ref[...] * lk1_ref[...], axis=1, keepdims=True))
           - jnp.exp(jnp.sum(lq2_ref[...] * lk2_ref[...], axis=1, keepdims=True)) + lam_init)
    for hd in range(C_HEADS):
        o = acc_ref[2 * hd] / l_ref[2 * hd] - lam * (acc_ref[2 * hd + 1] / l_ref[2 * hd + 1])
        o_ref[:, hd * 256:(hd + 1) * 256] = (_rms(o, sn_ref[...], 2 * C_DIM) * (1.0 - lam_init)).astype(BF16)


def _diff_attn(q, k, v, lq1, lk1, lq2, lk2, sn, *, tq, kb, q_off, s_valid, lam_init):
    b, t, c = q.shape
    s_pad = k.shape[1]
    vec = lambda n: pl.BlockSpec((1, n), lambda bi, i: (0, 0))
    return pl.pallas_call(
        functools.partial(_diff_body, tq=tq, kb=kb, q_off=q_off, s_valid=s_valid, lam_init=lam_init),
        grid=(b, t // tq),
        in_specs=[pl.BlockSpec((None, tq, c), lambda bi, i: (bi, i, 0)),
                  pl.BlockSpec((None, s_pad, c), lambda bi, i: (bi, 0, 0)),
                  pl.BlockSpec((None, s_pad, c), lambda bi, i: (bi, 0, 0)),
                  vec(128), vec(128), vec(128), vec(128), vec(256)],
        out_specs=pl.BlockSpec((None, tq, c), lambda bi, i: (bi, i, 0)),
        out_shape=jax.ShapeDtypeStruct((b, t, c), BF16),
        scratch_shapes=[pltpu.VMEM((2 * C_HEADS, tq, 1), F32), pltpu.VMEM((2 * C_HEADS, tq, 1), F32),
                        pltpu.VMEM((2 * C_HEADS, tq, 256), F32)],
        compiler_params=_cparams(2),
        name="diff_attn",
    )(q, k, v, lq1, lk1, lq2, lk2, sn)


SCORE_BLOCKS = 4
SEARCH_BLOCKS = 4
SEARCH_BITS_UNCHECKED = 26
SEARCH_BITS_PER_CHECK = 2


def _stage_bufs(n_chain, kb, r):
    return [pltpu.VMEM((n_chain, 2, kb, r), F32), pltpu.VMEM((n_chain, 2, kb, r), BF16),
            pltpu.VMEM((n_chain, 2, 1, r), F32)]


def _staged_flash_t(bufs, state, lo, hi, n_kb, qk, prep, vt_of, first=True, last=True):
    s_ref, p_ref, a_ref = bufs
    m_ref, l_ref, acc_ref = state
    n_chain = s_ref.shape[0]
    clamp = lambda j: jnp.clip(j, 0, n_kb - 1)

    def softmax(c, j, slot):
        st = prep(c, j, s_ref[c, slot])
        m_prev = m_ref[c]
        m_new = jnp.maximum(m_prev, jnp.max(st, axis=0, keepdims=True))
        alpha = jnp.exp2(m_prev - m_new)
        p = jnp.exp2(st - m_new)
        l_ref[c] = alpha * l_ref[c] + jnp.sum(p, axis=0, keepdims=True)
        m_ref[c] = m_new
        p_ref[c, slot] = p.astype(BF16)
        a_ref[c, slot] = alpha

    def values(c, j, slot):
        acc_ref[c] = a_ref[c, slot] * acc_ref[c] + _dot(vt_of(c, clamp(j)), p_ref[c, slot])

    if first:
        for c in range(n_chain):
            s_ref[c, 0] = qk(c, clamp(lo))
            p_ref[c, 1] = jnp.zeros(p_ref.shape[2:], BF16)
            a_ref[c, 1] = jnp.ones(a_ref.shape[2:], F32)

    def turn(t, carry):
        j0 = lo + 2 * t
        for c in range(n_chain):
            s_ref[c, 1] = qk(c, clamp(j0 + 1))
        for c in range(n_chain):
            softmax(c, j0, 0)
        for c in range(n_chain):
            values(c, j0 - 1, 1)
        for c in range(n_chain):
            s_ref[c, 0] = qk(c, clamp(j0 + 2))
        for c in range(n_chain):
            softmax(c, j0 + 1, 1)
        for c in range(n_chain):
            values(c, j0, 0)
        return carry

    n_turn = lax.shift_right_logical(hi - lo + 1, 1)
    lax.fori_loop(0, n_turn, turn, 0)
    if last:
        for c in range(n_chain):
            values(c, lo + 2 * n_turn - 1, 1)


def _dsa_t_body(q_ref, qi_ref, wi_ref, kidx_ref, k_ref, vt_ref, o_ref, keys_ref, sc_ref, s_ref, p_ref, a_ref,
                m_ref, l_ref, acc_ref, *, tq, kb, q_off, s_valid, topk):
    i = pl.program_id(1)
    n_blk = _n_key_blocks(i, tq, kb, q_off, s_valid)
    n_kb = keys_ref.shape[0]
    n_sb = lax.shift_right_logical(n_blk + SEARCH_BLOCKS - 1, int(math.log2(SEARCH_BLOCKS)))
    limit = _key_limit(i, tq, q_off, s_valid)
    kidx0 = lax.broadcasted_iota(I32, (kb, tq), 0)
    wit = wi_ref[...].T
    wrow = [wit[hd:hd + 1, :] for hd in range(IDX_HEADS)]

    def score_blocks(jj, carry):
        ks = pl.multiple_of(jj * (SCORE_BLOCKS * kb), SCORE_BLOCKS * kb)
        half = SCORE_BLOCKS // 2
        q_all = jnp.concatenate([qi_ref[:, p * 128:(p + 1) * 128] for p in range(IDX_HEADS // 2)], axis=0)
        for hf in range(2):
            rows = pl.ds(ks + hf * half * kb, half * kb)
            sc_ref[hf, 0] = _dot_nt(kidx_ref[rows, 0:128], q_all)
            sc_ref[hf, 1] = _dot_nt(kidx_ref[rows, 128:256], q_all)
        for hf in range(2):
            for u in range(half):
                blk = slice(u * kb, (u + 1) * kb)
                sc = jnp.zeros((kb, tq), F32)
                for p in range(IDX_HEADS // 2):
                    sc = sc + wrow[2 * p] * jnp.maximum(sc_ref[hf, 0, blk, p * tq:(p + 1) * tq], 0.0)
                    sc = sc + wrow[2 * p + 1] * jnp.maximum(sc_ref[hf, 1, blk, p * tq:(p + 1) * tq], 0.0)
                bits = lax.bitcast_convert_type(sc, I32)
                key = bits ^ ((bits >> 31) & 0x7FFFFFFF)
                j = jj * SCORE_BLOCKS + hf * half + u
                keys_ref[j] = jnp.where(kidx0 < limit - j * kb, key, INT_MIN)
        return carry

    def pad_block(j, carry):
        keys_ref[j] = jnp.full((kb, tq), INT_MIN, I32)
        return carry

    n_scored = lax.shift_right_logical(n_blk + SCORE_BLOCKS - 1, int(math.log2(SCORE_BLOCKS)))
    lax.fori_loop(0, n_scored, score_blocks, 0)
    lax.fori_loop(n_scored * SCORE_BLOCKS, n_sb * SEARCH_BLOCKS, pad_block, 0)

    def count(pred):
        def group(jj, cs):
            out = []
            for u in range(SEARCH_BLOCKS):
                j = jj * SEARCH_BLOCKS + u
                x = jnp.where(pred(keys_ref[j], j * kb), 1.0, 0.0).reshape(kb // 8, 8, tq)
                h = kb // 32
                out.append(cs[u] + ((jnp.sum(x[:h], axis=0) + jnp.sum(x[h:2 * h], axis=0))
                                    + (jnp.sum(x[2 * h:3 * h], axis=0) + jnp.sum(x[3 * h:], axis=0))))
            return tuple(out)
        cs = lax.fori_loop(0, n_sb, group, tuple(jnp.zeros((8, tq), F32) for _ in range(SEARCH_BLOCKS)))
        return jnp.sum(functools.reduce(lambda a, b: a + b, cs), axis=0, keepdims=True)

    kf = float(topk)

    def bit_step(b, state):
        t, n_ge = state
        cand = t + jnp.left_shift(jnp.int32(1), 31 - b)
        cnt = count(lambda kblk, _: kblk >= cand)
        take = cnt >= kf
        return jnp.where(take, cand, t), jnp.where(take, cnt, n_ge)

    def unsettled(n_ge):
        open_ = jnp.logical_and(n_ge != kf, limit.astype(F32) > kf)
        return jnp.max(jnp.where(open_, 1.0, 0.0)) > 0.0

    def more_bits(state):
        b, _, n_ge = state
        return jnp.logical_and(b < 32, unsettled(n_ge))

    def four_bits(state):
        b, t, n_ge = state
        for u in range(SEARCH_BITS_PER_CHECK):
            t, n_ge = bit_step(b + u, (t, n_ge))
        return b + SEARCH_BITS_PER_CHECK, t, n_ge

    start = (jnp.full((1, tq), INT_MIN, I32), jnp.full((1, tq), float(n_kb * kb), F32))
    head = lax.fori_loop(0, SEARCH_BITS_UNCHECKED, bit_step, start)
    _, thr, _ = lax.while_loop(more_bits, four_bits, (jnp.int32(SEARCH_BITS_UNCHECKED),) + head)
    need = kf - count(lambda kblk, _: kblk > thr)
    n_eq = count(lambda kblk, _: kblk == thr)
    partial = jnp.logical_and(n_eq > need, thr != INT_MIN)

    @pl.when(jnp.max(jnp.where(partial, 1.0, 0.0)) > 0.0)
    def _():
        def idx_step(b, x):
            cand = x + jnp.left_shift(jnp.int32(1), b)
            cnt = count(lambda kblk, k0: jnp.logical_and(kblk == thr, kidx0 < cand - k0))
            return jnp.where(cnt < need, cand, x)

        nbits = max(1, int(s_valid - 1).bit_length())
        cut = lax.fori_loop(0, nbits, lambda b, x: idx_step(nbits - 1 - b, x), jnp.zeros((1, tq), I32))

        def demote(j, carry):
            kblk = keys_ref[j]
            drop = jnp.logical_and(jnp.logical_and(kblk == thr, kidx0 > cut - j * kb), partial)
            keys_ref[j] = jnp.where(drop, kblk - 1, kblk)
            return carry

        lax.fori_loop(0, n_blk, demote, 0)

    thr_sel = jnp.maximum(thr, INT_MIN + 1)

    _flash_init(m_ref, l_ref, acc_ref)
    n_pair = A_HEADS // 2

    def to_bias(j, carry):
        keys_ref[j] = lax.bitcast_convert_type(jnp.where(keys_ref[j] >= thr_sel, 0.0, NEG), I32)
        return carry

    lax.fori_loop(0, n_sb * SEARCH_BLOCKS, to_bias, 0)
    for g in range(A_KV_HEADS):
        def qk(c, j, g=g):
            p = g * (A_GROUP // 2) + c
            qp = jnp.concatenate([q_ref[:, (2 * p) * 128:(2 * p + 1) * 128],
                                  q_ref[:, (2 * p + 1) * 128:(2 * p + 2) * 128]], axis=0)
            return _dot_nt(k_ref[pl.ds(pl.multiple_of(j * kb, kb), kb), g * 128:(g + 1) * 128], qp)

        def prep(c, j, st):
            bias = lax.bitcast_convert_type(keys_ref[j], F32)
            return st + jnp.concatenate([bias, bias], axis=1)

        def vt_of(c, j, g=g):
            return vt_ref[j, g * 128:(g + 1) * 128, :]

        chains = pl.ds(g * (A_GROUP // 2), A_GROUP // 2)
        _staged_flash_t((s_ref.at[chains], p_ref.at[chains], a_ref.at[chains]),
                        (m_ref.at[chains], l_ref.at[chains], acc_ref.at[chains]), 0, n_blk, n_kb, qk, prep, vt_of)
    for p in range(n_pair):
        ot = acc_ref[p] / l_ref[p]
        for a in range(2):
            hd = 2 * p + a
            o_ref[:, hd * 128:(hd + 1) * 128] = ot[:, a * tq:(a + 1) * tq].T.astype(BF16)


def _dsa_attn_t(q, qi, wi, kidx, k, vt, *, tq, kb, q_off, s_valid, topk):
    b, t, _ = q.shape
    s_pad = k.shape[1]
    n_kb = s_pad // kb
    qspec = lambda c: pl.BlockSpec((None, tq, c), lambda bi, i: (bi, i, 0))
    kspec = lambda c: pl.BlockSpec((None, s_pad, c), lambda bi, i: (bi, 0, 0))
    n_pair = A_HEADS // 2
    return pl.pallas_call(
        functools.partial(_dsa_t_body, tq=tq, kb=kb, q_off=q_off, s_valid=s_valid, topk=topk),
        grid=(b, t // tq),
        in_specs=[qspec(1024), qspec(512), qspec(LANES),
                  kspec(256), kspec(256), pl.BlockSpec((None, n_kb, 256, kb), lambda bi, i: (bi, 0, 0, 0))],
        out_specs=qspec(1024),
        out_shape=jax.ShapeDtypeStruct((b, t, 1024), BF16),
        scratch_shapes=[pltpu.VMEM((n_kb, kb, tq), I32),
                        pltpu.VMEM((2, 2, SCORE_BLOCKS // 2 * kb, IDX_HEADS // 2 * tq), F32)]
        + _stage_bufs(n_pair, kb, 2 * tq)
        + [pltpu.VMEM((n_pair, 1, 2 * tq), F32), pltpu.VMEM((n_pair, 1, 2 * tq), F32),
           pltpu.VMEM((n_pair, 128, 2 * tq), F32)],
        compiler_params=_cparams(2),
        name="dsa_attn_t",
    )(q, qi, wi, kidx, k, vt)


def _causal_flash_t(bufs, state, i, n_kb, tq, kb, q_off, s_valid, qk, vt_of):
    n_blk = _n_key_blocks(i, tq, kb, q_off, s_valid)
    n_plain = _n_full_blocks(i, tq, kb, q_off, s_valid) & -2
    limit = _key_limit(i, tq, q_off, s_valid)

    def masked(c, j, st):
        return jnp.where(lax.broadcasted_iota(I32, (kb, tq), 0) < limit - j * kb, st, NEG)

    _staged_flash_t(bufs, state, 0, n_plain, n_kb, qk, lambda c, j, st: st, vt_of, last=False)
    _staged_flash_t(bufs, state, n_plain, n_blk, n_kb, qk, masked, vt_of, first=False)


def _mla_t_body(qt_ref, kn_ref, kr_ref, vt_ref, o_ref, s_ref, p_ref, a_ref, m_ref, l_ref, acc_ref,
                *, tq, kb, q_off, s_valid):
    i = pl.program_id(2)
    _flash_init(m_ref, l_ref, acc_ref)

    def qk(a, j):
        ks = pl.multiple_of(j * kb, kb)
        kcat = jnp.concatenate([kn_ref[pl.ds(ks, kb), a * 128:(a + 1) * 128], kr_ref[pl.ds(ks, kb), :]], axis=1)
        return _dot(kcat, qt_ref[a * 256:(a + 1) * 256, :])

    def vt_of(a, j):
        return vt_ref[j, a * 128:(a + 1) * 128, :]

    _causal_flash_t((s_ref, p_ref, a_ref), (m_ref, l_ref, acc_ref), i, vt_ref.shape[0], tq, kb, q_off, s_valid,
                    qk, vt_of)
    for a in range(2):
        o_ref[:, a * 128:(a + 1) * 128] = (acc_ref[a] / l_ref[a]).T.astype(BF16)


def _mla_attn_t(qt, kn, kr, vt, *, tq, kb, q_off, s_valid):
    b, _, t = qt.shape
    s_pad = kn.shape[1]
    n_kb = s_pad // kb
    return pl.pallas_call(
        functools.partial(_mla_t_body, tq=tq, kb=kb, q_off=q_off, s_valid=s_valid),
        grid=(b, B_HEADS // 2, t // tq),
        in_specs=[pl.BlockSpec((None, 512, tq), lambda bi, h, i: (bi, h, i)),
                  pl.BlockSpec((None, s_pad, 256), lambda bi, h, i: (bi, 0, h)),
                  pl.BlockSpec((None, s_pad, 128), lambda bi, h, i: (bi, 0, 0)),
                  pl.BlockSpec((None, n_kb, 256, kb), lambda bi, h, i: (bi, 0, h, 0))],
        out_specs=pl.BlockSpec((None, tq, 256), lambda bi, h, i: (bi, i, h)),
        out_shape=jax.ShapeDtypeStruct((b, t, B_HEADS * V_DIM), BF16),
        scratch_shapes=_stage_bufs(2, kb, tq) + [pltpu.VMEM((2, 1, tq), F32), pltpu.VMEM((2, 1, tq), F32),
                                                 pltpu.VMEM((2, 128, tq), F32)],
        compiler_params=_cparams(3),
        name="mla_attn_t",
    )(qt, kn, kr, vt)


def _diff_t_body(q_ref, k_ref, vt_ref, lq1_ref, lk1_ref, lq2_ref, lk2_ref, sn_ref, o_ref, s_ref, p_ref, a_ref,
                 m_ref, l_ref, acc_ref, *, tq, kb, q_off, s_valid, lam_init):
    i = pl.program_id(2)
    _flash_init(m_ref, l_ref, acc_ref)

    def qk(p, j):
        ks = pl.multiple_of(j * kb, kb)
        return _dot_nt(k_ref[pl.ds(ks, kb), p * 128:(p + 1) * 128], q_ref[:, p * 128:(p + 1) * 128])

    _causal_flash_t((s_ref, p_ref, a_ref), (m_ref, l_ref, acc_ref), i, vt_ref.shape[0], tq, kb, q_off, s_valid,
                    qk, lambda p, j: vt_ref[j])
    lam = (jnp.exp(jnp.sum(lq1_ref[...] * lk1_ref[...], axis=1, keepdims=True))
           - jnp.exp(jnp.sum(lq2_ref[...] * lk2_ref[...], axis=1, keepdims=True)) + lam_init)
    ot = acc_ref[0] / l_ref[0] - lam * (acc_ref[1] / l_ref[1])
    o = jnp.concatenate([ot[0:128, :].T, ot[128:256, :].T], axis=1)
    o_ref[...] = (_rms(o, sn_ref[...], 2 * C_DIM) * (1.0 - lam_init)).astype(BF16)


def _diff_attn_t(q, k, vt, lq1, lk1, lq2, lk2, sn, *, tq, kb, q_off, s_valid, lam_init):
    b, t, _ = q.shape
    s_pad = k.shape[1]
    n_kb = s_pad // kb
    vec = lambda c: pl.BlockSpec((1, c), lambda bi, h, i: (0, 0))
    return pl.pallas_call(
        functools.partial(_diff_t_body, tq=tq, kb=kb, q_off=q_off, s_valid=s_valid, lam_init=lam_init),
        grid=(b, C_HEADS, t // tq),
        in_specs=[pl.BlockSpec((None, tq, 256), lambda bi, h, i: (bi, i, h)),
                  pl.BlockSpec((None, s_pad, 256), lambda bi, h, i: (bi, 0, h)),
                  pl.BlockSpec((None, n_kb, 256, kb), lambda bi, h, i: (bi, 0, h, 0)),
                  vec(128), vec(128), vec(128), vec(128), vec(256)],
        out_specs=pl.BlockSpec((None, tq, 256), lambda bi, h, i: (bi, i, h)),
        out_shape=jax.ShapeDtypeStruct((b, t, C_HEADS * 2 * C_DIM), BF16),
        scratch_shapes=_stage_bufs(2, kb, tq) + [pltpu.VMEM((2, 1, tq), F32), pltpu.VMEM((2, 1, tq), F32),
                                                 pltpu.VMEM((2, 256, tq), F32)],
        compiler_params=_cparams(3),
        name="diff_attn_t",
    )(q, k, vt, lq1, lk1, lq2, lk2, sn)


def _blocked_t(v, kb):
    b, s, c = v.shape
    return jnp.swapaxes(v.reshape(b, s // kb, kb, c), 2, 3)


def _rope_tables(pos, reps):
    p = pos.astype(F32)[:, None]
    inv64 = jnp.power(ROPE_THETA, -jnp.arange(64, dtype=F32) / 64)
    inv32 = jnp.power(ROPE_THETA, -jnp.arange(32, dtype=F32) / 32)
    c, s = jnp.cos(p * inv64), jnp.sin(p * inv64)
    c3, s3 = jnp.cos(p * inv32), jnp.sin(p * inv32)
    z = jnp.zeros_like(s3)
    tabs = {
        "c128": jnp.concatenate([c, c], axis=1),
        "s128": jnp.concatenate([-s, s], axis=1),
        "c64": jnp.concatenate([c3, c3, c3, c3], axis=1),
        "sa64": jnp.concatenate([-s3, z, -s3, z], axis=1),
        "sb64": jnp.concatenate([z, s3, z, s3], axis=1),
    }
    return {k: jnp.tile(v, (reps, 1)) for k, v in tabs.items()}


def _pad_cols(w, n):
    return jnp.pad(w, ((0, 0), (0, n - w.shape[1])))


def _pad_lanes(g, n=128):
    g = g.reshape(1, -1)
    return jnp.pad(g, ((0, 0), (0, n - g.shape[1])))


def _prep_weights(W):
    P = {}
    for nm in ("ffn1", "ffn2"):
        P[nm] = [(W[nm + "_norm"][i].reshape(1, -1), W[nm + "_wg"][i].astype(BF16), W[nm + "_wu"][i].astype(BF16),
                  W[nm + "_wd"][i].astype(BF16)) for i in range(DEPTH)]
    P["mix_norm"] = [W["mix_norm"][i].reshape(1, -1) for i in range(DEPTH)]
    P["a"] = [dict(w=_pad_cols(W["a_w_in"][j], A_IN_PAD).astype(BF16), qn=W["a_q_norm"][j].reshape(1, -1),
                   kn=W["a_k_norm"][j].reshape(1, -1), ikn=_pad_lanes(W["a_idx_k_norm"][j]),
                   wo=W["a_w_out"][j].astype(BF16)) for j in range(W["a_w_in"].shape[0])]
    P["b"] = []
    for j in range(W["b_w_in"].shape[0]):
        wuq = W["b_w_uq"][j].reshape(Q_LORA, B_HEADS, NOPE_DIM + ROPE_DIM)
        wuq_rope = jnp.pad(wuq[:, :, NOPE_DIM:], ((0, 0), (0, 0), (0, 128 - ROPE_DIM)))
        wuq = jnp.concatenate([wuq[:, :, :NOPE_DIM], wuq_rope], axis=2).reshape(Q_LORA, -1)
        P["b"].append(dict(
            w=_pad_cols(W["b_w_in"][j], B_IN_PAD).astype(BF16), qan=W["b_q_a_norm"][j].reshape(1, -1),
            kvan=W["b_kv_a_norm"][j].reshape(1, -1), krn=_pad_lanes(W["b_k_rope_norm"][j]), wuq=wuq.astype(BF16),
            qnn=W["b_q_nope_norm"][j].reshape(1, -1), qrn=_pad_lanes(W["b_q_rope_norm"][j]),
            wukv=W["b_w_ukv"][j].astype(BF16), knn=W["b_k_nope_norm"][j].reshape(1, -1),
            wo=W["b_w_out"][j].astype(BF16)))
    P["c"] = [dict(w=W["c_w_in"][j].astype(BF16), qn=W["c_q_norm"][j].reshape(1, -1),
                   kn=W["c_k_norm"][j].reshape(1, -1), lq1=W["c_lambda_q1"][j].reshape(1, -1),
                   lk1=W["c_lambda_k1"][j].reshape(1, -1), lq2=W["c_lambda_q2"][j].reshape(1, -1),
                   lk2=W["c_lambda_k2"][j].reshape(1, -1), sn=W["c_sub_norm"][j].reshape(1, -1),
                   wo=W["c_w_out"][j].astype(BF16)) for j in range(W["c_w_in"].shape[0])]
    return P


def _with_past(past, new, s_pad):
    x = new if past is None else jnp.concatenate([past.astype(new.dtype), new], axis=1)
    return x if x.shape[1] == s_pad else jnp.pad(x, ((0, 0), (0, s_pad - x.shape[1]), (0, 0)))


def _trunk(x, offset, past, P, cfg):
    b, t, _ = x.shape
    n = b * t
    tm, tq_a, tq, kb, key_major = cfg["tm"], cfg["tq_a"], cfg["tq"], cfg["kb"], cfg["key_major"]
    p_len = 0 if past is None else past["a_k"].shape[2]
    s_valid = p_len + t
    s_pad = -(-s_valid // kb) * kb
    tabs = _rope_tables(offset + jnp.arange(t, dtype=I32), tm // t if tm > t else 1)
    att = dict(kb=kb, q_off=offset, s_valid=s_valid)
    rows = {k: [] for k in ("a_k", "a_v", "a_ik", "b_ckv", "b_kr", "c_k", "c_v")}
    x = x.reshape(n, D_MODEL)
    r3 = lambda a: a.reshape(b, t, a.shape[-1])
    t3 = lambda a: jnp.swapaxes(r3(a), 1, 2)
    pj = lambda nm, j: None if past is None else past[nm][j].reshape(b, p_len, -1)
    for i in range(DEPTH):
        x = _ffn(x, *P["ffn1"][i], cfg["tm_ffn"])
        kind, j = i % N_MIXERS, i // N_MIXERS
        g = P["mix_norm"][i]
        if kind == 0:
            pa = P["a"][j]
            q, k32, v32, kbf, vbf, qi, ki32, kidx, wi = _a_proj(x, g, pa["w"], pa["qn"], pa["kn"], pa["ikn"], tabs, tm)
            rows["a_k"].append(k32.reshape(b, t, A_KV_HEADS, A_HEAD_DIM))
            rows["a_v"].append(v32.reshape(b, t, A_KV_HEADS, A_HEAD_DIM))
            rows["a_ik"].append(ki32.reshape(b, t, IDX_DIM))
            pik = pj("a_ik", j)
            if pik is not None:
                z = jnp.zeros_like(pik)
                pik = jnp.concatenate([pik, z, z, pik], axis=-1)
            kidx_all = _with_past(pik, r3(kidx), s_pad)
            k_all = _with_past(pj("a_k", j), r3(kbf), s_pad)
            v_all = _with_past(pj("a_v", j), r3(vbf), s_pad)
            topk = min(TOPK_MAX, s_valid // 4)
            if key_major:
                o = _dsa_attn_t(r3(q), r3(qi), r3(wi), kidx_all, k_all, _blocked_t(v_all, kb),
                                tq=tq_a, topk=topk, **att)
            else:
                o = _dsa_attn(r3(q), r3(qi), r3(wi), kidx_all, k_all, v_all, tq=tq_a, topk=topk, **att)
        elif kind == 1:
            pb = P["b"][j]
            q, ckv, kr32, krbf = _b_proj(x, g, pb["w"], pb["qan"], pb["kvan"], pb["krn"], pb["wuq"], pb["qnn"],
                                         pb["qrn"], tabs, tm)
            rows["b_ckv"].append(ckv.reshape(b, t, KV_LORA))
            rows["b_kr"].append(kr32.reshape(b, t, ROPE_DIM))
            ckv_all = _with_past(pj("b_ckv", j), r3(ckv), s_pad)
            pkr = pj("b_kr", j)
            if pkr is not None:
                pkr = jnp.concatenate([pkr, jnp.zeros_like(pkr)], axis=-1)
            kn, v = _kv_up(ckv_all.reshape(b * s_pad, KV_LORA), pb["wukv"], pb["knn"], math.gcd(b * s_pad, 512))
            kn, v, kr_all = kn.reshape(b, s_pad, -1), v.reshape(b, s_pad, -1), _with_past(pkr, r3(krbf), s_pad)
            if key_major:
                o = _mla_attn_t(t3(q), kn, kr_all, _blocked_t(v, kb), tq=cfg["tq_mla"], **att)
            else:
                o = _mla_attn(r3(q), kn, kr_all, v, tq=tq, **att)
        else:
            pc = P["c"][j]
            q, k32, v32, kbf, vbf = _c_proj(x, g, pc["w"], pc["qn"], pc["kn"], tabs, tm)
            rows["c_k"].append(k32.reshape(b, t, C_HEADS, 2, C_DIM))
            rows["c_v"].append(v32.reshape(b, t, C_HEADS, 2 * C_DIM))
            k_all = _with_past(pj("c_k", j), r3(kbf), s_pad)
            v_all = _with_past(pj("c_v", j), r3(vbf), s_pad)
            lam = (pc["lq1"], pc["lk1"], pc["lq2"], pc["lk2"], pc["sn"])
            lam_init = 0.8 - 0.6 * math.exp(-0.3 * i)
            if key_major:
                att_c = dict(att, kb=cfg["kb_diff"])
                o = _diff_attn_t(r3(q), k_all, _blocked_t(v_all, cfg["kb_diff"]), *lam, tq=tq, lam_init=lam_init,
                                 **att_c)
            else:
                o = _diff_attn(r3(q), k_all, v_all, *lam, tq=tq, lam_init=lam_init, **att)
        wo = (P["a"], P["b"], P["c"])[kind][j]["wo"]
        x = _ffn(x, *P["ffn2"][i], cfg["tm_ffn"], attn=o.reshape(n, -1), wo=wo)
    order = ("a_k", "a_v", "a_ik", "b_ckv", "b_kr", "c_k", "c_v")
    stack = lambda rs: rs[0][None] if len(rs) == 1 else jnp.stack(rs)
    return x.reshape(b, t, D_MODEL), tuple(stack(rows[k]) for k in order)


PROMPT_CFG = dict(tm=512, tm_ffn=1024, tq_a=128, tq=256, tq_mla=512, kb=256, kb_diff=512, key_major=True)
SAMPLE_CFG = dict(tm=128, tm_ffn=128, tq_a=16, tq=16, kb=1280, key_major=False)


@jax.jit
def _forward(x_prompt, x_sample, past, W):
    P = _prep_weights(W)
    y_p, rows_p = _trunk(x_prompt, 0, None, P, PROMPT_CFG)
    y_s, rows_s = _trunk(x_sample, past["a_k"].shape[2], past, P, SAMPLE_CFG)
    return (y_p, y_s) + rows_p + rows_s


def kernel(x_prompt, x_sample, cache_a_k, cache_a_v, cache_a_idx_k, cache_b_ckv, cache_b_krope, cache_c_k, cache_c_v, ffn1_norm, ffn1_wg, ffn1_wu, ffn1_wd, mix_norm, ffn2_norm, ffn2_wg, ffn2_wu, ffn2_wd, a_w_in, a_q_norm, a_k_norm, a_idx_k_norm, a_w_out, b_w_in, b_q_a_norm, b_kv_a_norm, b_w_uq, b_w_ukv, b_q_nope_norm, b_q_rope_norm, b_k_nope_norm, b_k_rope_norm, b_w_out, c_w_in, c_q_norm, c_k_norm, c_lambda_q1, c_lambda_k1, c_lambda_q2, c_lambda_k2, c_sub_norm, c_w_out):
    W = dict(ffn1_norm=ffn1_norm, ffn1_wg=ffn1_wg, ffn1_wu=ffn1_wu, ffn1_wd=ffn1_wd, mix_norm=mix_norm,
             ffn2_norm=ffn2_norm, ffn2_wg=ffn2_wg, ffn2_wu=ffn2_wu, ffn2_wd=ffn2_wd,
             a_w_in=a_w_in, a_q_norm=a_q_norm, a_k_norm=a_k_norm, a_idx_k_norm=a_idx_k_norm, a_w_out=a_w_out,
             b_w_in=b_w_in, b_q_a_norm=b_q_a_norm, b_kv_a_norm=b_kv_a_norm, b_w_uq=b_w_uq, b_w_ukv=b_w_ukv,
             b_q_nope_norm=b_q_nope_norm, b_q_rope_norm=b_q_rope_norm, b_k_nope_norm=b_k_nope_norm,
             b_k_rope_norm=b_k_rope_norm, b_w_out=b_w_out,
             c_w_in=c_w_in, c_q_norm=c_q_norm, c_k_norm=c_k_norm, c_lambda_q1=c_lambda_q1,
             c_lambda_k1=c_lambda_k1, c_lambda_q2=c_lambda_q2, c_lambda_k2=c_lambda_k2, c_sub_norm=c_sub_norm,
             c_w_out=c_w_out)
    past = dict(a_k=cache_a_k, a_v=cache_a_v, a_ik=cache_a_idx_k, b_ckv=cache_b_ckv, b_kr=cache_b_krope,
                c_k=cache_c_k, c_v=cache_c_v)
    return _forward(x_prompt, x_sample, past, W)
```

```python
import functools
import math

import jax
import jax.numpy as jnp
from jax import lax
from jax.experimental import pallas as pl
from jax.experimental.pallas import tpu as pltpu

F32 = jnp.float32
BF16 = jnp.bfloat16
I32 = jnp.int32

D_MODEL = 1024
DEPTH = 4
CHUNK_SHIFT = 6
N_MIXERS = 3
ROPE_THETA = 10000.0
EPS = 1e-6
D_FF = 2816

A_HEADS = 8
A_KV_HEADS = 2
A_GROUP = A_HEADS // A_KV_HEADS
A_HEAD_DIM = 128
IDX_HEADS = 8
IDX_DIM = 64
TOPK_MAX = 256
LOG2E = math.log2(math.e)
A_SCALE = A_HEAD_DIM ** -0.5 * LOG2E
IDX_W_SCALE = (IDX_HEADS * IDX_DIM) ** -0.5
A_IN = 2120
A_IN_PAD = 2176

B_HEADS = 8
Q_LORA = 384
KV_LORA = 256
NOPE_DIM = 128
ROPE_DIM = 64
V_DIM = 128
B_SCALE = (NOPE_DIM + ROPE_DIM) ** -0.5 * LOG2E
B_IN = 704
B_IN_PAD = 768

C_HEADS = 4
C_DIM = 128
C_SCALE = C_DIM ** -0.5 * LOG2E

LANES = 128
NEG = -1e30
INT_MIN = -(2 ** 31)
VMEM_LIMIT = 56 * 1024 * 1024


def _cparams(n_axes):
    return pltpu.CompilerParams(dimension_semantics=("arbitrary",) * n_axes, vmem_limit_bytes=VMEM_LIMIT)


def _dot(a, b):
    return jnp.dot(a, b, preferred_element_type=F32)


def _dot_nt(a, b):
    return lax.dot_general(a, b, (((1,), (1,)), ((), ())), preferred_element_type=F32)


def _rms(x, g, n):
    ms = jnp.sum(x * x, axis=-1, keepdims=True) * (1.0 / n)
    return x * lax.rsqrt(ms + EPS) * g


def _rope128(x, c, s):
    return x * c + pltpu.roll(x, 64, 1) * s


def _rope64(x, c, sa, sb):
    return x * c + pltpu.roll(x, 96, 1) * sa + pltpu.roll(x, 32, 1) * sb


FFN_CHUNK = 256


def _ffn_body(*refs, fc, mixed):
    if mixed:
        attn_ref, wo_ref, x_ref, g_ref, wg_ref, wu_ref, wd_ref, o_ref = refs
        x = x_ref[...] + _dot(attn_ref[...], wo_ref[...])
    else:
        x_ref, g_ref, wg_ref, wu_ref, wd_ref, o_ref = refs
        x = x_ref[...]
    h = _rms(x, g_ref[...], D_MODEL).astype(BF16)
    y = None
    for c in range(D_FF // fc):
        a = _dot(h, wg_ref[:, c * fc:(c + 1) * fc])
        u = _dot(h, wu_ref[:, c * fc:(c + 1) * fc])
        act = (a * jax.nn.sigmoid(a) * u).astype(BF16)
        part = _dot(act, wd_ref[c * fc:(c + 1) * fc, :])
        y = part if y is None else y + part
    o_ref[...] = x + 0.5 * y


def _const_spec(shape):
    nd = len(shape)
    return pl.BlockSpec(shape, lambda *_: (0,) * nd, pipeline_mode=pl.Buffered(1))


def _row_spec(tm, n):
    return pl.BlockSpec((tm, n), lambda i: (i, 0))


def _ffn(x, g, wg, wu, wd, tm, attn=None, wo=None):
    n = x.shape[0]
    mixed = attn is not None
    pre_specs = [_row_spec(tm, attn.shape[1]), _const_spec(wo.shape)] if mixed else []
    return pl.pallas_call(
        functools.partial(_ffn_body, fc=FFN_CHUNK, mixed=mixed),
        grid=(n // tm,),
        in_specs=pre_specs + [_row_spec(tm, D_MODEL), _const_spec((1, D_MODEL)), _const_spec((D_MODEL, D_FF)),
                              _const_spec((D_MODEL, D_FF)), _const_spec((D_FF, D_MODEL))],
        out_specs=_row_spec(tm, D_MODEL),
        out_shape=jax.ShapeDtypeStruct((n, D_MODEL), F32),
        compiler_params=_cparams(1),
        name="ffn_mixed" if mixed else "ffn",
    )(*((attn, wo) if mixed else ()), x, g, wg, wu, wd)


PROJ_PARTS = 2

def _a_proj_body(x_ref, g_ref, w_ref, qn_ref, kn_ref, ikn_ref, c128_ref, s128_ref, c64_ref, sa64_ref, sb64_ref,
                 q_ref, k32_ref, v32_ref, kbf_ref, vbf_ref, qi_ref, ki32_ref, kidx_ref, wi_ref, y_ref):
    hm = x_ref.shape[0] // PROJ_PARTS
    for part in range(PROJ_PARTS):
        rows = slice(part * hm, (part + 1) * hm)
        y_ref[part] = _dot(_rms(x_ref[rows, :], g_ref[...], D_MODEL).astype(BF16), w_ref[...])
    for part in range(PROJ_PARTS):
        rows = slice(part * hm, (part + 1) * hm)
        c128, s128 = c128_ref[rows, :], s128_ref[rows, :]
        c64, sa64, sb64 = c64_ref[rows, :], sa64_ref[rows, :], sb64_ref[rows, :]
        for hd in range(A_HEADS):
            sl = slice(hd * 128, (hd + 1) * 128)
            qh = _rope128(_rms(y_ref[part, :, sl], qn_ref[...], A_HEAD_DIM), c128, s128)
            q_ref[rows, sl] = (qh * A_SCALE).astype(BF16)
        for hd in range(A_KV_HEADS):
            sl = slice(hd * 128, (hd + 1) * 128)
            kh = _rope128(_rms(y_ref[part, :, 1024 + hd * 128:1024 + (hd + 1) * 128], kn_ref[...], A_HEAD_DIM),
                          c128, s128)
            k32_ref[rows, sl] = kh
            kbf_ref[rows, sl] = kh.astype(BF16)
        v = y_ref[part, :, 1280:1536]
        v32_ref[rows, :] = v
        vbf_ref[rows, :] = v.astype(BF16)
        for p in range(IDX_HEADS // 2):
            sl = slice(p * 128, (p + 1) * 128)
            qi_ref[rows, sl] = _rope64(y_ref[part, :, 1536 + p * 128:1536 + (p + 1) * 128],
                                       c64, sa64, sb64).astype(BF16)
        tail = y_ref[part, :, 2048:2176]
        lane = lax.broadcasted_iota(I32, tail.shape, 1)
        low = lane < IDX_DIM
        kin = jnp.where(low, tail, 0.0)
        ki = _rope64(_rms(kin, ikn_ref[...], IDX_DIM), c64, sa64, sb64)
        ki = jnp.where(low, ki, 0.0)
        ki32_ref[rows, :] = ki[:, :IDX_DIM]
        kidx_ref[rows, 0:128] = ki.astype(BF16)
        kidx_ref[rows, 128:256] = pltpu.roll(ki, 64, 1).astype(BF16)
        wi_ref[rows, :] = pltpu.roll(tail, 64, 1) * IDX_W_SCALE


def _tab_spec(tm, t):
    nt = t // tm
    return pl.BlockSpec((tm, LANES), lambda i: (i % nt, 0))


def _a_proj(x, g, w, qn, kn, ikn, tabs, tm):
    n = x.shape[0]
    t = tabs["c128"].shape[0]
    outs = [(1024, BF16), (256, F32), (256, F32), (256, BF16), (256, BF16), (512, BF16), (IDX_DIM, F32),
            (256, BF16), (LANES, F32)]
    return pl.pallas_call(
        _a_proj_body,
        grid=(n // tm,),
        in_specs=[_row_spec(tm, D_MODEL), _const_spec((1, D_MODEL)), _const_spec((D_MODEL, A_IN_PAD)),
                  _const_spec((1, 128)), _const_spec((1, 128)), _const_spec((1, 128))]
        + [_tab_spec(tm, t)] * 5,
        out_specs=[_row_spec(tm, c) for c, _ in outs],
        out_shape=[jax.ShapeDtypeStruct((n, c), dt) for c, dt in outs],
        scratch_shapes=[pltpu.VMEM((PROJ_PARTS, tm // PROJ_PARTS, A_IN_PAD), F32)],
        compiler_params=_cparams(1),
        name="a_proj",
    )(x, g, w, qn, kn, ikn, tabs["c128"], tabs["s128"], tabs["c64"], tabs["sa64"], tabs["sb64"])


def _b_proj_body(x_ref, g_ref, w_ref, qan_ref, kvan_ref, krn_ref, wuq_ref, qnn_ref, qrn_ref,
                 c64_ref, sa64_ref, sb64_ref, q_ref, ckv_ref, kr32_ref, krbf_ref, qq_ref):
    hm = x_ref.shape[0] // PROJ_PARTS
    for part in range(PROJ_PARTS):
        rows = slice(part * hm, (part + 1) * hm)
        h = _rms(x_ref[rows, :], g_ref[...], D_MODEL).astype(BF16)
        y = _dot(h, w_ref[...])
        ckv_ref[rows, :] = _rms(y[:, 384:640], kvan_ref[...], KV_LORA)
        kr = _rope64(_rms(y[:, 640:768], krn_ref[...], ROPE_DIM), c64_ref[rows, :], sa64_ref[rows, :],
                     sb64_ref[rows, :])
        kr32_ref[rows, :] = kr[:, :ROPE_DIM]
        krbf_ref[rows, :] = kr.astype(BF16)
        qq_ref[part] = _dot(_rms(y[:, 0:384], qan_ref[...], Q_LORA).astype(BF16), wuq_ref[...])
    for part in range(PROJ_PARTS):
        rows = slice(part * hm, (part + 1) * hm)
        c64, sa64, sb64 = c64_ref[rows, :], sa64_ref[rows, :], sb64_ref[rows, :]
        for hd in range(B_HEADS):
            qn = _rms(qq_ref[part, :, hd * 256:hd * 256 + 128], qnn_ref[...], NOPE_DIM)
            qr = _rope64(_rms(qq_ref[part, :, hd * 256 + 128:(hd + 1) * 256], qrn_ref[...], ROPE_DIM),
                         c64, sa64, sb64)
            q_ref[rows, hd * 256:hd * 256 + 128] = (qn * B_SCALE).astype(BF16)
            q_ref[rows, hd * 256 + 128:(hd + 1) * 256] = (qr * B_SCALE).astype(BF16)


def _b_proj(x, g, w, qan, kvan, krn, wuq, qnn, qrn, tabs, tm):
    n = x.shape[0]
    t = tabs["c64"].shape[0]
    outs = [(B_HEADS * 256, BF16), (KV_LORA, F32), (ROPE_DIM, F32), (LANES, BF16)]
    return pl.pallas_call(
        _b_proj_body,
        grid=(n // tm,),
        in_specs=[_row_spec(tm, D_MODEL), _const_spec((1, D_MODEL)), _const_spec((D_MODEL, B_IN_PAD)),
                  _const_spec((1, Q_LORA)), _const_spec((1, KV_LORA)), _const_spec((1, 128)),
                  _const_spec((Q_LORA, 2048)), _const_spec((1, 128)), _const_spec((1, 128))]
        + [_tab_spec(tm, t)] * 3,
        out_specs=[_row_spec(tm, c) for c, _ in outs],
        out_shape=[jax.ShapeDtypeStruct((n, c), dt) for c, dt in outs],
        scratch_shapes=[pltpu.VMEM((PROJ_PARTS, tm // PROJ_PARTS, 2048), F32)],
        compiler_params=_cparams(1),
        name="b_proj",
    )(x, g, w, qan, kvan, krn, wuq, qnn, qrn, tabs["c64"], tabs["sa64"], tabs["sb64"])


def _kv_up_body(ckv_ref, w_ref, knn_ref, kn_ref, v_ref):
    y = _dot(ckv_ref[...].astype(BF16), w_ref[...])
    for hd in range(B_HEADS):
        kn_ref[:, hd * 128:(hd + 1) * 128] = _rms(y[:, hd * 256:hd * 256 + 128], knn_ref[...], NOPE_DIM).astype(BF16)
        v_ref[:, hd * 128:(hd + 1) * 128] = y[:, hd * 256 + 128:(hd + 1) * 256].astype(BF16)


def _kv_up(ckv, w, knn, tm):
    n = ckv.shape[0]
    return pl.pallas_call(
        _kv_up_body,
        grid=(n // tm,),
        in_specs=[_row_spec(tm, KV_LORA), _const_spec((KV_LORA, 2048)), _const_spec((1, 128))],
        out_specs=[_row_spec(tm, 1024), _row_spec(tm, 1024)],
        out_shape=[jax.ShapeDtypeStruct((n, 1024), BF16)] * 2,
        compiler_params=_cparams(1),
        name="kv_up",
    )(ckv, w, knn)


def _c_proj_body(x_ref, g_ref, w_ref, qn_ref, kn_ref, c128_ref, s128_ref,
                 q_ref, k32_ref, v32_ref, kbf_ref, vbf_ref, y_ref):
    hm = x_ref.shape[0] // PROJ_PARTS
    for part in range(PROJ_PARTS):
        rows = slice(part * hm, (part + 1) * hm)
        y_ref[part] = _dot(_rms(x_ref[rows, :], g_ref[...], D_MODEL).astype(BF16), w_ref[...])
    for part in range(PROJ_PARTS):
        rows = slice(part * hm, (part + 1) * hm)
        c128, s128 = c128_ref[rows, :], s128_ref[rows, :]
        for hd in range(2 * C_HEADS):
            sl = slice(hd * 128, (hd + 1) * 128)
            qh = _rope128(_rms(y_ref[part, :, sl], qn_ref[...], C_DIM), c128, s128)
            q_ref[rows, sl] = (qh * C_SCALE).astype(BF16)
            kh = _rope128(_rms(y_ref[part, :, 1024 + hd * 128:1024 + (hd + 1) * 128], kn_ref[...], C_DIM),
                          c128, s128)
            k32_ref[rows, sl] = kh
            kbf_ref[rows, sl] = kh.astype(BF16)
        v = y_ref[part, :, 2048:3072]
        v32_ref[rows, :] = v
        vbf_ref[rows, :] = v.astype(BF16)


def _c_proj(x, g, w, qn, kn, tabs, tm):
    n = x.shape[0]
    t = tabs["c128"].shape[0]
    outs = [(1024, BF16), (1024, F32), (1024, F32), (1024, BF16), (1024, BF16)]
    return pl.pallas_call(
        _c_proj_body,
        grid=(n // tm,),
        in_specs=[_row_spec(tm, D_MODEL), _const_spec((1, D_MODEL)), _const_spec((D_MODEL, 3072)),
                  _const_spec((1, 128)), _const_spec((1, 128))] + [_tab_spec(tm, t)] * 2,
        out_specs=[_row_spec(tm, c) for c, _ in outs],
        out_shape=[jax.ShapeDtypeStruct((n, c), dt) for c, dt in outs],
        scratch_shapes=[pltpu.VMEM((PROJ_PARTS, tm // PROJ_PARTS, 3072), F32)],
        compiler_params=_cparams(1),
        name="c_proj",
    )(x, g, w, qn, kn, tabs["c128"], tabs["s128"])


def _n_key_blocks(i, tq, kb, q_off, s_valid):
    last_chunk = lax.shift_right_logical(q_off + (i + 1) * tq - 1, CHUNK_SHIFT)
    kend = jnp.minimum((last_chunk + 1) << CHUNK_SHIFT, s_valid)
    return lax.div(kend + kb - 1, jnp.int32(kb))


def _q_chunk(i, tq, q_off):
    row = lax.broadcasted_iota(I32, (tq, 1), 0)
    return lax.shift_right_logical(q_off + i * tq + row, CHUNK_SHIFT)


def _k_chunk(ks, kb, s_valid):
    kpos = ks + lax.broadcasted_iota(I32, (1, kb), 1)
    return jnp.where(kpos < s_valid, lax.shift_right_logical(kpos, CHUNK_SHIFT), 2 ** 30)


def _flash_step(s, v, m_ref, l_ref, acc_ref):
    m_prev = m_ref[...]
    m_new = jnp.maximum(m_prev, jnp.max(s, axis=1, keepdims=True))
    alpha = jnp.exp2(m_prev - m_new)
    p = jnp.exp2(s - m_new)
    l_ref[...] = alpha * l_ref[...] + jnp.sum(p, axis=1, keepdims=True)
    acc_ref[...] = alpha * acc_ref[...] + _dot(p.astype(BF16), v)
    m_ref[...] = m_new


def _key_limit(i, tq, q_off, s_valid):
    qpos = q_off + i * tq + lax.broadcasted_iota(I32, (1, tq), 1)
    return jnp.minimum((lax.shift_right_logical(qpos, CHUNK_SHIFT) + 1) << CHUNK_SHIFT, s_valid)


def _n_full_blocks(i, tq, kb, q_off, s_valid):
    first_chunk = lax.shift_right_logical(q_off + i * tq, CHUNK_SHIFT)
    kend = jnp.minimum((first_chunk + 1) << CHUNK_SHIFT, s_valid)
    return lax.shift_right_logical(kend, int(math.log2(kb)))


def _flash_init(m_ref, l_ref, acc_ref):
    m_ref[...] = jnp.full(m_ref.shape, NEG, F32)
    l_ref[...] = jnp.zeros(l_ref.shape, F32)
    acc_ref[...] = jnp.zeros(acc_ref.shape, F32)


def _dsa_body(q_ref, qi_ref, wi_ref, kidx_ref, k_ref, v_ref, o_ref, keys_ref, m_ref, l_ref, acc_ref,
              *, tq, kb, q_off, s_valid, topk):
    i = pl.program_id(1)
    n_blk = _n_key_blocks(i, tq, kb, q_off, s_valid)
    qc = _q_chunk(i, tq, q_off)
    wi = wi_ref[...]
    wcol = [wi[:, hd:hd + 1] for hd in range(IDX_HEADS)]

    def score_block(j, carry):
        ks = pl.multiple_of(j * kb, kb)
        k_lo = kidx_ref[pl.ds(ks, kb), 0:128]
        k_hi = kidx_ref[pl.ds(ks, kb), 128:256]
        sc = jnp.zeros((tq, kb), F32)
        for p in range(IDX_HEADS // 2):
            qp = qi_ref[:, p * 128:(p + 1) * 128]
            sc = sc + wcol[2 * p] * jnp.maximum(_dot_nt(qp, k_lo), 0.0)
            sc = sc + wcol[2 * p + 1] * jnp.maximum(_dot_nt(qp, k_hi), 0.0)
        bits = lax.bitcast_convert_type(sc, I32)
        key = bits ^ ((bits >> 31) & 0x7FFFFFFF)
        keys_ref[j] = jnp.where(_k_chunk(ks, kb, s_valid) <= qc, key, INT_MIN)
        return carry

    lax.fori_loop(0, n_blk, score_block, 0)

    def count(pred):
        def blk(j, c):
            x = jnp.where(pred(keys_ref[j], j * kb), 1.0, 0.0)
            part = x[:, 0:LANES]
            for g in range(1, kb // LANES):
                part = part + x[:, g * LANES:(g + 1) * LANES]
            return c + part
        c = lax.fori_loop(0, n_blk, blk, jnp.zeros((tq, LANES), F32))
        return jnp.sum(c, axis=1, keepdims=True)

    kf = float(topk)

    def bit_step(b, t):
        cand = t + jnp.left_shift(jnp.int32(1), 31 - b)
        cnt = count(lambda kblk, _: kblk >= cand)
        return jnp.where(cnt >= kf, cand, t)

    thr = lax.fori_loop(0, 32, bit_step, jnp.full((tq, 1), INT_MIN, I32))
    need = kf - count(lambda kblk, _: kblk > thr)
    n_eq = count(lambda kblk, _: kblk == thr)
    partial = jnp.logical_and(n_eq > need, thr != INT_MIN)

    @pl.when(jnp.max(jnp.where(partial, 1.0, 0.0)) > 0.0)
    def _():
        lane = lax.broadcasted_iota(I32, (1, kb), 1)

        def idx_step(b, x):
            cand = x + jnp.left_shift(jnp.int32(1), b)
            cnt = count(lambda kblk, k0: jnp.logical_and(kblk == thr, k0 + lane < cand))
            return jnp.where(cnt < need, cand, x)

        nbits = max(1, int(s_valid - 1).bit_length())
        cut = lax.fori_loop(0, nbits, lambda b, x: idx_step(nbits - 1 - b, x), jnp.zeros((tq, 1), I32))

        def demote(j, carry):
            kblk = keys_ref[j]
            drop = jnp.logical_and(jnp.logical_and(kblk == thr, j * kb + lane > cut), partial)
            keys_ref[j] = jnp.where(drop, kblk - 1, kblk)
            return carry

        lax.fori_loop(0, n_blk, demote, 0)

    thr_sel = jnp.maximum(thr, INT_MIN + 1)

    _flash_init(m_ref, l_ref, acc_ref)

    def attend(j, carry):
        ks = pl.multiple_of(j * kb, kb)
        bias = jnp.where(keys_ref[j] >= thr_sel, 0.0, NEG)
        for g in range(A_KV_HEADS):
            qg = jnp.concatenate([q_ref[:, (g * A_GROUP + a) * 128:(g * A_GROUP + a + 1) * 128]
                                  for a in range(A_GROUP)], axis=0)
            s = _dot_nt(qg, k_ref[pl.ds(ks, kb), g * 128:(g + 1) * 128])
            s = (s.reshape(A_GROUP, tq, kb) + bias[None]).reshape(A_GROUP * tq, kb)
            _flash_step(s, v_ref[pl.ds(ks, kb), g * 128:(g + 1) * 128], m_ref.at[g], l_ref.at[g], acc_ref.at[g])
        return carry

    lax.fori_loop(0, n_blk, attend, 0)
    for g in range(A_KV_HEADS):
        o = acc_ref[g] / l_ref[g]
        for a in range(A_GROUP):
            hd = g * A_GROUP + a
            o_ref[:, hd * 128:(hd + 1) * 128] = o[a * tq:(a + 1) * tq].astype(BF16)


def _dsa_attn(q, qi, wi, kidx, k, v, *, tq, kb, q_off, s_valid, topk):
    b, t, _ = q.shape
    s_pad = k.shape[1]
    qspec = lambda c: pl.BlockSpec((None, tq, c), lambda bi, i: (bi, i, 0))
    kspec = lambda c: pl.BlockSpec((None, s_pad, c), lambda bi, i: (bi, 0, 0))
    rows = A_GROUP * tq
    return pl.pallas_call(
        functools.partial(_dsa_body, tq=tq, kb=kb, q_off=q_off, s_valid=s_valid, topk=topk),
        grid=(b, t // tq),
        in_specs=[qspec(1024), qspec(512), qspec(LANES), kspec(256), kspec(256), kspec(256)],
        out_specs=qspec(1024),
        out_shape=jax.ShapeDtypeStruct((b, t, 1024), BF16),
        scratch_shapes=[pltpu.VMEM((s_pad // kb, tq, kb), I32), pltpu.VMEM((A_KV_HEADS, rows, 1), F32),
                        pltpu.VMEM((A_KV_HEADS, rows, 1), F32), pltpu.VMEM((A_KV_HEADS, rows, 128), F32)],
        compiler_params=_cparams(2),
        name="dsa_attn",
    )(q, qi, wi, kidx, k, v)


def _mla_body(q_ref, kn_ref, kr_ref, v_ref, o_ref, m_ref, l_ref, acc_ref, *, tq, kb, q_off, s_valid):
    i = pl.program_id(1)
    n_blk = _n_key_blocks(i, tq, kb, q_off, s_valid)
    qc = _q_chunk(i, tq, q_off)
    _flash_init(m_ref, l_ref, acc_ref)

    def attend(j, carry):
        ks = pl.multiple_of(j * kb, kb)
        kr = kr_ref[pl.ds(ks, kb), :]
        ok = _k_chunk(ks, kb, s_valid) <= qc
        for hd in range(B_HEADS):
            kcat = jnp.concatenate([kn_ref[pl.ds(ks, kb), hd * 128:(hd + 1) * 128], kr], axis=1)
            s = jnp.where(ok, _dot_nt(q_ref[:, hd * 256:(hd + 1) * 256], kcat), NEG)
            _flash_step(s, v_ref[pl.ds(ks, kb), hd * 128:(hd + 1) * 128], m_ref.at[hd], l_ref.at[hd], acc_ref.at[hd])
        return carry

    lax.fori_loop(0, n_blk, attend, 0)
    for hd in range(B_HEADS):
        o_ref[:, hd * 128:(hd + 1) * 128] = (acc_ref[hd] / l_ref[hd]).astype(BF16)


def _mla_attn(q, kn, kr, v, *, tq, kb, q_off, s_valid):
    b, t, _ = q.shape
    s_pad = kn.shape[1]
    return pl.pallas_call(
        functools.partial(_mla_body, tq=tq, kb=kb, q_off=q_off, s_valid=s_valid),
        grid=(b, t // tq),
        in_specs=[pl.BlockSpec((None, tq, B_HEADS * 256), lambda bi, i: (bi, i, 0)),
                  pl.BlockSpec((None, s_pad, B_HEADS * 128), lambda bi, i: (bi, 0, 0)),
                  pl.BlockSpec((None, s_pad, 128), lambda bi, i: (bi, 0, 0)),
                  pl.BlockSpec((None, s_pad, B_HEADS * 128), lambda bi, i: (bi, 0, 0))],
        out_specs=pl.BlockSpec((None, tq, B_HEADS * V_DIM), lambda bi, i: (bi, i, 0)),
        out_shape=jax.ShapeDtypeStruct((b, t, B_HEADS * V_DIM), BF16),
        scratch_shapes=[pltpu.VMEM((B_HEADS, tq, 1), F32), pltpu.VMEM((B_HEADS, tq, 1), F32),
                        pltpu.VMEM((B_HEADS, tq, 128), F32)],
        compiler_params=_cparams(2),
        name="mla_attn",
    )(q, kn, kr, v)


def _diff_body(q_ref, k_ref, v_ref, lq1_ref, lk1_ref, lq2_ref, lk2_ref, sn_ref, o_ref, m_ref, l_ref, acc_ref,
               *, tq, kb, q_off, s_valid, lam_init):
    i = pl.program_id(1)
    n_blk = _n_key_blocks(i, tq, kb, q_off, s_valid)
    qc = _q_chunk(i, tq, q_off)
    _flash_init(m_ref, l_ref, acc_ref)

    def attend(j, carry):
        ks = pl.multiple_of(j * kb, kb)
        ok = _k_chunk(ks, kb, s_valid) <= qc
        for hd in range(C_HEADS):
            v = v_ref[pl.ds(ks, kb), hd * 256:(hd + 1) * 256]
            for p in range(2):
                c = 2 * hd + p
                s = _dot_nt(q_ref[:, c * 128:(c + 1) * 128], k_ref[pl.ds(ks, kb), c * 128:(c + 1) * 128])
                _flash_step(jnp.where(ok, s, NEG), v, m_ref.at[c], l_ref.at[c], acc_ref.at[c])
        return carry

    lax.fori_loop(0, n_blk, attend, 0)
    lam = (jnp.exp(jnp.sum(lq1_ref[...] * lk1_ref[...], axis=1, keepdims=True))
           - jnp.exp(jnp.sum(lq2_ref[...] * lk2_ref[...], axis=1, keepdims=True)) + lam_init)
    for hd in range(C_HEADS):
        o = acc_ref[2 * hd] / l_ref[2 * hd] - lam * (acc_ref[2 * hd + 1] / l_ref[2 * hd + 1])
        o_ref[:, hd * 256:(hd + 1) * 256] = (_rms(o, sn_ref[...], 2 * C_DIM) * (1.0 - lam_init)).astype(BF16)


def _diff_attn(q, k, v, lq1, lk1, lq2, lk2, sn, *, tq, kb, q_off, s_valid, lam_init):
    b, t, c = q.shape
    s_pad = k.shape[1]
    vec = lambda n: pl.BlockSpec((1, n), lambda bi, i: (0, 0))
    return pl.pallas_call(
        functools.partial(_diff_body, tq=tq, kb=kb, q_off=q_off, s_valid=s_valid, lam_init=lam_init),
        grid=(b, t // tq),
        in_specs=[pl.BlockSpec((None, tq, c), lambda bi, i: (bi, i, 0)),
                  pl.BlockSpec((None, s_pad, c), lambda bi, i: (bi, 0, 0)),
                  pl.BlockSpec((None, s_pad, c), lambda bi, i: (bi, 0, 0)),
                  vec(128), vec(128), vec(128), vec(128), vec(256)],
        out_specs=pl.BlockSpec((None, tq, c), lambda bi, i: (bi, i, 0)),
        out_shape=jax.ShapeDtypeStruct((b, t, c), BF16),
        scratch_shapes=[pltpu.VMEM((2 * C_HEADS, tq, 1), F32), pltpu.VMEM((2 * C_HEADS, tq, 1), F32),
                        pltpu.VMEM((2 * C_HEADS, tq, 256), F32)],
        compiler_params=_cparams(2),
        name="diff_attn",
    )(q, k, v, lq1, lk1, lq2, lk2, sn)


SCORE_BLOCKS = 4
SEARCH_BLOCKS = 4
SEARCH_BITS_UNCHECKED = 26
SEARCH_BITS_PER_CHECK = 2


def _stage_bufs(n_chain, kb, r):
    return [pltpu.VMEM((n_chain, 2, kb, r), F32), pltpu.VMEM((n_chain, 2, kb, r), BF16),
            pltpu.VMEM((n_chain, 2, 1, r), F32)]


def _staged_flash_t(bufs, state, lo, hi, n_kb, qk, prep, vt_of, first=True, last=True):
    s_ref, p_ref, a_ref = bufs
    m_ref, l_ref, acc_ref = state
    n_chain = s_ref.shape[0]
    clamp = lambda j: jnp.clip(j, 0, n_kb - 1)

    def softmax(c, j, slot):
        st = prep(c, j, s_ref[c, slot])
        m_prev = m_ref[c]
        m_new = jnp.maximum(m_prev, jnp.max(st, axis=0, keepdims=True))
        alpha = jnp.exp2(m_prev - m_new)
        p = jnp.exp2(st - m_new)
        l_ref[c] = alpha * l_ref[c] + jnp.sum(p, axis=0, keepdims=True)
        m_ref[c] = m_new
        p_ref[c, slot] = p.astype(BF16)
        a_ref[c, slot] = alpha

    def values(c, j, slot):
        acc_ref[c] = a_ref[c, slot] * acc_ref[c] + _dot(vt_of(c, clamp(j)), p_ref[c, slot])

    if first:
        for c in range(n_chain):
            s_ref[c, 0] = qk(c, clamp(lo))
            p_ref[c, 1] = jnp.zeros(p_ref.shape[2:], BF16)
            a_ref[c, 1] = jnp.ones(a_ref.shape[2:], F32)

    def turn(t, carry):
        j0 = lo + 2 * t
        for c in range(n_chain):
            s_ref[c, 1] = qk(c, clamp(j0 + 1))
        for c in range(n_chain):
            softmax(c, j0, 0)
        for c in range(n_chain):
            values(c, j0 - 1, 1)
        for c in range(n_chain):
            s_ref[c, 0] = qk(c, clamp(j0 + 2))
        for c in range(n_chain):
            softmax(c, j0 + 1, 1)
        for c in range(n_chain):
            values(c, j0, 0)
        return carry

    n_turn = lax.shift_right_logical(hi - lo + 1, 1)
    lax.fori_loop(0, n_turn, turn, 0)
    if last:
        for c in range(n_chain):
            values(c, lo + 2 * n_turn - 1, 1)


def _dsa_t_body(q_ref, qi_ref, wi_ref, kidx_ref, k_ref, vt_ref, o_ref, keys_ref, sc_ref, s_ref, p_ref, a_ref,
                m_ref, l_ref, acc_ref, *, tq, kb, q_off, s_valid, topk):
    i = pl.program_id(1)
    n_blk = _n_key_blocks(i, tq, kb, q_off, s_valid)
    n_kb = keys_ref.shape[0]
    n_sb = lax.shift_right_logical(n_blk + SEARCH_BLOCKS - 1, int(math.log2(SEARCH_BLOCKS)))
    limit = _key_limit(i, tq, q_off, s_valid)
    kidx0 = lax.broadcasted_iota(I32, (kb, tq), 0)
    wit = wi_ref[...].T
    wrow = [wit[hd:hd + 1, :] for hd in range(IDX_HEADS)]

    def score_blocks(jj, carry):
        ks = pl.multiple_of(jj * (SCORE_BLOCKS * kb), SCORE_BLOCKS * kb)
        half = SCORE_BLOCKS // 2
        q_all = jnp.concatenate([qi_ref[:, p * 128:(p + 1) * 128] for p in range(IDX_HEADS // 2)], axis=0)
        for hf in range(2):
            rows = pl.ds(ks + hf * half * kb, half * kb)
            sc_ref[hf, 0] = _dot_nt(kidx_ref[rows, 0:128], q_all)
            sc_ref[hf, 1] = _dot_nt(kidx_ref[rows, 128:256], q_all)
        for hf in range(2):
            for u in range(half):
                blk = slice(u * kb, (u + 1) * kb)
                sc = jnp.zeros((kb, tq), F32)
                for p in range(IDX_HEADS // 2):
                    sc = sc + wrow[2 * p] * jnp.maximum(sc_ref[hf, 0, blk, p * tq:(p + 1) * tq], 0.0)
                    sc = sc + wrow[2 * p + 1] * jnp.maximum(sc_ref[hf, 1, blk, p * tq:(p + 1) * tq], 0.0)
                bits = lax.bitcast_convert_type(sc, I32)
                key = bits ^ ((bits >> 31) & 0x7FFFFFFF)
                j = jj * SCORE_BLOCKS + hf * half + u
                keys_ref[j] = jnp.where(kidx0 < limit - j * kb, key, INT_MIN)
        return carry

    def pad_block(j, carry):
        keys_ref[j] = jnp.full((kb, tq), INT_MIN, I32)
        return carry

    n_scored = lax.shift_right_logical(n_blk + SCORE_BLOCKS - 1, int(math.log2(SCORE_BLOCKS)))
    lax.fori_loop(0, n_scored, score_blocks, 0)
    lax.fori_loop(n_scored * SCORE_BLOCKS, n_sb * SEARCH_BLOCKS, pad_block, 0)

    def count(pred):
        def group(jj, cs):
            out = []
            for u in range(SEARCH_BLOCKS):
                j = jj * SEARCH_BLOCKS + u
                x = jnp.where(pred(keys_ref[j], j * kb), 1.0, 0.0).reshape(kb // 8, 8, tq)
                h = kb // 32
                out.append(cs[u] + ((jnp.sum(x[:h], axis=0) + jnp.sum(x[h:2 * h], axis=0))
                                    + (jnp.sum(x[2 * h:3 * h], axis=0) + jnp.sum(x[3 * h:], axis=0))))
            return tuple(out)
        cs = lax.fori_loop(0, n_sb, group, tuple(jnp.zeros((8, tq), F32) for _ in range(SEARCH_BLOCKS)))
        return jnp.sum(functools.reduce(lambda a, b: a + b, cs), axis=0, keepdims=True)

    kf = float(topk)

    def bit_step(b, state):
        t, n_ge = state
        cand = t + jnp.left_shift(jnp.int32(1), 31 - b)
        cnt = count(lambda kblk, _: kblk >= cand)
        take = cnt >= kf
        return jnp.where(take, cand, t), jnp.where(take, cnt, n_ge)

    def unsettled(n_ge):
        open_ = jnp.logical_and(n_ge != kf, limit.astype(F32) > kf)
        return jnp.max(jnp.where(open_, 1.0, 0.0)) > 0.0

    def more_bits(state):
        b, _, n_ge = state
        return jnp.logical_and(b < 32, unsettled(n_ge))

    def four_bits(state):
        b, t, n_ge = state
        for u in range(SEARCH_BITS_PER_CHECK):
            t, n_ge = bit_step(b + u, (t, n_ge))
        return b + SEARCH_BITS_PER_CHECK, t, n_ge

    start = (jnp.full((1, tq), INT_MIN, I32), jnp.full((1, tq), float(n_kb * kb), F32))
    head = lax.fori_loop(0, SEARCH_BITS_UNCHECKED, bit_step, start)
    _, thr, _ = lax.while_loop(more_bits, four_bits, (jnp.int32(SEARCH_BITS_UNCHECKED),) + head)
    need = kf - count(lambda kblk, _: kblk > thr)
    n_eq = count(lambda kblk, _: kblk == thr)
    partial = jnp.logical_and(n_eq > need, thr != INT_MIN)

    @pl.when(jnp.max(jnp.where(partial, 1.0, 0.0)) > 0.0)
    def _():
        def idx_step(b, x):
            cand = x + jnp.left_shift(jnp.int32(1), b)
            cnt = count(lambda kblk, k0: jnp.logical_and(kblk == thr, kidx0 < cand - k0))
            return jnp.where(cnt < need, cand, x)

        nbits = max(1, int(s_valid - 1).bit_length())
        cut = lax.fori_loop(0, nbits, lambda b, x: idx_step(nbits - 1 - b, x), jnp.zeros((1, tq), I32))

        def demote(j, carry):
            kblk = keys_ref[j]
            drop = jnp.logical_and(jnp.logical_and(kblk == thr, kidx0 > cut - j * kb), partial)
            keys_ref[j] = jnp.where(drop, kblk - 1, kblk)
            return carry

        lax.fori_loop(0, n_blk, demote, 0)

    thr_sel = jnp.maximum(thr, INT_MIN + 1)

    _flash_init(m_ref, l_ref, acc_ref)
    n_pair = A_HEADS // 2

    def to_bias(j, carry):
        keys_ref[j] = lax.bitcast_convert_type(jnp.where(keys_ref[j] >= thr_sel, 0.0, NEG), I32)
        return carry

    lax.fori_loop(0, n_sb * SEARCH_BLOCKS, to_bias, 0)
    for g in range(A_KV_HEADS):
        def qk(c, j, g=g):
            p = g * (A_GROUP // 2) + c
            qp = jnp.concatenate([q_ref[:, (2 * p) * 128:(2 * p + 1) * 128],
                                  q_ref[:, (2 * p + 1) * 128:(2 * p + 2) * 128]], axis=0)
            return _dot_nt(k_ref[pl.ds(pl.multiple_of(j * kb, kb), kb), g * 128:(g + 1) * 128], qp)

        def prep(c, j, st):
            bias = lax.bitcast_convert_type(keys_ref[j], F32)
            return st + jnp.concatenate([bias, bias], axis=1)

        def vt_of(c, j, g=g):
            return vt_ref[j, g * 128:(g + 1) * 128, :]

        chains = pl.ds(g * (A_GROUP // 2), A_GROUP // 2)
        _staged_flash_t((s_ref.at[chains], p_ref.at[chains], a_ref.at[chains]),
                        (m_ref.at[chains], l_ref.at[chains], acc_ref.at[chains]), 0, n_blk, n_kb, qk, prep, vt_of)
    for p in range(n_pair):
        ot = acc_ref[p] / l_ref[p]
        for a in range(2):
            hd = 2 * p + a
            o_ref[:, hd * 128:(hd + 1) * 128] = ot[:, a * tq:(a + 1) * tq].T.astype(BF16)


def _dsa_attn_t(q, qi, wi, kidx, k, vt, *, tq, kb, q_off, s_valid, topk):
    b, t, _ = q.shape
    s_pad = k.shape[1]
    n_kb = s_pad // kb
    qspec = lambda c: pl.BlockSpec((None, tq, c), lambda bi, i: (bi, i, 0))
    kspec = lambda c: pl.BlockSpec((None, s_pad, c), lambda bi, i: (bi, 0, 0))
    n_pair = A_HEADS // 2
    return pl.pallas_call(
        functools.partial(_dsa_t_body, tq=tq, kb=kb, q_off=q_off, s_valid=s_valid, topk=topk),
        grid=(b, t // tq),
        in_specs=[qspec(1024), qspec(512), qspec(LANES),
                  kspec(256), kspec(256), pl.BlockSpec((None, n_kb, 256, kb), lambda bi, i: (bi, 0, 0, 0))],
        out_specs=qspec(1024),
        out_shape=jax.ShapeDtypeStruct((b, t, 1024), BF16),
        scratch_shapes=[pltpu.VMEM((n_kb, kb, tq), I32),
                        pltpu.VMEM((2, 2, SCORE_BLOCKS // 2 * kb, IDX_HEADS // 2 * tq), F32)]
        + _stage_bufs(n_pair, kb, 2 * tq)
        + [pltpu.VMEM((n_pair, 1, 2 * tq), F32), pltpu.VMEM((n_pair, 1, 2 * tq), F32),
           pltpu.VMEM((n_pair, 128, 2 * tq), F32)],
        compiler_params=_cparams(2),
        name="dsa_attn_t",
    )(q, qi, wi, kidx, k, vt)


def _causal_flash_t(bufs, state, i, n_kb, tq, kb, q_off, s_valid, qk, vt_of):
    n_blk = _n_key_blocks(i, tq, kb, q_off, s_valid)
    n_plain = _n_full_blocks(i, tq, kb, q_off, s_valid) & -2
    limit = _key_limit(i, tq, q_off, s_valid)

    def masked(c, j, st):
        return jnp.where(lax.broadcasted_iota(I32, (kb, tq), 0) < limit - j * kb, st, NEG)

    _staged_flash_t(bufs, state, 0, n_plain, n_kb, qk, lambda c, j, st: st, vt_of, last=False)
    _staged_flash_t(bufs, state, n_plain, n_blk, n_kb, qk, masked, vt_of, first=False)


def _mla_t_body(qt_ref, kn_ref, kr_ref, vt_ref, o_ref, s_ref, p_ref, a_ref, m_ref, l_ref, acc_ref,
                *, tq, kb, q_off, s_valid):
    i = pl.program_id(2)
    _flash_init(m_ref, l_ref, acc_ref)

    def qk(a, j):
        ks = pl.multiple_of(j * kb, kb)
        kcat = jnp.concatenate([kn_ref[pl.ds(ks, kb), a * 128:(a + 1) * 128], kr_ref[pl.ds(ks, kb), :]], axis=1)
        return _dot(kcat, qt_ref[a * 256:(a + 1) * 256, :])

    def vt_of(a, j):
        return vt_ref[j, a * 128:(a + 1) * 128, :]

    _causal_flash_t((s_ref, p_ref, a_ref), (m_ref, l_ref, acc_ref), i, vt_ref.shape[0], tq, kb, q_off, s_valid,
                    qk, vt_of)
    for a in range(2):
        o_ref[:, a * 128:(a + 1) * 128] = (acc_ref[a] / l_ref[a]).T.astype(BF16)


def _mla_attn_t(qt, kn, kr, vt, *, tq, kb, q_off, s_valid):
    b, _, t = qt.shape
    s_pad = kn.shape[1]
    n_kb = s_pad // kb
    return pl.pallas_call(
        functools.partial(_mla_t_body, tq=tq, kb=kb, q_off=q_off, s_valid=s_valid),
        grid=(b, B_HEADS // 2, t // tq),
        in_specs=[pl.BlockSpec((None, 512, tq), lambda bi, h, i: (bi, h, i)),
                  pl.BlockSpec((None, s_pad, 256), lambda bi, h, i: (bi, 0, h)),
                  pl.BlockSpec((None, s_pad, 128), lambda bi, h, i: (bi, 0, 0)),
                  pl.BlockSpec((None, n_kb, 256, kb), lambda bi, h, i: (bi, 0, h, 0))],
        out_specs=pl.BlockSpec((None, tq, 256), lambda bi, h, i: (bi, i, h)),
        out_shape=jax.ShapeDtypeStruct((b, t, B_HEADS * V_DIM), BF16),
        scratch_shapes=_stage_bufs(2, kb, tq) + [pltpu.VMEM((2, 1, tq), F32), pltpu.VMEM((2, 1, tq), F32),
                                                 pltpu.VMEM((2, 128, tq), F32)],
        compiler_params=_cparams(3),
        name="mla_attn_t",
    )(qt, kn, kr, vt)


def _diff_t_body(q_ref, k_ref, vt_ref, lq1_ref, lk1_ref, lq2_ref, lk2_ref, sn_ref, o_ref, s_ref, p_ref, a_ref,
                 m_ref, l_ref, acc_ref, *, tq, kb, q_off, s_valid, lam_init):
    i = pl.program_id(2)
    _flash_init(m_ref, l_ref, acc_ref)

    def qk(p, j):
        ks = pl.multiple_of(j * kb, kb)
        return _dot_nt(k_ref[pl.ds(ks, kb), p * 128:(p + 1) * 128], q_ref[:, p * 128:(p + 1) * 128])

    _causal_flash_t((s_ref, p_ref, a_ref), (m_ref, l_ref, acc_ref), i, vt_ref.shape[0], tq, kb, q_off, s_valid,
                    qk, lambda p, j: vt_ref[j])
    lam = (jnp.exp(jnp.sum(lq1_ref[...] * lk1_ref[...], axis=1, keepdims=True))
           - jnp.exp(jnp.sum(lq2_ref[...] * lk2_ref[...], axis=1, keepdims=True)) + lam_init)
    ot = acc_ref[0] / l_ref[0] - lam * (acc_ref[1] / l_ref[1])
    o = jnp.concatenate([ot[0:128, :].T, ot[128:256, :].T], axis=1)
    o_ref[...] = (_rms(o, sn_ref[...], 2 * C_DIM) * (1.0 - lam_init)).astype(BF16)


def _diff_attn_t(q, k, vt, lq1, lk1, lq2, lk2, sn, *, tq, kb, q_off, s_valid, lam_init):
    b, t, _ = q.shape
    s_pad = k.shape[1]
    n_kb = s_pad // kb
    vec = lambda c: pl.BlockSpec((1, c), lambda bi, h, i: (0, 0))
    return pl.pallas_call(
        functools.partial(_diff_t_body, tq=tq, kb=kb, q_off=q_off, s_valid=s_valid, lam_init=lam_init),
        grid=(b, C_HEADS, t // tq),
        in_specs=[pl.BlockSpec((None, tq, 256), lambda bi, h, i: (bi, i, h)),
                  pl.BlockSpec((None, s_pad, 256), lambda bi, h, i: (bi, 0, h)),
                  pl.BlockSpec((None, n_kb, 256, kb), lambda bi, h, i: (bi, 0, h, 0)),
                  vec(128), vec(128), vec(128), vec(128), vec(256)],
        out_specs=pl.BlockSpec((None, tq, 256), lambda bi, h, i: (bi, i, h)),
        out_shape=jax.ShapeDtypeStruct((b, t, C_HEADS * 2 * C_DIM), BF16),
        scratch_shapes=_stage_bufs(2, kb, tq) + [pltpu.VMEM((2, 1, tq), F32), pltpu.VMEM((2, 1, tq), F32),
                                                 pltpu.VMEM((2, 256, tq), F32)],
        compiler_params=_cparams(3),
        name="diff_attn_t",
    )(q, k, vt, lq1, lk1, lq2, lk2, sn)


def _blocked_t(v, kb):
    b, s, c = v.shape
    return jnp.swapaxes(v.reshape(b, s // kb, kb, c), 2, 3)


def _rope_tables(pos, reps):
    p = pos.astype(F32)[:, None]
    inv64 = jnp.power(ROPE_THETA, -jnp.arange(64, dtype=F32) / 64)
    inv32 = jnp.power(ROPE_THETA, -jnp.arange(32, dtype=F32) / 32)
    c, s = jnp.cos(p * inv64), jnp.sin(p * inv64)
    c3, s3 = jnp.cos(p * inv32), jnp.sin(p * inv32)
    z = jnp.zeros_like(s3)
    tabs = {
        "c128": jnp.concatenate([c, c], axis=1),
        "s128": jnp.concatenate([-s, s], axis=1),
        "c64": jnp.concatenate([c3, c3, c3, c3], axis=1),
        "sa64": jnp.concatenate([-s3, z, -s3, z], axis=1),
        "sb64": jnp.concatenate([z, s3, z, s3], axis=1),
    }
    return {k: jnp.tile(v, (reps, 1)) for k, v in tabs.items()}


def _pad_cols(w, n):
    return jnp.pad(w, ((0, 0), (0, n - w.shape[1])))


def _pad_lanes(g, n=128):
    g = g.reshape(1, -1)
    return jnp.pad(g, ((0, 0), (0, n - g.shape[1])))


def _prep_weights(W):
    P = {}
    for nm in ("ffn1", "ffn2"):
        P[nm] = [(W[nm + "_norm"][i].reshape(1, -1), W[nm + "_wg"][i].astype(BF16), W[nm + "_wu"][i].astype(BF16),
                  W[nm + "_wd"][i].astype(BF16)) for i in range(DEPTH)]
    P["mix_norm"] = [W["mix_norm"][i].reshape(1, -1) for i in range(DEPTH)]
    P["a"] = [dict(w=_pad_cols(W["a_w_in"][j], A_IN_PAD).astype(BF16), qn=W["a_q_norm"][j].reshape(1, -1),
                   kn=W["a_k_norm"][j].reshape(1, -1), ikn=_pad_lanes(W["a_idx_k_norm"][j]),
                   wo=W["a_w_out"][j].astype(BF16)) for j in range(W["a_w_in"].shape[0])]
    P["b"] = []
    for j in range(W["b_w_in"].shape[0]):
        wuq = W["b_w_uq"][j].reshape(Q_LORA, B_HEADS, NOPE_DIM + ROPE_DIM)
        wuq_rope = jnp.pad(wuq[:, :, NOPE_DIM:], ((0, 0), (0, 0), (0, 128 - ROPE_DIM)))
        wuq = jnp.concatenate([wuq[:, :, :NOPE_DIM], wuq_rope], axis=2).reshape(Q_LORA, -1)
        P["b"].append(dict(
            w=_pad_cols(W["b_w_in"][j], B_IN_PAD).astype(BF16), qan=W["b_q_a_norm"][j].reshape(1, -1),
            kvan=W["b_kv_a_norm"][j].reshape(1, -1), krn=_pad_lanes(W["b_k_rope_norm"][j]), wuq=wuq.astype(BF16),
            qnn=W["b_q_nope_norm"][j].reshape(1, -1), qrn=_pad_lanes(W["b_q_rope_norm"][j]),
            wukv=W["b_w_ukv"][j].astype(BF16), knn=W["b_k_nope_norm"][j].reshape(1, -1),
            wo=W["b_w_out"][j].astype(BF16)))
    P["c"] = [dict(w=W["c_w_in"][j].astype(BF16), qn=W["c_q_norm"][j].reshape(1, -1),
                   kn=W["c_k_norm"][j].reshape(1, -1), lq1=W["c_lambda_q1"][j].reshape(1, -1),
                   lk1=W["c_lambda_k1"][j].reshape(1, -1), lq2=W["c_lambda_q2"][j].reshape(1, -1),
                   lk2=W["c_lambda_k2"][j].reshape(1, -1), sn=W["c_sub_norm"][j].reshape(1, -1),
                   wo=W["c_w_out"][j].astype(BF16)) for j in range(W["c_w_in"].shape[0])]
    return P


def _with_past(past, new, s_pad):
    x = new if past is None else jnp.concatenate([past.astype(new.dtype), new], axis=1)
    return x if x.shape[1] == s_pad else jnp.pad(x, ((0, 0), (0, s_pad - x.shape[1]), (0, 0)))


def _trunk(x, offset, past, P, cfg):
    b, t, _ = x.shape
    n = b * t
    tm, tq_a, tq, kb, key_major = cfg["tm"], cfg["tq_a"], cfg["tq"], cfg["kb"], cfg["key_major"]
    p_len = 0 if past is None else past["a_k"].shape[2]
    s_valid = p_len + t
    s_pad = -(-s_valid // kb) * kb
    tabs = _rope_tables(offset + jnp.arange(t, dtype=I32), tm // t if tm > t else 1)
    att = dict(kb=kb, q_off=offset, s_valid=s_valid)
    rows = {k: [] for k in ("a_k", "a_v", "a_ik", "b_ckv", "b_kr", "c_k", "c_v")}
    x = x.reshape(n, D_MODEL)
    r3 = lambda a: a.reshape(b, t, a.shape[-1])
    t3 = lambda a: jnp.swapaxes(r3(a), 1, 2)
    pj = lambda nm, j: None if past is None else past[nm][j].reshape(b, p_len, -1)
    for i in range(DEPTH):
        x = _ffn(x, *P["ffn1"][i], cfg["tm_ffn"])
        kind, j = i % N_MIXERS, i // N_MIXERS
        g = P["mix_norm"][i]
        if kind == 0:
            pa = P["a"][j]
            q, k32, v32, kbf, vbf, qi, ki32, kidx, wi = _a_proj(x, g, pa["w"], pa["qn"], pa["kn"], pa["ikn"], tabs, tm)
            rows["a_k"].append(k32.reshape(b, t, A_KV_HEADS, A_HEAD_DIM))
            rows["a_v"].append(v32.reshape(b, t, A_KV_HEADS, A_HEAD_DIM))
            rows["a_ik"].append(ki32.reshape(b, t, IDX_DIM))
            pik = pj("a_ik", j)
            if pik is not None:
                z = jnp.zeros_like(pik)
                pik = jnp.concatenate([pik, z, z, pik], axis=-1)
            kidx_all = _with_past(pik, r3(kidx), s_pad)
            k_all = _with_past(pj("a_k", j), r3(kbf), s_pad)
            v_all = _with_past(pj("a_v", j), r3(vbf), s_pad)
            topk = min(TOPK_MAX, s_valid // 4)
            if key_major:
                o = _dsa_attn_t(r3(q), r3(qi), r3(wi), kidx_all, k_all, _blocked_t(v_all, kb),
                                tq=tq_a, topk=topk, **att)
            else:
                o = _dsa_attn(r3(q), r3(qi), r3(wi), kidx_all, k_all, v_all, tq=tq_a, topk=topk, **att)
        elif kind == 1:
            pb = P["b"][j]
            q, ckv, kr32, krbf = _b_proj(x, g, pb["w"], pb["qan"], pb["kvan"], pb["krn"], pb["wuq"], pb["qnn"],
                                         pb["qrn"], tabs, tm)
            rows["b_ckv"].append(ckv.reshape(b, t, KV_LORA))
            rows["b_kr"].append(kr32.reshape(b, t, ROPE_DIM))
            ckv_all = _with_past(pj("b_ckv", j), r3(ckv), s_pad)
            pkr = pj("b_kr", j)
            if pkr is not None:
                pkr = jnp.concatenate([pkr, jnp.zeros_like(pkr)], axis=-1)
            kn, v = _kv_up(ckv_all.reshape(b * s_pad, KV_LORA), pb["wukv"], pb["knn"], math.gcd(b * s_pad, 512))
            kn, v, kr_all = kn.reshape(b, s_pad, -1), v.reshape(b, s_pad, -1), _with_past(pkr, r3(krbf), s_pad)
            if key_major:
                o = _mla_attn_t(t3(q), kn, kr_all, _blocked_t(v, kb), tq=cfg["tq_mla"], **att)
            else:
                o = _mla_attn(r3(q), kn, kr_all, v, tq=tq, **att)
        else:
            pc = P["c"][j]
            q, k32, v32, kbf, vbf = _c_proj(x, g, pc["w"], pc["qn"], pc["kn"], tabs, tm)
            rows["c_k"].append(k32.reshape(b, t, C_HEADS, 2, C_DIM))
            rows["c_v"].append(v32.reshape(b, t, C_HEADS, 2 * C_DIM))
            k_all = _with_past(pj("c_k", j), r3(kbf), s_pad)
            v_all = _with_past(pj("c_v", j), r3(vbf), s_pad)
            lam = (pc["lq1"], pc["lk1"], pc["lq2"], pc["lk2"], pc["sn"])
            lam_init = 0.8 - 0.6 * math.exp(-0.3 * i)
            if key_major:
                att_c = dict(att, kb=cfg["kb_diff"])
                o = _diff_attn_t(r3(q), k_all, _blocked_t(v_all, cfg["kb_diff"]), *lam, tq=tq, lam_init=lam_init,
                                 **att_c)
            else:
                o = _diff_attn(r3(q), k_all, v_all, *lam, tq=tq, lam_init=lam_init, **att)
        wo = (P["a"], P["b"], P["c"])[kind][j]["wo"]
        x = _ffn(x, *P["ffn2"][i], cfg["tm_ffn"], attn=o.reshape(n, -1), wo=wo)
    order = ("a_k", "a_v", "a_ik", "b_ckv", "b_kr", "c_k", "c_v")
    stack = lambda rs: rs[0][None] if len(rs) == 1 else jnp.stack(rs)
    return x.reshape(b, t, D_MODEL), tuple(stack(rows[k]) for k in order)


PROMPT_CFG = dict(tm=512, tm_ffn=1024, tq_a=128, tq=512, tq_mla=512, kb=256, kb_diff=512, key_major=True)
SAMPLE_CFG = dict(tm=128, tm_ffn=128, tq_a=16, tq=16, kb=1280, key_major=False)


@jax.jit
def _forward(x_prompt, x_sample, past, W):
    P = _prep_weights(W)
    y_p, rows_p = _trunk(x_prompt, 0, None, P, PROMPT_CFG)
    y_s, rows_s = _trunk(x_sample, past["a_k"].shape[2], past, P, SAMPLE_CFG)
    return (y_p, y_s) + rows_p + rows_s


def kernel(x_prompt, x_sample, cache_a_k, cache_a_v, cache_a_idx_k, cache_b_ckv, cache_b_krope, cache_c_k, cache_c_v, ffn1_norm, ffn1_wg, ffn1_wu, ffn1_wd, mix_norm, ffn2_norm, ffn2_wg, ffn2_wu, ffn2_wd, a_w_in, a_q_norm, a_k_norm, a_idx_k_norm, a_w_out, b_w_in, b_q_a_norm, b_kv_a_norm, b_w_uq, b_w_ukv, b_q_nope_norm, b_q_rope_norm, b_k_nope_norm, b_k_rope_norm, b_w_out, c_w_in, c_q_norm, c_k_norm, c_lambda_q1, c_lambda_k1, c_lambda_q2, c_lambda_k2, c_sub_norm, c_w_out):
    W = dict(ffn1_norm=ffn1_norm, ffn1_wg=ffn1_wg, ffn1_wu=ffn1_wu, ffn1_wd=ffn1_wd, mix_norm=mix_norm,
             ffn2_norm=ffn2_norm, ffn2_wg=ffn2_wg, ffn2_wu=ffn2_wu, ffn2_wd=ffn2_wd,
             a_w_in=a_w_in, a_q_norm=a_q_norm, a_k_norm=a_k_norm, a_idx_k_norm=a_idx_k_norm, a_w_out=a_w_out,
             b_w_in=b_w_in, b_q_a_norm=b_q_a_norm, b_kv_a_norm=b_kv_a_norm, b_w_uq=b_w_uq, b_w_ukv=b_w_ukv,
             b_q_nope_norm=b_q_nope_norm, b_q_rope_norm=b_q_rope_norm, b_k_nope_norm=b_k_nope_norm,
             b_k_rope_norm=b_k_rope_norm, b_w_out=b_w_out,
             c_w_in=c_w_in, c_q_norm=c_q_norm, c_k_norm=c_k_norm, c_lambda_q1=c_lambda_q1,
             c_lambda_k1=c_lambda_k1, c_lambda_q2=c_lambda_q2, c_lambda_k2=c_lambda_k2, c_sub_norm=c_sub_norm,
             c_w_out=c_w_out)
    past = dict(a_k=cache_a_k, a_v=cache_a_v, a_ik=cache_a_idx_k, b_ckv=cache_b_ckv, b_kr=cache_b_krope,
                c_k=cache_c_k, c_v=cache_c_v)
    return _forward(x_prompt, x_sample, past, W)
```

```python
import functools
import math

import jax
import jax.numpy as jnp
from jax import lax
from jax.experimental import pallas as pl
from jax.experimental.pallas import tpu as pltpu

F32 = jnp.float32
BF16 = jnp.bfloat16
I32 = jnp.int32

D_MODEL = 1024
DEPTH = 4
CHUNK_SHIFT = 6
N_MIXERS = 3
ROPE_THETA = 10000.0
EPS = 1e-6
D_FF = 2816

A_HEADS = 8
A_KV_HEADS = 2
A_GROUP = A_HEADS // A_KV_HEADS
A_HEAD_DIM = 128
IDX_HEADS = 8
IDX_DIM = 64
TOPK_MAX = 256
LOG2E = math.log2(math.e)
A_SCALE = A_HEAD_DIM ** -0.5 * LOG2E
IDX_W_SCALE = (IDX_HEADS * IDX_DIM) ** -0.5
A_IN = 2120
A_IN_PAD = 2176

B_HEADS = 8
Q_LORA = 384
KV_LORA = 256
NOPE_DIM = 128
ROPE_DIM = 64
V_DIM = 128
B_SCALE = (NOPE_DIM + ROPE_DIM) ** -0.5 * LOG2E
B_IN = 704
B_IN_PAD = 768

C_HEADS = 4
C_DIM = 128
C_SCALE = C_DIM ** -0.5 * LOG2E

LANES = 128
NEG = -1e30
INT_MIN = -(2 ** 31)
VMEM_LIMIT = 56 * 1024 * 1024


def _cparams(n_axes):
    return pltpu.CompilerParams(dimension_semantics=("arbitrary",) * n_axes, vmem_limit_bytes=VMEM_LIMIT)


def _dot(a, b):
    return jnp.dot(a, b, preferred_element_type=F32)


def _dot_nt(a, b):
    return lax.dot_general(a, b, (((1,), (1,)), ((), ())), preferred_element_type=F32)


def _rms(x, g, n):
    ms = jnp.sum(x * x, axis=-1, keepdims=True) * (1.0 / n)
    return x * lax.rsqrt(ms + EPS) * g


def _rope128(x, c, s):
    return x * c + pltpu.roll(x, 64, 1) * s


def _rope64(x, c, sa, sb):
    return x * c + pltpu.roll(x, 96, 1) * sa + pltpu.roll(x, 32, 1) * sb


FFN_CHUNK = 256


def _ffn_body(*refs, fc, mixed):
    if mixed:
        attn_ref, wo_ref, x_ref, g_ref, wg_ref, wu_ref, wd_ref, o_ref = refs
        x = x_ref[...] + _dot(attn_ref[...], wo_ref[...])
    else:
        x_ref, g_ref, wg_ref, wu_ref, wd_ref, o_ref = refs
        x = x_ref[...]
    h = _rms(x, g_ref[...], D_MODEL).astype(BF16)
    y = None
    for c in range(D_FF // fc):
        a = _dot(h, wg_ref[:, c * fc:(c + 1) * fc])
        u = _dot(h, wu_ref[:, c * fc:(c + 1) * fc])
        act = (a * jax.nn.sigmoid(a) * u).astype(BF16)
        part = _dot(act, wd_ref[c * fc:(c + 1) * fc, :])
        y = part if y is None else y + part
    o_ref[...] = x + 0.5 * y


def _const_spec(shape):
    nd = len(shape)
    return pl.BlockSpec(shape, lambda *_: (0,) * nd, pipeline_mode=pl.Buffered(1))


def _row_spec(tm, n):
    return pl.BlockSpec((tm, n), lambda i: (i, 0))


def _ffn(x, g, wg, wu, wd, tm, attn=None, wo=None):
    n = x.shape[0]
    mixed = attn is not None
    pre_specs = [_row_spec(tm, attn.shape[1]), _const_spec(wo.shape)] if mixed else []
    return pl.pallas_call(
        functools.partial(_ffn_body, fc=FFN_CHUNK, mixed=mixed),
        grid=(n // tm,),
        in_specs=pre_specs + [_row_spec(tm, D_MODEL), _const_spec((1, D_MODEL)), _const_spec((D_MODEL, D_FF)),
                              _const_spec((D_MODEL, D_FF)), _const_spec((D_FF, D_MODEL))],
        out_specs=_row_spec(tm, D_MODEL),
        out_shape=jax.ShapeDtypeStruct((n, D_MODEL), F32),
        compiler_params=_cparams(1),
        name="ffn_mixed" if mixed else "ffn",
    )(*((attn, wo) if mixed else ()), x, g, wg, wu, wd)


PROJ_PARTS = 2

def _a_proj_body(x_ref, g_ref, w_ref, qn_ref, kn_ref, ikn_ref, c128_ref, s128_ref, c64_ref, sa64_ref, sb64_ref,
                 q_ref, k32_ref, v32_ref, kbf_ref, vbf_ref, qi_ref, ki32_ref, kidx_ref, wi_ref, y_ref):
    hm = x_ref.shape[0] // PROJ_PARTS
    for part in range(PROJ_PARTS):
        rows = slice(part * hm, (part + 1) * hm)
        y_ref[part] = _dot(_rms(x_ref[rows, :], g_ref[...], D_MODEL).astype(BF16), w_ref[...])
    for part in range(PROJ_PARTS):
        rows = slice(part * hm, (part + 1) * hm)
        c128, s128 = c128_ref[rows, :], s128_ref[rows, :]
        c64, sa64, sb64 = c64_ref[rows, :], sa64_ref[rows, :], sb64_ref[rows, :]
        for hd in range(A_HEADS):
            sl = slice(hd * 128, (hd + 1) * 128)
            qh = _rope128(_rms(y_ref[part, :, sl], qn_ref[...], A_HEAD_DIM), c128, s128)
            q_ref[rows, sl] = (qh * A_SCALE).astype(BF16)
        for hd in range(A_KV_HEADS):
            sl = slice(hd * 128, (hd + 1) * 128)
            kh = _rope128(_rms(y_ref[part, :, 1024 + hd * 128:1024 + (hd + 1) * 128], kn_ref[...], A_HEAD_DIM),
                          c128, s128)
            k32_ref[rows, sl] = kh
            kbf_ref[rows, sl] = kh.astype(BF16)
        v = y_ref[part, :, 1280:1536]
        v32_ref[rows, :] = v
        vbf_ref[rows, :] = v.astype(BF16)
        for p in range(IDX_HEADS // 2):
            sl = slice(p * 128, (p + 1) * 128)
            qi_ref[rows, sl] = _rope64(y_ref[part, :, 1536 + p * 128:1536 + (p + 1) * 128],
                                       c64, sa64, sb64).astype(BF16)
        tail = y_ref[part, :, 2048:2176]
        lane = lax.broadcasted_iota(I32, tail.shape, 1)
        low = lane < IDX_DIM
        kin = jnp.where(low, tail, 0.0)
        ki = _rope64(_rms(kin, ikn_ref[...], IDX_DIM), c64, sa64, sb64)
        ki = jnp.where(low, ki, 0.0)
        ki32_ref[rows, :] = ki[:, :IDX_DIM]
        kidx_ref[rows, 0:128] = ki.astype(BF16)
        kidx_ref[rows, 128:256] = pltpu.roll(ki, 64, 1).astype(BF16)
        wi_ref[rows, :] = pltpu.roll(tail, 64, 1) * IDX_W_SCALE


def _tab_spec(tm, t):
    nt = t // tm
    return pl.BlockSpec((tm, LANES), lambda i: (i % nt, 0))


def _a_proj(x, g, w, qn, kn, ikn, tabs, tm):
    n = x.shape[0]
    t = tabs["c128"].shape[0]
    outs = [(1024, BF16), (256, F32), (256, F32), (256, BF16), (256, BF16), (512, BF16), (IDX_DIM, F32),
            (256, BF16), (LANES, F32)]
    return pl.pallas_call(
        _a_proj_body,
        grid=(n // tm,),
        in_specs=[_row_spec(tm, D_MODEL), _const_spec((1, D_MODEL)), _const_spec((D_MODEL, A_IN_PAD)),
                  _const_spec((1, 128)), _const_spec((1, 128)), _const_spec((1, 128))]
        + [_tab_spec(tm, t)] * 5,
        out_specs=[_row_spec(tm, c) for c, _ in outs],
        out_shape=[jax.ShapeDtypeStruct((n, c), dt) for c, dt in outs],
        scratch_shapes=[pltpu.VMEM((PROJ_PARTS, tm // PROJ_PARTS, A_IN_PAD), F32)],
        compiler_params=_cparams(1),
        name="a_proj",
    )(x, g, w, qn, kn, ikn, tabs["c128"], tabs["s128"], tabs["c64"], tabs["sa64"], tabs["sb64"])


def _b_proj_body(x_ref, g_ref, w_ref, qan_ref, kvan_ref, krn_ref, wuq_ref, qnn_ref, qrn_ref,
                 c64_ref, sa64_ref, sb64_ref, q_ref, ckv_ref, kr32_ref, krbf_ref, qq_ref):
    hm = x_ref.shape[0] // PROJ_PARTS
    for part in range(PROJ_PARTS):
        rows = slice(part * hm, (part + 1) * hm)
        h = _rms(x_ref[rows, :], g_ref[...], D_MODEL).astype(BF16)
        y = _dot(h, w_ref[...])
        ckv_ref[rows, :] = _rms(y[:, 384:640], kvan_ref[...], KV_LORA)
        kr = _rope64(_rms(y[:, 640:768], krn_ref[...], ROPE_DIM), c64_ref[rows, :], sa64_ref[rows, :],
                     sb64_ref[rows, :])
        kr32_ref[rows, :] = kr[:, :ROPE_DIM]
        krbf_ref[rows, :] = kr.astype(BF16)
        qq_ref[part] = _dot(_rms(y[:, 0:384], qan_ref[...], Q_LORA).astype(BF16), wuq_ref[...])
    for part in range(PROJ_PARTS):
        rows = slice(part * hm, (part + 1) * hm)
        c64, sa64, sb64 = c64_ref[rows, :], sa64_ref[rows, :], sb64_ref[rows, :]
        for hd in range(B_HEADS):
            qn = _rms(qq_ref[part, :, hd * 256:hd * 256 + 128], qnn_ref[...], NOPE_DIM)
            qr = _rope64(_rms(qq_ref[part, :, hd * 256 + 128:(hd + 1) * 256], qrn_ref[...], ROPE_DIM),
                         c64, sa64, sb64)
            q_ref[rows, hd * 256:hd * 256 + 128] = (qn * B_SCALE).astype(BF16)
            q_ref[rows, hd * 256 + 128:(hd + 1) * 256] = (qr * B_SCALE).astype(BF16)


def _b_proj(x, g, w, qan, kvan, krn, wuq, qnn, qrn, tabs, tm):
    n = x.shape[0]
    t = tabs["c64"].shape[0]
    outs = [(B_HEADS * 256, BF16), (KV_LORA, F32), (ROPE_DIM, F32), (LANES, BF16)]
    return pl.pallas_call(
        _b_proj_body,
        grid=(n // tm,),
        in_specs=[_row_spec(tm, D_MODEL), _const_spec((1, D_MODEL)), _const_spec((D_MODEL, B_IN_PAD)),
                  _const_spec((1, Q_LORA)), _const_spec((1, KV_LORA)), _const_spec((1, 128)),
                  _const_spec((Q_LORA, 2048)), _const_spec((1, 128)), _const_spec((1, 128))]
        + [_tab_spec(tm, t)] * 3,
        out_specs=[_row_spec(tm, c) for c, _ in outs],
        out_shape=[jax.ShapeDtypeStruct((n, c), dt) for c, dt in outs],
        scratch_shapes=[pltpu.VMEM((PROJ_PARTS, tm // PROJ_PARTS, 2048), F32)],
        compiler_params=_cparams(1),
        name="b_proj",
    )(x, g, w, qan, kvan, krn, wuq, qnn, qrn, tabs["c64"], tabs["sa64"], tabs["sb64"])


def _kv_up_body(ckv_ref, w_ref, knn_ref, kn_ref, v_ref):
    y = _dot(ckv_ref[...].astype(BF16), w_ref[...])
    for hd in range(B_HEADS):
        kn_ref[:, hd * 128:(hd + 1) * 128] = _rms(y[:, hd * 256:hd * 256 + 128], knn_ref[...], NOPE_DIM).astype(BF16)
        v_ref[:, hd * 128:(hd + 1) * 128] = y[:, hd * 256 + 128:(hd + 1) * 256].astype(BF16)


def _kv_up(ckv, w, knn, tm):
    n = ckv.shape[0]
    return pl.pallas_call(
        _kv_up_body,
        grid=(n // tm,),
        in_specs=[_row_spec(tm, KV_LORA), _const_spec((KV_LORA, 2048)), _const_spec((1, 128))],
        out_specs=[_row_spec(tm, 1024), _row_spec(tm, 1024)],
        out_shape=[jax.ShapeDtypeStruct((n, 1024), BF16)] * 2,
        compiler_params=_cparams(1),
        name="kv_up",
    )(ckv, w, knn)


def _c_proj_body(x_ref, g_ref, w_ref, qn_ref, kn_ref, c128_ref, s128_ref,
                 q_ref, k32_ref, v32_ref, kbf_ref, vbf_ref, y_ref):
    hm = x_ref.shape[0] // PROJ_PARTS
    for part in range(PROJ_PARTS):
        rows = slice(part * hm, (part + 1) * hm)
        y_ref[part] = _dot(_rms(x_ref[rows, :], g_ref[...], D_MODEL).astype(BF16), w_ref[...])
    for part in range(PROJ_PARTS):
        rows = slice(part * hm, (part + 1) * hm)
        c128, s128 = c128_ref[rows, :], s128_ref[rows, :]
        for hd in range(2 * C_HEADS):
            sl = slice(hd * 128, (hd + 1) * 128)
            qh = _rope128(_rms(y_ref[part, :, sl], qn_ref[...], C_DIM), c128, s128)
            q_ref[rows, sl] = (qh * C_SCALE).astype(BF16)
            kh = _rope128(_rms(y_ref[part, :, 1024 + hd * 128:1024 + (hd + 1) * 128], kn_ref[...], C_DIM),
                          c128, s128)
            k32_ref[rows, sl] = kh
            kbf_ref[rows, sl] = kh.astype(BF16)
        v = y_ref[part, :, 2048:3072]
        v32_ref[rows, :] = v
        vbf_ref[rows, :] = v.astype(BF16)


def _c_proj(x, g, w, qn, kn, tabs, tm):
    n = x.shape[0]
    t = tabs["c128"].shape[0]
    outs = [(1024, BF16), (1024, F32), (1024, F32), (1024, BF16), (1024, BF16)]
    return pl.pallas_call(
        _c_proj_body,
        grid=(n // tm,),
        in_specs=[_row_spec(tm, D_MODEL), _const_spec((1, D_MODEL)), _const_spec((D_MODEL, 3072)),
                  _const_spec((1, 128)), _const_spec((1, 128))] + [_tab_spec(tm, t)] * 2,
        out_specs=[_row_spec(tm, c) for c, _ in outs],
        out_shape=[jax.ShapeDtypeStruct((n, c), dt) for c, dt in outs],
        scratch_shapes=[pltpu.VMEM((PROJ_PARTS, tm // PROJ_PARTS, 3072), F32)],
        compiler_params=_cparams(1),
        name="c_proj",
    )(x, g, w, qn, kn, tabs["c128"], tabs["s128"])


def _n_key_blocks(i, tq, kb, q_off, s_valid):
    last_chunk = lax.shift_right_logical(q_off + (i + 1) * tq - 1, CHUNK_SHIFT)
    kend = jnp.minimum((last_chunk + 1) << CHUNK_SHIFT, s_valid)
    return lax.div(kend + kb - 1, jnp.int32(kb))


def _q_chunk(i, tq, q_off):
    row = lax.broadcasted_iota(I32, (tq, 1), 0)
    return lax.shift_right_logical(q_off + i * tq + row, CHUNK_SHIFT)


def _k_chunk(ks, kb, s_valid):
    kpos = ks + lax.broadcasted_iota(I32, (1, kb), 1)
    return jnp.where(kpos < s_valid, lax.shift_right_logical(kpos, CHUNK_SHIFT), 2 ** 30)


def _flash_step(s, v, m_ref, l_ref, acc_ref):
    m_prev = m_ref[...]
    m_new = jnp.maximum(m_prev, jnp.max(s, axis=1, keepdims=True))
    alpha = jnp.exp2(m_prev - m_new)
    p = jnp.exp2(s - m_new)
    l_ref[...] = alpha * l_ref[...] + jnp.sum(p, axis=1, keepdims=True)
    acc_ref[...] = alpha * acc_ref[...] + _dot(p.astype(BF16), v)
    m_ref[...] = m_new


def _key_limit(i, tq, q_off, s_valid):
    qpos = q_off + i * tq + lax.broadcasted_iota(I32, (1, tq), 1)
    return jnp.minimum((lax.shift_right_logical(qpos, CHUNK_SHIFT) + 1) << CHUNK_SHIFT, s_valid)


def _n_full_blocks(i, tq, kb, q_off, s_valid):
    first_chunk = lax.shift_right_logical(q_off + i * tq, CHUNK_SHIFT)
    kend = jnp.minimum((first_chunk + 1) << CHUNK_SHIFT, s_valid)
    return lax.shift_right_logical(kend, int(math.log2(kb)))


def _flash_init(m_ref, l_ref, acc_ref):
    m_ref[...] = jnp.full(m_ref.shape, NEG, F32)
    l_ref[...] = jnp.zeros(l_ref.shape, F32)
    acc_ref[...] = jnp.zeros(acc_ref.shape, F32)


def _dsa_body(q_ref, qi_ref, wi_ref, kidx_ref, k_ref, v_ref, o_ref, keys_ref, m_ref, l_ref, acc_ref,
              *, tq, kb, q_off, s_valid, topk):
    i = pl.program_id(1)
    n_blk = _n_key_blocks(i, tq, kb, q_off, s_valid)
    qc = _q_chunk(i, tq, q_off)
    wi = wi_ref[...]
    wcol = [wi[:, hd:hd + 1] for hd in range(IDX_HEADS)]

    def score_block(j, carry):
        ks = pl.multiple_of(j * kb, kb)
        k_lo = kidx_ref[pl.ds(ks, kb), 0:128]
        k_hi = kidx_ref[pl.ds(ks, kb), 128:256]
        sc = jnp.zeros((tq, kb), F32)
        for p in range(IDX_HEADS // 2):
            qp = qi_ref[:, p * 128:(p + 1) * 128]
            sc = sc + wcol[2 * p] * jnp.maximum(_dot_nt(qp, k_lo), 0.0)
            sc = sc + wcol[2 * p + 1] * jnp.maximum(_dot_nt(qp, k_hi), 0.0)
        bits = lax.bitcast_convert_type(sc, I32)
        key = bits ^ ((bits >> 31) & 0x7FFFFFFF)
        keys_ref[j] = jnp.where(_k_chunk(ks, kb, s_valid) <= qc, key, INT_MIN)
        return carry

    lax.fori_loop(0, n_blk, score_block, 0)

    def count(pred):
        def blk(j, c):
            x = jnp.where(pred(keys_ref[j], j * kb), 1.0, 0.0)
            part = x[:, 0:LANES]
            for g in range(1, kb // LANES):
                part = part + x[:, g * LANES:(g + 1) * LANES]
            return c + part
        c = lax.fori_loop(0, n_blk, blk, jnp.zeros((tq, LANES), F32))
        return jnp.sum(c, axis=1, keepdims=True)

    kf = float(topk)

    def bit_step(b, t):
        cand = t + jnp.left_shift(jnp.int32(1), 31 - b)
        cnt = count(lambda kblk, _: kblk >= cand)
        return jnp.where(cnt >= kf, cand, t)

    thr = lax.fori_loop(0, 32, bit_step, jnp.full((tq, 1), INT_MIN, I32))
    need = kf - count(lambda kblk, _: kblk > thr)
    n_eq = count(lambda kblk, _: kblk == thr)
    partial = jnp.logical_and(n_eq > need, thr != INT_MIN)

    @pl.when(jnp.max(jnp.where(partial, 1.0, 0.0)) > 0.0)
    def _():
        lane = lax.broadcasted_iota(I32, (1, kb), 1)

        def idx_step(b, x):
            cand = x + jnp.left_shift(jnp.int32(1), b)
            cnt = count(lambda kblk, k0: jnp.logical_and(kblk == thr, k0 + lane < cand))
            return jnp.where(cnt < need, cand, x)

        nbits = max(1, int(s_valid - 1).bit_length())
        cut = lax.fori_loop(0, nbits, lambda b, x: idx_step(nbits - 1 - b, x), jnp.zeros((tq, 1), I32))

        def demote(j, carry):
            kblk = keys_ref[j]
            drop = jnp.logical_and(jnp.logical_and(kblk == thr, j * kb + lane > cut), partial)
            keys_ref[j] = jnp.where(drop, kblk - 1, kblk)
            return carry

        lax.fori_loop(0, n_blk, demote, 0)

    thr_sel = jnp.maximum(thr, INT_MIN + 1)

    _flash_init(m_ref, l_ref, acc_ref)

    def attend(j, carry):
        ks = pl.multiple_of(j * kb, kb)
        bias = jnp.where(keys_ref[j] >= thr_sel, 0.0, NEG)
        for g in range(A_KV_HEADS):
            qg = jnp.concatenate([q_ref[:, (g * A_GROUP + a) * 128:(g * A_GROUP + a + 1) * 128]
                                  for a in range(A_GROUP)], axis=0)
            s = _dot_nt(qg, k_ref[pl.ds(ks, kb), g * 128:(g + 1) * 128])
            s = (s.reshape(A_GROUP, tq, kb) + bias[None]).reshape(A_GROUP * tq, kb)
            _flash_step(s, v_ref[pl.ds(ks, kb), g * 128:(g + 1) * 128], m_ref.at[g], l_ref.at[g], acc_ref.at[g])
        return carry

    lax.fori_loop(0, n_blk, attend, 0)
    for g in range(A_KV_HEADS):
        o = acc_ref[g] / l_ref[g]
        for a in range(A_GROUP):
            hd = g * A_GROUP + a
            o_ref[:, hd * 128:(hd + 1) * 128] = o[a * tq:(a + 1) * tq].astype(BF16)


def _dsa_attn(q, qi, wi, kidx, k, v, *, tq, kb, q_off, s_valid, topk):
    b, t, _ = q.shape
    s_pad = k.shape[1]
    qspec = lambda c: pl.BlockSpec((None, tq, c), lambda bi, i: (bi, i, 0))
    kspec = lambda c: pl.BlockSpec((None, s_pad, c), lambda bi, i: (bi, 0, 0))
    rows = A_GROUP * tq
    return pl.pallas_call(
        functools.partial(_dsa_body, tq=tq, kb=kb, q_off=q_off, s_valid=s_valid, topk=topk),
        grid=(b, t // tq),
        in_specs=[qspec(1024), qspec(512), qspec(LANES), kspec(256), kspec(256), kspec(256)],
        out_specs=qspec(1024),
        out_shape=jax.ShapeDtypeStruct((b, t, 1024), BF16),
        scratch_shapes=[pltpu.VMEM((s_pad // kb, tq, kb), I32), pltpu.VMEM((A_KV_HEADS, rows, 1), F32),
                        pltpu.VMEM((A_KV_HEADS, rows, 1), F32), pltpu.VMEM((A_KV_HEADS, rows, 128), F32)],
        compiler_params=_cparams(2),
        name="dsa_attn",
    )(q, qi, wi, kidx, k, v)


def _mla_body(q_ref, kn_ref, kr_ref, v_ref, o_ref, m_ref, l_ref, acc_ref, *, tq, kb, q_off, s_valid):
    i = pl.program_id(1)
    n_blk = _n_key_blocks(i, tq, kb, q_off, s_valid)
    qc = _q_chunk(i, tq, q_off)
    _flash_init(m_ref, l_ref, acc_ref)

    def attend(j, carry):
        ks = pl.multiple_of(j * kb, kb)
        kr = kr_ref[pl.ds(ks, kb), :]
        ok = _k_chunk(ks, kb, s_valid) <= qc
        for hd in range(B_HEADS):
            kcat = jnp.concatenate([kn_ref[pl.ds(ks, kb), hd * 128:(hd + 1) * 128], kr], axis=1)
            s = jnp.where(ok, _dot_nt(q_ref[:, hd * 256:(hd + 1) * 256], kcat), NEG)
            _flash_step(s, v_ref[pl.ds(ks, kb), hd * 128:(hd + 1) * 128], m_ref.at[hd], l_ref.at[hd], acc_ref.at[hd])
        return carry

    lax.fori_loop(0, n_blk, attend, 0)
    for hd in range(B_HEADS):
        o_ref[:, hd * 128:(hd + 1) * 128] = (acc_ref[hd] / l_ref[hd]).astype(BF16)


def _mla_attn(q, kn, kr, v, *, tq, kb, q_off, s_valid):
    b, t, _ = q.shape
    s_pad = kn.shape[1]
    return pl.pallas_call(
        functools.partial(_mla_body, tq=tq, kb=kb, q_off=q_off, s_valid=s_valid),
        grid=(b, t // tq),
        in_specs=[pl.BlockSpec((None, tq, B_HEADS * 256), lambda bi, i: (bi, i, 0)),
                  pl.BlockSpec((None, s_pad, B_HEADS * 128), lambda bi, i: (bi, 0, 0)),
                  pl.BlockSpec((None, s_pad, 128), lambda bi, i: (bi, 0, 0)),
                  pl.BlockSpec((None, s_pad, B_HEADS * 128), lambda bi, i: (bi, 0, 0))],
        out_specs=pl.BlockSpec((None, tq, B_HEADS * V_DIM), lambda bi, i: (bi, i, 0)),
        out_shape=jax.ShapeDtypeStruct((b, t, B_HEADS * V_DIM), BF16),
        scratch_shapes=[pltpu.VMEM((B_HEADS, tq, 1), F32), pltpu.VMEM((B_HEADS, tq, 1), F32),
                        pltpu.VMEM((B_HEADS, tq, 128), F32)],
        compiler_params=_cparams(2),
        name="mla_attn",
    )(q, kn, kr, v)


def _diff_body(q_ref, k_ref, v_ref, lq1_ref, lk1_ref, lq2_ref, lk2_ref, sn_ref, o_ref, m_ref, l_ref, acc_ref,
               *, tq, kb, q_off, s_valid, lam_init):
    i = pl.program_id(1)
    n_blk = _n_key_blocks(i, tq, kb, q_off, s_valid)
    qc = _q_chunk(i, tq, q_off)
    _flash_init(m_ref, l_ref, acc_ref)

    def attend(j, carry):
        ks = pl.multiple_of(j * kb, kb)
        ok = _k_chunk(ks, kb, s_valid) <= qc
        for hd in range(C_HEADS):
            v = v_ref[pl.ds(ks, kb), hd * 256:(hd + 1) * 256]
            for p in range(2):
                c = 2 * hd + p
                s = _dot_nt(q_ref[:, c * 128:(c + 1) * 128], k_ref[pl.ds(ks, kb), c * 128:(c + 1) * 128])
                _flash_step(jnp.where(ok, s, NEG), v, m_ref.at[c], l_ref.at[c], acc_ref.at[c])
        return carry

    lax.fori_loop(0, n_blk, attend, 0)
    lam = (jnp.exp(jnp.sum(lq1_ref[...] * lk1_ref[...], axis=1, keepdims=True))
           - jnp.exp(jnp.sum(lq2_ref[...] * lk2_ref[...], axis=1, keepdims=True)) + lam_init)
    for hd in range(C_HEADS):
        o = acc_ref[2 * hd] / l_ref[2 * hd] - lam * (acc_ref[2 * hd + 1] / l_ref[2 * hd + 1])
        o_ref[:, hd * 256:(hd + 1) * 256] = (_rms(o, sn_ref[...], 2 * C_DIM) * (1.0 - lam_init)).astype(BF16)


def _diff_attn(q, k, v, lq1, lk1, lq2, lk2, sn, *, tq, kb, q_off, s_valid, lam_init):
    b, t, c = q.shape
    s_pad = k.shape[1]
    vec = lambda n: pl.BlockSpec((1, n), lambda bi, i: (0, 0))
    return pl.pallas_call(
        functools.partial(_diff_body, tq=tq, kb=kb, q_off=q_off, s_valid=s_valid, lam_init=lam_init),
        grid=(b, t // tq),
        in_specs=[pl.BlockSpec((None, tq, c), lambda bi, i: (bi, i, 0)),
                  pl.BlockSpec((None, s_pad, c), lambda bi, i: (bi, 0, 0)),
                  pl.BlockSpec((None, s_pad, c), lambda bi, i: (bi, 0, 0)),
                  vec(128), vec(128), vec(128), vec(128), vec(256)],
        out_specs=pl.BlockSpec((None, tq, c), lambda bi, i: (bi, i, 0)),
        out_shape=jax.ShapeDtypeStruct((b, t, c), BF16),
        scratch_shapes=[pltpu.VMEM((2 * C_HEADS, tq, 1), F32), pltpu.VMEM((2 * C_HEADS, tq, 1), F32),
                        pltpu.VMEM((2 * C_HEADS, tq, 256), F32)],
        compiler_params=_cparams(2),
        name="diff_attn",
    )(q, k, v, lq1, lk1, lq2, lk2, sn)


SCORE_BLOCKS = 4
SEARCH_BLOCKS = 4
SEARCH_BITS_UNCHECKED = 26
SEARCH_BITS_PER_CHECK = 2


def _stage_bufs(n_chain, kb, r):
    return [pltpu.VMEM((n_chain, 2, kb, r), F32), pltpu.VMEM((n_chain, 2, kb, r), BF16),
            pltpu.VMEM((n_chain, 2, 1, r), F32)]


def _staged_flash_t(bufs, state, lo, hi, n_kb, qk, prep, vt_of, first=True, last=True):
    s_ref, p_ref, a_ref = bufs
    m_ref, l_ref, acc_ref = state
    n_chain = s_ref.shape[0]
    clamp = lambda j: jnp.clip(j, 0, n_kb - 1)

    def softmax(c, j, slot):
        st = prep(c, j, s_ref[c, slot])
        m_prev = m_ref[c]
        m_new = jnp.maximum(m_prev, jnp.max(st, axis=0, keepdims=True))
        alpha = jnp.exp2(m_prev - m_new)
        p = jnp.exp2(st - m_new)
        l_ref[c] = alpha * l_ref[c] + jnp.sum(p, axis=0, keepdims=True)
        m_ref[c] = m_new
        p_ref[c, slot] = p.astype(BF16)
        a_ref[c, slot] = alpha

    def values(c, j, slot):
        acc_ref[c] = a_ref[c, slot] * acc_ref[c] + _dot(vt_of(c, clamp(j)), p_ref[c, slot])

    if first:
        for c in range(n_chain):
            s_ref[c, 0] = qk(c, clamp(lo))
            p_ref[c, 1] = jnp.zeros(p_ref.shape[2:], BF16)
            a_ref[c, 1] = jnp.ones(a_ref.shape[2:], F32)

    def turn(t, carry):
        j0 = lo + 2 * t
        for c in range(n_chain):
            s_ref[c, 1] = qk(c, clamp(j0 + 1))
        for c in range(n_chain):
            softmax(c, j0, 0)
        for c in range(n_chain):
            values(c, j0 - 1, 1)
        for c in range(n_chain):
            s_ref[c, 0] = qk(c, clamp(j0 + 2))
        for c in range(n_chain):
            softmax(c, j0 + 1, 1)
        for c in range(n_chain):
            values(c, j0, 0)
        return carry

    n_turn = lax.shift_right_logical(hi - lo + 1, 1)
    lax.fori_loop(0, n_turn, turn, 0)
    if last:
        for c in range(n_chain):
            values(c, lo + 2 * n_turn - 1, 1)


def _dsa_t_body(q_ref, qi_ref, wi_ref, kidx_ref, k_ref, vt_ref, o_ref, keys_ref, sc_ref, s_ref, p_ref, a_ref,
                m_ref, l_ref, acc_ref, *, tq, kb, q_off, s_valid, topk):
    i = pl.program_id(1)
    n_blk = _n_key_blocks(i, tq, kb, q_off, s_valid)
    n_kb = keys_ref.shape[0]
    n_sb = lax.shift_right_logical(n_blk + SEARCH_BLOCKS - 1, int(math.log2(SEARCH_BLOCKS)))
    limit = _key_limit(i, tq, q_off, s_valid)
    kidx0 = lax.broadcasted_iota(I32, (kb, tq), 0)
    wit = wi_ref[...].T
    wrow = [wit[hd:hd + 1, :] for hd in range(IDX_HEADS)]

    def score_blocks(jj, carry):
        ks = pl.multiple_of(jj * (SCORE_BLOCKS * kb), SCORE_BLOCKS * kb)
        half = SCORE_BLOCKS // 2
        q_all = jnp.concatenate([qi_ref[:, p * 128:(p + 1) * 128] for p in range(IDX_HEADS // 2)], axis=0)
        for hf in range(2):
            rows = pl.ds(ks + hf * half * kb, half * kb)
            sc_ref[hf, 0] = _dot_nt(kidx_ref[rows, 0:128], q_all)
            sc_ref[hf, 1] = _dot_nt(kidx_ref[rows, 128:256], q_all)
        for hf in range(2):
            for u in range(half):
                blk = slice(u * kb, (u + 1) * kb)
                sc = jnp.zeros((kb, tq), F32)
                for p in range(IDX_HEADS // 2):
                    sc = sc + wrow[2 * p] * jnp.maximum(sc_ref[hf, 0, blk, p * tq:(p + 1) * tq], 0.0)
                    sc = sc + wrow[2 * p + 1] * jnp.maximum(sc_ref[hf, 1, blk, p * tq:(p + 1) * tq], 0.0)
                bits = lax.bitcast_convert_type(sc, I32)
                key = bits ^ ((bits >> 31) & 0x7FFFFFFF)
                j = jj * SCORE_BLOCKS + hf * half + u
                keys_ref[j] = jnp.where(kidx0 < limit - j * kb, key, INT_MIN)
        return carry

    def pad_block(j, carry):
        keys_ref[j] = jnp.full((kb, tq), INT_MIN, I32)
        return carry

    n_scored = lax.shift_right_logical(n_blk + SCORE_BLOCKS - 1, int(math.log2(SCORE_BLOCKS)))
    lax.fori_loop(0, n_scored, score_blocks, 0)
    lax.fori_loop(n_scored * SCORE_BLOCKS, n_sb * SEARCH_BLOCKS, pad_block, 0)

    def count(pred):
        def group(jj, cs):
            out = []
            for u in range(SEARCH_BLOCKS):
                j = jj * SEARCH_BLOCKS + u
                x = jnp.where(pred(keys_ref[j], j * kb), 1.0, 0.0).reshape(kb // 8, 8, tq)
                h = kb // 32
                out.append(cs[u] + ((jnp.sum(x[:h], axis=0) + jnp.sum(x[h:2 * h], axis=0))
                                    + (jnp.sum(x[2 * h:3 * h], axis=0) + jnp.sum(x[3 * h:], axis=0))))
            return tuple(out)
        cs = lax.fori_loop(0, n_sb, group, tuple(jnp.zeros((8, tq), F32) for _ in range(SEARCH_BLOCKS)))
        return jnp.sum(functools.reduce(lambda a, b: a + b, cs), axis=0, keepdims=True)

    kf = float(topk)

    def bit_step(b, state):
        t, n_ge = state
        cand = t + jnp.left_shift(jnp.int32(1), 31 - b)
        cnt = count(lambda kblk, _: kblk >= cand)
        take = cnt >= kf
        return jnp.where(take, cand, t), jnp.where(take, cnt, n_ge)

    def unsettled(n_ge):
        open_ = jnp.logical_and(n_ge != kf, limit.astype(F32) > kf)
        return jnp.max(jnp.where(open_, 1.0, 0.0)) > 0.0

    def more_bits(state):
        b, _, n_ge = state
        return jnp.logical_and(b < 32, unsettled(n_ge))

    def four_bits(state):
        b, t, n_ge = state
        for u in range(SEARCH_BITS_PER_CHECK):
            t, n_ge = bit_step(b + u, (t, n_ge))
        return b + SEARCH_BITS_PER_CHECK, t, n_ge

    start = (jnp.full((1, tq), INT_MIN, I32), jnp.full((1, tq), float(n_kb * kb), F32))
    head = lax.fori_loop(0, SEARCH_BITS_UNCHECKED, bit_step, start)
    _, thr, _ = lax.while_loop(more_bits, four_bits, (jnp.int32(SEARCH_BITS_UNCHECKED),) + head)
    need = kf - count(lambda kblk, _: kblk > thr)
    n_eq = count(lambda kblk, _: kblk == thr)
    partial = jnp.logical_and(n_eq > need, thr != INT_MIN)

    @pl.when(jnp.max(jnp.where(partial, 1.0, 0.0)) > 0.0)
    def _():
        def idx_step(b, x):
            cand = x + jnp.left_shift(jnp.int32(1), b)
            cnt = count(lambda kblk, k0: jnp.logical_and(kblk == thr, kidx0 < cand - k0))
            return jnp.where(cnt < need, cand, x)

        nbits = max(1, int(s_valid - 1).bit_length())
        cut = lax.fori_loop(0, nbits, lambda b, x: idx_step(nbits - 1 - b, x), jnp.zeros((1, tq), I32))

        def demote(j, carry):
            kblk = keys_ref[j]
            drop = jnp.logical_and(jnp.logical_and(kblk == thr, kidx0 > cut - j * kb), partial)
            keys_ref[j] = jnp.where(drop, kblk - 1, kblk)
            return carry

        lax.fori_loop(0, n_blk, demote, 0)

    thr_sel = jnp.maximum(thr, INT_MIN + 1)

    _flash_init(m_ref, l_ref, acc_ref)
    n_pair = A_HEADS // 2

    def to_bias(j, carry):
        keys_ref[j] = lax.bitcast_convert_type(jnp.where(keys_ref[j] >= thr_sel, 0.0, NEG), I32)
        return carry

    lax.fori_loop(0, n_sb * SEARCH_BLOCKS, to_bias, 0)
    for g in range(A_KV_HEADS):
        def qk(c, j, g=g):
            p = g * (A_GROUP // 2) + c
            qp = jnp.concatenate([q_ref[:, (2 * p) * 128:(2 * p + 1) * 128],
                                  q_ref[:, (2 * p + 1) * 128:(2 * p + 2) * 128]], axis=0)
            return _dot_nt(k_ref[pl.ds(pl.multiple_of(j * kb, kb), kb), g * 128:(g + 1) * 128], qp)

        def prep(c, j, st):
            bias = lax.bitcast_convert_type(keys_ref[j], F32)
            return st + jnp.concatenate([bias, bias], axis=1)

        def vt_of(c, j, g=g):
            return vt_ref[j, g * 128:(g + 1) * 128, :]

        chains = pl.ds(g * (A_GROUP // 2), A_GROUP // 2)
        _staged_flash_t((s_ref.at[chains], p_ref.at[chains], a_ref.at[chains]),
                        (m_ref.at[chains], l_ref.at[chains], acc_ref.at[chains]), 0, n_blk, n_kb, qk, prep, vt_of)
    for p in range(n_pair):
        ot = acc_ref[p] / l_ref[p]
        for a in range(2):
            hd = 2 * p + a
            o_ref[:, hd * 128:(hd + 1) * 128] = ot[:, a * tq:(a + 1) * tq].T.astype(BF16)


def _dsa_attn_t(q, qi, wi, kidx, k, vt, *, tq, kb, q_off, s_valid, topk):
    b, t, _ = q.shape
    s_pad = k.shape[1]
    n_kb = s_pad // kb
    qspec = lambda c: pl.BlockSpec((None, tq, c), lambda bi, i: (bi, i, 0))
    kspec = lambda c: pl.BlockSpec((None, s_pad, c), lambda bi, i: (bi, 0, 0))
    n_pair = A_HEADS // 2
    return pl.pallas_call(
        functools.partial(_dsa_t_body, tq=tq, kb=kb, q_off=q_off, s_valid=s_valid, topk=topk),
        grid=(b, t // tq),
        in_specs=[qspec(1024), qspec(512), qspec(LANES),
                  kspec(256), kspec(256), pl.BlockSpec((None, n_kb, 256, kb), lambda bi, i: (bi, 0, 0, 0))],
        out_specs=qspec(1024),
        out_shape=jax.ShapeDtypeStruct((b, t, 1024), BF16),
        scratch_shapes=[pltpu.VMEM((n_kb, kb, tq), I32),
                        pltpu.VMEM((2, 2, SCORE_BLOCKS // 2 * kb, IDX_HEADS // 2 * tq), F32)]
        + _stage_bufs(n_pair, kb, 2 * tq)
        + [pltpu.VMEM((n_pair, 1, 2 * tq), F32), pltpu.VMEM((n_pair, 1, 2 * tq), F32),
           pltpu.VMEM((n_pair, 128, 2 * tq), F32)],
        compiler_params=_cparams(2),
        name="dsa_attn_t",
    )(q, qi, wi, kidx, k, vt)


def _causal_flash_t(bufs, state, i, n_kb, tq, kb, q_off, s_valid, qk, vt_of):
    n_blk = _n_key_blocks(i, tq, kb, q_off, s_valid)
    n_plain = _n_full_blocks(i, tq, kb, q_off, s_valid) & -2
    limit = _key_limit(i, tq, q_off, s_valid)

    def masked(c, j, st):
        return jnp.where(lax.broadcasted_iota(I32, (kb, tq), 0) < limit - j * kb, st, NEG)

    _staged_flash_t(bufs, state, 0, n_plain, n_kb, qk, lambda c, j, st: st, vt_of, last=False)
    _staged_flash_t(bufs, state, n_plain, n_blk, n_kb, qk, masked, vt_of, first=False)


def _mla_t_body(qt_ref, kn_ref, kr_ref, vt_ref, o_ref, s_ref, p_ref, a_ref, m_ref, l_ref, acc_ref,
                *, tq, kb, q_off, s_valid):
    i = pl.program_id(2)
    _flash_init(m_ref, l_ref, acc_ref)

    def qk(a, j):
        ks = pl.multiple_of(j * kb, kb)
        kcat = jnp.concatenate([kn_ref[pl.ds(ks, kb), a * 128:(a + 1) * 128], kr_ref[pl.ds(ks, kb), :]], axis=1)
        return _dot(kcat, qt_ref[a * 256:(a + 1) * 256, :])

    def vt_of(a, j):
        return vt_ref[j, a * 128:(a + 1) * 128, :]

    _causal_flash_t((s_ref, p_ref, a_ref), (m_ref, l_ref, acc_ref), i, vt_ref.shape[0], tq, kb, q_off, s_valid,
                    qk, vt_of)
    for a in range(2):
        o_ref[:, a * 128:(a + 1) * 128] = (acc_ref[a] / l_ref[a]).T.astype(BF16)


def _mla_attn_t(qt, kn, kr, vt, *, tq, kb, q_off, s_valid):
    b, _, t = qt.shape
    s_pad = kn.shape[1]
    n_kb = s_pad // kb
    return pl.pallas_call(
        functools.partial(_mla_t_body, tq=tq, kb=kb, q_off=q_off, s_valid=s_valid),
        grid=(b, B_HEADS // 2, t // tq),
        in_specs=[pl.BlockSpec((None, 512, tq), lambda bi, h, i: (bi, h, i)),
                  pl.BlockSpec((None, s_pad, 256), lambda bi, h, i: (bi, 0, h)),
                  pl.BlockSpec((None, s_pad, 128), lambda bi, h, i: (bi, 0, 0)),
                  pl.BlockSpec((None, n_kb, 256, kb), lambda bi, h, i: (bi, 0, h, 0))],
        out_specs=pl.BlockSpec((None, tq, 256), lambda bi, h, i: (bi, i, h)),
        out_shape=jax.ShapeDtypeStruct((b, t, B_HEADS * V_DIM), BF16),
        scratch_shapes=_stage_bufs(2, kb, tq) + [pltpu.VMEM((2, 1, tq), F32), pltpu.VMEM((2, 1, tq), F32),
                                                 pltpu.VMEM((2, 128, tq), F32)],
        compiler_params=_cparams(3),
        name="mla_attn_t",
    )(qt, kn, kr, vt)


def _diff_t_body(q_ref, k_ref, vt_ref, lq1_ref, lk1_ref, lq2_ref, lk2_ref, sn_ref, o_ref, s_ref, p_ref, a_ref,
                 m_ref, l_ref, acc_ref, *, tq, kb, q_off, s_valid, lam_init):
    i = pl.program_id(2)
    _flash_init(m_ref, l_ref, acc_ref)

    def qk(p, j):
        ks = pl.multiple_of(j * kb, kb)
        return _dot_nt(k_ref[pl.ds(ks, kb), p * 128:(p + 1) * 128], q_ref[:, p * 128:(p + 1) * 128])

    _causal_flash_t((s_ref, p_ref, a_ref), (m_ref, l_ref, acc_ref), i, vt_ref.shape[0], tq, kb, q_off, s_valid,
                    qk, lambda p, j: vt_ref[j])
    lam = (jnp.exp(jnp.sum(lq1_ref[...] * lk1_ref[...], axis=1, keepdims=True))
           - jnp.exp(jnp.sum(lq2_ref[...] * lk2_ref[...], axis=1, keepdims=True)) + lam_init)
    ot = acc_ref[0] / l_ref[0] - lam * (acc_ref[1] / l_ref[1])
    o = jnp.concatenate([ot[0:128, :].T, ot[128:256, :].T], axis=1)
    o_ref[...] = (_rms(o, sn_ref[...], 2 * C_DIM) * (1.0 - lam_init)).astype(BF16)


def _diff_attn_t(q, k, vt, lq1, lk1, lq2, lk2, sn, *, tq, kb, q_off, s_valid, lam_init):
    b, t, _ = q.shape
    s_pad = k.shape[1]
    n_kb = s_pad // kb
    vec = lambda c: pl.BlockSpec((1, c), lambda bi, h, i: (0, 0))
    return pl.pallas_call(
        functools.partial(_diff_t_body, tq=tq, kb=kb, q_off=q_off, s_valid=s_valid, lam_init=lam_init),
        grid=(b, C_HEADS, t // tq),
        in_specs=[pl.BlockSpec((None, tq, 256), lambda bi, h, i: (bi, i, h)),
                  pl.BlockSpec((None, s_pad, 256), lambda bi, h, i: (bi, 0, h)),
                  pl.BlockSpec((None, n_kb, 256, kb), lambda bi, h, i: (bi, 0, h, 0)),
                  vec(128), vec(128), vec(128), vec(128), vec(256)],
        out_specs=pl.BlockSpec((None, tq, 256), lambda bi, h, i: (bi, i, h)),
        out_shape=jax.ShapeDtypeStruct((b, t, C_HEADS * 2 * C_DIM), BF16),
        scratch_shapes=_stage_bufs(2, kb, tq) + [pltpu.VMEM((2, 1, tq), F32), pltpu.VMEM((2, 1, tq), F32),
                                                 pltpu.VMEM((2, 256, tq), F32)],
        compiler_params=_cparams(3),
        name="diff_attn_t",
    )(q, k, vt, lq1, lk1, lq2, lk2, sn)


def _blocked_t(v, kb):
    b, s, c = v.shape
    return jnp.swapaxes(v.reshape(b, s // kb, kb, c), 2, 3)


def _rope_tables(pos, reps):
    p = pos.astype(F32)[:, None]
    inv64 = jnp.power(ROPE_THETA, -jnp.arange(64, dtype=F32) / 64)
    inv32 = jnp.power(ROPE_THETA, -jnp.arange(32, dtype=F32) / 32)
    c, s = jnp.cos(p * inv64), jnp.sin(p * inv64)
    c3, s3 = jnp.cos(p * inv32), jnp.sin(p * inv32)
    z = jnp.zeros_like(s3)
    tabs = {
        "c128": jnp.concatenate([c, c], axis=1),
        "s128": jnp.concatenate([-s, s], axis=1),
        "c64": jnp.concatenate([c3, c3, c3, c3], axis=1),
        "sa64": jnp.concatenate([-s3, z, -s3, z], axis=1),
        "sb64": jnp.concatenate([z, s3, z, s3], axis=1),
    }
    return {k: jnp.tile(v, (reps, 1)) for k, v in tabs.items()}


def _pad_cols(w, n):
    return jnp.pad(w, ((0, 0), (0, n - w.shape[1])))


def _pad_lanes(g, n=128):
    g = g.reshape(1, -1)
    return jnp.pad(g, ((0, 0), (0, n - g.shape[1])))


def _prep_weights(W):
    P = {}
    for nm in ("ffn1", "ffn2"):
        P[nm] = [(W[nm + "_norm"][i].reshape(1, -1), W[nm + "_wg"][i].astype(BF16), W[nm + "_wu"][i].astype(BF16),
                  W[nm + "_wd"][i].astype(BF16)) for i in range(DEPTH)]
    P["mix_norm"] = [W["mix_norm"][i].reshape(1, -1) for i in range(DEPTH)]
    P["a"] = [dict(w=_pad_cols(W["a_w_in"][j], A_IN_PAD).astype(BF16), qn=W["a_q_norm"][j].reshape(1, -1),
                   kn=W["a_k_norm"][j].reshape(1, -1), ikn=_pad_lanes(W["a_idx_k_norm"][j]),
                   wo=W["a_w_out"][j].astype(BF16)) for j in range(W["a_w_in"].shape[0])]
    P["b"] = []
    for j in range(W["b_w_in"].shape[0]):
        wuq = W["b_w_uq"][j].reshape(Q_LORA, B_HEADS, NOPE_DIM + ROPE_DIM)
        wuq_rope = jnp.pad(wuq[:, :, NOPE_DIM:], ((0, 0), (0, 0), (0, 128 - ROPE_DIM)))
        wuq = jnp.concatenate([wuq[:, :, :NOPE_DIM], wuq_rope], axis=2).reshape(Q_LORA, -1)
        P["b"].append(dict(
            w=_pad_cols(W["b_w_in"][j], B_IN_PAD).astype(BF16), qan=W["b_q_a_norm"][j].reshape(1, -1),
            kvan=W["b_kv_a_norm"][j].reshape(1, -1), krn=_pad_lanes(W["b_k_rope_norm"][j]), wuq=wuq.astype(BF16),
            qnn=W["b_q_nope_norm"][j].reshape(1, -1), qrn=_pad_lanes(W["b_q_rope_norm"][j]),
            wukv=W["b_w_ukv"][j].astype(BF16), knn=W["b_k_nope_norm"][j].reshape(1, -1),
            wo=W["b_w_out"][j].astype(BF16)))
    P["c"] = [dict(w=W["c_w_in"][j].astype(BF16), qn=W["c_q_norm"][j].reshape(1, -1),
                   kn=W["c_k_norm"][j].reshape(1, -1), lq1=W["c_lambda_q1"][j].reshape(1, -1),
                   lk1=W["c_lambda_k1"][j].reshape(1, -1), lq2=W["c_lambda_q2"][j].reshape(1, -1),
                   lk2=W["c_lambda_k2"][j].reshape(1, -1), sn=W["c_sub_norm"][j].reshape(1, -1),
                   wo=W["c_w_out"][j].astype(BF16)) for j in range(W["c_w_in"].shape[0])]
    return P


def _with_past(past, new, s_pad):
    x = new if past is None else jnp.concatenate([past.astype(new.dtype), new], axis=1)
    return x if x.shape[1] == s_pad else jnp.pad(x, ((0, 0), (0, s_pad - x.shape[1]), (0, 0)))


def _trunk(x, offset, past, P, cfg):
    b, t, _ = x.shape
    n = b * t
    tm, tq_a, tq, kb, key_major = cfg["tm"], cfg["tq_a"], cfg["tq"], cfg["kb"], cfg["key_major"]
    p_len = 0 if past is None else past["a_k"].shape[2]
    s_valid = p_len + t
    s_pad = -(-s_valid // kb) * kb
    tabs = _rope_tables(offset + jnp.arange(t, dtype=I32), tm // t if tm > t else 1)
    att = dict(kb=kb, q_off=offset, s_valid=s_valid)
    rows = {k: [] for k in ("a_k", "a_v", "a_ik", "b_ckv", "b_kr", "c_k", "c_v")}
    x = x.reshape(n, D_MODEL)
    r3 = lambda a: a.reshape(b, t, a.shape[-1])
    t3 = lambda a: jnp.swapaxes(r3(a), 1, 2)
    pj = lambda nm, j: None if past is None else past[nm][j].reshape(b, p_len, -1)
    for i in range(DEPTH):
        x = _ffn(x, *P["ffn1"][i], cfg["tm_ffn"])
        kind, j = i % N_MIXERS, i // N_MIXERS
        g = P["mix_norm"][i]
        if kind == 0:
            pa = P["a"][j]
            q, k32, v32, kbf, vbf, qi, ki32, kidx, wi = _a_proj(x, g, pa["w"], pa["qn"], pa["kn"], pa["ikn"], tabs, tm)
            rows["a_k"].append(k32.reshape(b, t, A_KV_HEADS, A_HEAD_DIM))
            rows["a_v"].append(v32.reshape(b, t, A_KV_HEADS, A_HEAD_DIM))
            rows["a_ik"].append(ki32.reshape(b, t, IDX_DIM))
            pik = pj("a_ik", j)
            if pik is not None:
                z = jnp.zeros_like(pik)
                pik = jnp.concatenate([pik, z, z, pik], axis=-1)
            kidx_all = _with_past(pik, r3(kidx), s_pad)
            k_all = _with_past(pj("a_k", j), r3(kbf), s_pad)
            v_all = _with_past(pj("a_v", j), r3(vbf), s_pad)
            topk = min(TOPK_MAX, s_valid // 4)
            if key_major:
                o = _dsa_attn_t(r3(q), r3(qi), r3(wi), kidx_all, k_all, _blocked_t(v_all, kb),
                                tq=tq_a, topk=topk, **att)
            else:
                o = _dsa_attn(r3(q), r3(qi), r3(wi), kidx_all, k_all, v_all, tq=tq_a, topk=topk, **att)
        elif kind == 1:
            pb = P["b"][j]
            q, ckv, kr32, krbf = _b_proj(x, g, pb["w"], pb["qan"], pb["kvan"], pb["krn"], pb["wuq"], pb["qnn"],
                                         pb["qrn"], tabs, tm)
            rows["b_ckv"].append(ckv.reshape(b, t, KV_LORA))
            rows["b_kr"].append(kr32.reshape(b, t, ROPE_DIM))
            ckv_all = _with_past(pj("b_ckv", j), r3(ckv), s_pad)
            pkr = pj("b_kr", j)
            if pkr is not None:
                pkr = jnp.concatenate([pkr, jnp.zeros_like(pkr)], axis=-1)
            kn, v = _kv_up(ckv_all.reshape(b * s_pad, KV_LORA), pb["wukv"], pb["knn"], math.gcd(b * s_pad, 512))
            kn, v, kr_all = kn.reshape(b, s_pad, -1), v.reshape(b, s_pad, -1), _with_past(pkr, r3(krbf), s_pad)
            if key_major:
                o = _mla_attn_t(t3(q), kn, kr_all, _blocked_t(v, kb), tq=cfg["tq_mla"], **att)
            else:
                o = _mla_attn(r3(q), kn, kr_all, v, tq=tq, **att)
        else:
            pc = P["c"][j]
            q, k32, v32, kbf, vbf = _c_proj(x, g, pc["w"], pc["qn"], pc["kn"], tabs, tm)
            rows["c_k"].append(k32.reshape(b, t, C_HEADS, 2, C_DIM))
            rows["c_v"].append(v32.reshape(b, t, C_HEADS, 2 * C_DIM))
            k_all = _with_past(pj("c_k", j), r3(kbf), s_pad)
            v_all = _with_past(pj("c_v", j), r3(vbf), s_pad)
            lam = (pc["lq1"], pc["lk1"], pc["lq2"], pc["lk2"], pc["sn"])
            lam_init = 0.8 - 0.6 * math.exp(-0.3 * i)
            if key_major:
                att_c = dict(att, kb=cfg["kb_diff"])
                o = _diff_attn_t(r3(q), k_all, _blocked_t(v_all, cfg["kb_diff"]), *lam, tq=tq, lam_init=lam_init,
                                 **att_c)
            else:
                o = _diff_attn(r3(q), k_all, v_all, *lam, tq=tq, lam_init=lam_init, **att)
        wo = (P["a"], P["b"], P["c"])[kind][j]["wo"]
        x = _ffn(x, *P["ffn2"][i], cfg["tm_ffn"], attn=o.reshape(n, -1), wo=wo)
    order = ("a_k", "a_v", "a_ik", "b_ckv", "b_kr", "c_k", "c_v")
    stack = lambda rs: rs[0][None] if len(rs) == 1 else jnp.stack(rs)
    return x.reshape(b, t, D_MODEL), tuple(stack(rows[k]) for k in order)


PROMPT_CFG = dict(tm=512, tm_ffn=1024, tq_a=256, tq=1024, tq_mla=1024, kb=256, kb_diff=512, key_major=True)
SAMPLE_CFG = dict(tm=128, tm_ffn=128, tq_a=16, tq=16, kb=1280, key_major=False)


@jax.jit
def _forward(x_prompt, x_sample, past, W):
    P = _prep_weights(W)
    y_p, rows_p = _trunk(x_prompt, 0, None, P, PROMPT_CFG)
    y_s, rows_s = _trunk(x_sample, past["a_k"].shape[2], past, P, SAMPLE_CFG)
    return (y_p, y_s) + rows_p + rows_s


def kernel(x_prompt, x_sample, cache_a_k, cache_a_v, cache_a_idx_k, cache_b_ckv, cache_b_krope, cache_c_k, cache_c_v, ffn1_norm, ffn1_wg, ffn1_wu, ffn1_wd, mix_norm, ffn2_norm, ffn2_wg, ffn2_wu, ffn2_wd, a_w_in, a_q_norm, a_k_norm, a_idx_k_norm, a_w_out, b_w_in, b_q_a_norm, b_kv_a_norm, b_w_uq, b_w_ukv, b_q_nope_norm, b_q_rope_norm, b_k_nope_norm, b_k_rope_norm, b_w_out, c_w_in, c_q_norm, c_k_norm, c_lambda_q1, c_lambda_k1, c_lambda_q2, c_lambda_k2, c_sub_norm, c_w_out):
    W = dict(ffn1_norm=ffn1_norm, ffn1_wg=ffn1_wg, ffn1_wu=ffn1_wu, ffn1_wd=ffn1_wd, mix_norm=mix_norm,
             ffn2_norm=ffn2_norm, ffn2_wg=ffn2_wg, ffn2_wu=ffn2_wu, ffn2_wd=ffn2_wd,
             a_w_in=a_w_in, a_q_norm=a_q_norm, a_k_norm=a_k_norm, a_idx_k_norm=a_idx_k_norm, a_w_out=a_w_out,
             b_w_in=b_w_in, b_q_a_norm=b_q_a_norm, b_kv_a_norm=b_kv_a_norm, b_w_uq=b_w_uq, b_w_ukv=b_w_ukv,
             b_q_nope_norm=b_q_nope_norm, b_q_rope_norm=b_q_rope_norm, b_k_nope_norm=b_k_nope_norm,
             b_k_rope_norm=b_k_rope_norm, b_w_out=b_w_out,
             c_w_in=c_w_in, c_q_norm=c_q_norm, c_k_norm=c_k_norm, c_lambda_q1=c_lambda_q1,
             c_lambda_k1=c_lambda_k1, c_lambda_q2=c_lambda_q2, c_lambda_k2=c_lambda_k2, c_sub_norm=c_sub_norm,
             c_w_out=c_w_out)
    past = dict(a_k=cache_a_k, a_v=cache_a_v, a_ik=cache_a_idx_k, b_ckv=cache_b_ckv, b_kr=cache_b_krope,
                c_k=cache_c_k, c_v=cache_c_v)
    return _forward(x_prompt, x_sample, past, W)
```

```python
import functools
import math

import jax
import jax.numpy as jnp
from jax import lax
from jax.experimental import pallas as pl
from jax.experimental.pallas import tpu as pltpu

F32 = jnp.float32
BF16 = jnp.bfloat16
I32 = jnp.int32

D_MODEL = 1024
DEPTH = 4
CHUNK_SHIFT = 6
N_MIXERS = 3
ROPE_THETA = 10000.0
EPS = 1e-6
D_FF = 2816

A_HEADS = 8
A_KV_HEADS = 2
A_GROUP = A_HEADS // A_KV_HEADS
A_HEAD_DIM = 128
IDX_HEADS = 8
IDX_DIM = 64
TOPK_MAX = 256
LOG2E = math.log2(math.e)
A_SCALE = A_HEAD_DIM ** -0.5 * LOG2E
IDX_W_SCALE = (IDX_HEADS * IDX_DIM) ** -0.5
A_IN = 2120
A_IN_PAD = 2176

B_HEADS = 8
Q_LORA = 384
KV_LORA = 256
NOPE_DIM = 128
ROPE_DIM = 64
V_DIM = 128
B_SCALE = (NOPE_DIM + ROPE_DIM) ** -0.5 * LOG2E
B_IN = 704
B_IN_PAD = 768

C_HEADS = 4
C_DIM = 128
C_SCALE = C_DIM ** -0.5 * LOG2E

LANES = 128
NEG = -1e30
INT_MIN = -(2 ** 31)
VMEM_LIMIT = 56 * 1024 * 1024


def _cparams(n_axes):
    return pltpu.CompilerParams(dimension_semantics=("arbitrary",) * n_axes, vmem_limit_bytes=VMEM_LIMIT)


def _dot(a, b):
    return jnp.dot(a, b, preferred_element_type=F32)


def _dot_nt(a, b):
    return lax.dot_general(a, b, (((1,), (1,)), ((), ())), preferred_element_type=F32)


def _rms(x, g, n):
    ms = jnp.sum(x * x, axis=-1, keepdims=True) * (1.0 / n)
    return x * lax.rsqrt(ms + EPS) * g


def _rope128(x, c, s):
    return x * c + pltpu.roll(x, 64, 1) * s


def _rope64(x, c, sa, sb):
    return x * c + pltpu.roll(x, 96, 1) * sa + pltpu.roll(x, 32, 1) * sb


FFN_CHUNK = 256


def _ffn_body(*refs, fc, mixed):
    if mixed:
        attn_ref, wo_ref, x_ref, g_ref, wg_ref, wu_ref, wd_ref, o_ref = refs
        x = x_ref[...] + _dot(attn_ref[...], wo_ref[...])
    else:
        x_ref, g_ref, wg_ref, wu_ref, wd_ref, o_ref = refs
        x = x_ref[...]
    h = _rms(x, g_ref[...], D_MODEL).astype(BF16)
    y = None
    for c in range(D_FF // fc):
        a = _dot(h, wg_ref[:, c * fc:(c + 1) * fc])
        u = _dot(h, wu_ref[:, c * fc:(c + 1) * fc])
        act = (a * jax.nn.sigmoid(a) * u).astype(BF16)
        part = _dot(act, wd_ref[c * fc:(c + 1) * fc, :])
        y = part if y is None else y + part
    o_ref[...] = x + 0.5 * y


def _const_spec(shape):
    nd = len(shape)
    return pl.BlockSpec(shape, lambda *_: (0,) * nd, pipeline_mode=pl.Buffered(1))


def _row_spec(tm, n):
    return pl.BlockSpec((tm, n), lambda i: (i, 0))


def _ffn(x, g, wg, wu, wd, tm, attn=None, wo=None):
    n = x.shape[0]
    mixed = attn is not None
    pre_specs = [_row_spec(tm, attn.shape[1]), _const_spec(wo.shape)] if mixed else []
    return pl.pallas_call(
        functools.partial(_ffn_body, fc=FFN_CHUNK, mixed=mixed),
        grid=(n // tm,),
        in_specs=pre_specs + [_row_spec(tm, D_MODEL), _const_spec((1, D_MODEL)), _const_spec((D_MODEL, D_FF)),
                              _const_spec((D_MODEL, D_FF)), _const_spec((D_FF, D_MODEL))],
        out_specs=_row_spec(tm, D_MODEL),
        out_shape=jax.ShapeDtypeStruct((n, D_MODEL), F32),
        compiler_params=_cparams(1),
        name="ffn_mixed" if mixed else "ffn",
    )(*((attn, wo) if mixed else ()), x, g, wg, wu, wd)


PROJ_PARTS = 2

def _a_proj_body(x_ref, g_ref, w_ref, qn_ref, kn_ref, ikn_ref, c128_ref, s128_ref, c64_ref, sa64_ref, sb64_ref,
                 q_ref, k32_ref, v32_ref, kbf_ref, vbf_ref, qi_ref, ki32_ref, kidx_ref, wi_ref, y_ref):
    hm = x_ref.shape[0] // PROJ_PARTS
    for part in range(PROJ_PARTS):
        rows = slice(part * hm, (part + 1) * hm)
        y_ref[part] = _dot(_rms(x_ref[rows, :], g_ref[...], D_MODEL).astype(BF16), w_ref[...])
    for part in range(PROJ_PARTS):
        rows = slice(part * hm, (part + 1) * hm)
        c128, s128 = c128_ref[rows, :], s128_ref[rows, :]
        c64, sa64, sb64 = c64_ref[rows, :], sa64_ref[rows, :], sb64_ref[rows, :]
        for hd in range(A_HEADS):
            sl = slice(hd * 128, (hd + 1) * 128)
            qh = _rope128(_rms(y_ref[part, :, sl], qn_ref[...], A_HEAD_DIM), c128, s128)
            q_ref[rows, sl] = (qh * A_SCALE).astype(BF16)
        for hd in range(A_KV_HEADS):
            sl = slice(hd * 128, (hd + 1) * 128)
            kh = _rope128(_rms(y_ref[part, :, 1024 + hd * 128:1024 + (hd + 1) * 128], kn_ref[...], A_HEAD_DIM),
                          c128, s128)
            k32_ref[rows, sl] = kh
            kbf_ref[rows, sl] = kh.astype(BF16)
        v = y_ref[part, :, 1280:1536]
        v32_ref[rows, :] = v
        vbf_ref[rows, :] = v.astype(BF16)
        for p in range(IDX_HEADS // 2):
            sl = slice(p * 128, (p + 1) * 128)
            qi_ref[rows, sl] = _rope64(y_ref[part, :, 1536 + p * 128:1536 + (p + 1) * 128],
                                       c64, sa64, sb64).astype(BF16)
        tail = y_ref[part, :, 2048:2176]
        lane = lax.broadcasted_iota(I32, tail.shape, 1)
        low = lane < IDX_DIM
        kin = jnp.where(low, tail, 0.0)
        ki = _rope64(_rms(kin, ikn_ref[...], IDX_DIM), c64, sa64, sb64)
        ki = jnp.where(low, ki, 0.0)
        ki32_ref[rows, :] = ki[:, :IDX_DIM]
        kidx_ref[rows, 0:128] = ki.astype(BF16)
        kidx_ref[rows, 128:256] = pltpu.roll(ki, 64, 1).astype(BF16)
        wi_ref[rows, :] = pltpu.roll(tail, 64, 1) * IDX_W_SCALE


def _tab_spec(tm, t):
    nt = t // tm
    return pl.BlockSpec((tm, LANES), lambda i: (i % nt, 0))


def _a_proj(x, g, w, qn, kn, ikn, tabs, tm):
    n = x.shape[0]
    t = tabs["c128"].shape[0]
    outs = [(1024, BF16), (256, F32), (256, F32), (256, BF16), (256, BF16), (512, BF16), (IDX_DIM, F32),
            (256, BF16), (LANES, F32)]
    return pl.pallas_call(
        _a_proj_body,
        grid=(n // tm,),
        in_specs=[_row_spec(tm, D_MODEL), _const_spec((1, D_MODEL)), _const_spec((D_MODEL, A_IN_PAD)),
                  _const_spec((1, 128)), _const_spec((1, 128)), _const_spec((1, 128))]
        + [_tab_spec(tm, t)] * 5,
        out_specs=[_row_spec(tm, c) for c, _ in outs],
        out_shape=[jax.ShapeDtypeStruct((n, c), dt) for c, dt in outs],
        scratch_shapes=[pltpu.VMEM((PROJ_PARTS, tm // PROJ_PARTS, A_IN_PAD), F32)],
        compiler_params=_cparams(1),
        name="a_proj",
    )(x, g, w, qn, kn, ikn, tabs["c128"], tabs["s128"], tabs["c64"], tabs["sa64"], tabs["sb64"])


def _b_proj_body(x_ref, g_ref, w_ref, qan_ref, kvan_ref, krn_ref, wuq_ref, qnn_ref, qrn_ref,
                 c64_ref, sa64_ref, sb64_ref, q_ref, ckv_ref, kr32_ref, krbf_ref, qq_ref):
    hm = x_ref.shape[0] // PROJ_PARTS
    for part in range(PROJ_PARTS):
        rows = slice(part * hm, (part + 1) * hm)
        h = _rms(x_ref[rows, :], g_ref[...], D_MODEL).astype(BF16)
        y = _dot(h, w_ref[...])
        ckv_ref[rows, :] = _rms(y[:, 384:640], kvan_ref[...], KV_LORA)
        kr = _rope64(_rms(y[:, 640:768], krn_ref[...], ROPE_DIM), c64_ref[rows, :], sa64_ref[rows, :],
                     sb64_ref[rows, :])
        kr32_ref[rows, :] = kr[:, :ROPE_DIM]
        krbf_ref[rows, :] = kr.astype(BF16)
        qq_ref[part] = _dot(_rms(y[:, 0:384], qan_ref[...], Q_LORA).astype(BF16), wuq_ref[...])
    for part in range(PROJ_PARTS):
        rows = slice(part * hm, (part + 1) * hm)
        c64, sa64, sb64 = c64_ref[rows, :], sa64_ref[rows, :], sb64_ref[rows, :]
        for hd in range(B_HEADS):
            qn = _rms(qq_ref[part, :, hd * 256:hd * 256 + 128], qnn_ref[...], NOPE_DIM)
            qr = _rope64(_rms(qq_ref[part, :, hd * 256 + 128:(hd + 1) * 256], qrn_ref[...], ROPE_DIM),
                         c64, sa64, sb64)
            q_ref[rows, hd * 256:hd * 256 + 128] = (qn * B_SCALE).astype(BF16)
            q_ref[rows, hd * 256 + 128:(hd + 1) * 256] = (qr * B_SCALE).astype(BF16)


def _b_proj(x, g, w, qan, kvan, krn, wuq, qnn, qrn, tabs, tm):
    n = x.shape[0]
    t = tabs["c64"].shape[0]
    outs = [(B_HEADS * 256, BF16), (KV_LORA, F32), (ROPE_DIM, F32), (LANES, BF16)]
    return pl.pallas_call(
        _b_proj_body,
        grid=(n // tm,),
        in_specs=[_row_spec(tm, D_MODEL), _const_spec((1, D_MODEL)), _const_spec((D_MODEL, B_IN_PAD)),
                  _const_spec((1, Q_LORA)), _const_spec((1, KV_LORA)), _const_spec((1, 128)),
                  _const_spec((Q_LORA, 2048)), _const_spec((1, 128)), _const_spec((1, 128))]
        + [_tab_spec(tm, t)] * 3,
        out_specs=[_row_spec(tm, c) for c, _ in outs],
        out_shape=[jax.ShapeDtypeStruct((n, c), dt) for c, dt in outs],
        scratch_shapes=[pltpu.VMEM((PROJ_PARTS, tm // PROJ_PARTS, 2048), F32)],
        compiler_params=_cparams(1),
        name="b_proj",
    )(x, g, w, qan, kvan, krn, wuq, qnn, qrn, tabs["c64"], tabs["sa64"], tabs["sb64"])


def _kv_up_body(ckv_ref, w_ref, knn_ref, kn_ref, v_ref):
    y = _dot(ckv_ref[...].astype(BF16), w_ref[...])
    for hd in range(B_HEADS):
        kn_ref[:, hd * 128:(hd + 1) * 128] = _rms(y[:, hd * 256:hd * 256 + 128], knn_ref[...], NOPE_DIM).astype(BF16)
        v_ref[:, hd * 128:(hd + 1) * 128] = y[:, hd * 256 + 128:(hd + 1) * 256].astype(BF16)


def _kv_up(ckv, w, knn, tm):
    n = ckv.shape[0]
    return pl.pallas_call(
        _kv_up_body,
        grid=(n // tm,),
        in_specs=[_row_spec(tm, KV_LORA), _const_spec((KV_LORA, 2048)), _const_spec((1, 128))],
        out_specs=[_row_spec(tm, 1024), _row_spec(tm, 1024)],
        out_shape=[jax.ShapeDtypeStruct((n, 1024), BF16)] * 2,
        compiler_params=_cparams(1),
        name="kv_up",
    )(ckv, w, knn)


def _c_proj_body(x_ref, g_ref, w_ref, qn_ref, kn_ref, c128_ref, s128_ref,
                 q_ref, k32_ref, v32_ref, kbf_ref, vbf_ref, y_ref):
    hm = x_ref.shape[0] // PROJ_PARTS
    for part in range(PROJ_PARTS):
        rows = slice(part * hm, (part + 1) * hm)
        y_ref[part] = _dot(_rms(x_ref[rows, :], g_ref[...], D_MODEL).astype(BF16), w_ref[...])
    for part in range(PROJ_PARTS):
        rows = slice(part * hm, (part + 1) * hm)
        c128, s128 = c128_ref[rows, :], s128_ref[rows, :]
        for hd in range(2 * C_HEADS):
            sl = slice(hd * 128, (hd + 1) * 128)
            qh = _rope128(_rms(y_ref[part, :, sl], qn_ref[...], C_DIM), c128, s128)
            q_ref[rows, sl] = (qh * C_SCALE).astype(BF16)
            kh = _rope128(_rms(y_ref[part, :, 1024 + hd * 128:1024 + (hd + 1) * 128], kn_ref[...], C_DIM),
                          c128, s128)
            k32_ref[rows, sl] = kh
            kbf_ref[rows, sl] = kh.astype(BF16)
        v = y_ref[part, :, 2048:3072]
        v32_ref[rows, :] = v
        vbf_ref[rows, :] = v.astype(BF16)


def _c_proj(x, g, w, qn, kn, tabs, tm):
    n = x.shape[0]
    t = tabs["c128"].shape[0]
    outs = [(1024, BF16), (1024, F32), (1024, F32), (1024, BF16), (1024, BF16)]
    return pl.pallas_call(
        _c_proj_body,
        grid=(n // tm,),
        in_specs=[_row_spec(tm, D_MODEL), _const_spec((1, D_MODEL)), _const_spec((D_MODEL, 3072)),
                  _const_spec((1, 128)), _const_spec((1, 128))] + [_tab_spec(tm, t)] * 2,
        out_specs=[_row_spec(tm, c) for c, _ in outs],
        out_shape=[jax.ShapeDtypeStruct((n, c), dt) for c, dt in outs],
        scratch_shapes=[pltpu.VMEM((PROJ_PARTS, tm // PROJ_PARTS, 3072), F32)],
        compiler_params=_cparams(1),
        name="c_proj",
    )(x, g, w, qn, kn, tabs["c128"], tabs["s128"])


def _n_key_blocks(i, tq, kb, q_off, s_valid):
    last_chunk = lax.shift_right_logical(q_off + (i + 1) * tq - 1, CHUNK_SHIFT)
    kend = jnp.minimum((last_chunk + 1) << CHUNK_SHIFT, s_valid)
    return lax.div(kend + kb - 1, jnp.int32(kb))


def _q_chunk(i, tq, q_off):
    row = lax.broadcasted_iota(I32, (tq, 1), 0)
    return lax.shift_right_logical(q_off + i * tq + row, CHUNK_SHIFT)


def _k_chunk(ks, kb, s_valid):
    kpos = ks + lax.broadcasted_iota(I32, (1, kb), 1)
    return jnp.where(kpos < s_valid, lax.shift_right_logical(kpos, CHUNK_SHIFT), 2 ** 30)


def _flash_step(s, v, m_ref, l_ref, acc_ref):
    m_prev = m_ref[...]
    m_new = jnp.maximum(m_prev, jnp.max(s, axis=1, keepdims=True))
    alpha = jnp.exp2(m_prev - m_new)
    p = jnp.exp2(s - m_new)
    l_ref[...] = alpha * l_ref[...] + jnp.sum(p, axis=1, keepdims=True)
    acc_ref[...] = alpha * acc_ref[...] + _dot(p.astype(BF16), v)
    m_ref[...] = m_new


def _key_limit(i, tq, q_off, s_valid):
    qpos = q_off + i * tq + lax.broadcasted_iota(I32, (1, tq), 1)
    return jnp.minimum((lax.shift_right_logical(qpos, CHUNK_SHIFT) + 1) << CHUNK_SHIFT, s_valid)


def _n_full_blocks(i, tq, kb, q_off, s_valid):
    first_chunk = lax.shift_right_logical(q_off + i * tq, CHUNK_SHIFT)
    kend = jnp.minimum((first_chunk + 1) << CHUNK_SHIFT, s_valid)
    return lax.shift_right_logical(kend, int(math.log2(kb)))


def _flash_init(m_ref, l_ref, acc_ref):
    m_ref[...] = jnp.full(m_ref.shape, NEG, F32)
    l_ref[...] = jnp.zeros(l_ref.shape, F32)
    acc_ref[...] = jnp.zeros(acc_ref.shape, F32)


def _dsa_body(q_ref, qi_ref, wi_ref, kidx_ref, k_ref, v_ref, o_ref, keys_ref, m_ref, l_ref, acc_ref,
              *, tq, kb, q_off, s_valid, topk):
    i = pl.program_id(1)
    n_blk = _n_key_blocks(i, tq, kb, q_off, s_valid)
    qc = _q_chunk(i, tq, q_off)
    wi = wi_ref[...]
    wcol = [wi[:, hd:hd + 1] for hd in range(IDX_HEADS)]

    def score_block(j, carry):
        ks = pl.multiple_of(j * kb, kb)
        k_lo = kidx_ref[pl.ds(ks, kb), 0:128]
        k_hi = kidx_ref[pl.ds(ks, kb), 128:256]
        sc = jnp.zeros((tq, kb), F32)
        for p in range(IDX_HEADS // 2):
            qp = qi_ref[:, p * 128:(p + 1) * 128]
            sc = sc + wcol[2 * p] * jnp.maximum(_dot_nt(qp, k_lo), 0.0)
            sc = sc + wcol[2 * p + 1] * jnp.maximum(_dot_nt(qp, k_hi), 0.0)
        bits = lax.bitcast_convert_type(sc, I32)
        key = bits ^ ((bits >> 31) & 0x7FFFFFFF)
        keys_ref[j] = jnp.where(_k_chunk(ks, kb, s_valid) <= qc, key, INT_MIN)
        return carry

    lax.fori_loop(0, n_blk, score_block, 0)

    def count(pred):
        def blk(j, c):
            x = jnp.where(pred(keys_ref[j], j * kb), 1.0, 0.0)
            part = x[:, 0:LANES]
            for g in range(1, kb // LANES):
                part = part + x[:, g * LANES:(g + 1) * LANES]
            return c + part
        c = lax.fori_loop(0, n_blk, blk, jnp.zeros((tq, LANES), F32))
        return jnp.sum(c, axis=1, keepdims=True)

    kf = float(topk)

    def bit_step(b, t):
        cand = t + jnp.left_shift(jnp.int32(1), 31 - b)
        cnt = count(lambda kblk, _: kblk >= cand)
        return jnp.where(cnt >= kf, cand, t)

    thr = lax.fori_loop(0, 32, bit_step, jnp.full((tq, 1), INT_MIN, I32))
    need = kf - count(lambda kblk, _: kblk > thr)
    n_eq = count(lambda kblk, _: kblk == thr)
    partial = jnp.logical_and(n_eq > need, thr != INT_MIN)

    @pl.when(jnp.max(jnp.where(partial, 1.0, 0.0)) > 0.0)
    def _():
        lane = lax.broadcasted_iota(I32, (1, kb), 1)

        def idx_step(b, x):
            cand = x + jnp.left_shift(jnp.int32(1), b)
            cnt = count(lambda kblk, k0: jnp.logical_and(kblk == thr, k0 + lane < cand))
            return jnp.where(cnt < need, cand, x)

        nbits = max(1, int(s_valid - 1).bit_length())
        cut = lax.fori_loop(0, nbits, lambda b, x: idx_step(nbits - 1 - b, x), jnp.zeros((tq, 1), I32))

        def demote(j, carry):
            kblk = keys_ref[j]
            drop = jnp.logical_and(jnp.logical_and(kblk == thr, j * kb + lane > cut), partial)
            keys_ref[j] = jnp.where(drop, kblk - 1, kblk)
            return carry

        lax.fori_loop(0, n_blk, demote, 0)

    thr_sel = jnp.maximum(thr, INT_MIN + 1)

    _flash_init(m_ref, l_ref, acc_ref)

    def attend(j, carry):
        ks = pl.multiple_of(j * kb, kb)
        bias = jnp.where(keys_ref[j] >= thr_sel, 0.0, NEG)
        for g in range(A_KV_HEADS):
            qg = jnp.concatenate([q_ref[:, (g * A_GROUP + a) * 128:(g * A_GROUP + a + 1) * 128]
                                  for a in range(A_GROUP)], axis=0)
            s = _dot_nt(qg, k_ref[pl.ds(ks, kb), g * 128:(g + 1) * 128])
            s = (s.reshape(A_GROUP, tq, kb) + bias[None]).reshape(A_GROUP * tq, kb)
            _flash_step(s, v_ref[pl.ds(ks, kb), g * 128:(g + 1) * 128], m_ref.at[g], l_ref.at[g], acc_ref.at[g])
        return carry

    lax.fori_loop(0, n_blk, attend, 0)
    for g in range(A_KV_HEADS):
        o = acc_ref[g] / l_ref[g]
        for a in range(A_GROUP):
            hd = g * A_GROUP + a
            o_ref[:, hd * 128:(hd + 1) * 128] = o[a * tq:(a + 1) * tq].astype(BF16)


def _dsa_attn(q, qi, wi, kidx, k, v, *, tq, kb, q_off, s_valid, topk):
    b, t, _ = q.shape
    s_pad = k.shape[1]
    qspec = lambda c: pl.BlockSpec((None, tq, c), lambda bi, i: (bi, i, 0))
    kspec = lambda c: pl.BlockSpec((None, s_pad, c), lambda bi, i: (bi, 0, 0))
    rows = A_GROUP * tq
    return pl.pallas_call(
        functools.partial(_dsa_body, tq=tq, kb=kb, q_off=q_off, s_valid=s_valid, topk=topk),
        grid=(b, t // tq),
        in_specs=[qspec(1024), qspec(512), qspec(LANES), kspec(256), kspec(256), kspec(256)],
        out_specs=qspec(1024),
        out_shape=jax.ShapeDtypeStruct((b, t, 1024), BF16),
        scratch_shapes=[pltpu.VMEM((s_pad // kb, tq, kb), I32), pltpu.VMEM((A_KV_HEADS, rows, 1), F32),
                        pltpu.VMEM((A_KV_HEADS, rows, 1), F32), pltpu.VMEM((A_KV_HEADS, rows, 128), F32)],
        compiler_params=_cparams(2),
        name="dsa_attn",
    )(q, qi, wi, kidx, k, v)


def _mla_body(q_ref, kn_ref, kr_ref, v_ref, o_ref, m_ref, l_ref, acc_ref, *, tq, kb, q_off, s_valid):
    i = pl.program_id(1)
    n_blk = _n_key_blocks(i, tq, kb, q_off, s_valid)
    qc = _q_chunk(i, tq, q_off)
    _flash_init(m_ref, l_ref, acc_ref)

    def attend(j, carry):
        ks = pl.multiple_of(j * kb, kb)
        kr = kr_ref[pl.ds(ks, kb), :]
        ok = _k_chunk(ks, kb, s_valid) <= qc
        for hd in range(B_HEADS):
            kcat = jnp.concatenate([kn_ref[pl.ds(ks, kb), hd * 128:(hd + 1) * 128], kr], axis=1)
            s = jnp.where(ok, _dot_nt(q_ref[:, hd * 256:(hd + 1) * 256], kcat), NEG)
            _flash_step(s, v_ref[pl.ds(ks, kb), hd * 128:(hd + 1) * 128], m_ref.at[hd], l_ref.at[hd], acc_ref.at[hd])
        return carry

    lax.fori_loop(0, n_blk, attend, 0)
    for hd in range(B_HEADS):
        o_ref[:, hd * 128:(hd + 1) * 128] = (acc_ref[hd] / l_ref[hd]).astype(BF16)


def _mla_attn(q, kn, kr, v, *, tq, kb, q_off, s_valid):
    b, t, _ = q.shape
    s_pad = kn.shape[1]
    return pl.pallas_call(
        functools.partial(_mla_body, tq=tq, kb=kb, q_off=q_off, s_valid=s_valid),
        grid=(b, t // tq),
        in_specs=[pl.BlockSpec((None, tq, B_HEADS * 256), lambda bi, i: (bi, i, 0)),
                  pl.BlockSpec((None, s_pad, B_HEADS * 128), lambda bi, i: (bi, 0, 0)),
                  pl.BlockSpec((None, s_pad, 128), lambda bi, i: (bi, 0, 0)),
                  pl.BlockSpec((None, s_pad, B_HEADS * 128), lambda bi, i: (bi, 0, 0))],
        out_specs=pl.BlockSpec((None, tq, B_HEADS * V_DIM), lambda bi, i: (bi, i, 0)),
        out_shape=jax.ShapeDtypeStruct((b, t, B_HEADS * V_DIM), BF16),
        scratch_shapes=[pltpu.VMEM((B_HEADS, tq, 1), F32), pltpu.VMEM((B_HEADS, tq, 1), F32),
                        pltpu.VMEM((B_HEADS, tq, 128), F32)],
        compiler_params=_cparams(2),
        name="mla_attn",
    )(q, kn, kr, v)


def _diff_body(q_ref, k_ref, v_ref, lq1_ref, lk1_ref, lq2_ref, lk2_ref, sn_ref, o_ref, m_ref, l_ref, acc_ref,
               *, tq, kb, q_off, s_valid, lam_init):
    i = pl.program_id(1)
    n_blk = _n_key_blocks(i, tq, kb, q_off, s_valid)
    qc = _q_chunk(i, tq, q_off)
    _flash_init(m_ref, l_ref, acc_ref)

    def attend(j, carry):
        ks = pl.multiple_of(j * kb, kb)
        ok = _k_chunk(ks, kb, s_valid) <= qc
        for hd in range(C_HEADS):
            v = v_ref[pl.ds(ks, kb), hd * 256:(hd + 1) * 256]
            for p in range(2):
                c = 2 * hd + p
                s = _dot_nt(q_ref[:, c * 128:(c + 1) * 128], k_ref[pl.ds(ks, kb), c * 128:(c + 1) * 128])
                _flash_step(jnp.where(ok, s, NEG), v, m_ref.at[c], l_ref.at[c], acc_ref.at[c])
        return carry

    lax.fori_loop(0, n_blk, attend, 0)
    lam = (jnp.exp(jnp.sum(lq1_ref[...] * lk1_ref[...], axis=1, keepdims=True))
           - jnp.exp(jnp.sum(lq2_ref[...] * lk2_ref[...], axis=1, keepdims=True)) + lam_init)
    for hd in range(C_HEADS):
        o = acc_ref[2 * hd] / l_ref[2 * hd] - lam * (acc_ref[2 * hd + 1] / l_ref[2 * hd + 1])
        o_ref[:, hd * 256:(hd + 1) * 256] = (_rms(o, sn_ref[...], 2 * C_DIM) * (1.0 - lam_init)).astype(BF16)


def _diff_attn(q, k, v, lq1, lk1, lq2, lk2, sn, *, tq, kb, q_off, s_valid, lam_init):
    b, t, c = q.shape
    s_pad = k.shape[1]
    vec = lambda n: pl.BlockSpec((1, n), lambda bi, i: (0, 0))
    return pl.pallas_call(
        functools.partial(_diff_body, tq=tq, kb=kb, q_off=q_off, s_valid=s_valid, lam_init=lam_init),
        grid=(b, t // tq),
        in_specs=[pl.BlockSpec((None, tq, c), lambda bi, i: (bi, i, 0)),
                  pl.BlockSpec((None, s_pad, c), lambda bi, i: (bi, 0, 0)),
                  pl.BlockSpec((None, s_pad, c), lambda bi, i: (bi, 0, 0)),
                  vec(128), vec(128), vec(128), vec(128), vec(256)],
        out_specs=pl.BlockSpec((None, tq, c), lambda bi, i: (bi, i, 0)),
        out_shape=jax.ShapeDtypeStruct((b, t, c), BF16),
        scratch_shapes=[pltpu.VMEM((2 * C_HEADS, tq, 1), F32), pltpu.VMEM((2 * C_HEADS, tq, 1), F32),
                        pltpu.VMEM((2 * C_HEADS, tq, 256), F32)],
        compiler_params=_cparams(2),
        name="diff_attn",
    )(q, k, v, lq1, lk1, lq2, lk2, sn)


SCORE_BLOCKS = 4
SEARCH_BLOCKS = 4
SEARCH_BITS_UNCHECKED = 26
SEARCH_BITS_PER_CHECK = 2


def _stage_bufs(n_chain, kb, r):
    return [pltpu.VMEM((n_chain, 2, kb, r), F32), pltpu.VMEM((n_chain, 2, kb, r), BF16),
            pltpu.VMEM((n_chain, 2, 1, r), F32)]


def _staged_flash_t(bufs, state, lo, hi, n_kb, qk, prep, vt_of, first=True, last=True):
    s_ref, p_ref, a_ref = bufs
    m_ref, l_ref, acc_ref = state
    n_chain = s_ref.shape[0]
    clamp = lambda j: jnp.clip(j, 0, n_kb - 1)

    def softmax(c, j, slot):
        st = prep(c, j, s_ref[c, slot])
        m_prev = m_ref[c]
        m_new = jnp.maximum(m_prev, jnp.max(st, axis=0, keepdims=True))
        alpha = jnp.exp2(m_prev - m_new)
        p = jnp.exp2(st - m_new)
        l_ref[c] = alpha * l_ref[c] + jnp.sum(p, axis=0, keepdims=True)
        m_ref[c] = m_new
        p_ref[c, slot] = p.astype(BF16)
        a_ref[c, slot] = alpha

    def values(c, j, slot):
        acc_ref[c] = a_ref[c, slot] * acc_ref[c] + _dot(vt_of(c, clamp(j)), p_ref[c, slot])

    if first:
        for c in range(n_chain):
            s_ref[c, 0] = qk(c, clamp(lo))
            p_ref[c, 1] = jnp.zeros(p_ref.shape[2:], BF16)
            a_ref[c, 1] = jnp.ones(a_ref.shape[2:], F32)

    def turn(t, carry):
        j0 = lo + 2 * t
        for c in range(n_chain):
            s_ref[c, 1] = qk(c, clamp(j0 + 1))
        for c in range(n_chain):
            softmax(c, j0, 0)
        for c in range(n_chain):
            values(c, j0 - 1, 1)
        for c in range(n_chain):
            s_ref[c, 0] = qk(c, clamp(j0 + 2))
        for c in range(n_chain):
            softmax(c, j0 + 1, 1)
        for c in range(n_chain):
            values(c, j0, 0)
        return carry

    n_turn = lax.shift_right_logical(hi - lo + 1, 1)
    lax.fori_loop(0, n_turn, turn, 0)
    if last:
        for c in range(n_chain):
            values(c, lo + 2 * n_turn - 1, 1)


def _dsa_t_body(q_ref, qi_ref, wi_ref, kidx_ref, k_ref, vt_ref, o_ref, keys_ref, sc_ref, s_ref, p_ref, a_ref,
                m_ref, l_ref, acc_ref, *, tq, kb, q_off, s_valid, topk):
    i = pl.program_id(1)
    n_blk = _n_key_blocks(i, tq, kb, q_off, s_valid)
    n_kb = keys_ref.shape[0]
    n_sb = lax.shift_right_logical(n_blk + SEARCH_BLOCKS - 1, int(math.log2(SEARCH_BLOCKS)))
    limit = _key_limit(i, tq, q_off, s_valid)
    kidx0 = lax.broadcasted_iota(I32, (kb, tq), 0)
    wit = wi_ref[...].T
    wrow = [wit[hd:hd + 1, :] for hd in range(IDX_HEADS)]

    def score_blocks(jj, carry):
        ks = pl.multiple_of(jj * (SCORE_BLOCKS * kb), SCORE_BLOCKS * kb)
        half = SCORE_BLOCKS // 2
        q_all = jnp.concatenate([qi_ref[:, p * 128:(p + 1) * 128] for p in range(IDX_HEADS // 2)], axis=0)
        for hf in range(2):
            rows = pl.ds(ks + hf * half * kb, half * kb)
            sc_ref[hf, 0] = _dot_nt(kidx_ref[rows, 0:128], q_all)
            sc_ref[hf, 1] = _dot_nt(kidx_ref[rows, 128:256], q_all)
        for hf in range(2):
            for u in range(half):
                blk = slice(u * kb, (u + 1) * kb)
                sc = jnp.zeros((kb, tq), F32)
                for p in range(IDX_HEADS // 2):
                    sc = sc + wrow[2 * p] * jnp.maximum(sc_ref[hf, 0, blk, p * tq:(p + 1) * tq], 0.0)
                    sc = sc + wrow[2 * p + 1] * jnp.maximum(sc_ref[hf, 1, blk, p * tq:(p + 1) * tq], 0.0)
                bits = lax.bitcast_convert_type(sc, I32)
                key = bits ^ ((bits >> 31) & 0x7FFFFFFF)
                j = jj * SCORE_BLOCKS + hf * half + u
                keys_ref[j] = jnp.where(kidx0 < limit - j * kb, key, INT_MIN)
        return carry

    def pad_block(j, carry):
        keys_ref[j] = jnp.full((kb, tq), INT_MIN, I32)
        return carry

    n_scored = lax.shift_right_logical(n_blk + SCORE_BLOCKS - 1, int(math.log2(SCORE_BLOCKS)))
    lax.fori_loop(0, n_scored, score_blocks, 0)
    lax.fori_loop(n_scored * SCORE_BLOCKS, n_sb * SEARCH_BLOCKS, pad_block, 0)

    def count(pred):
        def group(jj, cs):
            out = []
            for u in range(SEARCH_BLOCKS):
                j = jj * SEARCH_BLOCKS + u
                x = jnp.where(pred(keys_ref[j], j * kb), 1.0, 0.0).reshape(kb // 8, 8, tq)
                h = kb // 32
                out.append(cs[u] + ((jnp.sum(x[:h], axis=0) + jnp.sum(x[h:2 * h], axis=0))
                                    + (jnp.sum(x[2 * h:3 * h], axis=0) + jnp.sum(x[3 * h:], axis=0))))
            return tuple(out)
        cs = lax.fori_loop(0, n_sb, group, tuple(jnp.zeros((8, tq), F32) for _ in range(SEARCH_BLOCKS)))
        return jnp.sum(functools.reduce(lambda a, b: a + b, cs), axis=0, keepdims=True)

    kf = float(topk)

    def bit_step(b, state):
        t, n_ge = state
        cand = t + jnp.left_shift(jnp.int32(1), 31 - b)
        cnt = count(lambda kblk, _: kblk >= cand)
        take = cnt >= kf
        return jnp.where(take, cand, t), jnp.where(take, cnt, n_ge)

    def unsettled(n_ge):
        open_ = jnp.logical_and(n_ge != kf, limit.astype(F32) > kf)
        return jnp.max(jnp.where(open_, 1.0, 0.0)) > 0.0

    def more_bits(state):
        b, _, n_ge = state
        return jnp.logical_and(b < 32, unsettled(n_ge))

    def four_bits(state):
        b, t, n_ge = state
        for u in range(SEARCH_BITS_PER_CHECK):
            t, n_ge = bit_step(b + u, (t, n_ge))
        return b + SEARCH_BITS_PER_CHECK, t, n_ge

    start = (jnp.full((1, tq), INT_MIN, I32), jnp.full((1, tq), float(n_kb * kb), F32))
    head = lax.fori_loop(0, SEARCH_BITS_UNCHECKED, bit_step, start)
    _, thr, _ = lax.while_loop(more_bits, four_bits, (jnp.int32(SEARCH_BITS_UNCHECKED),) + head)
    need = kf - count(lambda kblk, _: kblk > thr)
    n_eq = count(lambda kblk, _: kblk == thr)
    partial = jnp.logical_and(n_eq > need, thr != INT_MIN)

    @pl.when(jnp.max(jnp.where(partial, 1.0, 0.0)) > 0.0)
    def _():
        def idx_step(b, x):
            cand = x + jnp.left_shift(jnp.int32(1), b)
            cnt = count(lambda kblk, k0: jnp.logical_and(kblk == thr, kidx0 < cand - k0))
            return jnp.where(cnt < need, cand, x)

        nbits = max(1, int(s_valid - 1).bit_length())
        cut = lax.fori_loop(0, nbits, lambda b, x: idx_step(nbits - 1 - b, x), jnp.zeros((1, tq), I32))

        def demote(j, carry):
            kblk = keys_ref[j]
            drop = jnp.logical_and(jnp.logical_and(kblk == thr, kidx0 > cut - j * kb), partial)
            keys_ref[j] = jnp.where(drop, kblk - 1, kblk)
            return carry

        lax.fori_loop(0, n_blk, demote, 0)

    thr_sel = jnp.maximum(thr, INT_MIN + 1)

    _flash_init(m_ref, l_ref, acc_ref)
    n_pair = A_HEADS // 2

    def to_bias(j, carry):
        keys_ref[j] = lax.bitcast_convert_type(jnp.where(keys_ref[j] >= thr_sel, 0.0, NEG), I32)
        return carry

    lax.fori_loop(0, n_sb * SEARCH_BLOCKS, to_bias, 0)
    for g in range(A_KV_HEADS):
        def qk(c, j, g=g):
            p = g * (A_GROUP // 2) + c
            qp = jnp.concatenate([q_ref[:, (2 * p) * 128:(2 * p + 1) * 128],
                                  q_ref[:, (2 * p + 1) * 128:(2 * p + 2) * 128]], axis=0)
            return _dot_nt(k_ref[pl.ds(pl.multiple_of(j * kb, kb), kb), g * 128:(g + 1) * 128], qp)

        def prep(c, j, st):
            bias = lax.bitcast_convert_type(keys_ref[j], F32)
            return st + jnp.concatenate([bias, bias], axis=1)

        def vt_of(c, j, g=g):
            return vt_ref[j, g * 128:(g + 1) * 128, :]

        chains = pl.ds(g * (A_GROUP // 2), A_GROUP // 2)
        _staged_flash_t((s_ref.at[chains], p_ref.at[chains], a_ref.at[chains]),
                        (m_ref.at[chains], l_ref.at[chains], acc_ref.at[chains]), 0, n_blk, n_kb, qk, prep, vt_of)
    for p in range(n_pair):
        ot = acc_ref[p] / l_ref[p]
        for a in range(2):
            hd = 2 * p + a
            o_ref[:, hd * 128:(hd + 1) * 128] = ot[:, a * tq:(a + 1) * tq].T.astype(BF16)


def _dsa_attn_t(q, qi, wi, kidx, k, vt, *, tq, kb, q_off, s_valid, topk):
    b, t, _ = q.shape
    s_pad = k.shape[1]
    n_kb = s_pad // kb
    qspec = lambda c: pl.BlockSpec((None, tq, c), lambda bi, i: (bi, i, 0))
    kspec = lambda c: pl.BlockSpec((None, s_pad, c), lambda bi, i: (bi, 0, 0))
    n_pair = A_HEADS // 2
    return pl.pallas_call(
        functools.partial(_dsa_t_body, tq=tq, kb=kb, q_off=q_off, s_valid=s_valid, topk=topk),
        grid=(b, t // tq),
        in_specs=[qspec(1024), qspec(512), qspec(LANES),
                  kspec(256), kspec(256), pl.BlockSpec((None, n_kb, 256, kb), lambda bi, i: (bi, 0, 0, 0))],
        out_specs=qspec(1024),
        out_shape=jax.ShapeDtypeStruct((b, t, 1024), BF16),
        scratch_shapes=[pltpu.VMEM((n_kb, kb, tq), I32),
                        pltpu.VMEM((2, 2, SCORE_BLOCKS // 2 * kb, IDX_HEADS // 2 * tq), F32)]
        + _stage_bufs(n_pair, kb, 2 * tq)
        + [pltpu.VMEM((n_pair, 1, 2 * tq), F32), pltpu.VMEM((n_pair, 1, 2 * tq), F32),
           pltpu.VMEM((n_pair, 128, 2 * tq), F32)],
        compiler_params=_cparams(2),
        name="dsa_attn_t",
    )(q, qi, wi, kidx, k, vt)


def _causal_flash_t(bufs, state, i, n_kb, tq, kb, q_off, s_valid, qk, vt_of):
    n_blk = _n_key_blocks(i, tq, kb, q_off, s_valid)
    n_plain = _n_full_blocks(i, tq, kb, q_off, s_valid) & -2
    limit = _key_limit(i, tq, q_off, s_valid)

    def masked(c, j, st):
        return jnp.where(lax.broadcasted_iota(I32, (kb, tq), 0) < limit - j * kb, st, NEG)

    _staged_flash_t(bufs, state, 0, n_plain, n_kb, qk, lambda c, j, st: st, vt_of, last=False)
    _staged_flash_t(bufs, state, n_plain, n_blk, n_kb, qk, masked, vt_of, first=False)


def _mla_t_body(qt_ref, kn_ref, kr_ref, vt_ref, o_ref, s_ref, p_ref, a_ref, m_ref, l_ref, acc_ref,
                *, tq, kb, q_off, s_valid):
    i = pl.program_id(2)
    _flash_init(m_ref, l_ref, acc_ref)

    def qk(a, j):
        ks = pl.multiple_of(j * kb, kb)
        kcat = jnp.concatenate([kn_ref[pl.ds(ks, kb), a * 128:(a + 1) * 128], kr_ref[pl.ds(ks, kb), :]], axis=1)
        return _dot(kcat, qt_ref[a * 256:(a + 1) * 256, :])

    def vt_of(a, j):
        return vt_ref[j, a * 128:(a + 1) * 128, :]

    _causal_flash_t((s_ref, p_ref, a_ref), (m_ref, l_ref, acc_ref), i, vt_ref.shape[0], tq, kb, q_off, s_valid,
                    qk, vt_of)
    for a in range(2):
        o_ref[:, a * 128:(a + 1) * 128] = (acc_ref[a] / l_ref[a]).T.astype(BF16)


def _mla_attn_t(qt, kn, kr, vt, *, tq, kb, q_off, s_valid):
    b, _, t = qt.shape
    s_pad = kn.shape[1]
    n_kb = s_pad // kb
    return pl.pallas_call(
        functools.partial(_mla_t_body, tq=tq, kb=kb, q_off=q_off, s_valid=s_valid),
        grid=(b, B_HEADS // 2, t // tq),
        in_specs=[pl.BlockSpec((None, 512, tq), lambda bi, h, i: (bi, h, i)),
                  pl.BlockSpec((None, s_pad, 256), lambda bi, h, i: (bi, 0, h)),
                  pl.BlockSpec((None, s_pad, 128), lambda bi, h, i: (bi, 0, 0)),
                  pl.BlockSpec((None, n_kb, 256, kb), lambda bi, h, i: (bi, 0, h, 0))],
        out_specs=pl.BlockSpec((None, tq, 256), lambda bi, h, i: (bi, i, h)),
        out_shape=jax.ShapeDtypeStruct((b, t, B_HEADS * V_DIM), BF16),
        scratch_shapes=_stage_bufs(2, kb, tq) + [pltpu.VMEM((2, 1, tq), F32), pltpu.VMEM((2, 1, tq), F32),
                                                 pltpu.VMEM((2, 128, tq), F32)],
        compiler_params=_cparams(3),
        name="mla_attn_t",
    )(qt, kn, kr, vt)


def _diff_t_body(q_ref, k_ref, vt_ref, lq1_ref, lk1_ref, lq2_ref, lk2_ref, sn_ref, o_ref, s_ref, p_ref, a_ref,
                 m_ref, l_ref, acc_ref, *, tq, kb, q_off, s_valid, lam_init):
    i = pl.program_id(2)
    _flash_init(m_ref, l_ref, acc_ref)

    def qk(p, j):
        ks = pl.multiple_of(j * kb, kb)
        return _dot_nt(k_ref[pl.ds(ks, kb), p * 128:(p + 1) * 128], q_ref[:, p * 128:(p + 1) * 128])

    _causal_flash_t((s_ref, p_ref, a_ref), (m_ref, l_ref, acc_ref), i, vt_ref.shape[0], tq, kb, q_off, s_valid,
                    qk, lambda p, j: vt_ref[j])
    lam = (jnp.exp(jnp.sum(lq1_ref[...] * lk1_ref[...], axis=1, keepdims=True))
           - jnp.exp(jnp.sum(lq2_ref[...] * lk2_ref[...], axis=1, keepdims=True)) + lam_init)
    ot = acc_ref[0] / l_ref[0] - lam * (acc_ref[1] / l_ref[1])
    o = jnp.concatenate([ot[0:128, :].T, ot[128:256, :].T], axis=1)
    o_ref[...] = (_rms(o, sn_ref[...], 2 * C_DIM) * (1.0 - lam_init)).astype(BF16)


def _diff_attn_t(q, k, vt, lq1, lk1, lq2, lk2, sn, *, tq, kb, q_off, s_valid, lam_init):
    b, t, _ = q.shape
    s_pad = k.shape[1]
    n_kb = s_pad // kb
    vec = lambda c: pl.BlockSpec((1, c), lambda bi, h, i: (0, 0))
    return pl.pallas_call(
        functools.partial(_diff_t_body, tq=tq, kb=kb, q_off=q_off, s_valid=s_valid, lam_init=lam_init),
        grid=(b, C_HEADS, t // tq),
        in_specs=[pl.BlockSpec((None, tq, 256), lambda bi, h, i: (bi, i, h)),
                  pl.BlockSpec((None, s_pad, 256), lambda bi, h, i: (bi, 0, h)),
                  pl.BlockSpec((None, n_kb, 256, kb), lambda bi, h, i: (bi, 0, h, 0)),
                  vec(128), vec(128), vec(128), vec(128), vec(256)],
        out_specs=pl.BlockSpec((None, tq, 256), lambda bi, h, i: (bi, i, h)),
        out_shape=jax.ShapeDtypeStruct((b, t, C_HEADS * 2 * C_DIM), BF16),
        scratch_shapes=_stage_bufs(2, kb, tq) + [pltpu.VMEM((2, 1, tq), F32), pltpu.VMEM((2, 1, tq), F32),
                                                 pltpu.VMEM((2, 256, tq), F32)],
        compiler_params=_cparams(3),
        name="diff_attn_t",
    )(q, k, vt, lq1, lk1, lq2, lk2, sn)


def _blocked_t(v, kb):
    b, s, c = v.shape
    return jnp.swapaxes(v.reshape(b, s // kb, kb, c), 2, 3)


def _rope_tables(pos, reps):
    p = pos.astype(F32)[:, None]
    inv64 = jnp.power(ROPE_THETA, -jnp.arange(64, dtype=F32) / 64)
    inv32 = jnp.power(ROPE_THETA, -jnp.arange(32, dtype=F32) / 32)
    c, s = jnp.cos(p * inv64), jnp.sin(p * inv64)
    c3, s3 = jnp.cos(p * inv32), jnp.sin(p * inv32)
    z = jnp.zeros_like(s3)
    tabs = {
        "c128": jnp.concatenate([c, c], axis=1),
        "s128": jnp.concatenate([-s, s], axis=1),
        "c64": jnp.concatenate([c3, c3, c3, c3], axis=1),
        "sa64": jnp.concatenate([-s3, z, -s3, z], axis=1),
        "sb64": jnp.concatenate([z, s3, z, s3], axis=1),
    }
    return {k: jnp.tile(v, (reps, 1)) for k, v in tabs.items()}


def _pad_cols(w, n):
    return jnp.pad(w, ((0, 0), (0, n - w.shape[1])))


def _pad_lanes(g, n=128):
    g = g.reshape(1, -1)
    return jnp.pad(g, ((0, 0), (0, n - g.shape[1])))


def _prep_weights(W):
    P = {}
    for nm in ("ffn1", "ffn2"):
        P[nm] = [(W[nm + "_norm"][i].reshape(1, -1), W[nm + "_wg"][i].astype(BF16), W[nm + "_wu"][i].astype(BF16),
                  W[nm + "_wd"][i].astype(BF16)) for i in range(DEPTH)]
    P["mix_norm"] = [W["mix_norm"][i].reshape(1, -1) for i in range(DEPTH)]
    P["a"] = [dict(w=_pad_cols(W["a_w_in"][j], A_IN_PAD).astype(BF16), qn=W["a_q_norm"][j].reshape(1, -1),
                   kn=W["a_k_norm"][j].reshape(1, -1), ikn=_pad_lanes(W["a_idx_k_norm"][j]),
                   wo=W["a_w_out"][j].astype(BF16)) for j in range(W["a_w_in"].shape[0])]
    P["b"] = []
    for j in range(W["b_w_in"].shape[0]):
        wuq = W["b_w_uq"][j].reshape(Q_LORA, B_HEADS, NOPE_DIM + ROPE_DIM)
        wuq_rope = jnp.pad(wuq[:, :, NOPE_DIM:], ((0, 0), (0, 0), (0, 128 - ROPE_DIM)))
        wuq = jnp.concatenate([wuq[:, :, :NOPE_DIM], wuq_rope], axis=2).reshape(Q_LORA, -1)
        P["b"].append(dict(
            w=_pad_cols(W["b_w_in"][j], B_IN_PAD).astype(BF16), qan=W["b_q_a_norm"][j].reshape(1, -1),
            kvan=W["b_kv_a_norm"][j].reshape(1, -1), krn=_pad_lanes(W["b_k_rope_norm"][j]), wuq=wuq.astype(BF16),
            qnn=W["b_q_nope_norm"][j].reshape(1, -1), qrn=_pad_lanes(W["b_q_rope_norm"][j]),
            wukv=W["b_w_ukv"][j].astype(BF16), knn=W["b_k_nope_norm"][j].reshape(1, -1),
            wo=W["b_w_out"][j].astype(BF16)))
    P["c"] = [dict(w=W["c_w_in"][j].astype(BF16), qn=W["c_q_norm"][j].reshape(1, -1),
                   kn=W["c_k_norm"][j].reshape(1, -1), lq1=W["c_lambda_q1"][j].reshape(1, -1),
                   lk1=W["c_lambda_k1"][j].reshape(1, -1), lq2=W["c_lambda_q2"][j].reshape(1, -1),
                   lk2=W["c_lambda_k2"][j].reshape(1, -1), sn=W["c_sub_norm"][j].reshape(1, -1),
                   wo=W["c_w_out"][j].astype(BF16)) for j in range(W["c_w_in"].shape[0])]
    return P


def _with_past(past, new, s_pad):
    x = new if past is None else jnp.concatenate([past.astype(new.dtype), new], axis=1)
    return x if x.shape[1] == s_pad else jnp.pad(x, ((0, 0), (0, s_pad - x.shape[1]), (0, 0)))


def _trunk(x, offset, past, P, cfg):
    b, t, _ = x.shape
    n = b * t
    tm, tq_a, tq, kb, key_major = cfg["tm"], cfg["tq_a"], cfg["tq"], cfg["kb"], cfg["key_major"]
    p_len = 0 if past is None else past["a_k"].shape[2]
    s_valid = p_len + t
    s_pad = -(-s_valid // kb) * kb
    tabs = _rope_tables(offset + jnp.arange(t, dtype=I32), tm // t if tm > t else 1)
    att = dict(kb=kb, q_off=offset, s_valid=s_valid)
    rows = {k: [] for k in ("a_k", "a_v", "a_ik", "b_ckv", "b_kr", "c_k", "c_v")}
    x = x.reshape(n, D_MODEL)
    r3 = lambda a: a.reshape(b, t, a.shape[-1])
    t3 = lambda a: jnp.swapaxes(r3(a), 1, 2)
    pj = lambda nm, j: None if past is None else past[nm][j].reshape(b, p_len, -1)
    for i in range(DEPTH):
        x = _ffn(x, *P["ffn1"][i], cfg["tm_ffn"])
        kind, j = i % N_MIXERS, i // N_MIXERS
        g = P["mix_norm"][i]
        if kind == 0:
            pa = P["a"][j]
            q, k32, v32, kbf, vbf, qi, ki32, kidx, wi = _a_proj(x, g, pa["w"], pa["qn"], pa["kn"], pa["ikn"], tabs, tm)
            rows["a_k"].append(k32.reshape(b, t, A_KV_HEADS, A_HEAD_DIM))
            rows["a_v"].append(v32.reshape(b, t, A_KV_HEADS, A_HEAD_DIM))
            rows["a_ik"].append(ki32.reshape(b, t, IDX_DIM))
            pik = pj("a_ik", j)
            if pik is not None:
                z = jnp.zeros_like(pik)
                pik = jnp.concatenate([pik, z, z, pik], axis=-1)
            kidx_all = _with_past(pik, r3(kidx), s_pad)
            k_all = _with_past(pj("a_k", j), r3(kbf), s_pad)
            v_all = _with_past(pj("a_v", j), r3(vbf), s_pad)
            topk = min(TOPK_MAX, s_valid // 4)
            if key_major:
                o = _dsa_attn_t(r3(q), r3(qi), r3(wi), kidx_all, k_all, _blocked_t(v_all, kb),
                                tq=tq_a, topk=topk, **att)
            else:
                o = _dsa_attn(r3(q), r3(qi), r3(wi), kidx_all, k_all, v_all, tq=tq_a, topk=topk, **att)
        elif kind == 1:
            pb = P["b"][j]
            q, ckv, kr32, krbf = _b_proj(x, g, pb["w"], pb["qan"], pb["kvan"], pb["krn"], pb["wuq"], pb["qnn"],
                                         pb["qrn"], tabs, tm)
            rows["b_ckv"].append(ckv.reshape(b, t, KV_LORA))
            rows["b_kr"].append(kr32.reshape(b, t, ROPE_DIM))
            ckv_all = _with_past(pj("b_ckv", j), r3(ckv), s_pad)
            pkr = pj("b_kr", j)
            if pkr is not None:
                pkr = jnp.concatenate([pkr, jnp.zeros_like(pkr)], axis=-1)
            kn, v = _kv_up(ckv_all.reshape(b * s_pad, KV_LORA), pb["wukv"], pb["knn"], math.gcd(b * s_pad, 512))
            kn, v, kr_all = kn.reshape(b, s_pad, -1), v.reshape(b, s_pad, -1), _with_past(pkr, r3(krbf), s_pad)
            if key_major:
                o = _mla_attn_t(t3(q), kn, kr_all, _blocked_t(v, cfg["kb_mla"]), tq=cfg["tq_mla"],
                                **dict(att, kb=cfg["kb_mla"]))
            else:
                o = _mla_attn(r3(q), kn, kr_all, v, tq=tq, **att)
        else:
            pc = P["c"][j]
            q, k32, v32, kbf, vbf = _c_proj(x, g, pc["w"], pc["qn"], pc["kn"], tabs, tm)
            rows["c_k"].append(k32.reshape(b, t, C_HEADS, 2, C_DIM))
            rows["c_v"].append(v32.reshape(b, t, C_HEADS, 2 * C_DIM))
            k_all = _with_past(pj("c_k", j), r3(kbf), s_pad)
            v_all = _with_past(pj("c_v", j), r3(vbf), s_pad)
            lam = (pc["lq1"], pc["lk1"], pc["lq2"], pc["lk2"], pc["sn"])
            lam_init = 0.8 - 0.6 * math.exp(-0.3 * i)
            if key_major:
                att_c = dict(att, kb=cfg["kb_diff"])
                o = _diff_attn_t(r3(q), k_all, _blocked_t(v_all, cfg["kb_diff"]), *lam, tq=tq, lam_init=lam_init,
                                 **att_c)
            else:
                o = _diff_attn(r3(q), k_all, v_all, *lam, tq=tq, lam_init=lam_init, **att)
        wo = (P["a"], P["b"], P["c"])[kind][j]["wo"]
        x = _ffn(x, *P["ffn2"][i], cfg["tm_ffn"], attn=o.reshape(n, -1), wo=wo)
    order = ("a_k", "a_v", "a_ik", "b_ckv", "b_kr", "c_k", "c_v")
    stack = lambda rs: rs[0][None] if len(rs) == 1 else jnp.stack(rs)
    return x.reshape(b, t, D_MODEL), tuple(stack(rows[k]) for k in order)


PROMPT_CFG = dict(tm=512, tm_ffn=1024, tq_a=128, tq=512, tq_mla=512, kb=256, kb_mla=512, kb_diff=512, key_major=True)
SAMPLE_CFG = dict(tm=128, tm_ffn=128, tq_a=16, tq=16, kb=1280, key_major=False)


@jax.jit
def _forward(x_prompt, x_sample, past, W):
    P = _prep_weights(W)
    y_p, rows_p = _trunk(x_prompt, 0, None, P, PROMPT_CFG)
    y_s, rows_s = _trunk(x_sample, past["a_k"].shape[2], past, P, SAMPLE_CFG)
    return (y_p, y_s) + rows_p + rows_s


def kernel(x_prompt, x_sample, cache_a_k, cache_a_v, cache_a_idx_k, cache_b_ckv, cache_b_krope, cache_c_k, cache_c_v, ffn1_norm, ffn1_wg, ffn1_wu, ffn1_wd, mix_norm, ffn2_norm, ffn2_wg, ffn2_wu, ffn2_wd, a_w_in, a_q_norm, a_k_norm, a_idx_k_norm, a_w_out, b_w_in, b_q_a_norm, b_kv_a_norm, b_w_uq, b_w_ukv, b_q_nope_norm, b_q_rope_norm, b_k_nope_norm, b_k_rope_norm, b_w_out, c_w_in, c_q_norm, c_k_norm, c_lambda_q1, c_lambda_k1, c_lambda_q2, c_lambda_k2, c_sub_norm, c_w_out):
    W = dict(ffn1_norm=ffn1_norm, ffn1_wg=ffn1_wg, ffn1_wu=ffn1_wu, ffn1_wd=ffn1_wd, mix_norm=mix_norm,
             ffn2_norm=ffn2_norm, ffn2_wg=ffn2_wg, ffn2_wu=ffn2_wu, ffn2_wd=ffn2_wd,
             a_w_in=a_w_in, a_q_norm=a_q_norm, a_k_norm=a_k_norm, a_idx_k_norm=a_idx_k_norm, a_w_out=a_w_out,
             b_w_in=b_w_in, b_q_a_norm=b_q_a_norm, b_kv_a_norm=b_kv_a_norm, b_w_uq=b_w_uq, b_w_ukv=b_w_ukv,
             b_q_nope_norm=b_q_nope_norm, b_q_rope_norm=b_q_rope_norm, b_k_nope_norm=b_k_nope_norm,
             b_k_rope_norm=b_k_rope_norm, b_w_out=b_w_out,
             c_w_in=c_w_in, c_q_norm=c_q_norm, c_k_norm=c_k_norm, c_lambda_q1=c_lambda_q1,
             c_lambda_k1=c_lambda_k1, c_lambda_q2=c_lambda_q2, c_lambda_k2=c_lambda_k2, c_sub_norm=c_sub_norm,
             c_w_out=c_w_out)
    past = dict(a_k=cache_a_k, a_v=cache_a_v, a_ik=cache_a_idx_k, b_ckv=cache_b_ckv, b_kr=cache_b_krope,
                c_k=cache_c_k, c_v=cache_c_v)
    return _forward(x_prompt, x_sample, past, W)
```
